```python
import jax, jax.numpy as jnp
from jax import lax
import numpy as np

D_MODEL = 1024
BATCH = 8
SEQ = 2048
DEPTH = 1
DEC_BATCH = 128
DEC_SEQ = 4
PAST_LEN = 16384
PAGE_SIZE = 128

ML_HEADS = 4
ML_HEAD_DIM = D_MODEL // ML_HEADS
ML_WIDTH = ML_HEADS * ML_HEAD_DIM
ML_CHUNK = 128
GM_GROUPS = 4
GM_WIDTH = D_MODEL
GM_GROUP_DIM = GM_WIDTH // GM_GROUPS
GM_CHUNK = 128
N_EXPERTS = 32
TOP_K = 4
D_FF = D_MODEL
SWIGLU_LIMIT = 7.0
SWIGLU_ALPHA = 1.702
MOE_BLOCK = 128
NORM_EPS = 1e-6
N_MOD = 6
NEG_BIG = -1e30
IN_SIZES = (ML_WIDTH, ML_WIDTH, ML_WIDTH, ML_WIDTH, ML_HEADS, ML_HEADS, GM_WIDTH, GM_WIDTH, D_MODEL, D_MODEL)
N_IN = sum(IN_SIZES)

kernel_name = 'hybrid_mlstm_chunkmlp_moe_step'


def rmsnorm(x, g):
    xf = x.astype(jnp.float32)
    y = xf * lax.rsqrt(jnp.mean(xf * xf, axis=-1, keepdims=True) + NORM_EPS)
    return (y * g.astype(jnp.float32)).astype(x.dtype)


def layernorm(x, g, b):
    xf = x.astype(jnp.float32)
    mu = jnp.mean(xf, axis=-1, keepdims=True)
    var = jnp.mean(jnp.square(xf - mu), axis=-1, keepdims=True)
    y = (xf - mu) * lax.rsqrt(var + NORM_EPS)
    return (y * g.astype(jnp.float32) + b.astype(jnp.float32)).astype(x.dtype)


def mlstm_chunkwise(q, k, v, i_pre, f_pre, C0, n0, m0):
    B, T, H, DK = q.shape
    DV = v.shape[-1]
    L = min(ML_CHUNK, T)
    nc = -(-T // L)
    pad = nc * L - T
    q = q.astype(jnp.float32)
    k = k.astype(jnp.float32) * (DK ** -0.5)
    v = v.astype(jnp.float32)
    ig = i_pre.astype(jnp.float32)
    lf = jax.nn.log_sigmoid(f_pre.astype(jnp.float32))
    if pad:
        pw = ((0, 0), (0, pad), (0, 0), (0, 0))
        q, k, v = jnp.pad(q, pw), jnp.pad(k, pw), jnp.pad(v, pw)
        ig = jnp.pad(ig, ((0, 0), (0, pad), (0, 0)), constant_values=NEG_BIG)
        lf = jnp.pad(lf, ((0, 0), (0, pad), (0, 0)))

    def to_chunks(a):
        a = a.reshape((B, nc, L, H) + a.shape[3:])
        return jnp.moveaxis(jnp.moveaxis(a, 1, 0), 3, 2)

    causal = jnp.tril(jnp.ones((L, L), dtype=bool))

    def step(carry, xs):
        C, n, m = carry
        qc, kc, vc, igc, lfc = xs
        b = jnp.cumsum(lfc, axis=-1)
        logD = jnp.where(causal, b[..., :, None] - b[..., None, :] + igc[..., None, :], -jnp.inf)
        inter = b + m[..., None]
        mt = jnp.maximum(jnp.max(logD, axis=-1), inter)
        S = jnp.einsum('bhtd,bhsd->bhts', qc, kc) * jnp.exp(logD - mt[..., None])
        w_int = jnp.exp(inter - mt)
        num = jnp.einsum('bhts,bhsv->bhtv', S, vc) + w_int[..., None] * jnp.einsum('bhvd,bhtd->bhtv', C, qc)
        den = jnp.sum(S, axis=-1) + w_int * jnp.einsum('bhd,bhtd->bht', n, qc)
        h = num / jnp.maximum(jnp.abs(den), jnp.exp(-mt))[..., None]
        bL = b[..., -1]
        g = bL[..., None] - b + igc
        m_new = jnp.maximum(bL + m, jnp.max(g, axis=-1))
        w_old = jnp.exp(bL + m - m_new)
        w_tok = jnp.exp(g - m_new[..., None])
        C_new = w_old[..., None, None] * C + jnp.einsum('bhs,bhsv,bhsd->bhvd', w_tok, vc, kc)
        n_new = w_old[..., None] * n + jnp.einsum('bhs,bhsd->bhd', w_tok, kc)
        return (C_new, n_new, m_new), h

    carry0 = (C0.astype(jnp.float32), n0.astype(jnp.float32), m0.astype(jnp.float32))
    xs = (to_chunks(q), to_chunks(k), to_chunks(v), to_chunks(ig), to_chunks(lf))
    (C, n, m), hs = lax.scan(step, carry0, xs)
    hs = jnp.transpose(hs, (1, 0, 3, 2, 4)).reshape(B, nc * L, H, DV)[:, :T]
    return hs, C, n, m


def chunk_spatial_gate(u, v, w_s, b_s):
    B, T, G, DG = v.shape
    nc = -(-T // GM_CHUNK)
    pad = nc * GM_CHUNK - T
    vp = jnp.pad(v, ((0, 0), (0, pad), (0, 0), (0, 0))).reshape(B, nc, GM_CHUNK, G, DG)
    w = jnp.where(jnp.tril(jnp.ones((GM_CHUNK, GM_CHUNK), dtype=bool)), w_s, 0.0).astype(v.dtype)
    mixed = jnp.einsum('gts,bnsgd->bntgd', w, vp) + b_s.T[:, :, None].astype(v.dtype)
    mixed = mixed.reshape(B, nc * GM_CHUNK, G, DG)[:, :T]
    return u * mixed


def moe_ffn(h, w_router, b_router, w_gu, b_gu, w_down, b_down):
    N, D = h.shape
    logits = h.astype(jnp.float32) @ w_router.astype(jnp.float32) + b_router.astype(jnp.float32)
    top_val, top_idx = lax.top_k(logits, TOP_K)
    gates = jax.nn.softmax(top_val, axis=-1).astype(h.dtype)
    NK = N * TOP_K
    flat_e = top_idx.reshape(NK)
    flat_tok = jnp.arange(NK, dtype=jnp.int32) // TOP_K
    flat_w = gates.reshape(NK)
    order = jnp.argsort(flat_e)
    se = flat_e[order]
    counts = jnp.bincount(flat_e, length=N_EXPERTS)
    padded = ((counts + MOE_BLOCK - 1) // MOE_BLOCK) * MOE_BLOCK
    starts = jnp.cumsum(counts) - counts
    pends = jnp.cumsum(padded)
    pstarts = pends - padded
    dest = pstarts[se] + (jnp.arange(NK, dtype=jnp.int32) - starts[se])
    n_blocks = -(-(NK + N_EXPERTS * (MOE_BLOCK - 1)) // MOE_BLOCK)
    P = n_blocks * MOE_BLOCK
    slot_tok = jnp.full((P,), N, dtype=jnp.int32).at[dest].set(flat_tok[order])
    slot_w = jnp.zeros((P,), dtype=h.dtype).at[dest].set(flat_w[order])
    block_start = jnp.arange(n_blocks, dtype=jnp.int32) * MOE_BLOCK
    block_e = jnp.clip(jnp.searchsorted(pends, block_start, side='right'), 0, N_EXPERTS - 1)
    h_pad = jnp.concatenate([h, jnp.zeros((1, D), dtype=h.dtype)], axis=0)
    xb = h_pad[slot_tok].reshape(n_blocks, MOE_BLOCK, D)

    def expert_block(args):
        xe, e = args
        gu = xe @ w_gu[e] + b_gu[e]
        gate, up = gu[:, :D_FF], gu[:, D_FF:]
        gate = jnp.minimum(gate, SWIGLU_LIMIT)
        up = jnp.clip(up, -SWIGLU_LIMIT, SWIGLU_LIMIT)
        act = gate * jax.nn.sigmoid(SWIGLU_ALPHA * gate) * (up + 1.0)
        return act @ w_down[e] + b_down[e]

    yb = lax.map(expert_block, (xb, block_e)).reshape(P, D)
    y = jax.ops.segment_sum(yb * slot_w[:, None], slot_tok, num_segments=N + 1)[:N]
    return y


def hybrid_layer(x, c, C0, n0, m0, p):
    B, T, D = x.shape
    mod = jax.nn.silu(c) @ p['w_ada'] + p['b_ada']
    sh1, sc1, gt1, sh2, sc2, gt2 = jnp.split(mod[:, None, :], N_MOD, axis=-1)
    h = rmsnorm(x, p['g_mix']) * (1.0 + sc1) + sh1
    z = h @ p['w_in']
    offsets = [sum(IN_SIZES[:i + 1]) for i in range(len(IN_SIZES) - 1)]
    q, k, v, o, ig, fg, u_gm, v_gm, ga, gb = jnp.split(z, offsets, axis=-1)
    ig = ig + p['b_if'][:ML_HEADS]
    fg = fg + p['b_if'][ML_HEADS:]
    hd = lambda a: a.reshape(B, T, ML_HEADS, ML_HEAD_DIM)
    hm, C1, n1, m1 = mlstm_chunkwise(hd(q), hd(k), hd(v), ig, fg, C0, n0, m0)
    hm = jax.nn.sigmoid(hd(o).astype(jnp.float32)) * hm
    hm = rmsnorm(hm, p['g_head'].reshape(ML_HEADS, ML_HEAD_DIM)).astype(x.dtype).reshape(B, T, ML_WIDTH)
    u_gm = jax.nn.gelu(u_gm)
    v_gm = layernorm(jax.nn.gelu(v_gm), p['g_sgu'], p['b_sgu'])
    gr = lambda a: a.reshape(B, T, GM_GROUPS, GM_GROUP_DIM)
    yg = chunk_spatial_gate(gr(u_gm), gr(v_gm), p['w_s'], p['b_s']).reshape(B, T, GM_WIDTH)
    merged = jax.nn.sigmoid(ga) * (hm @ p['w_pa']) + jax.nn.sigmoid(gb) * (yg @ p['w_pb'])
    x = x + gt1 * (merged @ p['w_out'])
    h2 = rmsnorm(x, p['g_ffn']) * (1.0 + sc2) + sh2
    y2 = moe_ffn(h2.reshape(B * T, D), p['w_router'], p['b_router'], p['w_gu'], p['b_gu'],
                 p['w_down'], p['b_down']).reshape(B, T, D)
    x = x + gt2 * y2
    return x, C1, n1, m1, v_gm


def run_trunk(x, c, C0s, n0s, m0s, layer_w, g_final):
    Cs, ns, ms, vs = [], [], [], []
    for l in range(DEPTH):
        p = {name: arr[l] for name, arr in layer_w.items()}
        x, C1, n1, m1, v_rows = hybrid_layer(x, c, C0s[l], n0s[l], m0s[l], p)
        Cs.append(C1)
        ns.append(n1)
        ms.append(m1)
        vs.append(v_rows)
    y = rmsnorm(x, g_final)
    return y, jnp.stack(Cs), jnp.stack(ns), jnp.stack(ms), jnp.stack(vs)


def setup_inputs(seed: int = 0) -> dict:
    key = jax.random.key(seed)
    ks = jax.random.split(key, 32)
    nrm = lambda k, s, sc: jax.random.normal(k, s, dtype=jnp.float32) * sc
    D = D_MODEL
    b_i = nrm(ks[10], (DEPTH, ML_HEADS), 0.1)
    b_f = jnp.linspace(3.0, 6.0, ML_HEADS, dtype=jnp.float32)[None, :] + nrm(ks[11], (DEPTH, ML_HEADS), 0.1)
    return {
        'x_prompt': nrm(ks[0], (BATCH, SEQ, D), 1.0),
        'x_sample': nrm(ks[1], (DEC_BATCH, DEC_SEQ, D), 1.0),
        'state_C': nrm(ks[2], (DEPTH, DEC_BATCH, ML_HEADS, ML_HEAD_DIM, ML_HEAD_DIM), 0.1),
        'state_n': nrm(ks[3], (DEPTH, DEC_BATCH, ML_HEADS, ML_HEAD_DIM), 0.1),
        'state_m': jax.random.uniform(ks[4], (DEPTH, DEC_BATCH, ML_HEADS), dtype=jnp.float32, minval=0.0, maxval=3.0),
        'c_prompt': nrm(ks[5], (BATCH, D), 1.0),
        'c_sample': nrm(ks[6], (DEC_BATCH, D), 1.0),
        'w_ada': nrm(ks[7], (DEPTH, D, N_MOD * D), 0.5 * D ** -0.5),
        'b_ada': nrm(ks[8], (DEPTH, N_MOD * D), 0.02),
        'g_mix': 1.0 + nrm(ks[9], (DEPTH, D), 0.02),
        'w_in': nrm(ks[12], (DEPTH, D, N_IN), D ** -0.5),
        'b_if': jnp.concatenate([b_i, b_f], axis=-1),
        'g_head': 1.0 + nrm(ks[13], (DEPTH, ML_WIDTH), 0.02),
        'g_sgu': 1.0 + nrm(ks[14], (DEPTH, GM_WIDTH), 0.02),
        'b_sgu': nrm(ks[15], (DEPTH, GM_WIDTH), 0.02),
        'w_s': nrm(ks[16], (DEPTH, GM_GROUPS, GM_CHUNK, GM_CHUNK), GM_CHUNK ** -0.5),
        'b_s': 1.0 + nrm(ks[17], (DEPTH, GM_GROUPS, GM_CHUNK), 0.02),
        'w_pa': nrm(ks[18], (DEPTH, ML_WIDTH, D), ML_WIDTH ** -0.5),
        'w_pb': nrm(ks[19], (DEPTH, GM_WIDTH, D), GM_WIDTH ** -0.5),
        'w_out': nrm(ks[20], (DEPTH, D, D), D ** -0.5),
        'g_ffn': 1.0 + nrm(ks[21], (DEPTH, D), 0.02),
        'w_router': nrm(ks[22], (DEPTH, D, N_EXPERTS), D ** -0.5),
        'b_router': nrm(ks[23], (DEPTH, N_EXPERTS), 0.01),
        'w_gu': nrm(ks[24], (DEPTH, N_EXPERTS, D, 2 * D_FF), D ** -0.5),
        'b_gu': nrm(ks[25], (DEPTH, N_EXPERTS, 2 * D_FF), 0.02),
        'w_down': nrm(ks[26], (DEPTH, N_EXPERTS, D_FF, D), D_FF ** -0.5),
        'b_down': nrm(ks[27], (DEPTH, N_EXPERTS, D), 0.02),
        'g_final': 1.0 + nrm(ks[28], (D,), 0.02),
    }


def reference(x_prompt, x_sample, state_C, state_n, state_m, c_prompt, c_sample,
              w_ada, b_ada, g_mix, w_in, b_if, g_head, g_sgu, b_sgu, w_s, b_s,
              w_pa, w_pb, w_out, g_ffn, w_router, b_router, w_gu, b_gu, w_down, b_down, g_final):
    layer_w = {
        'w_ada': w_ada, 'b_ada': b_ada, 'g_mix': g_mix, 'w_in': w_in, 'b_if': b_if,
        'g_head': g_head, 'g_sgu': g_sgu, 'b_sgu': b_sgu, 'w_s': w_s, 'b_s': b_s,
        'w_pa': w_pa, 'w_pb': w_pb, 'w_out': w_out, 'g_ffn': g_ffn,
        'w_router': w_router, 'b_router': b_router, 'w_gu': w_gu, 'b_gu': b_gu,
        'w_down': w_down, 'b_down': b_down,
    }
    B = x_prompt.shape[0]
    C0p = jnp.zeros((DEPTH, B, ML_HEADS, ML_HEAD_DIM, ML_HEAD_DIM), dtype=jnp.float32)
    n0p = jnp.zeros((DEPTH, B, ML_HEADS, ML_HEAD_DIM), dtype=jnp.float32)
    m0p = jnp.zeros((DEPTH, B, ML_HEADS), dtype=jnp.float32)
    y_prompt, C_p, n_p, m_p, _ = run_trunk(x_prompt, c_prompt, C0p, n0p, m0p, layer_w, g_final)
    y_sample, C_s, n_s, m_s, v_s = run_trunk(x_sample, c_sample, state_C, state_n, state_m, layer_w, g_final)
    return (y_prompt, y_sample, C_p, n_p, m_p, C_s, n_s, m_s, v_s)
```

```python
import functools

import jax
import jax.numpy as jnp
from jax import lax
from jax.experimental import pallas as pl
from jax.experimental.pallas import tpu as pltpu

F32 = jnp.float32
BF16 = jnp.bfloat16

D = 1024
HEADS = 4
HD = D // HEADS
ML_CHUNK = 128
GROUPS = 4
GD = D // GROUPS
GM_CHUNK = 128
N_EXPERTS = 32
TOP_K = 4
D_FF = D
SWIGLU_LIMIT = 7.0
SWIGLU_ALPHA = 1.702
NORM_EPS = 1e-6
N_MOD = 6
LANES = 128
MOE_ROWS = 256
SAMPLE_SEQS = 4
VMEM_LIMIT = 48 * 1024 * 1024


def _dot(a, b):
    return jnp.dot(a, b, preferred_element_type=F32)


def _dot_nt(a, b):
    return lax.dot_general(a, b, (((1,), (1,)), ((), ())), preferred_element_type=F32)


def _dot_tn(a, b):
    return lax.dot_general(a, b, (((0,), (0,)), ((), ())), preferred_element_type=F32)


def _split_bf16(a):
    hi = a.astype(BF16)
    lo = (a - hi.astype(F32)).astype(BF16)
    return hi, lo


def _dot3(a, b):
    ah, al = _split_bf16(a)
    bh, bl = _split_bf16(b)
    return _dot(ah, bh) + (_dot(ah, bl) + _dot(al, bh))


def _log_sigmoid(x):
    return jnp.minimum(x, 0.0) - jnp.log1p(jnp.exp(-jnp.abs(x)))


def _rms(x):
    return x * lax.rsqrt(jnp.mean(x * x, axis=-1, keepdims=True) + NORM_EPS)


def _mod_spec(per_token, tm, tokens_per_seq, col):
    if per_token:
        return pl.BlockSpec((tm, D), lambda i, *_: (i, col))
    return pl.BlockSpec((None, 1, D), lambda i, *_: ((i * tm) // tokens_per_seq, 0, col))


def _ada_kernel(c_ref, w_ref, b_ref, o_ref):
    c = c_ref[...]
    s = (c * jax.nn.sigmoid(c)).astype(BF16)
    o_ref[...] = _dot(s, w_ref[...].astype(BF16)) + b_ref[...]


def _ada(c, w, b):
    m, n = c.shape[0], w.shape[1]
    tn = 512
    return pl.pallas_call(
        _ada_kernel,
        grid=(n // tn,),
        in_specs=[pl.BlockSpec((m, D), lambda j: (0, 0)),
                  pl.BlockSpec((D, tn), lambda j: (0, j)),
                  pl.BlockSpec((1, tn), lambda j: (0, j))],
        out_specs=pl.BlockSpec((m, tn), lambda j: (0, j)),
        out_shape=jax.ShapeDtypeStruct((m, n), F32),
        name="ada",
    )(c, w, b.reshape(1, n))


def _in_kernel(x_ref, g_ref, sh_ref, sc_ref, w_ref, wif_ref, bif_ref, z_ref, gc_ref, gt_ref, h_scr):
    @pl.when(pl.program_id(1) == 0)
    def _():
        h = (_rms(x_ref[...]) * g_ref[...]) * (1.0 + sc_ref[...]) + sh_ref[...]
        h_scr[...] = h.astype(BF16)
        gates = _dot3(h, wif_ref[...]) + bif_ref[...]
        gc_ref[...] = gates
        gt_ref[...] = gates.T[:2 * HEADS, :]

    z_ref[...] = _dot(h_scr[...], w_ref[...]).astype(z_ref.dtype)


def _in_proj(x, mod, per_token, tokens_per_seq, g_mix, w_main, w_if, b_if, tm, tn, z_dtype):
    m = x.shape[0]
    n = w_main.shape[1]
    return pl.pallas_call(
        _in_kernel,
        grid=(m // tm, n // tn),
        in_specs=[pl.BlockSpec((tm, D), lambda i, j: (i, 0)),
                  pl.BlockSpec((1, D), lambda i, j: (0, 0)),
                  _mod_spec(per_token, tm, tokens_per_seq, 0),
                  _mod_spec(per_token, tm, tokens_per_seq, 1),
                  pl.BlockSpec((D, tn), lambda i, j: (0, j)),
                  pl.BlockSpec((D, LANES), lambda i, j: (0, 0)),
                  pl.BlockSpec((1, LANES), lambda i, j: (0, 0))],
        out_specs=[pl.BlockSpec((tm, tn), lambda i, j: (i, j)),
                   pl.BlockSpec((tm, LANES), lambda i, j: (i, 0)),
                   pl.BlockSpec((2 * HEADS, tm), lambda i, j: (0, i))],
        out_shape=[jax.ShapeDtypeStruct((m, n), z_dtype),
                   jax.ShapeDtypeStruct((m, LANES), F32),
                   jax.ShapeDtypeStruct((2 * HEADS, m), F32)],
        scratch_shapes=[pltpu.VMEM((tm, D), BF16)],
        compiler_params=pltpu.CompilerParams(
            dimension_semantics=("parallel", "arbitrary"), vmem_limit_bytes=VMEM_LIMIT),
        name="in_proj",
    )(x, g_mix.reshape(1, D), mod, mod, w_main, w_if, b_if)


def _mlstm_prompt_kernel(q_ref, k_ref, v_ref, o_ref, gt_ref, gh_ref, hm_ref, C_ref, n_ref, m_ref):
    head = pl.program_id(1)
    L = q_ref.shape[0]

    @pl.when(pl.program_id(2) == 0)
    def _():
        C_ref[...] = jnp.zeros_like(C_ref)
        n_ref[...] = jnp.zeros_like(n_ref)
        m_ref[...] = jnp.zeros_like(m_ref)

    gates = gt_ref[...]
    rid = lax.broadcasted_iota(jnp.int32, gates.shape, 0)
    ig_row = jnp.sum(jnp.where(rid == head, gates, 0.0), axis=0, keepdims=True)
    fg_row = jnp.sum(jnp.where(rid == head + HEADS, gates, 0.0), axis=0, keepdims=True)
    lf_row = _log_sigmoid(fg_row)

    r = lax.broadcasted_iota(jnp.int32, (L, L), 0)
    s = lax.broadcasted_iota(jnp.int32, (L, L), 1)
    eye = r == s
    causal = s <= r

    def to_col(x_row):
        return jnp.sum(jnp.where(eye, x_row, 0.0), axis=1, keepdims=True)

    lf_col = to_col(lf_row)
    b_row = jnp.sum(jnp.where(r <= s, lf_col, 0.0), axis=0, keepdims=True)
    b_col = to_col(b_row)
    m_prev = m_ref[...][:, :1]

    logD = jnp.where(causal, b_col - b_row + ig_row, -jnp.inf)
    inter = b_col + m_prev
    mt = jnp.maximum(jnp.max(logD, axis=1, keepdims=True), inter)
    q = q_ref[...]
    ks = k_ref[...] * (HD ** -0.5)
    v = v_ref[...]
    S = _dot_nt(q, ks) * jnp.exp(logD - mt)
    w_int = jnp.exp(inter - mt)
    Cmat = C_ref[...]
    nvec = n_ref[...]
    num = _dot(S.astype(BF16), v) + w_int * _dot_nt(q, Cmat.astype(BF16))
    nq = jnp.sum(q.astype(F32) * nvec, axis=1, keepdims=True)
    den = jnp.sum(S, axis=1, keepdims=True) + w_int * nq
    hh = num / jnp.maximum(jnp.abs(den), jnp.exp(-mt))
    hg = jax.nn.sigmoid(o_ref[...].astype(F32)) * hh
    hm_ref[...] = (_rms(hg) * gh_ref[...]).astype(hm_ref.dtype)

    bL = b_row[:, L - 1:L]
    g_row = bL - b_row + ig_row
    m_new = jnp.maximum(bL + m_prev, jnp.max(g_row, axis=1, keepdims=True))
    w_old = jnp.exp(bL + m_prev - m_new)
    kw = ks.astype(F32) * to_col(jnp.exp(g_row - m_new))
    C_ref[...] = w_old * Cmat + _dot_tn(v, kw.astype(BF16))
    n_ref[...] = w_old * nvec + jnp.sum(kw, axis=0, keepdims=True)
    m_ref[...] = jnp.broadcast_to(m_new, m_ref.shape)


def _mlstm_prompt(z, gates_t, g_head, batch, seq):
    nc = seq // ML_CHUNK
    m = batch * seq

    def zspec(col0):
        return pl.BlockSpec((ML_CHUNK, HD), lambda b, h, c: (b * nc + c, col0 + h))

    return pl.pallas_call(
        _mlstm_prompt_kernel,
        grid=(batch, HEADS, nc),
        in_specs=[zspec(0), zspec(HEADS), zspec(2 * HEADS), zspec(3 * HEADS),
                  pl.BlockSpec((2 * HEADS, ML_CHUNK), lambda b, h, c: (0, b * nc + c)),
                  pl.BlockSpec((None, 1, HD), lambda b, h, c: (h, 0, 0))],
        out_specs=[pl.BlockSpec((ML_CHUNK, HD), lambda b, h, c: (b * nc + c, h)),
                   pl.BlockSpec((None, None, HD, HD), lambda b, h, c: (b, h, 0, 0)),
                   pl.BlockSpec((None, None, 1, HD), lambda b, h, c: (b, h, 0, 0)),
                   pl.BlockSpec((None, None, 1, LANES), lambda b, h, c: (b, h, 0, 0))],
        out_shape=[jax.ShapeDtypeStruct((m, D), BF16),
                   jax.ShapeDtypeStruct((batch, HEADS, HD, HD), F32),
                   jax.ShapeDtypeStruct((batch, HEADS, 1, HD), F32),
                   jax.ShapeDtypeStruct((batch, HEADS, 1, LANES), F32)],
        compiler_params=pltpu.CompilerParams(
            dimension_semantics=("parallel", "parallel", "arbitrary")),
        name="mlstm_prompt",
    )(z, z, z, z, gates_t, g_head.reshape(HEADS, 1, HD))


def _mlstm_sample_kernel(seq_len, q_ref, k_ref, v_ref, o_ref, gc_ref, m0_ref, C0_ref, n0_ref, gh_ref,
                         hm_ref, C_ref, n_ref, m_ref):
    R = q_ref.shape[0]
    nseq = R // seq_len
    r = lax.broadcasted_iota(jnp.int32, (R, R), 0)
    s = lax.broadcasted_iota(jnp.int32, (R, R), 1)
    rseq = lax.broadcasted_iota(jnp.int32, (R, 1), 0) // seq_len
    eye = r == s
    same = (r // seq_len) == (s // seq_len)
    causal = same & (s <= r)

    def to_row(x_col):
        return jnp.sum(jnp.where(eye, x_col, 0.0), axis=0, keepdims=True)

    gc = gc_ref[...]
    for h in range(HEADS):
        cols = slice(h * HD, (h + 1) * HD)
        ig_col = gc[:, h:h + 1]
        lf_col = _log_sigmoid(gc[:, HEADS + h:HEADS + h + 1])
        lf_row = to_row(lf_col)
        b_col = jnp.sum(jnp.where(causal, lf_row, 0.0), axis=1, keepdims=True)
        bL_col = jnp.sum(jnp.where(same, lf_row, 0.0), axis=1, keepdims=True)
        b_row = to_row(b_col)
        ig_row = to_row(ig_col)
        m0_col = m0_ref[:, h:h + 1]

        logD = jnp.where(causal, b_col - b_row + ig_row, -jnp.inf)
        inter = b_col + m0_col
        mt = jnp.maximum(jnp.max(logD, axis=1, keepdims=True), inter)
        qf = q_ref[:, cols]
        q = qf.astype(BF16)
        ksf = k_ref[:, cols] * (HD ** -0.5)
        v = v_ref[:, cols].astype(BF16)
        S = _dot_nt(q, ksf.astype(BF16)) * jnp.exp(logD - mt)
        w_int = jnp.exp(inter - mt)

        Cq = jnp.zeros((R, HD), F32)
        nq = jnp.zeros((R, 1), F32)
        for g in range(nseq):
            Cq = jnp.where(rseq == g, _dot_nt(q, C0_ref[g, h].astype(BF16)), Cq)
            nq = jnp.where(rseq == g, jnp.sum(qf * n0_ref[g, h], axis=1, keepdims=True), nq)
        num = _dot(S.astype(BF16), v) + w_int * Cq
        den = jnp.sum(S, axis=1, keepdims=True) + w_int * nq
        hh = num / jnp.maximum(jnp.abs(den), jnp.exp(-mt))
        hg = jax.nn.sigmoid(o_ref[:, cols]) * hh
        hm_ref[:, cols] = (_rms(hg) * gh_ref[h]).astype(hm_ref.dtype)

        g_col = bL_col - b_col + ig_col
        gmax_col = jnp.max(jnp.where(same, to_row(g_col), -jnp.inf), axis=1, keepdims=True)
        m_new_col = jnp.maximum(bL_col + m0_col, gmax_col)
        w_old_col = jnp.exp(bL_col + m0_col - m_new_col)
        kw = ksf * jnp.exp(g_col - m_new_col)
        for g in range(nseq):
            kw_g = jnp.where(rseq == g, kw, 0.0)
            w_old = w_old_col[g * seq_len:g * seq_len + 1, :]
            C_ref[g, h] = w_old * C0_ref[g, h] + _dot_tn(v, kw_g.astype(BF16))
            n_ref[g, h] = w_old * n0_ref[g, h] + jnp.sum(kw_g, axis=0, keepdims=True)
        m_ref[:, h * LANES:(h + 1) * LANES] = jnp.broadcast_to(m_new_col, (R, LANES))


def _mlstm_sample(z, gates_c, m0_tok, C0, n0, g_head, batch, seq):
    rows = SAMPLE_SEQS * seq
    m = batch * seq
    nblk = D // D

    def zspec(col):
        return pl.BlockSpec((rows, D), lambda i: (i, col))

    state_c = pl.BlockSpec((SAMPLE_SEQS, HEADS, HD, HD), lambda i: (i, 0, 0, 0))
    state_n = pl.BlockSpec((SAMPLE_SEQS, HEADS, 1, HD), lambda i: (i, 0, 0, 0))
    return pl.pallas_call(
        functools.partial(_mlstm_sample_kernel, seq),
        grid=(batch // SAMPLE_SEQS,),
        in_specs=[zspec(0), zspec(nblk), zspec(2 * nblk), zspec(3 * nblk),
                  pl.BlockSpec((rows, LANES), lambda i: (i, 0)),
                  pl.BlockSpec((rows, HEADS), lambda i: (i, 0)),
                  state_c, state_n,
                  pl.BlockSpec((HEADS, 1, HD), lambda i: (0, 0, 0))],
        out_specs=[pl.BlockSpec((rows, D), lambda i: (i, 0)),
                   state_c, state_n,
                   pl.BlockSpec((rows, HEADS * LANES), lambda i: (i, 0))],
        out_shape=[jax.ShapeDtypeStruct((m, D), BF16),
                   jax.ShapeDtypeStruct((batch, HEADS, HD, HD), F32),
                   jax.ShapeDtypeStruct((batch, HEADS, 1, HD), F32),
                   jax.ShapeDtypeStruct((m, HEADS * LANES), F32)],
        compiler_params=pltpu.CompilerParams(
            dimension_semantics=("parallel",), vmem_limit_bytes=VMEM_LIMIT),
        name="mlstm_sample",
    )(z, z, z, z, gates_c, m0_tok, C0, n0.reshape(batch, HEADS, 1, HD), g_head.reshape(HEADS, 1, HD))


def _mix_kernel(x_ref, hm_ref, u_ref, v_ref, ga_ref, gb_ref, gt1_ref, sh2_ref, sc2_ref,
                gsgu_ref, bsgu_ref, ws_ref, bs_ref, wpa_ref, wpb_ref, wout_ref, gffn_ref,
                wr_ref, br_ref, x1_ref, h2_ref, lg_ref, vg_ref, yg_scr):
    tm = x_ref.shape[0]
    u = jax.nn.gelu(u_ref[...].astype(F32))
    vv = jax.nn.gelu(v_ref[...].astype(F32))
    mu = jnp.mean(vv, axis=-1, keepdims=True)
    var = jnp.mean(jnp.square(vv - mu), axis=-1, keepdims=True)
    vg = (vv - mu) * lax.rsqrt(var + NORM_EPS) * gsgu_ref[...] + bsgu_ref[...]
    vg_ref[...] = vg
    vgb = vg.astype(BF16)

    r = lax.broadcasted_iota(jnp.int32, (GM_CHUNK, GM_CHUNK), 0)
    s = lax.broadcasted_iota(jnp.int32, (GM_CHUNK, GM_CHUNK), 1)
    for g in range(GROUPS):
        w = jnp.where(s <= r, ws_ref[g], 0.0).astype(BF16)
        bias = bs_ref[:, g:g + 1]
        for c in range(tm // GM_CHUNK):
            rows = slice(c * GM_CHUNK, (c + 1) * GM_CHUNK)
            cols = slice(g * GD, (g + 1) * GD)
            mixed = _dot(w, vgb[rows, cols]) + bias
            yg_scr[rows, cols] = (u[rows, cols] * mixed).astype(BF16)

    a = _dot(hm_ref[...], wpa_ref[...])
    b = _dot(yg_scr[...], wpb_ref[...])
    merged = (jax.nn.sigmoid(ga_ref[...].astype(F32)) * a
              + jax.nn.sigmoid(gb_ref[...].astype(F32)) * b)
    x1 = x_ref[...] + gt1_ref[...] * _dot(merged.astype(BF16), wout_ref[...])
    x1_ref[...] = x1
    h2 = (_rms(x1) * gffn_ref[...]) * (1.0 + sc2_ref[...]) + sh2_ref[...]
    h2_ref[...] = h2.astype(BF16)
    lg_ref[...] = _dot3(h2, wr_ref[...]) + br_ref[...]


def _mix(x, hm, z, mod, per_token, tokens_per_seq, tm, p):
    m = x.shape[0]
    zcol = (3 * D) // D + 1

    def zspec(blk):
        return pl.BlockSpec((tm, D), lambda i: (i, blk))

    def full(shape):
        return pl.BlockSpec(shape, lambda i: (0,) * len(shape))

    row = pl.BlockSpec((tm, D), lambda i: (i, 0))
    return pl.pallas_call(
        _mix_kernel,
        grid=(m // tm,),
        in_specs=[row, row, zspec(zcol), zspec(zcol + 1), zspec(zcol + 2), zspec(zcol + 3),
                  _mod_spec(per_token, tm, tokens_per_seq, 2),
                  _mod_spec(per_token, tm, tokens_per_seq, 3),
                  _mod_spec(per_token, tm, tokens_per_seq, 4),
                  full((1, D)), full((1, D)),
                  full((GROUPS, GM_CHUNK, GM_CHUNK)), full((GM_CHUNK, GROUPS)),
                  full((D, D)), full((D, D)), full((D, D)), full((1, D)),
                  full((D, LANES)), full((1, LANES))],
        out_specs=[row, row, pl.BlockSpec((tm, LANES), lambda i: (i, 0)), row],
        out_shape=[jax.ShapeDtypeStruct((m, D), F32),
                   jax.ShapeDtypeStruct((m, D), BF16),
                   jax.ShapeDtypeStruct((m, LANES), F32),
                   jax.ShapeDtypeStruct((m, D), F32)],
        scratch_shapes=[pltpu.VMEM((tm, D), BF16)],
        compiler_params=pltpu.CompilerParams(
            dimension_semantics=("parallel",), vmem_limit_bytes=VMEM_LIMIT),
        name="mix",
    )(x, hm, z, z, z, z, mod, mod, mod, p["g_sgu"], p["b_sgu"], p["w_s"], p["b_s"],
      p["w_pa"], p["w_pb"], p["w_out"], p["g_ffn"], p["w_router"], p["b_router"])


def _moe_kernel(be_ref, nused_ref, x_ref, wgu_ref, bgu_ref, wd_ref, bd_ref, o_ref):
    i = pl.program_id(0)

    @pl.when(i < nused_ref[0])
    def _():
        gu = _dot(x_ref[...], wgu_ref[...]) + bgu_ref[...]
        gate = jnp.minimum(gu[:, :D_FF], SWIGLU_LIMIT)
        up = jnp.clip(gu[:, D_FF:], -SWIGLU_LIMIT, SWIGLU_LIMIT)
        act = gate * jax.nn.sigmoid(SWIGLU_ALPHA * gate) * (up + 1.0)
        o_ref[...] = _dot(act.astype(BF16), wd_ref[...]) + bd_ref[...]

    @pl.when(i >= nused_ref[0])
    def _():
        o_ref[...] = jnp.zeros_like(o_ref)


def _moe(xb, block_e, n_used, w_gu, b_gu, w_down, b_down):
    rows = xb.shape[0]
    nb = rows // MOE_ROWS
    grid_spec = pltpu.PrefetchScalarGridSpec(
        num_scalar_prefetch=2,
        grid=(nb,),
        in_specs=[pl.BlockSpec((MOE_ROWS, D), lambda i, be, nu: (i, 0)),
                  pl.BlockSpec((None, D, 2 * D_FF), lambda i, be, nu: (be[i], 0, 0)),
                  pl.BlockSpec((None, 1, 2 * D_FF), lambda i, be, nu: (be[i], 0, 0)),
                  pl.BlockSpec((None, D_FF, D), lambda i, be, nu: (be[i], 0, 0)),
                  pl.BlockSpec((None, 1, D), lambda i, be, nu: (be[i], 0, 0))],
        out_specs=pl.BlockSpec((MOE_ROWS, D), lambda i, be, nu: (i, 0)),
    )
    return pl.pallas_call(
        _moe_kernel,
        grid_spec=grid_spec,
        out_shape=jax.ShapeDtypeStruct((rows, D), F32),
        compiler_params=pltpu.CompilerParams(
            dimension_semantics=("arbitrary",), vmem_limit_bytes=VMEM_LIMIT),
        name="moe",
    )(block_e, n_used, xb, w_gu, b_gu.reshape(N_EXPERTS, 1, 2 * D_FF), w_down,
      b_down.reshape(N_EXPERTS, 1, D))


def _final_kernel(x1_ref, y2_ref, gt2_ref, g_ref, o_ref):
    o_ref[...] = _rms(x1_ref[...] + gt2_ref[...] * y2_ref[...]) * g_ref[...]


def _final(x1, y2, row_off, mod, per_token, tokens_per_seq, tm, g_final):
    m = x1.shape[0]
    off = row_off // tm
    return pl.pallas_call(
        _final_kernel,
        grid=(m // tm,),
        in_specs=[pl.BlockSpec((tm, D), lambda i: (i, 0)),
                  pl.BlockSpec((tm, D), lambda i: (i + off, 0)),
                  _mod_spec(per_token, tm, tokens_per_seq, 5),
                  pl.BlockSpec((1, D), lambda i: (0, 0))],
        out_specs=pl.BlockSpec((tm, D), lambda i: (i, 0)),
        out_shape=jax.ShapeDtypeStruct((m, D), F32),
        compiler_params=pltpu.CompilerParams(dimension_semantics=("parallel",)),
        name="final",
    )(x1, y2, mod, g_final.reshape(1, D))


def _route(logits):
    n = logits.shape[0]
    nk = n * TOP_K
    top_val, top_idx = lax.top_k(logits, TOP_K)
    gates = jax.nn.softmax(top_val, axis=-1)
    flat_e = top_idx.reshape(nk).astype(jnp.int32)
    order = jnp.argsort(flat_e)
    se = flat_e[order]
    counts = jnp.bincount(flat_e, length=N_EXPERTS).astype(jnp.int32)
    padded = ((counts + MOE_ROWS - 1) // MOE_ROWS) * MOE_ROWS
    starts = jnp.cumsum(counts) - counts
    pends = jnp.cumsum(padded)
    pstarts = pends - padded
    dest_sorted = pstarts[se] + (jnp.arange(nk, dtype=jnp.int32) - starts[se])
    n_blocks = -(-(nk + N_EXPERTS * (MOE_ROWS - 1)) // MOE_ROWS)
    slots = n_blocks * MOE_ROWS
    slot_tok = jnp.full((slots,), n, dtype=jnp.int32).at[dest_sorted].set(order.astype(jnp.int32) // TOP_K)
    dest = jnp.zeros((nk,), jnp.int32).at[order].set(dest_sorted).reshape(n, TOP_K)
    block_start = jnp.arange(n_blocks, dtype=jnp.int32) * MOE_ROWS
    block_e = jnp.clip(jnp.searchsorted(pends, block_start, side="right"), 0, N_EXPERTS - 1).astype(jnp.int32)
    n_used = (pends[-1:] // MOE_ROWS).astype(jnp.int32)
    return gates, slot_tok, dest, block_e, n_used


def kernel(x_prompt, x_sample, state_C, state_n, state_m, c_prompt, c_sample, w_ada, b_ada, g_mix, w_in,
           b_if, g_head, g_sgu, b_sgu, w_s, b_s, w_pa, w_pb, w_out, g_ffn, w_router, b_router, w_gu, b_gu,
           w_down, b_down, g_final):
    depth = w_ada.shape[0]
    assert depth == 1
    bp, tp, _ = x_prompt.shape
    bs, ts, _ = x_sample.shape
    mp, ms = bp * tp, bs * ts
    assert tp % ML_CHUNK == 0 and ts <= ML_CHUNK and GM_CHUNK % ts == 0

    w_in0 = w_in[0]
    nqkvo = 4 * D
    w_main = jnp.concatenate([w_in0[:, :nqkvo], w_in0[:, nqkvo + 2 * HEADS:]], axis=1).astype(BF16)
    w_if = jnp.pad(w_in0[:, nqkvo:nqkvo + 2 * HEADS], ((0, 0), (0, LANES - 2 * HEADS)))
    b_if_p = jnp.pad(b_if[0], (0, LANES - 2 * HEADS)).reshape(1, LANES)
    tril_s = w_s[0]
    reps = GM_CHUNK // ts
    eye_r = jnp.eye(reps, dtype=F32)
    w_s_sample = jnp.einsum("ab,gts->gatbs", eye_r, w_s[0][:, :ts, :ts]).reshape(GROUPS, GM_CHUNK, GM_CHUNK)
    b_s_prompt = b_s[0].T
    b_s_sample = jnp.tile(b_s[0][:, :ts].T, (reps, 1))
    mix_p = {
        "g_sgu": g_sgu[0].reshape(1, D), "b_sgu": b_sgu[0].reshape(1, D),
        "w_pa": w_pa[0].astype(BF16), "w_pb": w_pb[0].astype(BF16), "w_out": w_out[0].astype(BF16),
        "g_ffn": g_ffn[0].reshape(1, D),
        "w_router": jnp.pad(w_router[0], ((0, 0), (0, LANES - N_EXPERTS))),
        "b_router": jnp.pad(b_router[0], (0, LANES - N_EXPERTS)).reshape(1, LANES),
    }
    mix_prompt = dict(mix_p, w_s=tril_s, b_s=b_s_prompt)
    mix_sample = dict(mix_p, w_s=w_s_sample, b_s=b_s_sample)

    mod = _ada(jnp.concatenate([c_prompt, c_sample], axis=0), w_ada[0], b_ada[0])
    mod_p = mod[:bp].reshape(bp, 1, N_MOD * D)
    mod_s = jnp.repeat(mod[bp:], ts, axis=0)

    xp = x_prompt.reshape(mp, D)
    xs = x_sample.reshape(ms, D)
    z_p, _, gt_p = _in_proj(xp, mod_p, False, tp, g_mix[0], w_main, w_if, b_if_p, 512, 2048, BF16)
    z_s, gc_s, _ = _in_proj(xs, mod_s, True, ts, g_mix[0], w_main, w_if, b_if_p, 256, 2048, F32)

    hm_p, C_p, n_p, m_p = _mlstm_prompt(z_p, gt_p, g_head[0], bp, tp)
    m0_tok = jnp.repeat(state_m[0], ts, axis=0)
    hm_s, C_s, n_s, m_s = _mlstm_sample(z_s, gc_s, m0_tok, state_C[0], state_n[0], g_head[0], bs, ts)

    x1_p, h2_p, lg_p, _ = _mix(xp, hm_p, z_p, mod_p, False, tp, 256, mix_prompt)
    x1_s, h2_s, lg_s, vg_s = _mix(xs, hm_s, z_s, mod_s, True, ts, 128, mix_sample)

    logits = jnp.concatenate([lg_p[:, :N_EXPERTS], lg_s[:, :N_EXPERTS]], axis=0)
    gates, slot_tok, dest, block_e, n_used = _route(logits)
    h2 = jnp.concatenate([h2_p, h2_s, jnp.zeros((1, D), BF16)], axis=0)
    xb = h2[slot_tok]
    yb = _moe(xb, block_e, n_used, w_gu[0].astype(BF16), b_gu[0], w_down[0].astype(BF16), b_down[0])
    y2 = jnp.sum(yb[dest] * gates[:, :, None], axis=1)

    y_p = _final(x1_p, y2, 0, mod_p, False, tp, 256, g_final)
    y_s = _final(x1_s, y2, mp, mod_s, True, ts, 128, g_final)

    return (y_p.reshape(bp, tp, D), y_s.reshape(bs, ts, D),
            C_p[None], n_p.reshape(1, bp, HEADS, HD), m_p[:, :, 0, 0][None],
            C_s[None], n_s.reshape(1, bs, HEADS, HD),
            m_s.reshape(bs, ts, HEADS, LANES)[:, 0, :, 0][None],
            vg_s.reshape(1, bs, ts, D))
```

```python
import functools

import jax
import jax.numpy as jnp
from jax import lax
from jax.experimental import pallas as pl
from jax.experimental.pallas import tpu as pltpu

F32 = jnp.float32
BF16 = jnp.bfloat16

D = 1024
HEADS = 4
HD = D // HEADS
ML_CHUNK = 128
GROUPS = 4
GD = D // GROUPS
GM_CHUNK = 128
N_EXPERTS = 32
TOP_K = 4
D_FF = D
SWIGLU_LIMIT = 7.0
SWIGLU_ALPHA = 1.702
NORM_EPS = 1e-6
N_MOD = 6
LANES = 128
SUBLANES = 8
MOE_ROWS = 256
SAMPLE_SEQS = 4
ISSUE_GROUP = 4
VMEM_LIMIT = 48 * 1024 * 1024
MOE_VMEM_LIMIT = 56 * 1024 * 1024


def _dot(a, b):
    return jnp.dot(a, b, preferred_element_type=F32)


def _dot_nt(a, b):
    return lax.dot_general(a, b, (((1,), (1,)), ((), ())), preferred_element_type=F32)


def _dot_tn(a, b):
    return lax.dot_general(a, b, (((0,), (0,)), ((), ())), preferred_element_type=F32)


def _split_bf16(a):
    hi = a.astype(BF16)
    lo = (a - hi.astype(F32)).astype(BF16)
    return hi, lo


def _dot3(a, b):
    ah, al = _split_bf16(a)
    bh, bl = _split_bf16(b)
    return _dot(ah, bh) + (_dot(ah, bl) + _dot(al, bh))


def _log_sigmoid(x):
    return jnp.minimum(x, 0.0) - jnp.log1p(jnp.exp(-jnp.abs(x)))


def _rms(x):
    return x * lax.rsqrt(jnp.mean(x * x, axis=-1, keepdims=True) + NORM_EPS)


def _store_token_tiles(ref, x):
    for c in range(D // LANES):
        ref[:, c, :] = x[:, c * LANES:(c + 1) * LANES]


def _load_token_tiles(ref):
    return jnp.concatenate([ref[:, c, :] for c in range(D // LANES)], axis=1)


def _mod_spec(per_token, tm, tokens_per_seq, col):
    if per_token:
        return pl.BlockSpec((tm, D), lambda i, *_: (i, col))
    return pl.BlockSpec((None, 1, D), lambda i, *_: ((i * tm) // tokens_per_seq, 0, col))


def _ada_kernel(c_ref, w_ref, b_ref, o_ref):
    c = c_ref[...]
    s = (c * jax.nn.sigmoid(c)).astype(BF16)
    o_ref[...] = _dot(s, w_ref[...].astype(BF16)) + b_ref[...]


def _ada(c, w, b):
    m, n = c.shape[0], w.shape[1]
    tn = 512
    return pl.pallas_call(
        _ada_kernel,
        grid=(n // tn,),
        in_specs=[pl.BlockSpec((m, D), lambda j: (0, 0)),
                  pl.BlockSpec((D, tn), lambda j: (0, j)),
                  pl.BlockSpec((1, tn), lambda j: (0, j))],
        out_specs=pl.BlockSpec((m, tn), lambda j: (0, j)),
        out_shape=jax.ShapeDtypeStruct((m, n), F32),
        name="ada",
    )(c, w, b.reshape(1, n))


def _in_kernel(x_ref, g_ref, sh_ref, sc_ref, w_ref, wif_ref, bif_ref, z_ref, gc_ref, gt_ref, h_scr):
    @pl.when(pl.program_id(1) == 0)
    def _():
        h = (_rms(x_ref[...]) * g_ref[...]) * (1.0 + sc_ref[...]) + sh_ref[...]
        h_scr[...] = h.astype(BF16)
        gates = _dot3(h, wif_ref[...]) + bif_ref[...]
        gc_ref[...] = gates
        gt_ref[...] = gates.T[:2 * HEADS, :]

    z_ref[...] = _dot(h_scr[...], w_ref[...]).astype(z_ref.dtype)


def _in_proj(x, mod, per_token, tokens_per_seq, g_mix, w_main, w_if, b_if, tm, tn, z_dtype):
    m = x.shape[0]
    n = w_main.shape[1]
    return pl.pallas_call(
        _in_kernel,
        grid=(m // tm, n // tn),
        in_specs=[pl.BlockSpec((tm, D), lambda i, j: (i, 0)),
                  pl.BlockSpec((1, D), lambda i, j: (0, 0)),
                  _mod_spec(per_token, tm, tokens_per_seq, 0),
                  _mod_spec(per_token, tm, tokens_per_seq, 1),
                  pl.BlockSpec((D, tn), lambda i, j: (0, j)),
                  pl.BlockSpec((D, LANES), lambda i, j: (0, 0)),
                  pl.BlockSpec((1, LANES), lambda i, j: (0, 0))],
        out_specs=[pl.BlockSpec((tm, tn), lambda i, j: (i, j)),
                   pl.BlockSpec((tm, LANES), lambda i, j: (i, 0)),
                   pl.BlockSpec((2 * HEADS, tm), lambda i, j: (0, i))],
        out_shape=[jax.ShapeDtypeStruct((m, n), z_dtype),
                   jax.ShapeDtypeStruct((m, LANES), F32),
                   jax.ShapeDtypeStruct((2 * HEADS, m), F32)],
        scratch_shapes=[pltpu.VMEM((tm, D), BF16)],
        compiler_params=pltpu.CompilerParams(
            dimension_semantics=("arbitrary", "arbitrary"), vmem_limit_bytes=VMEM_LIMIT),
        name="in_proj",
    )(x, g_mix.reshape(1, D), mod, mod, w_main, w_if, b_if)


def _mlstm_prompt_kernel(q_ref, k_ref, v_ref, o_ref, gt_ref, gh_ref, hm_ref, C_ref, n_ref, m_ref):
    head = pl.program_id(1)
    L = q_ref.shape[0]

    @pl.when(pl.program_id(2) == 0)
    def _():
        C_ref[...] = jnp.zeros_like(C_ref)
        n_ref[...] = jnp.zeros_like(n_ref)
        m_ref[...] = jnp.zeros_like(m_ref)

    gates = gt_ref[...]
    rid = lax.broadcasted_iota(jnp.int32, gates.shape, 0)
    ig_row = jnp.sum(jnp.where(rid == head, gates, 0.0), axis=0, keepdims=True)
    fg_row = jnp.sum(jnp.where(rid == head + HEADS, gates, 0.0), axis=0, keepdims=True)
    lf_row = _log_sigmoid(fg_row)

    r = lax.broadcasted_iota(jnp.int32, (L, L), 0)
    s = lax.broadcasted_iota(jnp.int32, (L, L), 1)
    eye = r == s
    causal = s <= r

    def to_col(x_row):
        return jnp.sum(jnp.where(eye, x_row, 0.0), axis=1, keepdims=True)

    lf_col = to_col(lf_row)
    b_row = jnp.sum(jnp.where(r <= s, lf_col, 0.0), axis=0, keepdims=True)
    b_col = to_col(b_row)
    m_prev = m_ref[...][:, :1]

    logD = jnp.where(causal, b_col - b_row + ig_row, -jnp.inf)
    inter = b_col + m_prev
    mt = jnp.maximum(jnp.max(logD, axis=1, keepdims=True), inter)
    q = q_ref[...]
    ks = k_ref[...] * (HD ** -0.5)
    v = v_ref[...]
    S = _dot_nt(q, ks) * jnp.exp(logD - mt)
    w_int = jnp.exp(inter - mt)
    Cmat = C_ref[...]
    nvec = n_ref[...]
    num = _dot(S.astype(BF16), v) + w_int * _dot_nt(q, Cmat.astype(BF16))
    nq = jnp.sum(q.astype(F32) * nvec, axis=1, keepdims=True)
    den = jnp.sum(S, axis=1, keepdims=True) + w_int * nq
    hh = num / jnp.maximum(jnp.abs(den), jnp.exp(-mt))
    hg = jax.nn.sigmoid(o_ref[...].astype(F32)) * hh
    hm_ref[...] = (_rms(hg) * gh_ref[...]).astype(hm_ref.dtype)

    bL = b_row[:, L - 1:L]
    g_row = bL - b_row + ig_row
    m_new = jnp.maximum(bL + m_prev, jnp.max(g_row, axis=1, keepdims=True))
    w_old = jnp.exp(bL + m_prev - m_new)
    kw = ks.astype(F32) * to_col(jnp.exp(g_row - m_new))
    C_ref[...] = w_old * Cmat + _dot_tn(v, kw.astype(BF16))
    n_ref[...] = w_old * nvec + jnp.sum(kw, axis=0, keepdims=True)
    m_ref[...] = jnp.broadcast_to(m_new, m_ref.shape)


def _mlstm_prompt(z, gates_t, g_head, batch, seq):
    nc = seq // ML_CHUNK
    m = batch * seq

    def zspec(col0):
        return pl.BlockSpec((ML_CHUNK, HD), lambda b, h, c: (b * nc + c, col0 + h))

    return pl.pallas_call(
        _mlstm_prompt_kernel,
        grid=(batch, HEADS, nc),
        in_specs=[zspec(0), zspec(HEADS), zspec(2 * HEADS), zspec(3 * HEADS),
                  pl.BlockSpec((2 * HEADS, ML_CHUNK), lambda b, h, c: (0, b * nc + c)),
                  pl.BlockSpec((None, 1, HD), lambda b, h, c: (h, 0, 0))],
        out_specs=[pl.BlockSpec((ML_CHUNK, HD), lambda b, h, c: (b * nc + c, h)),
                   pl.BlockSpec((None, None, HD, HD), lambda b, h, c: (b, h, 0, 0)),
                   pl.BlockSpec((None, None, 1, HD), lambda b, h, c: (b, h, 0, 0)),
                   pl.BlockSpec((None, None, 1, LANES), lambda b, h, c: (b, h, 0, 0))],
        out_shape=[jax.ShapeDtypeStruct((m, D), BF16),
                   jax.ShapeDtypeStruct((batch, HEADS, HD, HD), F32),
                   jax.ShapeDtypeStruct((batch, HEADS, 1, HD), F32),
                   jax.ShapeDtypeStruct((batch, HEADS, 1, LANES), F32)],
        compiler_params=pltpu.CompilerParams(
            dimension_semantics=("arbitrary", "arbitrary", "arbitrary")),
        name="mlstm_prompt",
    )(z, z, z, z, gates_t, g_head.reshape(HEADS, 1, HD))


def _mlstm_sample_kernel(seq_len, q_ref, k_ref, v_ref, o_ref, gc_ref, m0_ref, C0_ref, n0_ref, gh_ref,
                         hm_ref, C_ref, n_ref, m_ref):
    R = q_ref.shape[0]
    nseq = R // seq_len
    r = lax.broadcasted_iota(jnp.int32, (R, R), 0)
    s = lax.broadcasted_iota(jnp.int32, (R, R), 1)
    rseq = lax.broadcasted_iota(jnp.int32, (R, 1), 0) // seq_len
    eye = r == s
    same = (r // seq_len) == (s // seq_len)
    causal = same & (s <= r)

    def to_row(x_col):
        return jnp.sum(jnp.where(eye, x_col, 0.0), axis=0, keepdims=True)

    gc = gc_ref[...]
    for h in range(HEADS):
        cols = slice(h * HD, (h + 1) * HD)
        ig_col = gc[:, h:h + 1]
        lf_col = _log_sigmoid(gc[:, HEADS + h:HEADS + h + 1])
        lf_row = to_row(lf_col)
        b_col = jnp.sum(jnp.where(causal, lf_row, 0.0), axis=1, keepdims=True)
        bL_col = jnp.sum(jnp.where(same, lf_row, 0.0), axis=1, keepdims=True)
        b_row = to_row(b_col)
        ig_row = to_row(ig_col)
        m0_col = m0_ref[:, h:h + 1]

        logD = jnp.where(causal, b_col - b_row + ig_row, -jnp.inf)
        inter = b_col + m0_col
        mt = jnp.maximum(jnp.max(logD, axis=1, keepdims=True), inter)
        qf = q_ref[:, cols]
        q = qf.astype(BF16)
        ksf = k_ref[:, cols] * (HD ** -0.5)
        v = v_ref[:, cols].astype(BF16)
        S = _dot_nt(q, ksf.astype(BF16)) * jnp.exp(logD - mt)
        w_int = jnp.exp(inter - mt)

        Cq = jnp.zeros((R, HD), F32)
        nq = jnp.zeros((R, 1), F32)
        for g in range(nseq):
            Cq = jnp.where(rseq == g, _dot_nt(q, C0_ref[g, h].astype(BF16)), Cq)
            nq = jnp.where(rseq == g, jnp.sum(qf * n0_ref[g, h], axis=1, keepdims=True), nq)
        num = _dot(S.astype(BF16), v) + w_int * Cq
        den = jnp.sum(S, axis=1, keepdims=True) + w_int * nq
        hh = num / jnp.maximum(jnp.abs(den), jnp.exp(-mt))
        hg = jax.nn.sigmoid(o_ref[:, cols]) * hh
        hm_ref[:, cols] = (_rms(hg) * gh_ref[h]).astype(hm_ref.dtype)

        g_col = bL_col - b_col + ig_col
        gmax_col = jnp.max(jnp.where(same, to_row(g_col), -jnp.inf), axis=1, keepdims=True)
        m_new_col = jnp.maximum(bL_col + m0_col, gmax_col)
        w_old_col = jnp.exp(bL_col + m0_col - m_new_col)
        kw = ksf * jnp.exp(g_col - m_new_col)
        for g in range(nseq):
            kw_g = jnp.where(rseq == g, kw, 0.0)
            w_old = w_old_col[g * seq_len:g * seq_len + 1, :]
            C_ref[g, h] = w_old * C0_ref[g, h] + _dot_tn(v, kw_g.astype(BF16))
            n_ref[g, h] = w_old * n0_ref[g, h] + jnp.sum(kw_g, axis=0, keepdims=True)
        m_ref[:, h * LANES:(h + 1) * LANES] = jnp.broadcast_to(m_new_col, (R, LANES))


def _mlstm_sample(z, gates_c, m0_tok, C0, n0, g_head, batch, seq):
    rows = SAMPLE_SEQS * seq
    m = batch * seq
    nblk = D // D

    def zspec(col):
        return pl.BlockSpec((rows, D), lambda i: (i, col))

    state_c = pl.BlockSpec((SAMPLE_SEQS, HEADS, HD, HD), lambda i: (i, 0, 0, 0))
    state_n = pl.BlockSpec((SAMPLE_SEQS, HEADS, 1, HD), lambda i: (i, 0, 0, 0))
    return pl.pallas_call(
        functools.partial(_mlstm_sample_kernel, seq),
        grid=(batch // SAMPLE_SEQS,),
        in_specs=[zspec(0), zspec(nblk), zspec(2 * nblk), zspec(3 * nblk),
                  pl.BlockSpec((rows, LANES), lambda i: (i, 0)),
                  pl.BlockSpec((rows, HEADS), lambda i: (i, 0)),
                  state_c, state_n,
                  pl.BlockSpec((HEADS, 1, HD), lambda i: (0, 0, 0))],
        out_specs=[pl.BlockSpec((rows, D), lambda i: (i, 0)),
                   state_c, state_n,
                   pl.BlockSpec((rows, HEADS * LANES), lambda i: (i, 0))],
        out_shape=[jax.ShapeDtypeStruct((m, D), BF16),
                   jax.ShapeDtypeStruct((batch, HEADS, HD, HD), F32),
                   jax.ShapeDtypeStruct((batch, HEADS, 1, HD), F32),
                   jax.ShapeDtypeStruct((m, HEADS * LANES), F32)],
        compiler_params=pltpu.CompilerParams(
            dimension_semantics=("arbitrary",), vmem_limit_bytes=VMEM_LIMIT),
        name="mlstm_sample",
    )(z, z, z, z, gates_c, m0_tok, C0, n0.reshape(batch, HEADS, 1, HD), g_head.reshape(HEADS, 1, HD))


def _mix_kernel(x_ref, hm_ref, u_ref, v_ref, ga_ref, gb_ref, gt1_ref, sh2_ref, sc2_ref,
                gsgu_ref, bsgu_ref, ws_ref, bs_ref, wpa_ref, wpb_ref, wout_ref, gffn_ref,
                wr_ref, br_ref, x1_ref, h2_ref, lg_ref, vg_ref, yg_scr):
    tm = x_ref.shape[0]
    u = jax.nn.gelu(u_ref[...].astype(F32))
    vv = jax.nn.gelu(v_ref[...].astype(F32))
    mu = jnp.mean(vv, axis=-1, keepdims=True)
    var = jnp.mean(jnp.square(vv - mu), axis=-1, keepdims=True)
    vg = (vv - mu) * lax.rsqrt(var + NORM_EPS) * gsgu_ref[...] + bsgu_ref[...]
    vg_ref[...] = vg
    vgb = vg.astype(BF16)

    r = lax.broadcasted_iota(jnp.int32, (GM_CHUNK, GM_CHUNK), 0)
    s = lax.broadcasted_iota(jnp.int32, (GM_CHUNK, GM_CHUNK), 1)
    for g in range(GROUPS):
        w = jnp.where(s <= r, ws_ref[g], 0.0).astype(BF16)
        bias = bs_ref[:, g:g + 1]
        for c in range(tm // GM_CHUNK):
            rows = slice(c * GM_CHUNK, (c + 1) * GM_CHUNK)
            cols = slice(g * GD, (g + 1) * GD)
            mixed = _dot(w, vgb[rows, cols]) + bias
            yg_scr[rows, cols] = (u[rows, cols] * mixed).astype(BF16)

    a = _dot(hm_ref[...], wpa_ref[...])
    b = _dot(yg_scr[...], wpb_ref[...])
    merged = (jax.nn.sigmoid(ga_ref[...].astype(F32)) * a
              + jax.nn.sigmoid(gb_ref[...].astype(F32)) * b)
    x1 = x_ref[...] + gt1_ref[...] * _dot(merged.astype(BF16), wout_ref[...])
    x1_ref[...] = x1
    h2 = (_rms(x1) * gffn_ref[...]) * (1.0 + sc2_ref[...]) + sh2_ref[...]
    _store_token_tiles(h2_ref, h2)
    lg_ref[...] = _dot3(h2, wr_ref[...]) + br_ref[...]


def _mix(x, hm, z, mod, per_token, tokens_per_seq, tm, p):
    m = x.shape[0]
    zcol = (3 * D) // D + 1

    def zspec(blk):
        return pl.BlockSpec((tm, D), lambda i: (i, blk))

    def full(shape):
        return pl.BlockSpec(shape, lambda i: (0,) * len(shape))

    row = pl.BlockSpec((tm, D), lambda i: (i, 0))
    return pl.pallas_call(
        _mix_kernel,
        grid=(m // tm,),
        in_specs=[row, row, zspec(zcol), zspec(zcol + 1), zspec(zcol + 2), zspec(zcol + 3),
                  _mod_spec(per_token, tm, tokens_per_seq, 2),
                  _mod_spec(per_token, tm, tokens_per_seq, 3),
                  _mod_spec(per_token, tm, tokens_per_seq, 4),
                  full((1, D)), full((1, D)),
                  full((GROUPS, GM_CHUNK, GM_CHUNK)), full((GM_CHUNK, GROUPS)),
                  full((D, D)), full((D, D)), full((D, D)), full((1, D)),
                  full((D, LANES)), full((1, LANES))],
        out_specs=[row, pl.BlockSpec((tm, SUBLANES, LANES), lambda i: (i, 0, 0)),
                   pl.BlockSpec((tm, LANES), lambda i: (i, 0)), row],
        out_shape=[jax.ShapeDtypeStruct((m, D), F32),
                   jax.ShapeDtypeStruct((m, SUBLANES, LANES), F32),
                   jax.ShapeDtypeStruct((m, LANES), F32),
                   jax.ShapeDtypeStruct((m, D), F32)],
        scratch_shapes=[pltpu.VMEM((tm, D), BF16)],
        compiler_params=pltpu.CompilerParams(
            dimension_semantics=("arbitrary",), vmem_limit_bytes=VMEM_LIMIT),
        name="mix",
    )(x, hm, z, z, z, z, mod, mod, mod, p["g_sgu"], p["b_sgu"], p["w_s"], p["b_s"],
      p["w_pa"], p["w_pb"], p["w_out"], p["g_ffn"], p["w_router"], p["b_router"])


def _route_kernel(lg_ref, dest_ref, gate_ref, be_ref, pend_ref, cnt_scr, base_scr):
    pss = pl.program_id(0)
    i = pl.program_id(1)
    tm = lg_ref.shape[0]
    lane = lax.broadcasted_iota(jnp.int32, (tm, LANES), 1)
    lg = jnp.where(lane < N_EXPERTS, lg_ref[...], -jnp.inf)
    sels, vals = [], []
    for _ in range(TOP_K):
        mx = jnp.max(lg, axis=1, keepdims=True)
        idx = jnp.min(jnp.where(lg == mx, lane, LANES), axis=1, keepdims=True)
        sel = lane == idx
        sels.append(sel)
        vals.append(mx)
        lg = jnp.where(sel, -jnp.inf, lg)
    onehot = sum(s.astype(F32) for s in sels)
    tile_cnt = jnp.sum(onehot, axis=0, keepdims=True)

    @pl.when((pss == 0) & (i == 0))
    def _():
        cnt_scr[...] = jnp.zeros_like(cnt_scr)

    @pl.when(pss == 0)
    def _():
        cnt_scr[...] += tile_cnt

    @pl.when((pss == 1) & (i == 0))
    def _():
        cnt = cnt_scr[...]
        padded = jnp.floor((cnt + (MOE_ROWS - 1)) * (1.0 / MOE_ROWS)) * MOE_ROWS
        r = lax.broadcasted_iota(jnp.int32, (LANES, LANES), 0)
        s = lax.broadcasted_iota(jnp.int32, (LANES, LANES), 1)
        padded_col = jnp.sum(jnp.where(r == s, padded, 0.0), axis=1, keepdims=True)
        pstart = jnp.sum(jnp.where(r < s, padded_col, 0.0), axis=0, keepdims=True)
        pend = pstart + padded
        base_scr[...] = pstart
        pend_ref[...] = pend
        nbp = be_ref.shape[0]
        bstart = (lax.broadcasted_iota(jnp.int32, (nbp, LANES), 0) * MOE_ROWS).astype(F32)
        elane = lax.broadcasted_iota(jnp.int32, (nbp, LANES), 1)
        be = jnp.sum(jnp.where((elane < N_EXPERTS) & (pend <= bstart), 1.0, 0.0), axis=1, keepdims=True)
        be_ref[...] = jnp.broadcast_to(jnp.minimum(be, N_EXPERTS - 1.0), (nbp, LANES)).astype(jnp.int32)

    @pl.when(pss == 1)
    def _():
        r = lax.broadcasted_iota(jnp.int32, (tm, tm), 0)
        s = lax.broadcasted_iota(jnp.int32, (tm, tm), 1)
        before = _dot((s < r).astype(BF16), onehot.astype(BF16))
        slot = before + base_scr[...]
        dest = jnp.zeros((tm, LANES), F32)
        gates = jnp.zeros((tm, LANES), F32)
        ex = [jnp.exp(v - vals[0]) for v in vals]
        denom = sum(ex)
        for k in range(TOP_K):
            d_k = jnp.sum(jnp.where(sels[k], slot, 0.0), axis=1, keepdims=True)
            dest = jnp.where(lane == k, d_k, dest)
            gates = jnp.where(lane == k, ex[k] / denom, gates)
        dest_ref[...] = dest[:, :TOP_K].astype(jnp.int32)
        gate_ref[...] = gates[:, :TOP_K]
        base_scr[...] += tile_cnt


def _route(logits, n_blocks):
    n = logits.shape[0]
    tm = 256
    tok = lambda p, i: (i * p, 0)
    fixed = lambda p, i: (0, 0)
    return pl.pallas_call(
        _route_kernel,
        grid=(2, n // tm),
        in_specs=[pl.BlockSpec((tm, LANES), lambda p, i: (i, 0))],
        out_specs=[pl.BlockSpec((tm, TOP_K), tok), pl.BlockSpec((tm, TOP_K), tok),
                   pl.BlockSpec((n_blocks, LANES), fixed), pl.BlockSpec((1, LANES), fixed)],
        out_shape=[jax.ShapeDtypeStruct((n, TOP_K), jnp.int32),
                   jax.ShapeDtypeStruct((n, TOP_K), F32),
                   jax.ShapeDtypeStruct((n_blocks, LANES), jnp.int32),
                   jax.ShapeDtypeStruct((1, LANES), F32)],
        scratch_shapes=[pltpu.VMEM((1, LANES), F32), pltpu.VMEM((1, LANES), F32)],
        compiler_params=pltpu.CompilerParams(dimension_semantics=("arbitrary", "arbitrary")),
        name="route",
    )(logits)


def _dispatch_kernel(tm, prompt_tiles, pend_ref, dest_hbm, h2p_hbm, h2s_hbm, xs_hbm,
                     idx_smem, zero_vmem, isem, rsem, zsem):
    i = pl.program_id(0)
    nt = pl.num_programs(0)
    n_idx = tm * TOP_K

    def idx_copy(tile, slot):
        return pltpu.make_async_copy(dest_hbm.at[pl.ds(tile * n_idx, n_idx)],
                                     idx_smem.at[pl.ds(slot * n_idx, n_idx)], isem.at[slot])

    def row_copy(src_hbm, row, d):
        return pltpu.make_async_copy(src_hbm.at[row], xs_hbm.at[d], rsem)

    def drain_rows():
        def body(t, c):
            row_copy(h2p_hbm, 0, 0).wait()
            return c
        lax.fori_loop(0, n_idx, body, 0)

    @pl.when(i == 0)
    def _():
        idx_copy(0, 0).start()
        zero_vmem[...] = jnp.zeros_like(zero_vmem)

        def fill(e):
            lo = jnp.where(e > 0, pend_ref[jnp.maximum(e - 1, 0)], 0)
            hi = pend_ref[e]
            last = jnp.maximum(hi - MOE_ROWS, 0)
            return hi > lo, pltpu.make_async_copy(zero_vmem, xs_hbm.at[pl.ds(last, MOE_ROWS)], zsem)

        def start(e, c):
            used, cp = fill(e)

            @pl.when(used)
            def _():
                cp.start()
            return c

        def wait(e, c):
            used, cp = fill(e)

            @pl.when(used)
            def _():
                cp.wait()
            return c

        lax.fori_loop(0, N_EXPERTS, start, 0)
        lax.fori_loop(0, N_EXPERTS, wait, 0)

    @pl.when(i + 1 < nt)
    def _():
        idx_copy(i + 1, (i + 1) % 2).start()

    slot = i % 2
    idx_copy(i, slot).wait()

    def issue(src_hbm, row0):
        def body(g, c):
            base = slot * n_idx + g * (ISSUE_GROUP * TOP_K)
            dst = [idx_smem[base + j] for j in range(ISSUE_GROUP * TOP_K)]
            for j, d in enumerate(dst):
                row_copy(src_hbm, row0 + g * ISSUE_GROUP + j // TOP_K, d).start()
            return c
        lax.fori_loop(0, tm // ISSUE_GROUP, body, 0)

    @pl.when(i < prompt_tiles)
    def _():
        issue(h2p_hbm, i * tm)

    @pl.when(i >= prompt_tiles)
    def _():
        issue(h2s_hbm, (i - prompt_tiles) * tm)

    @pl.when(i > 0)
    def _():
        drain_rows()

    @pl.when(i == nt - 1)
    def _():
        drain_rows()


def _dispatch(pend, dest_flat, h2_p, h2_s, slots):
    tm = 256
    mp, ms = h2_p.shape[0], h2_s.shape[0]
    grid_spec = pltpu.PrefetchScalarGridSpec(
        num_scalar_prefetch=1,
        grid=((mp + ms) // tm,),
        in_specs=[pl.BlockSpec(memory_space=pl.ANY)] * 3,
        out_specs=pl.BlockSpec(memory_space=pl.ANY),
        scratch_shapes=[pltpu.SMEM((2 * tm * TOP_K,), jnp.int32),
                        pltpu.VMEM((MOE_ROWS, SUBLANES, LANES), F32),
                        pltpu.SemaphoreType.DMA((2,)),
                        pltpu.SemaphoreType.DMA,
                        pltpu.SemaphoreType.DMA],
    )
    return pl.pallas_call(
        functools.partial(_dispatch_kernel, tm, mp // tm),
        grid_spec=grid_spec,
        out_shape=jax.ShapeDtypeStruct((slots, SUBLANES, LANES), F32),
        compiler_params=pltpu.CompilerParams(dimension_semantics=("arbitrary",), has_side_effects=True),
        name="dispatch",
    )(pend, dest_flat, h2_p, h2_s)


def _moe_kernel(be_ref, nused_ref, x_ref, wgu_ref, bgu_ref, wd_ref, bd_ref, o_ref, wgu_bf, wd_bf):
    i = pl.program_id(0)

    @pl.when((i == 0) | (be_ref[i] != be_ref[jnp.maximum(i - 1, 0)]))
    def _():
        wgu_bf[...] = wgu_ref[...].astype(BF16)
        wd_bf[...] = wd_ref[...].astype(BF16)

    @pl.when(i < nused_ref[0])
    def _():
        gu = _dot(_load_token_tiles(x_ref).astype(BF16), wgu_bf[...]) + bgu_ref[...]
        gate = jnp.minimum(gu[:, :D_FF], SWIGLU_LIMIT)
        up = jnp.clip(gu[:, D_FF:], -SWIGLU_LIMIT, SWIGLU_LIMIT)
        act = gate * jax.nn.sigmoid(SWIGLU_ALPHA * gate) * (up + 1.0)
        _store_token_tiles(o_ref, _dot(act.astype(BF16), wd_bf[...]) + bd_ref[...])

    @pl.when(i >= nused_ref[0])
    def _():
        o_ref[...] = jnp.zeros_like(o_ref)


def _moe(xs, block_e, n_used, w_gu, b_gu, w_down, b_down):
    rows = xs.shape[0]
    nb = rows // MOE_ROWS
    grid_spec = pltpu.PrefetchScalarGridSpec(
        num_scalar_prefetch=2,
        grid=(nb,),
        in_specs=[pl.BlockSpec((MOE_ROWS, SUBLANES, LANES), lambda i, be, nu: (jnp.minimum(i, nu[0] - 1), 0, 0)),
                  pl.BlockSpec((None, D, 2 * D_FF), lambda i, be, nu: (be[i], 0, 0)),
                  pl.BlockSpec((None, 1, 2 * D_FF), lambda i, be, nu: (be[i], 0, 0)),
                  pl.BlockSpec((None, D_FF, D), lambda i, be, nu: (be[i], 0, 0)),
                  pl.BlockSpec((None, 1, D), lambda i, be, nu: (be[i], 0, 0))],
        out_specs=pl.BlockSpec((MOE_ROWS, SUBLANES, LANES), lambda i, be, nu: (i, 0, 0)),
        scratch_shapes=[pltpu.VMEM((D, 2 * D_FF), BF16), pltpu.VMEM((D_FF, D), BF16)],
    )
    return pl.pallas_call(
        _moe_kernel,
        grid_spec=grid_spec,
        out_shape=jax.ShapeDtypeStruct((rows, SUBLANES, LANES), F32),
        compiler_params=pltpu.CompilerParams(
            dimension_semantics=("arbitrary",), vmem_limit_bytes=MOE_VMEM_LIMIT),
        name="moe",
    )(block_e, n_used, xs, w_gu, b_gu.reshape(N_EXPERTS, 1, 2 * D_FF), w_down,
      b_down.reshape(N_EXPERTS, 1, D))


def _final_kernel(tm, tile_off, dest_hbm, x1_ref, gate_ref, gt2_ref, g_ref, yb_hbm, o_ref,
                  idx_smem, buf, isem, rsem):
    i = pl.program_id(0)
    nt = pl.num_programs(0)
    n_idx = tm * TOP_K

    def idx_copy(tile, slot):
        return pltpu.make_async_copy(dest_hbm.at[pl.ds((tile + tile_off) * n_idx, n_idx)],
                                     idx_smem.at[pl.ds(slot * n_idx, n_idx)], isem.at[slot])

    def row_copy(slot, k, t, d):
        return pltpu.make_async_copy(yb_hbm.at[d], buf.at[slot, k, t], rsem.at[slot])

    def fetch(tile, slot):
        idx_copy(tile, slot).start()
        idx_copy(tile, slot).wait()

        def body(g, c):
            base = slot * n_idx + g * (ISSUE_GROUP * TOP_K)
            src = [idx_smem[base + j] for j in range(ISSUE_GROUP * TOP_K)]
            for j, d in enumerate(src):
                row_copy(slot, j % TOP_K, g * ISSUE_GROUP + j // TOP_K, d).start()
            return c
        lax.fori_loop(0, tm // ISSUE_GROUP, body, 0)

    @pl.when(i == 0)
    def _():
        fetch(0, 0)

    @pl.when(i + 1 < nt)
    def _():
        fetch(i + 1, (i + 1) % 2)

    slot = i % 2

    def drain(t, c):
        row_copy(slot, 0, 0, 0).wait()
        return c
    lax.fori_loop(0, n_idx, drain, 0)

    y2 = gate_ref[:, 0:1] * _load_token_tiles(buf.at[slot, 0])
    for k in range(1, TOP_K):
        y2 = y2 + gate_ref[:, k:k + 1] * _load_token_tiles(buf.at[slot, k])
    o_ref[...] = _rms(x1_ref[...] + gt2_ref[...] * y2) * g_ref[...]


def _final(x1, yb, dest_flat, gates, row_off, mod, per_token, tokens_per_seq, g_final):
    tm = 256
    m = x1.shape[0]
    off = row_off // tm
    grid_spec = pltpu.PrefetchScalarGridSpec(
        num_scalar_prefetch=0,
        grid=(m // tm,),
        in_specs=[pl.BlockSpec(memory_space=pl.ANY),
                  pl.BlockSpec((tm, D), lambda i: (i, 0)),
                  pl.BlockSpec((tm, TOP_K), lambda i: (i + off, 0)),
                  _mod_spec(per_token, tm, tokens_per_seq, 5),
                  pl.BlockSpec((1, D), lambda i: (0, 0)),
                  pl.BlockSpec(memory_space=pl.ANY)],
        out_specs=pl.BlockSpec((tm, D), lambda i: (i, 0)),
        scratch_shapes=[pltpu.SMEM((2 * tm * TOP_K,), jnp.int32),
                        pltpu.VMEM((2, TOP_K, tm, SUBLANES, LANES), F32),
                        pltpu.SemaphoreType.DMA((2,)),
                        pltpu.SemaphoreType.DMA((2,))],
    )
    return pl.pallas_call(
        functools.partial(_final_kernel, tm, off),
        grid_spec=grid_spec,
        out_shape=jax.ShapeDtypeStruct((m, D), F32),
        compiler_params=pltpu.CompilerParams(
            dimension_semantics=("arbitrary",), vmem_limit_bytes=VMEM_LIMIT),
        name="final",
    )(dest_flat, x1, gates, mod, g_final.reshape(1, D), yb)


def kernel(x_prompt, x_sample, state_C, state_n, state_m, c_prompt, c_sample, w_ada, b_ada, g_mix, w_in,
           b_if, g_head, g_sgu, b_sgu, w_s, b_s, w_pa, w_pb, w_out, g_ffn, w_router, b_router, w_gu, b_gu,
           w_down, b_down, g_final):
    depth = w_ada.shape[0]
    assert depth == 1
    bp, tp, _ = x_prompt.shape
    bs, ts, _ = x_sample.shape
    mp, ms = bp * tp, bs * ts
    assert tp % ML_CHUNK == 0 and ts <= ML_CHUNK and GM_CHUNK % ts == 0

    w_in0 = w_in[0]
    nqkvo = 4 * D
    w_main = jnp.concatenate([w_in0[:, :nqkvo], w_in0[:, nqkvo + 2 * HEADS:]], axis=1).astype(BF16)
    w_if = jnp.pad(w_in0[:, nqkvo:nqkvo + 2 * HEADS], ((0, 0), (0, LANES - 2 * HEADS)))
    b_if_p = jnp.pad(b_if[0], (0, LANES - 2 * HEADS)).reshape(1, LANES)
    tril_s = w_s[0]
    reps = GM_CHUNK // ts
    eye_r = jnp.eye(reps, dtype=F32)
    w_s_sample = jnp.einsum("ab,gts->gatbs", eye_r, w_s[0][:, :ts, :ts]).reshape(GROUPS, GM_CHUNK, GM_CHUNK)
    b_s_prompt = b_s[0].T
    b_s_sample = jnp.tile(b_s[0][:, :ts].T, (reps, 1))
    mix_p = {
        "g_sgu": g_sgu[0].reshape(1, D), "b_sgu": b_sgu[0].reshape(1, D),
        "w_pa": w_pa[0].astype(BF16), "w_pb": w_pb[0].astype(BF16), "w_out": w_out[0].astype(BF16),
        "g_ffn": g_ffn[0].reshape(1, D),
        "w_router": jnp.pad(w_router[0], ((0, 0), (0, LANES - N_EXPERTS))),
        "b_router": jnp.pad(b_router[0], (0, LANES - N_EXPERTS)).reshape(1, LANES),
    }
    mix_prompt = dict(mix_p, w_s=tril_s, b_s=b_s_prompt)
    mix_sample = dict(mix_p, w_s=w_s_sample, b_s=b_s_sample)

    mod = _ada(jnp.concatenate([c_prompt, c_sample], axis=0), w_ada[0], b_ada[0])
    mod_p = mod[:bp].reshape(bp, 1, N_MOD * D)
    mod_s = jnp.repeat(mod[bp:], ts, axis=0)

    xp = x_prompt.reshape(mp, D)
    xs = x_sample.reshape(ms, D)
    z_p, _, gt_p = _in_proj(xp, mod_p, False, tp, g_mix[0], w_main, w_if, b_if_p, 512, 2048, BF16)
    z_s, gc_s, _ = _in_proj(xs, mod_s, True, ts, g_mix[0], w_main, w_if, b_if_p, 256, 2048, F32)

    hm_p, C_p, n_p, m_p = _mlstm_prompt(z_p, gt_p, g_head[0], bp, tp)
    m0_tok = jnp.repeat(state_m[0], ts, axis=0)
    hm_s, C_s, n_s, m_s = _mlstm_sample(z_s, gc_s, m0_tok, state_C[0], state_n[0], g_head[0], bs, ts)

    x1_p, h2_p, lg_p, _ = _mix(xp, hm_p, z_p, mod_p, False, tp, 256, mix_prompt)
    x1_s, h2_s, lg_s, vg_s = _mix(xs, hm_s, z_s, mod_s, True, ts, 128, mix_sample)

    n_tok = mp + ms
    n_blocks = -(-(n_tok * TOP_K + N_EXPERTS * (MOE_ROWS - 1)) // MOE_ROWS)
    dest, gates, be, pend = _route(jnp.concatenate([lg_p, lg_s], axis=0), n_blocks)
    dest_flat = dest.reshape(n_tok * TOP_K)
    pend_i = pend[0, :N_EXPERTS].astype(jnp.int32)
    n_used = pend_i[N_EXPERTS - 1:] // MOE_ROWS
    xslots = _dispatch(pend_i, dest_flat, h2_p, h2_s, n_blocks * MOE_ROWS)
    yb = _moe(xslots, be[:, 0], n_used, w_gu[0], b_gu[0], w_down[0], b_down[0])

    y_p = _final(x1_p, yb, dest_flat, gates, 0, mod_p, False, tp, g_final)
    y_s = _final(x1_s, yb, dest_flat, gates, mp, mod_s, True, ts, g_final)

    return (y_p.reshape(bp, tp, D), y_s.reshape(bs, ts, D),
            C_p[None], n_p.reshape(1, bp, HEADS, HD), m_p[:, :, 0, 0][None],
            C_s[None], n_s.reshape(1, bs, HEADS, HD),
            m_s.reshape(bs, ts, HEADS, LANES)[:, 0, :, 0][None],
            vg_s.reshape(1, bs, ts, D))
```

```python
import functools

import jax
import jax.numpy as jnp
from jax import lax
from jax.experimental import pallas as pl
from jax.experimental.pallas import tpu as pltpu
from jax.experimental.pallas import tpu_sc as plsc

F32 = jnp.float32
BF16 = jnp.bfloat16

D = 1024
HEADS = 4
HD = D // HEADS
ML_CHUNK = 128
GROUPS = 4
GD = D // GROUPS
GM_CHUNK = 128
N_EXPERTS = 32
TOP_K = 4
D_FF = D
SWIGLU_LIMIT = 7.0
SWIGLU_ALPHA = 1.702
NORM_EPS = 1e-6
N_MOD = 6
LANES = 128
SC_CORES = 2
SC_SUBCORES = 16
SC_WORKERS = SC_CORES * SC_SUBCORES
MOE_ROWS = 256
SAMPLE_SEQS = 4
SC_SCATTER_CHUNK = 48
SC_GATHER_CHUNK = 64
VMEM_LIMIT = 48 * 1024 * 1024
MOE_VMEM_LIMIT = 56 * 1024 * 1024


def _dot(a, b):
    return jnp.dot(a, b, preferred_element_type=F32)


def _dot_nt(a, b):
    return lax.dot_general(a, b, (((1,), (1,)), ((), ())), preferred_element_type=F32)


def _dot_tn(a, b):
    return lax.dot_general(a, b, (((0,), (0,)), ((), ())), preferred_element_type=F32)


def _split_bf16(a):
    hi = a.astype(BF16)
    lo = (a - hi.astype(F32)).astype(BF16)
    return hi, lo


def _dot3(a, b):
    ah, al = _split_bf16(a)
    bh, bl = _split_bf16(b)
    return _dot(ah, bh) + (_dot(ah, bl) + _dot(al, bh))


def _log_sigmoid(x):
    return jnp.minimum(x, 0.0) - jnp.log1p(jnp.exp(-jnp.abs(x)))


def _rms(x):
    return x * lax.rsqrt(jnp.mean(x * x, axis=-1, keepdims=True) + NORM_EPS)


def _mod_spec(per_token, tm, tokens_per_seq, col):
    if per_token:
        return pl.BlockSpec((tm, D), lambda i, *_: (i, col))
    return pl.BlockSpec((None, 1, D), lambda i, *_: ((i * tm) // tokens_per_seq, 0, col))


def _ada_kernel(c_ref, w_ref, b_ref, o_ref):
    c = c_ref[...]
    s = (c * jax.nn.sigmoid(c)).astype(BF16)
    o_ref[...] = _dot(s, w_ref[...].astype(BF16)) + b_ref[...]


def _ada(c, w, b):
    m, n = c.shape[0], w.shape[1]
    tn = 512
    return pl.pallas_call(
        _ada_kernel,
        grid=(n // tn,),
        in_specs=[pl.BlockSpec((m, D), lambda j: (0, 0)),
                  pl.BlockSpec((D, tn), lambda j: (0, j)),
                  pl.BlockSpec((1, tn), lambda j: (0, j))],
        out_specs=pl.BlockSpec((m, tn), lambda j: (0, j)),
        out_shape=jax.ShapeDtypeStruct((m, n), F32),
        name="ada",
    )(c, w, b.reshape(1, n))


def _in_kernel(x_ref, g_ref, sh_ref, sc_ref, w_ref, wif_ref, bif_ref, z_ref, gc_ref, gt_ref, h_scr):
    @pl.when(pl.program_id(1) == 0)
    def _():
        h = (_rms(x_ref[...]) * g_ref[...]) * (1.0 + sc_ref[...]) + sh_ref[...]
        h_scr[...] = h.astype(BF16)
        gates = _dot3(h, wif_ref[...]) + bif_ref[...]
        gc_ref[...] = gates
        gt_ref[...] = gates.T[:2 * HEADS, :]

    z_ref[...] = _dot(h_scr[...], w_ref[...]).astype(z_ref.dtype)


def _in_proj(x, mod, per_token, tokens_per_seq, g_mix, w_main, w_if, b_if, tm, tn, z_dtype):
    m = x.shape[0]
    n = w_main.shape[1]
    return pl.pallas_call(
        _in_kernel,
        grid=(m // tm, n // tn),
        in_specs=[pl.BlockSpec((tm, D), lambda i, j: (i, 0)),
                  pl.BlockSpec((1, D), lambda i, j: (0, 0)),
                  _mod_spec(per_token, tm, tokens_per_seq, 0),
                  _mod_spec(per_token, tm, tokens_per_seq, 1),
                  pl.BlockSpec((D, tn), lambda i, j: (0, j)),
                  pl.BlockSpec((D, LANES), lambda i, j: (0, 0)),
                  pl.BlockSpec((1, LANES), lambda i, j: (0, 0))],
        out_specs=[pl.BlockSpec((tm, tn), lambda i, j: (i, j)),
                   pl.BlockSpec((tm, LANES), lambda i, j: (i, 0)),
                   pl.BlockSpec((2 * HEADS, tm), lambda i, j: (0, i))],
        out_shape=[jax.ShapeDtypeStruct((m, n), z_dtype),
                   jax.ShapeDtypeStruct((m, LANES), F32),
                   jax.ShapeDtypeStruct((2 * HEADS, m), F32)],
        scratch_shapes=[pltpu.VMEM((tm, D), BF16)],
        compiler_params=pltpu.CompilerParams(
            dimension_semantics=("arbitrary", "arbitrary"), vmem_limit_bytes=VMEM_LIMIT),
        name="in_proj",
    )(x, g_mix.reshape(1, D), mod, mod, w_main, w_if, b_if)


def _mlstm_prompt_kernel(q_ref, k_ref, v_ref, o_ref, gt_ref, gh_ref, hm_ref, C_ref, n_ref, m_ref):
    head = pl.program_id(1)
    L = q_ref.shape[0]

    @pl.when(pl.program_id(2) == 0)
    def _():
        C_ref[...] = jnp.zeros_like(C_ref)
        n_ref[...] = jnp.zeros_like(n_ref)
        m_ref[...] = jnp.zeros_like(m_ref)

    gates = gt_ref[...]
    rid = lax.broadcasted_iota(jnp.int32, gates.shape, 0)
    ig_row = jnp.sum(jnp.where(rid == head, gates, 0.0), axis=0, keepdims=True)
    fg_row = jnp.sum(jnp.where(rid == head + HEADS, gates, 0.0), axis=0, keepdims=True)
    lf_row = _log_sigmoid(fg_row)

    r = lax.broadcasted_iota(jnp.int32, (L, L), 0)
    s = lax.broadcasted_iota(jnp.int32, (L, L), 1)
    eye = r == s
    causal = s <= r

    def to_col(x_row):
        return jnp.sum(jnp.where(eye, x_row, 0.0), axis=1, keepdims=True)

    lf_col = to_col(lf_row)
    b_row = jnp.sum(jnp.where(r <= s, lf_col, 0.0), axis=0, keepdims=True)
    b_col = to_col(b_row)
    m_prev = m_ref[...][:, :1]

    logD = jnp.where(causal, b_col - b_row + ig_row, -jnp.inf)
    inter = b_col + m_prev
    mt = jnp.maximum(jnp.max(logD, axis=1, keepdims=True), inter)
    q = q_ref[...]
    ks = k_ref[...] * (HD ** -0.5)
    v = v_ref[...]
    S = _dot_nt(q, ks) * jnp.exp(logD - mt)
    w_int = jnp.exp(inter - mt)
    Cmat = C_ref[...]
    nvec = n_ref[...]
    num = _dot(S.astype(BF16), v) + w_int * _dot_nt(q, Cmat.astype(BF16))
    nq = jnp.sum(q.astype(F32) * nvec, axis=1, keepdims=True)
    den = jnp.sum(S, axis=1, keepdims=True) + w_int * nq
    hh = num / jnp.maximum(jnp.abs(den), jnp.exp(-mt))
    hg = jax.nn.sigmoid(o_ref[...].astype(F32)) * hh
    hm_ref[...] = (_rms(hg) * gh_ref[...]).astype(hm_ref.dtype)

    bL = b_row[:, L - 1:L]
    g_row = bL - b_row + ig_row
    m_new = jnp.maximum(bL + m_prev, jnp.max(g_row, axis=1, keepdims=True))
    w_old = jnp.exp(bL + m_prev - m_new)
    kw = ks.astype(F32) * to_col(jnp.exp(g_row - m_new))
    C_ref[...] = w_old * Cmat + _dot_tn(v, kw.astype(BF16))
    n_ref[...] = w_old * nvec + jnp.sum(kw, axis=0, keepdims=True)
    m_ref[...] = jnp.broadcast_to(m_new, m_ref.shape)


def _mlstm_prompt(z, gates_t, g_head, batch, seq):
    nc = seq // ML_CHUNK
    m = batch * seq

    def zspec(col0):
        return pl.BlockSpec((ML_CHUNK, HD), lambda b, h, c: (b * nc + c, col0 + h))

    return pl.pallas_call(
        _mlstm_prompt_kernel,
        grid=(batch, HEADS, nc),
        in_specs=[zspec(0), zspec(HEADS), zspec(2 * HEADS), zspec(3 * HEADS),
                  pl.BlockSpec((2 * HEADS, ML_CHUNK), lambda b, h, c: (0, b * nc + c)),
                  pl.BlockSpec((None, 1, HD), lambda b, h, c: (h, 0, 0))],
        out_specs=[pl.BlockSpec((ML_CHUNK, HD), lambda b, h, c: (b * nc + c, h)),
                   pl.BlockSpec((None, None, HD, HD), lambda b, h, c: (b, h, 0, 0)),
                   pl.BlockSpec((None, None, 1, HD), lambda b, h, c: (b, h, 0, 0)),
                   pl.BlockSpec((None, None, 1, LANES), lambda b, h, c: (b, h, 0, 0))],
        out_shape=[jax.ShapeDtypeStruct((m, D), BF16),
                   jax.ShapeDtypeStruct((batch, HEADS, HD, HD), F32),
                   jax.ShapeDtypeStruct((batch, HEADS, 1, HD), F32),
                   jax.ShapeDtypeStruct((batch, HEADS, 1, LANES), F32)],
        compiler_params=pltpu.CompilerParams(
            dimension_semantics=("arbitrary", "arbitrary", "arbitrary")),
        name="mlstm_prompt",
    )(z, z, z, z, gates_t, g_head.reshape(HEADS, 1, HD))


def _mlstm_sample_kernel(seq_len, q_ref, k_ref, v_ref, o_ref, gc_ref, m0_ref, C0_ref, n0_ref, gh_ref,
                         hm_ref, C_ref, n_ref, m_ref):
    R = q_ref.shape[0]
    nseq = R // seq_len
    r = lax.broadcasted_iota(jnp.int32, (R, R), 0)
    s = lax.broadcasted_iota(jnp.int32, (R, R), 1)
    rseq = lax.broadcasted_iota(jnp.int32, (R, 1), 0) // seq_len
    eye = r == s
    same = (r // seq_len) == (s // seq_len)
    causal = same & (s <= r)

    def to_row(x_col):
        return jnp.sum(jnp.where(eye, x_col, 0.0), axis=0, keepdims=True)

    gc = gc_ref[...]
    for h in range(HEADS):
        cols = slice(h * HD, (h + 1) * HD)
        ig_col = gc[:, h:h + 1]
        lf_col = _log_sigmoid(gc[:, HEADS + h:HEADS + h + 1])
        lf_row = to_row(lf_col)
        b_col = jnp.sum(jnp.where(causal, lf_row, 0.0), axis=1, keepdims=True)
        bL_col = jnp.sum(jnp.where(same, lf_row, 0.0), axis=1, keepdims=True)
        b_row = to_row(b_col)
        ig_row = to_row(ig_col)
        m0_col = m0_ref[:, h:h + 1]

        logD = jnp.where(causal, b_col - b_row + ig_row, -jnp.inf)
        inter = b_col + m0_col
        mt = jnp.maximum(jnp.max(logD, axis=1, keepdims=True), inter)
        qf = q_ref[:, cols]
        q = qf.astype(BF16)
        ksf = k_ref[:, cols] * (HD ** -0.5)
        v = v_ref[:, cols].astype(BF16)
        S = _dot_nt(q, ksf.astype(BF16)) * jnp.exp(logD - mt)
        w_int = jnp.exp(inter - mt)

        Cq = jnp.zeros((R, HD), F32)
        nq = jnp.zeros((R, 1), F32)
        for g in range(nseq):
            Cq = jnp.where(rseq == g, _dot_nt(q, C0_ref[g, h].astype(BF16)), Cq)
            nq = jnp.where(rseq == g, jnp.sum(qf * n0_ref[g, h], axis=1, keepdims=True), nq)
        num = _dot(S.astype(BF16), v) + w_int * Cq
        den = jnp.sum(S, axis=1, keepdims=True) + w_int * nq
        hh = num / jnp.maximum(jnp.abs(den), jnp.exp(-mt))
        hg = jax.nn.sigmoid(o_ref[:, cols]) * hh
        hm_ref[:, cols] = (_rms(hg) * gh_ref[h]).astype(hm_ref.dtype)

        g_col = bL_col - b_col + ig_col
        gmax_col = jnp.max(jnp.where(same, to_row(g_col), -jnp.inf), axis=1, keepdims=True)
        m_new_col = jnp.maximum(bL_col + m0_col, gmax_col)
        w_old_col = jnp.exp(bL_col + m0_col - m_new_col)
        kw = ksf * jnp.exp(g_col - m_new_col)
        for g in range(nseq):
            kw_g = jnp.where(rseq == g, kw, 0.0)
            w_old = w_old_col[g * seq_len:g * seq_len + 1, :]
            C_ref[g, h] = w_old * C0_ref[g, h] + _dot_tn(v, kw_g.astype(BF16))
            n_ref[g, h] = w_old * n0_ref[g, h] + jnp.sum(kw_g, axis=0, keepdims=True)
        m_ref[:, h * LANES:(h + 1) * LANES] = jnp.broadcast_to(m_new_col, (R, LANES))


def _mlstm_sample(z, gates_c, m0_tok, C0, n0, g_head, batch, seq):
    rows = SAMPLE_SEQS * seq
    m = batch * seq

    def zspec(col):
        return pl.BlockSpec((rows, D), lambda i: (i, col))

    state_c = pl.BlockSpec((SAMPLE_SEQS, HEADS, HD, HD), lambda i: (i, 0, 0, 0))
    state_n = pl.BlockSpec((SAMPLE_SEQS, HEADS, 1, HD), lambda i: (i, 0, 0, 0))
    return pl.pallas_call(
        functools.partial(_mlstm_sample_kernel, seq),
        grid=(batch // SAMPLE_SEQS,),
        in_specs=[zspec(0), zspec(1), zspec(2), zspec(3),
                  pl.BlockSpec((rows, LANES), lambda i: (i, 0)),
                  pl.BlockSpec((rows, HEADS), lambda i: (i, 0)),
                  state_c, state_n,
                  pl.BlockSpec((HEADS, 1, HD), lambda i: (0, 0, 0))],
        out_specs=[pl.BlockSpec((rows, D), lambda i: (i, 0)),
                   state_c, state_n,
                   pl.BlockSpec((rows, HEADS * LANES), lambda i: (i, 0))],
        out_shape=[jax.ShapeDtypeStruct((m, D), BF16),
                   jax.ShapeDtypeStruct((batch, HEADS, HD, HD), F32),
                   jax.ShapeDtypeStruct((batch, HEADS, 1, HD), F32),
                   jax.ShapeDtypeStruct((m, HEADS * LANES), F32)],
        compiler_params=pltpu.CompilerParams(
            dimension_semantics=("arbitrary",), vmem_limit_bytes=VMEM_LIMIT),
        name="mlstm_sample",
    )(z, z, z, z, gates_c, m0_tok, C0, n0.reshape(batch, HEADS, 1, HD), g_head.reshape(HEADS, 1, HD))


def _mix_kernel(x_ref, hm_ref, u_ref, v_ref, ga_ref, gb_ref, gt1_ref, sh2_ref, sc2_ref,
                gsgu_ref, bsgu_ref, ws_ref, bs_ref, wpa_ref, wpb_ref, wout_ref, gffn_ref,
                wr_ref, br_ref, x1_ref, h2_ref, lg_ref, vg_ref, yg_scr):
    tm = x_ref.shape[0]
    u = jax.nn.gelu(u_ref[...].astype(F32))
    vv = jax.nn.gelu(v_ref[...].astype(F32))
    mu = jnp.mean(vv, axis=-1, keepdims=True)
    var = jnp.mean(jnp.square(vv - mu), axis=-1, keepdims=True)
    vg = (vv - mu) * lax.rsqrt(var + NORM_EPS) * gsgu_ref[...] + bsgu_ref[...]
    vg_ref[...] = vg
    vgb = vg.astype(BF16)

    r = lax.broadcasted_iota(jnp.int32, (GM_CHUNK, GM_CHUNK), 0)
    s = lax.broadcasted_iota(jnp.int32, (GM_CHUNK, GM_CHUNK), 1)
    for g in range(GROUPS):
        w = jnp.where(s <= r, ws_ref[g], 0.0).astype(BF16)
        bias = bs_ref[:, g:g + 1]
        for c in range(tm // GM_CHUNK):
            rows = slice(c * GM_CHUNK, (c + 1) * GM_CHUNK)
            cols = slice(g * GD, (g + 1) * GD)
            mixed = _dot(w, vgb[rows, cols]) + bias
            yg_scr[rows, cols] = (u[rows, cols] * mixed).astype(BF16)

    a = _dot(hm_ref[...], wpa_ref[...])
    b = _dot(yg_scr[...], wpb_ref[...])
    merged = (jax.nn.sigmoid(ga_ref[...].astype(F32)) * a
              + jax.nn.sigmoid(gb_ref[...].astype(F32)) * b)
    x1 = x_ref[...] + gt1_ref[...] * _dot(merged.astype(BF16), wout_ref[...])
    x1_ref[...] = x1
    h2 = (_rms(x1) * gffn_ref[...]) * (1.0 + sc2_ref[...]) + sh2_ref[...]
    h2_ref[...] = h2
    lg_ref[...] = _dot3(h2, wr_ref[...]) + br_ref[...]


def _mix(x, hm, z, mod, per_token, tokens_per_seq, tm, p):
    m = x.shape[0]
    zcol = 4

    def zspec(blk):
        return pl.BlockSpec((tm, D), lambda i: (i, blk))

    def full(shape):
        return pl.BlockSpec(shape, lambda i: (0,) * len(shape))

    row = pl.BlockSpec((tm, D), lambda i: (i, 0))
    return pl.pallas_call(
        _mix_kernel,
        grid=(m // tm,),
        in_specs=[row, row, zspec(zcol), zspec(zcol + 1), zspec(zcol + 2), zspec(zcol + 3),
                  _mod_spec(per_token, tm, tokens_per_seq, 2),
                  _mod_spec(per_token, tm, tokens_per_seq, 3),
                  _mod_spec(per_token, tm, tokens_per_seq, 4),
                  full((1, D)), full((1, D)),
                  full((GROUPS, GM_CHUNK, GM_CHUNK)), full((GM_CHUNK, GROUPS)),
                  full((D, D)), full((D, D)), full((D, D)), full((1, D)),
                  full((D, LANES)), full((1, LANES))],
        out_specs=[row, row, pl.BlockSpec((tm, LANES), lambda i: (i, 0)), row],
        out_shape=[jax.ShapeDtypeStruct((m, D), F32),
                   jax.ShapeDtypeStruct((m, D), F32),
                   jax.ShapeDtypeStruct((m, LANES), F32),
                   jax.ShapeDtypeStruct((m, D), F32)],
        scratch_shapes=[pltpu.VMEM((tm, D), BF16)],
        compiler_params=pltpu.CompilerParams(
            dimension_semantics=("arbitrary",), vmem_limit_bytes=VMEM_LIMIT),
        name="mix",
    )(x, hm, z, z, z, z, mod, mod, mod, p["g_sgu"], p["b_sgu"], p["w_s"], p["b_s"],
      p["w_pa"], p["w_pb"], p["w_out"], p["g_ffn"], p["w_router"], p["b_router"])


def _route_kernel(lg_ref, dest_ref, gate_ref, be_ref, pend_ref, cnt_scr, base_scr):
    pss = pl.program_id(0)
    i = pl.program_id(1)
    tm = lg_ref.shape[0]
    lane = lax.broadcasted_iota(jnp.int32, (tm, LANES), 1)
    lg = jnp.where(lane < N_EXPERTS, lg_ref[...], -jnp.inf)
    sels, vals = [], []
    for _ in range(TOP_K):
        mx = jnp.max(lg, axis=1, keepdims=True)
        idx = jnp.min(jnp.where(lg == mx, lane, LANES), axis=1, keepdims=True)
        sel = lane == idx
        sels.append(sel)
        vals.append(mx)
        lg = jnp.where(sel, -jnp.inf, lg)
    onehot = sum(s.astype(F32) for s in sels)
    tile_cnt = jnp.sum(onehot, axis=0, keepdims=True)

    @pl.when((pss == 0) & (i == 0))
    def _():
        cnt_scr[...] = jnp.zeros_like(cnt_scr)

    @pl.when(pss == 0)
    def _():
        cnt_scr[...] += tile_cnt

    @pl.when((pss == 1) & (i == 0))
    def _():
        cnt = cnt_scr[...]
        padded = jnp.floor((cnt + (MOE_ROWS - 1)) * (1.0 / MOE_ROWS)) * MOE_ROWS
        r = lax.broadcasted_iota(jnp.int32, (LANES, LANES), 0)
        s = lax.broadcasted_iota(jnp.int32, (LANES, LANES), 1)
        padded_col = jnp.sum(jnp.where(r == s, padded, 0.0), axis=1, keepdims=True)
        pstart = jnp.sum(jnp.where(r < s, padded_col, 0.0), axis=0, keepdims=True)
        pend = pstart + padded
        base_scr[...] = pstart
        pend_ref[...] = pend
        nbp = be_ref.shape[0]
        bstart = (lax.broadcasted_iota(jnp.int32, (nbp, LANES), 0) * MOE_ROWS).astype(F32)
        elane = lax.broadcasted_iota(jnp.int32, (nbp, LANES), 1)
        be = jnp.sum(jnp.where((elane < N_EXPERTS) & (pend <= bstart), 1.0, 0.0), axis=1, keepdims=True)
        be = jnp.minimum(be, N_EXPERTS - 1.0)
        owns = (elane < N_EXPERTS) & (pstart <= bstart) & (bstart < pend)
        last = jnp.sum(jnp.where(owns, pstart + cnt, 0.0), axis=1, keepdims=True)
        valid = jnp.clip(last - bstart[:, :1], 0.0, MOE_ROWS)
        be_ref[...] = jnp.where(elane == 0, be, jnp.where(elane == 1, valid, 0.0)).astype(jnp.int32)

    @pl.when(pss == 1)
    def _():
        r = lax.broadcasted_iota(jnp.int32, (tm, tm), 0)
        s = lax.broadcasted_iota(jnp.int32, (tm, tm), 1)
        before = _dot((s < r).astype(BF16), onehot.astype(BF16))
        slot = before + base_scr[...]
        dest = jnp.zeros((tm, LANES), F32)
        gates = jnp.zeros((tm, LANES), F32)
        ex = [jnp.exp(v - vals[0]) for v in vals]
        denom = sum(ex)
        for k in range(TOP_K):
            d_k = jnp.sum(jnp.where(sels[k], slot, 0.0), axis=1, keepdims=True)
            dest = jnp.where(lane == k, d_k, dest)
            gates = jnp.where(lane == k, ex[k] / denom, gates)
        dest_ref[...] = dest[:, :TOP_K].astype(jnp.int32)
        gate_ref[...] = gates[:, :TOP_K]
        base_scr[...] += tile_cnt


def _route(logits, n_blocks):
    n = logits.shape[0]
    tm = 256
    tok = lambda p, i: (i * p, 0)
    fixed = lambda p, i: (0, 0)
    return pl.pallas_call(
        _route_kernel,
        grid=(2, n // tm),
        in_specs=[pl.BlockSpec((tm, LANES), lambda p, i: (i, 0))],
        out_specs=[pl.BlockSpec((tm, TOP_K), tok), pl.BlockSpec((tm, TOP_K), tok),
                   pl.BlockSpec((n_blocks, LANES), fixed), pl.BlockSpec((1, LANES), fixed)],
        out_shape=[jax.ShapeDtypeStruct((n, TOP_K), jnp.int32),
                   jax.ShapeDtypeStruct((n, TOP_K), F32),
                   jax.ShapeDtypeStruct((n_blocks, LANES), jnp.int32),
                   jax.ShapeDtypeStruct((1, LANES), F32)],
        scratch_shapes=[pltpu.VMEM((1, LANES), F32), pltpu.VMEM((1, LANES), F32)],
        compiler_params=pltpu.CompilerParams(dimension_semantics=("arbitrary", "arbitrary")),
        name="route",
    )(logits)


def _sc_worker_base(per_worker):
    return (lax.axis_index("s") * SC_CORES + lax.axis_index("c")) * per_worker


def _sc_scatter_rows(rows, idx_flat, n_out, chunk):
    n = rows.shape[0]
    per_worker = n // SC_WORKERS
    assert per_worker * SC_WORKERS == n and per_worker % chunk == 0 and chunk % 8 == 0
    mesh = plsc.VectorSubcoreMesh(core_axis_name="c", subcore_axis_name="s")

    @functools.partial(
        pl.kernel, mesh=mesh,
        out_type=jax.ShapeDtypeStruct((n_out, D), F32),
        scratch_types=[pltpu.VMEM((chunk,), jnp.int32), pltpu.VMEM((chunk, D), F32), pltpu.SemaphoreType.DMA],
    )
    def scatter(rows_hbm, idx_hbm, out_hbm, idx_v, rows_v, sem):
        base = _sc_worker_base(per_worker)

        @pl.loop(0, per_worker // chunk)
        def _(j):
            off = pl.multiple_of(base + j * chunk, 8)
            pltpu.sync_copy(rows_hbm.at[pl.ds(off, chunk)], rows_v)
            for k in range(TOP_K):
                pltpu.sync_copy(idx_hbm.at[pl.ds(pl.multiple_of(k * n + off, 8), chunk)], idx_v)
                pltpu.async_copy(rows_v, out_hbm.at[idx_v], sem).wait()

    return scatter(rows, idx_flat)


def _sc_gather_rows(table, idx_flat, chunk):
    b = idx_flat.shape[0]
    per_worker = b // SC_WORKERS
    assert per_worker * SC_WORKERS == b and per_worker % chunk == 0 and chunk % 8 == 0
    mesh = plsc.VectorSubcoreMesh(core_axis_name="c", subcore_axis_name="s")

    @functools.partial(
        pl.kernel, mesh=mesh,
        out_type=jax.ShapeDtypeStruct((b, D), F32),
        scratch_types=[pltpu.VMEM((chunk,), jnp.int32), pltpu.VMEM((chunk, D), F32), pltpu.SemaphoreType.DMA],
    )
    def gather(table_hbm, idx_hbm, out_hbm, idx_v, rows_v, sem):
        base = _sc_worker_base(per_worker)

        @pl.loop(0, per_worker // chunk)
        def _(j):
            off = pl.multiple_of(base + j * chunk, 8)
            pltpu.sync_copy(idx_hbm.at[pl.ds(off, chunk)], idx_v)
            pltpu.async_copy(table_hbm.at[idx_v], rows_v, sem).wait()
            pltpu.sync_copy(rows_v, out_hbm.at[pl.ds(off, chunk)])

    return gather(table, idx_flat)


def _moe_kernel(be_ref, valid_ref, nused_ref, x_ref, wgu_ref, bgu_ref, wd_ref, bd_ref, o_ref, wgu_bf, wd_bf):
    i = pl.program_id(0)

    @pl.when((i == 0) | (be_ref[i] != be_ref[jnp.maximum(i - 1, 0)]))
    def _():
        wgu_bf[...] = wgu_ref[...].astype(BF16)
        wd_bf[...] = wd_ref[...].astype(BF16)

    @pl.when(i < nused_ref[0])
    def _():
        row = lax.broadcasted_iota(jnp.int32, (MOE_ROWS, 1), 0)
        x = jnp.where(row < valid_ref[i], x_ref[...], 0.0).astype(BF16)
        gu = _dot(x, wgu_bf[...]) + bgu_ref[...]
        gate = jnp.minimum(gu[:, :D_FF], SWIGLU_LIMIT)
        up = jnp.clip(gu[:, D_FF:], -SWIGLU_LIMIT, SWIGLU_LIMIT)
        act = gate * jax.nn.sigmoid(SWIGLU_ALPHA * gate) * (up + 1.0)
        o_ref[...] = _dot(act.astype(BF16), wd_bf[...]) + bd_ref[...]

    @pl.when(i >= nused_ref[0])
    def _():
        o_ref[...] = jnp.zeros_like(o_ref)


def _moe(xs, block_e, block_valid, n_used, w_gu, b_gu, w_down, b_down):
    rows = xs.shape[0]
    nb = rows // MOE_ROWS
    grid_spec = pltpu.PrefetchScalarGridSpec(
        num_scalar_prefetch=3,
        grid=(nb,),
        in_specs=[pl.BlockSpec((MOE_ROWS, D), lambda i, be, va, nu: (jnp.minimum(i, nu[0] - 1), 0)),
                  pl.BlockSpec((None, D, 2 * D_FF), lambda i, be, va, nu: (be[i], 0, 0)),
                  pl.BlockSpec((None, 1, 2 * D_FF), lambda i, be, va, nu: (be[i], 0, 0)),
                  pl.BlockSpec((None, D_FF, D), lambda i, be, va, nu: (be[i], 0, 0)),
                  pl.BlockSpec((None, 1, D), lambda i, be, va, nu: (be[i], 0, 0))],
        out_specs=pl.BlockSpec((MOE_ROWS, D), lambda i, be, va, nu: (i, 0)),
        scratch_shapes=[pltpu.VMEM((D, 2 * D_FF), BF16), pltpu.VMEM((D_FF, D), BF16)],
    )
    return pl.pallas_call(
        _moe_kernel,
        grid_spec=grid_spec,
        out_shape=jax.ShapeDtypeStruct((rows, D), F32),
        compiler_params=pltpu.CompilerParams(
            dimension_semantics=("arbitrary",), vmem_limit_bytes=MOE_VMEM_LIMIT),
        name="moe",
    )(block_e, block_valid, n_used, xs, w_gu, b_gu.reshape(N_EXPERTS, 1, 2 * D_FF), w_down,
      b_down.reshape(N_EXPERTS, 1, D))


def _final_kernel(x1_ref, yk_ref, gate_ref, gt2_ref, g_ref, o_ref):
    y2 = gate_ref[:, 0:1] * yk_ref[0]
    for k in range(1, TOP_K):
        y2 = y2 + gate_ref[:, k:k + 1] * yk_ref[k]
    o_ref[...] = _rms(x1_ref[...] + gt2_ref[...] * y2) * g_ref[...]


def _final(x1, yk, gates, row_off, mod, per_token, tokens_per_seq, g_final):
    tm = 256
    m = x1.shape[0]
    off = row_off // tm
    return pl.pallas_call(
        _final_kernel,
        grid=(m // tm,),
        in_specs=[pl.BlockSpec((tm, D), lambda i: (i, 0)),
                  pl.BlockSpec((TOP_K, tm, D), lambda i: (0, i + off, 0)),
                  pl.BlockSpec((tm, TOP_K), lambda i: (i + off, 0)),
                  _mod_spec(per_token, tm, tokens_per_seq, 5),
                  pl.BlockSpec((1, D), lambda i: (0, 0))],
        out_specs=pl.BlockSpec((tm, D), lambda i: (i, 0)),
        out_shape=jax.ShapeDtypeStruct((m, D), F32),
        compiler_params=pltpu.CompilerParams(
            dimension_semantics=("arbitrary",), vmem_limit_bytes=VMEM_LIMIT),
        name="final",
    )(x1, yk, gates, mod, g_final.reshape(1, D))


def kernel(x_prompt, x_sample, state_C, state_n, state_m, c_prompt, c_sample, w_ada, b_ada, g_mix, w_in,
           b_if, g_head, g_sgu, b_sgu, w_s, b_s, w_pa, w_pb, w_out, g_ffn, w_router, b_router, w_gu, b_gu,
           w_down, b_down, g_final):
    depth = w_ada.shape[0]
    assert depth == 1
    bp, tp, _ = x_prompt.shape
    bs, ts, _ = x_sample.shape
    mp, ms = bp * tp, bs * ts
    assert tp % ML_CHUNK == 0 and ts <= ML_CHUNK and GM_CHUNK % ts == 0

    w_in0 = w_in[0]
    nqkvo = 4 * D
    w_main = jnp.concatenate([w_in0[:, :nqkvo], w_in0[:, nqkvo + 2 * HEADS:]], axis=1).astype(BF16)
    w_if = jnp.pad(w_in0[:, nqkvo:nqkvo + 2 * HEADS], ((0, 0), (0, LANES - 2 * HEADS)))
    b_if_p = jnp.pad(b_if[0], (0, LANES - 2 * HEADS)).reshape(1, LANES)
    reps = GM_CHUNK // ts
    eye_r = jnp.eye(reps, dtype=F32)
    w_s_sample = jnp.einsum("ab,gts->gatbs", eye_r, w_s[0][:, :ts, :ts]).reshape(GROUPS, GM_CHUNK, GM_CHUNK)
    b_s_prompt = b_s[0].T
    b_s_sample = jnp.tile(b_s[0][:, :ts].T, (reps, 1))
    mix_p = {
        "g_sgu": g_sgu[0].reshape(1, D), "b_sgu": b_sgu[0].reshape(1, D),
        "w_pa": w_pa[0].astype(BF16), "w_pb": w_pb[0].astype(BF16), "w_out": w_out[0].astype(BF16),
        "g_ffn": g_ffn[0].reshape(1, D),
        "w_router": jnp.pad(w_router[0], ((0, 0), (0, LANES - N_EXPERTS))),
        "b_router": jnp.pad(b_router[0], (0, LANES - N_EXPERTS)).reshape(1, LANES),
    }
    mix_prompt = dict(mix_p, w_s=w_s[0], b_s=b_s_prompt)
    mix_sample = dict(mix_p, w_s=w_s_sample, b_s=b_s_sample)

    mod = _ada(jnp.concatenate([c_prompt, c_sample], axis=0), w_ada[0], b_ada[0])
    mod_p = mod[:bp].reshape(bp, 1, N_MOD * D)
    mod_s = jnp.repeat(mod[bp:], ts, axis=0)

    xp = x_prompt.reshape(mp, D)
    xs = x_sample.reshape(ms, D)
    z_p, _, gt_p = _in_proj(xp, mod_p, False, tp, g_mix[0], w_main, w_if, b_if_p, 512, 2048, BF16)
    z_s, gc_s, _ = _in_proj(xs, mod_s, True, ts, g_mix[0], w_main, w_if, b_if_p, 256, 2048, F32)

    hm_p, C_p, n_p, m_p = _mlstm_prompt(z_p, gt_p, g_head[0], bp, tp)
    m0_tok = jnp.repeat(state_m[0], ts, axis=0)
    hm_s, C_s, n_s, m_s = _mlstm_sample(z_s, gc_s, m0_tok, state_C[0], state_n[0], g_head[0], bs, ts)

    x1_p, h2_p, lg_p, _ = _mix(xp, hm_p, z_p, mod_p, False, tp, 256, mix_prompt)
    x1_s, h2_s, lg_s, vg_s = _mix(xs, hm_s, z_s, mod_s, True, ts, 128, mix_sample)

    n_tok = mp + ms
    n_blocks = -(-(n_tok * TOP_K + N_EXPERTS * (MOE_ROWS - 1)) // MOE_ROWS)
    dest, gates, table, pend = _route(jnp.concatenate([lg_p, lg_s], axis=0), n_blocks)
    dest_kmajor = dest.T.reshape(TOP_K * n_tok)
    n_used = pend[0, N_EXPERTS - 1:N_EXPERTS].astype(jnp.int32) // MOE_ROWS
    h2 = jnp.concatenate([h2_p, h2_s], axis=0)
    xslots = _sc_scatter_rows(h2, dest_kmajor, n_blocks * MOE_ROWS, SC_SCATTER_CHUNK)
    yb = _moe(xslots, table[:, 0], table[:, 1], n_used, w_gu[0], b_gu[0], w_down[0], b_down[0])
    yk = _sc_gather_rows(yb, dest_kmajor, SC_GATHER_CHUNK).reshape(TOP_K, n_tok, D)

    y_p = _final(x1_p, yk, gates, 0, mod_p, False, tp, g_final)
    y_s = _final(x1_s, yk, gates, mp, mod_s, True, ts, g_final)

    return (y_p.reshape(bp, tp, D), y_s.reshape(bs, ts, D),
            C_p[None], n_p.reshape(1, bp, HEADS, HD), m_p[:, :, 0, 0][None],
            C_s[None], n_s.reshape(1, bs, HEADS, HD),
            m_s.reshape(bs, ts, HEADS, LANES)[:, 0, :, 0][None],
            vg_s.reshape(1, bs, ts, D))
```

```python
import functools

import jax
import jax.numpy as jnp
from jax import lax
from jax.experimental import pallas as pl
from jax.experimental.pallas import tpu as pltpu
from jax.experimental.pallas import tpu_sc as plsc

F32 = jnp.float32
BF16 = jnp.bfloat16

D = 1024
HEADS = 4
HD = D // HEADS
ML_CHUNK = 128
GROUPS = 4
GD = D // GROUPS
GM_CHUNK = 128
N_EXPERTS = 32
TOP_K = 4
D_FF = D
SWIGLU_LIMIT = 7.0
SWIGLU_ALPHA = 1.702
NORM_EPS = 1e-6
N_MOD = 6
LANES = 128
SC_CORES = 2
SC_SUBCORES = 16
SC_WORKERS = SC_CORES * SC_SUBCORES
MOE_ROWS = 256
SAMPLE_SEQS = 4
SC_SCATTER_CHUNK = 32
SC_GATHER_CHUNK = 32
VMEM_LIMIT = 48 * 1024 * 1024
MOE_VMEM_LIMIT = 56 * 1024 * 1024


def _dot(a, b):
    return jnp.dot(a, b, preferred_element_type=F32)


def _dot_nt(a, b):
    return lax.dot_general(a, b, (((1,), (1,)), ((), ())), preferred_element_type=F32)


def _dot_tn(a, b):
    return lax.dot_general(a, b, (((0,), (0,)), ((), ())), preferred_element_type=F32)


def _split_bf16(a):
    hi = a.astype(BF16)
    lo = (a - hi.astype(F32)).astype(BF16)
    return hi, lo


def _dot3(a, b):
    ah, al = _split_bf16(a)
    bh, bl = _split_bf16(b)
    return _dot(ah, bh) + (_dot(ah, bl) + _dot(al, bh))


def _log_sigmoid(x):
    return jnp.minimum(x, 0.0) - jnp.log1p(jnp.exp(-jnp.abs(x)))


def _rms(x):
    return x * lax.rsqrt(jnp.mean(x * x, axis=-1, keepdims=True) + NORM_EPS)


def _mod_spec(per_token, tm, tokens_per_seq, col):
    if per_token:
        return pl.BlockSpec((tm, D), lambda i, *_: (i, col))
    return pl.BlockSpec((None, 1, D), lambda i, *_: ((i * tm) // tokens_per_seq, 0, col))


def _ada_kernel(c_ref, w_ref, b_ref, o_ref):
    c = c_ref[...]
    s = (c * jax.nn.sigmoid(c)).astype(BF16)
    o_ref[...] = _dot(s, w_ref[...].astype(BF16)) + b_ref[...]


def _ada(c, w, b):
    m, n = c.shape[0], w.shape[1]
    tn = 512
    return pl.pallas_call(
        _ada_kernel,
        grid=(n // tn,),
        in_specs=[pl.BlockSpec((m, D), lambda j: (0, 0)),
                  pl.BlockSpec((D, tn), lambda j: (0, j)),
                  pl.BlockSpec((1, tn), lambda j: (0, j))],
        out_specs=pl.BlockSpec((m, tn), lambda j: (0, j)),
        out_shape=jax.ShapeDtypeStruct((m, n), F32),
        name="ada",
    )(c, w, b.reshape(1, n))


def _in_kernel(x_ref, g_ref, sh_ref, sc_ref, w_ref, wif_ref, bif_ref, z_ref, gc_ref, gt_ref, h_scr):
    @pl.when(pl.program_id(1) == 0)
    def _():
        h = (_rms(x_ref[...]) * g_ref[...]) * (1.0 + sc_ref[...]) + sh_ref[...]
        h_scr[...] = h.astype(BF16)
        gates = _dot3(h, wif_ref[...]) + bif_ref[...]
        gc_ref[...] = gates
        gt_ref[...] = gates.T[:2 * HEADS, :]

    z_ref[...] = _dot(h_scr[...], w_ref[...]).astype(z_ref.dtype)


def _in_proj(x, mod, per_token, tokens_per_seq, g_mix, w_main, w_if, b_if, tm, tn, z_dtype):
    m = x.shape[0]
    n = w_main.shape[1]
    return pl.pallas_call(
        _in_kernel,
        grid=(m // tm, n // tn),
        in_specs=[pl.BlockSpec((tm, D), lambda i, j: (i, 0)),
                  pl.BlockSpec((1, D), lambda i, j: (0, 0)),
                  _mod_spec(per_token, tm, tokens_per_seq, 0),
                  _mod_spec(per_token, tm, tokens_per_seq, 1),
                  pl.BlockSpec((D, tn), lambda i, j: (0, j)),
                  pl.BlockSpec((D, LANES), lambda i, j: (0, 0)),
                  pl.BlockSpec((1, LANES), lambda i, j: (0, 0))],
        out_specs=[pl.BlockSpec((tm, tn), lambda i, j: (i, j)),
                   pl.BlockSpec((tm, LANES), lambda i, j: (i, 0)),
                   pl.BlockSpec((2 * HEADS, tm), lambda i, j: (0, i))],
        out_shape=[jax.ShapeDtypeStruct((m, n), z_dtype),
                   jax.ShapeDtypeStruct((m, LANES), F32),
                   jax.ShapeDtypeStruct((2 * HEADS, m), F32)],
        scratch_shapes=[pltpu.VMEM((tm, D), BF16)],
        compiler_params=pltpu.CompilerParams(
            dimension_semantics=("arbitrary", "arbitrary"), vmem_limit_bytes=VMEM_LIMIT),
        name="in_proj",
    )(x, g_mix.reshape(1, D), mod, mod, w_main, w_if, b_if)


def _mlstm_prompt_kernel(q_ref, k_ref, v_ref, o_ref, gt_ref, gh_ref, hm_ref, C_ref, n_ref, m_ref):
    L = q_ref.shape[0]

    @pl.when(pl.program_id(1) == 0)
    def _():
        C_ref[...] = jnp.zeros_like(C_ref)
        n_ref[...] = jnp.zeros_like(n_ref)
        m_ref[...] = jnp.zeros_like(m_ref)

    gates = gt_ref[...]
    r = lax.broadcasted_iota(jnp.int32, (L, L), 0)
    s = lax.broadcasted_iota(jnp.int32, (L, L), 1)
    eye = r == s
    causal = s <= r

    def to_col(x_row):
        return jnp.sum(jnp.where(eye, x_row, 0.0), axis=1, keepdims=True)

    for h in range(HEADS):
        cols = slice(h * HD, (h + 1) * HD)
        ig_row = gates[h:h + 1, :]
        lf_row = _log_sigmoid(gates[HEADS + h:HEADS + h + 1, :])
        lf_col = to_col(lf_row)
        b_row = jnp.sum(jnp.where(r <= s, lf_col, 0.0), axis=0, keepdims=True)
        b_col = to_col(b_row)
        m_prev = m_ref[h][:, :1]

        logD = jnp.where(causal, b_col - b_row + ig_row, -jnp.inf)
        inter = b_col + m_prev
        mt = jnp.maximum(jnp.max(logD, axis=1, keepdims=True), inter)
        q = q_ref[:, cols]
        ks = k_ref[:, cols] * (HD ** -0.5)
        v = v_ref[:, cols]
        S = _dot_nt(q, ks) * jnp.exp(logD - mt)
        w_int = jnp.exp(inter - mt)
        Cmat = C_ref[h]
        nvec = n_ref[h]
        num = _dot(S.astype(BF16), v) + w_int * _dot_nt(q, Cmat.astype(BF16))
        nq = jnp.sum(q.astype(F32) * nvec, axis=1, keepdims=True)
        den = jnp.sum(S, axis=1, keepdims=True) + w_int * nq
        hh = num / jnp.maximum(jnp.abs(den), jnp.exp(-mt))
        hg = jax.nn.sigmoid(o_ref[:, cols].astype(F32)) * hh
        hm_ref[:, cols] = (_rms(hg) * gh_ref[h]).astype(hm_ref.dtype)

        bL = b_row[:, L - 1:L]
        g_row = bL - b_row + ig_row
        m_new = jnp.maximum(bL + m_prev, jnp.max(g_row, axis=1, keepdims=True))
        w_old = jnp.exp(bL + m_prev - m_new)
        kw = ks.astype(F32) * to_col(jnp.exp(g_row - m_new))
        C_ref[h] = w_old * Cmat + _dot_tn(v, kw.astype(BF16))
        n_ref[h] = w_old * nvec + jnp.sum(kw, axis=0, keepdims=True)
        m_ref[h] = jnp.broadcast_to(m_new, (1, LANES))


def _mlstm_prompt(z, gates_t, g_head, batch, seq):
    nc = seq // ML_CHUNK
    m = batch * seq

    def zspec(col):
        return pl.BlockSpec((ML_CHUNK, D), lambda b, c: (b * nc + c, col))

    return pl.pallas_call(
        _mlstm_prompt_kernel,
        grid=(batch, nc),
        in_specs=[zspec(0), zspec(1), zspec(2), zspec(3),
                  pl.BlockSpec((2 * HEADS, ML_CHUNK), lambda b, c: (0, b * nc + c)),
                  pl.BlockSpec((HEADS, 1, HD), lambda b, c: (0, 0, 0))],
        out_specs=[pl.BlockSpec((ML_CHUNK, D), lambda b, c: (b * nc + c, 0)),
                   pl.BlockSpec((None, HEADS, HD, HD), lambda b, c: (b, 0, 0, 0)),
                   pl.BlockSpec((None, HEADS, 1, HD), lambda b, c: (b, 0, 0, 0)),
                   pl.BlockSpec((None, HEADS, 1, LANES), lambda b, c: (b, 0, 0, 0))],
        out_shape=[jax.ShapeDtypeStruct((m, D), BF16),
                   jax.ShapeDtypeStruct((batch, HEADS, HD, HD), F32),
                   jax.ShapeDtypeStruct((batch, HEADS, 1, HD), F32),
                   jax.ShapeDtypeStruct((batch, HEADS, 1, LANES), F32)],
        compiler_params=pltpu.CompilerParams(dimension_semantics=("arbitrary", "arbitrary")),
        name="mlstm_prompt",
    )(z, z, z, z, gates_t, g_head.reshape(HEADS, 1, HD))


def _mlstm_sample_kernel(seq_len, q_ref, k_ref, v_ref, o_ref, gc_ref, m0_ref, C0_ref, n0_ref, gh_ref,
                         hm_ref, C_ref, n_ref, m_ref):
    R = q_ref.shape[0]
    nseq = R // seq_len
    r = lax.broadcasted_iota(jnp.int32, (R, R), 0)
    s = lax.broadcasted_iota(jnp.int32, (R, R), 1)
    rseq = lax.broadcasted_iota(jnp.int32, (R, 1), 0) // seq_len
    eye = r == s
    same = (r // seq_len) == (s // seq_len)
    causal = same & (s <= r)

    def to_row(x_col):
        return jnp.sum(jnp.where(eye, x_col, 0.0), axis=0, keepdims=True)

    gc = gc_ref[...]
    for h in range(HEADS):
        cols = slice(h * HD, (h + 1) * HD)
        ig_col = gc[:, h:h + 1]
        lf_col = _log_sigmoid(gc[:, HEADS + h:HEADS + h + 1])
        lf_row = to_row(lf_col)
        b_col = jnp.sum(jnp.where(causal, lf_row, 0.0), axis=1, keepdims=True)
        bL_col = jnp.sum(jnp.where(same, lf_row, 0.0), axis=1, keepdims=True)
        b_row = to_row(b_col)
        ig_row = to_row(ig_col)
        m0_col = m0_ref[:, h:h + 1]

        logD = jnp.where(causal, b_col - b_row + ig_row, -jnp.inf)
        inter = b_col + m0_col
        mt = jnp.maximum(jnp.max(logD, axis=1, keepdims=True), inter)
        qf = q_ref[:, cols]
        q = qf.astype(BF16)
        ksf = k_ref[:, cols] * (HD ** -0.5)
        v = v_ref[:, cols].astype(BF16)
        S = _dot_nt(q, ksf.astype(BF16)) * jnp.exp(logD - mt)
        w_int = jnp.exp(inter - mt)

        Cq = jnp.zeros((R, HD), F32)
        nq = jnp.zeros((R, 1), F32)
        for g in range(nseq):
            Cq = jnp.where(rseq == g, _dot_nt(q, C0_ref[g, h].astype(BF16)), Cq)
            nq = jnp.where(rseq == g, jnp.sum(qf * n0_ref[g, h], axis=1, keepdims=True), nq)
        num = _dot(S.astype(BF16), v) + w_int * Cq
        den = jnp.sum(S, axis=1, keepdims=True) + w_int * nq
        hh = num / jnp.maximum(jnp.abs(den), jnp.exp(-mt))
        hg = jax.nn.sigmoid(o_ref[:, cols]) * hh
        hm_ref[:, cols] = (_rms(hg) * gh_ref[h]).astype(hm_ref.dtype)

        g_col = bL_col - b_col + ig_col
        gmax_col = jnp.max(jnp.where(same, to_row(g_col), -jnp.inf), axis=1, keepdims=True)
        m_new_col = jnp.maximum(bL_col + m0_col, gmax_col)
        w_old_col = jnp.exp(bL_col + m0_col - m_new_col)
        kw = ksf * jnp.exp(g_col - m_new_col)
        for g in range(nseq):
            kw_g = jnp.where(rseq == g, kw, 0.0)
            w_old = w_old_col[g * seq_len:g * seq_len + 1, :]
            C_ref[g, h] = w_old * C0_ref[g, h] + _dot_tn(v, kw_g.astype(BF16))
            n_ref[g, h] = w_old * n0_ref[g, h] + jnp.sum(kw_g, axis=0, keepdims=True)
        m_ref[:, h * LANES:(h + 1) * LANES] = jnp.broadcast_to(m_new_col, (R, LANES))


def _mlstm_sample(z, gates_c, m0_tok, C0, n0, g_head, batch, seq):
    rows = SAMPLE_SEQS * seq
    m = batch * seq

    def zspec(col):
        return pl.BlockSpec((rows, D), lambda i: (i, col))

    state_c = pl.BlockSpec((SAMPLE_SEQS, HEADS, HD, HD), lambda i: (i, 0, 0, 0))
    state_n = pl.BlockSpec((SAMPLE_SEQS, HEADS, 1, HD), lambda i: (i, 0, 0, 0))
    return pl.pallas_call(
        functools.partial(_mlstm_sample_kernel, seq),
        grid=(batch // SAMPLE_SEQS,),
        in_specs=[zspec(0), zspec(1), zspec(2), zspec(3),
                  pl.BlockSpec((rows, LANES), lambda i: (i, 0)),
                  pl.BlockSpec((rows, HEADS), lambda i: (i, 0)),
                  state_c, state_n,
                  pl.BlockSpec((HEADS, 1, HD), lambda i: (0, 0, 0))],
        out_specs=[pl.BlockSpec((rows, D), lambda i: (i, 0)),
                   state_c, state_n,
                   pl.BlockSpec((rows, HEADS * LANES), lambda i: (i, 0))],
        out_shape=[jax.ShapeDtypeStruct((m, D), BF16),
                   jax.ShapeDtypeStruct((batch, HEADS, HD, HD), F32),
                   jax.ShapeDtypeStruct((batch, HEADS, 1, HD), F32),
                   jax.ShapeDtypeStruct((m, HEADS * LANES), F32)],
        compiler_params=pltpu.CompilerParams(
            dimension_semantics=("arbitrary",), vmem_limit_bytes=VMEM_LIMIT),
        name="mlstm_sample",
    )(z, z, z, z, gates_c, m0_tok, C0, n0.reshape(batch, HEADS, 1, HD), g_head.reshape(HEADS, 1, HD))


def _mix_kernel(x_ref, hm_ref, u_ref, v_ref, ga_ref, gb_ref, gt1_ref, sh2_ref, sc2_ref,
                gsgu_ref, bsgu_ref, ws_ref, bs_ref, wpa_ref, wpb_ref, wout_ref, gffn_ref,
                wr_ref, br_ref, x1_ref, h2_ref, lg_ref, vg_ref, yg_scr):
    tm = x_ref.shape[0]
    u = jax.nn.gelu(u_ref[...].astype(F32))
    vv = jax.nn.gelu(v_ref[...].astype(F32))
    mu = jnp.mean(vv, axis=-1, keepdims=True)
    var = jnp.mean(jnp.square(vv - mu), axis=-1, keepdims=True)
    vg = (vv - mu) * lax.rsqrt(var + NORM_EPS) * gsgu_ref[...] + bsgu_ref[...]
    vg_ref[...] = vg
    vgb = vg.astype(BF16)

    r = lax.broadcasted_iota(jnp.int32, (GM_CHUNK, GM_CHUNK), 0)
    s = lax.broadcasted_iota(jnp.int32, (GM_CHUNK, GM_CHUNK), 1)
    for g in range(GROUPS):
        w = jnp.where(s <= r, ws_ref[g], 0.0).astype(BF16)
        bias = bs_ref[:, g:g + 1]
        for c in range(tm // GM_CHUNK):
            rows = slice(c * GM_CHUNK, (c + 1) * GM_CHUNK)
            cols = slice(g * GD, (g + 1) * GD)
            mixed = _dot(w, vgb[rows, cols]) + bias
            yg_scr[rows, cols] = (u[rows, cols] * mixed).astype(BF16)

    a = _dot(hm_ref[...], wpa_ref[...])
    b = _dot(yg_scr[...], wpb_ref[...])
    merged = (jax.nn.sigmoid(ga_ref[...].astype(F32)) * a
              + jax.nn.sigmoid(gb_ref[...].astype(F32)) * b)
    x1 = x_ref[...] + gt1_ref[...] * _dot(merged.astype(BF16), wout_ref[...])
    x1_ref[...] = x1
    h2 = (_rms(x1) * gffn_ref[...]) * (1.0 + sc2_ref[...]) + sh2_ref[...]
    h2_ref[...] = h2
    lg_ref[...] = _dot3(h2, wr_ref[...]) + br_ref[...]


def _mix(x, hm, z, mod, per_token, tokens_per_seq, tm, p):
    m = x.shape[0]
    zcol = 4

    def zspec(blk):
        return pl.BlockSpec((tm, D), lambda i: (i, blk))

    def full(shape):
        return pl.BlockSpec(shape, lambda i: (0,) * len(shape))

    row = pl.BlockSpec((tm, D), lambda i: (i, 0))
    return pl.pallas_call(
        _mix_kernel,
        grid=(m // tm,),
        in_specs=[row, row, zspec(zcol), zspec(zcol + 1), zspec(zcol + 2), zspec(zcol + 3),
                  _mod_spec(per_token, tm, tokens_per_seq, 2),
                  _mod_spec(per_token, tm, tokens_per_seq, 3),
                  _mod_spec(per_token, tm, tokens_per_seq, 4),
                  full((1, D)), full((1, D)),
                  full((GROUPS, GM_CHUNK, GM_CHUNK)), full((GM_CHUNK, GROUPS)),
                  full((D, D)), full((D, D)), full((D, D)), full((1, D)),
                  full((D, LANES)), full((1, LANES))],
        out_specs=[row, row, pl.BlockSpec((tm, LANES), lambda i: (i, 0)), row],
        out_shape=[jax.ShapeDtypeStruct((m, D), F32),
                   jax.ShapeDtypeStruct((m, D), F32),
                   jax.ShapeDtypeStruct((m, LANES), F32),
                   jax.ShapeDtypeStruct((m, D), F32)],
        scratch_shapes=[pltpu.VMEM((tm, D), BF16)],
        compiler_params=pltpu.CompilerParams(
            dimension_semantics=("arbitrary",), vmem_limit_bytes=VMEM_LIMIT),
        name="mix",
    )(x, hm, z, z, z, z, mod, mod, mod, p["g_sgu"], p["b_sgu"], p["w_s"], p["b_s"],
      p["w_pa"], p["w_pb"], p["w_out"], p["g_ffn"], p["w_router"], p["b_router"])


def _route_kernel(lg_ref, dest_ref, gate_ref, be_ref, pend_ref, cnt_scr, base_scr):
    pss = pl.program_id(0)
    i = pl.program_id(1)
    tm = lg_ref.shape[0]
    lane = lax.broadcasted_iota(jnp.int32, (tm, LANES), 1)
    lg = jnp.where(lane < N_EXPERTS, lg_ref[...], -jnp.inf)
    sels, vals = [], []
    for _ in range(TOP_K):
        mx = jnp.max(lg, axis=1, keepdims=True)
        idx = jnp.min(jnp.where(lg == mx, lane, LANES), axis=1, keepdims=True)
        sel = lane == idx
        sels.append(sel)
        vals.append(mx)
        lg = jnp.where(sel, -jnp.inf, lg)
    onehot = sum(s.astype(F32) for s in sels)
    tile_cnt = jnp.sum(onehot, axis=0, keepdims=True)

    @pl.when((pss == 0) & (i == 0))
    def _():
        cnt_scr[...] = jnp.zeros_like(cnt_scr)

    @pl.when(pss == 0)
    def _():
        cnt_scr[...] += tile_cnt

    @pl.when((pss == 1) & (i == 0))
    def _():
        cnt = cnt_scr[...]
        padded = jnp.floor((cnt + (MOE_ROWS - 1)) * (1.0 / MOE_ROWS)) * MOE_ROWS
        r = lax.broadcasted_iota(jnp.int32, (LANES, LANES), 0)
        s = lax.broadcasted_iota(jnp.int32, (LANES, LANES), 1)
        padded_col = jnp.sum(jnp.where(r == s, padded, 0.0), axis=1, keepdims=True)
        pstart = jnp.sum(jnp.where(r < s, padded_col, 0.0), axis=0, keepdims=True)
        pend = pstart + padded
        base_scr[...] = pstart
        pend_ref[...] = pend
        nbp = be_ref.shape[0]
        bstart = (lax.broadcasted_iota(jnp.int32, (nbp, LANES), 0) * MOE_ROWS).astype(F32)
        elane = lax.broadcasted_iota(jnp.int32, (nbp, LANES), 1)
        be = jnp.sum(jnp.where((elane < N_EXPERTS) & (pend <= bstart), 1.0, 0.0), axis=1, keepdims=True)
        be = jnp.minimum(be, N_EXPERTS - 1.0)
        owns = (elane < N_EXPERTS) & (pstart <= bstart) & (bstart < pend)
        last = jnp.sum(jnp.where(owns, pstart + cnt, 0.0), axis=1, keepdims=True)
        valid = jnp.clip(last - bstart[:, :1], 0.0, MOE_ROWS)
        be_ref[...] = jnp.where(elane == 0, be, jnp.where(elane == 1, valid, 0.0)).astype(jnp.int32)

    @pl.when(pss == 1)
    def _():
        r = lax.broadcasted_iota(jnp.int32, (tm, tm), 0)
        s = lax.broadcasted_iota(jnp.int32, (tm, tm), 1)
        before = _dot((s < r).astype(BF16), onehot.astype(BF16))
        slot = before + base_scr[...]
        dest = jnp.zeros((tm, LANES), F32)
        gates = jnp.zeros((tm, LANES), F32)
        ex = [jnp.exp(v - vals[0]) for v in vals]
        denom = sum(ex)
        for k in range(TOP_K):
            d_k = jnp.sum(jnp.where(sels[k], slot, 0.0), axis=1, keepdims=True)
            dest = jnp.where(lane == k, d_k, dest)
            gates = jnp.where(lane == k, ex[k] / denom, gates)
        dest_ref[...] = dest[:, :TOP_K].astype(jnp.int32)
        gate_ref[...] = gates[:, :TOP_K]
        base_scr[...] += tile_cnt


def _route(logits, n_blocks):
    n = logits.shape[0]
    tm = 256
    tok = lambda p, i: (i * p, 0)
    fixed = lambda p, i: (0, 0)
    return pl.pallas_call(
        _route_kernel,
        grid=(2, n // tm),
        in_specs=[pl.BlockSpec((tm, LANES), lambda p, i: (i, 0))],
        out_specs=[pl.BlockSpec((tm, TOP_K), tok), pl.BlockSpec((tm, TOP_K), tok),
                   pl.BlockSpec((n_blocks, LANES), fixed), pl.BlockSpec((1, LANES), fixed)],
        out_shape=[jax.ShapeDtypeStruct((n, TOP_K), jnp.int32),
                   jax.ShapeDtypeStruct((n, TOP_K), F32),
                   jax.ShapeDtypeStruct((n_blocks, LANES), jnp.int32),
                   jax.ShapeDtypeStruct((1, LANES), F32)],
        scratch_shapes=[pltpu.VMEM((1, LANES), F32), pltpu.VMEM((1, LANES), F32)],
        compiler_params=pltpu.CompilerParams(dimension_semantics=("arbitrary", "arbitrary")),
        name="route",
    )(logits)


def _sc_worker_base(per_worker):
    return (lax.axis_index("s") * SC_CORES + lax.axis_index("c")) * per_worker


def _sc_scatter_rows(rows_a, rows_b, idx_flat, n_out):
    na, nb = rows_a.shape[0], rows_b.shape[0]
    n = na + nb
    per_a, per_b = na // SC_WORKERS, nb // SC_WORKERS
    chunk = SC_SCATTER_CHUNK
    n_chunks = per_a // chunk
    assert per_a * SC_WORKERS == na and per_b * SC_WORKERS == nb and per_b % 8 == 0 and per_b <= chunk
    assert n_chunks * chunk == per_a and n_chunks % 2 == 0
    mesh = plsc.VectorSubcoreMesh(core_axis_name="c", subcore_axis_name="s")

    @functools.partial(
        pl.kernel, mesh=mesh,
        out_type=jax.ShapeDtypeStruct((n_out, D), F32),
        scratch_types=[pltpu.VMEM((chunk,), jnp.int32)] * TOP_K + [pltpu.VMEM((per_b,), jnp.int32)]
                      + [pltpu.VMEM((chunk, D), F32), pltpu.VMEM((chunk, D), F32), pltpu.VMEM((per_b, D), F32)]
                      + [pltpu.SemaphoreType.DMA] * 3,
    )
    def scatter(a_hbm, b_hbm, idx_hbm, out_hbm, i0, i1, i2, i3, ib, rows0, rows1, rowsb, rsem0, rsem1, wsem):
        base = _sc_worker_base(per_a)
        idx_bufs = (i0, i1, i2, i3)
        bufs = ((rows0, rsem0), (rows1, rsem1))

        def off(j):
            return pl.multiple_of(base + j * chunk, 8)

        def read(j, buf):
            rows_v, sem = buf
            return pltpu.make_async_copy(a_hbm.at[pl.ds(off(j), chunk)], rows_v, sem)

        def spread(j, buf):
            rows_v, _ = buf
            read(j, buf).wait()
            for k in range(TOP_K):
                pltpu.sync_copy(idx_hbm.at[pl.ds(pl.multiple_of(k * n + off(j), 8), chunk)], idx_bufs[k])
            for k in range(TOP_K):
                pltpu.make_async_copy(rows_v, out_hbm.at[idx_bufs[k]], wsem).start()
            for k in range(TOP_K):
                pltpu.make_async_copy(rows_v, out_hbm.at[idx_bufs[k]], wsem).wait()

        read(0, bufs[0]).start()

        @pl.loop(0, n_chunks, step=2)
        def _(j):
            read(j + 1, bufs[1]).start()
            spread(j, bufs[0])

            @pl.when(j + 2 < n_chunks)
            def _():
                read(j + 2, bufs[0]).start()
            spread(j + 1, bufs[1])

        off_b = pl.multiple_of(_sc_worker_base(per_b), 8)
        pltpu.sync_copy(b_hbm.at[pl.ds(off_b, per_b)], rowsb)
        for k in range(TOP_K):
            pltpu.sync_copy(idx_hbm.at[pl.ds(pl.multiple_of(k * n + na + off_b, 8), per_b)], ib)
            pltpu.async_copy(rowsb, out_hbm.at[ib], wsem).wait()

    return scatter(rows_a, rows_b, idx_flat)


def _sc_gather_rows(table, idx_flat):
    b = idx_flat.shape[0]
    per_worker = b // SC_WORKERS
    chunk = SC_GATHER_CHUNK
    n_chunks = per_worker // chunk
    assert per_worker * SC_WORKERS == b and n_chunks * chunk == per_worker and n_chunks % 2 == 0
    mesh = plsc.VectorSubcoreMesh(core_axis_name="c", subcore_axis_name="s")

    @functools.partial(
        pl.kernel, mesh=mesh,
        out_type=jax.ShapeDtypeStruct((b, D), F32),
        scratch_types=[pltpu.VMEM((chunk,), jnp.int32), pltpu.VMEM((chunk,), jnp.int32),
                       pltpu.VMEM((chunk, D), F32), pltpu.VMEM((chunk, D), F32),
                       pltpu.SemaphoreType.DMA, pltpu.SemaphoreType.DMA],
    )
    def gather(table_hbm, idx_hbm, out_hbm, idx0, idx1, rows0, rows1, sem0, sem1):
        base = _sc_worker_base(per_worker)
        bufs = ((idx0, rows0, sem0), (idx1, rows1, sem1))

        def off(j):
            return pl.multiple_of(base + j * chunk, 8)

        def start(j, buf):
            idx_v, rows_v, sem = buf
            pltpu.sync_copy(idx_hbm.at[pl.ds(off(j), chunk)], idx_v)
            pltpu.make_async_copy(table_hbm.at[idx_v], rows_v, sem).start()

        def finish(j, buf):
            idx_v, rows_v, sem = buf
            pltpu.make_async_copy(table_hbm.at[idx_v], rows_v, sem).wait()
            pltpu.sync_copy(rows_v, out_hbm.at[pl.ds(off(j), chunk)])

        start(0, bufs[0])

        @pl.loop(0, n_chunks, step=2)
        def _(j):
            start(j + 1, bufs[1])
            finish(j, bufs[0])

            @pl.when(j + 2 < n_chunks)
            def _():
                start(j + 2, bufs[0])
            finish(j + 1, bufs[1])

    return gather(table, idx_flat)


def _moe_kernel(be_ref, valid_ref, nused_ref, x_ref, wgu_ref, bgu_ref, wd_ref, bd_ref, o_ref, wgu_bf, wd_bf):
    i = pl.program_id(0)

    @pl.when((i == 0) | (be_ref[i] != be_ref[jnp.maximum(i - 1, 0)]))
    def _():
        wgu_bf[...] = wgu_ref[...].astype(BF16)
        wd_bf[...] = wd_ref[...].astype(BF16)

    @pl.when(i < nused_ref[0])
    def _():
        row = lax.broadcasted_iota(jnp.int32, (MOE_ROWS, 1), 0)
        x = jnp.where(row < valid_ref[i], x_ref[...], 0.0).astype(BF16)
        gu = _dot(x, wgu_bf[...]) + bgu_ref[...]
        gate = jnp.minimum(gu[:, :D_FF], SWIGLU_LIMIT)
        up = jnp.clip(gu[:, D_FF:], -SWIGLU_LIMIT, SWIGLU_LIMIT)
        act = gate * jax.nn.sigmoid(SWIGLU_ALPHA * gate) * (up + 1.0)
        o_ref[...] = _dot(act.astype(BF16), wd_bf[...]) + bd_ref[...]

    @pl.when(i >= nused_ref[0])
    def _():
        o_ref[...] = jnp.zeros_like(o_ref)


def _moe(xs, block_e, block_valid, n_used, w_gu, b_gu, w_down, b_down):
    rows = xs.shape[0]
    nb = rows // MOE_ROWS
    grid_spec = pltpu.PrefetchScalarGridSpec(
        num_scalar_prefetch=3,
        grid=(nb,),
        in_specs=[pl.BlockSpec((MOE_ROWS, D), lambda i, be, va, nu: (jnp.minimum(i, nu[0] - 1), 0)),
                  pl.BlockSpec((None, D, 2 * D_FF), lambda i, be, va, nu: (be[i], 0, 0)),
                  pl.BlockSpec((None, 1, 2 * D_FF), lambda i, be, va, nu: (be[i], 0, 0)),
                  pl.BlockSpec((None, D_FF, D), lambda i, be, va, nu: (be[i], 0, 0)),
                  pl.BlockSpec((None, 1, D), lambda i, be, va, nu: (be[i], 0, 0))],
        out_specs=pl.BlockSpec((MOE_ROWS, D), lambda i, be, va, nu: (i, 0)),
        scratch_shapes=[pltpu.VMEM((D, 2 * D_FF), BF16), pltpu.VMEM((D_FF, D), BF16)],
    )
    return pl.pallas_call(
        _moe_kernel,
        grid_spec=grid_spec,
        out_shape=jax.ShapeDtypeStruct((rows, D), F32),
        compiler_params=pltpu.CompilerParams(
            dimension_semantics=("arbitrary",), vmem_limit_bytes=MOE_VMEM_LIMIT),
        name="moe",
    )(block_e, block_valid, n_used, xs, w_gu, b_gu.reshape(N_EXPERTS, 1, 2 * D_FF), w_down,
      b_down.reshape(N_EXPERTS, 1, D))


def _final_kernel(x1_ref, yk_ref, gate_ref, gt2_ref, g_ref, o_ref):
    y2 = gate_ref[:, 0:1] * yk_ref[0]
    for k in range(1, TOP_K):
        y2 = y2 + gate_ref[:, k:k + 1] * yk_ref[k]
    o_ref[...] = _rms(x1_ref[...] + gt2_ref[...] * y2) * g_ref[...]


def _final(x1, yk, gates, row_off, mod, per_token, tokens_per_seq, g_final):
    tm = 256
    m = x1.shape[0]
    off = row_off // tm
    return pl.pallas_call(
        _final_kernel,
        grid=(m // tm,),
        in_specs=[pl.BlockSpec((tm, D), lambda i: (i, 0)),
                  pl.BlockSpec((TOP_K, tm, D), lambda i: (0, i + off, 0)),
                  pl.BlockSpec((tm, TOP_K), lambda i: (i + off, 0)),
                  _mod_spec(per_token, tm, tokens_per_seq, 5),
                  pl.BlockSpec((1, D), lambda i: (0, 0))],
        out_specs=pl.BlockSpec((tm, D), lambda i: (i, 0)),
        out_shape=jax.ShapeDtypeStruct((m, D), F32),
        compiler_params=pltpu.CompilerParams(
            dimension_semantics=("arbitrary",), vmem_limit_bytes=VMEM_LIMIT),
        name="final",
    )(x1, yk, gates, mod, g_final.reshape(1, D))


def kernel(x_prompt, x_sample, state_C, state_n, state_m, c_prompt, c_sample, w_ada, b_ada, g_mix, w_in,
           b_if, g_head, g_sgu, b_sgu, w_s, b_s, w_pa, w_pb, w_out, g_ffn, w_router, b_router, w_gu, b_gu,
           w_down, b_down, g_final):
    depth = w_ada.shape[0]
    assert depth == 1
    bp, tp, _ = x_prompt.shape
    bs, ts, _ = x_sample.shape
    mp, ms = bp * tp, bs * ts
    assert tp % ML_CHUNK == 0 and ts <= ML_CHUNK and GM_CHUNK % ts == 0

    w_in0 = w_in[0]
    nqkvo = 4 * D
    w_main = jnp.concatenate([w_in0[:, :nqkvo], w_in0[:, nqkvo + 2 * HEADS:]], axis=1).astype(BF16)
    w_if = jnp.pad(w_in0[:, nqkvo:nqkvo + 2 * HEADS], ((0, 0), (0, LANES - 2 * HEADS)))
    b_if_p = jnp.pad(b_if[0], (0, LANES - 2 * HEADS)).reshape(1, LANES)
    reps = GM_CHUNK // ts
    eye_r = jnp.eye(reps, dtype=F32)
    w_s_sample = jnp.einsum("ab,gts->gatbs", eye_r, w_s[0][:, :ts, :ts]).reshape(GROUPS, GM_CHUNK, GM_CHUNK)
    b_s_prompt = b_s[0].T
    b_s_sample = jnp.tile(b_s[0][:, :ts].T, (reps, 1))
    mix_p = {
        "g_sgu": g_sgu[0].reshape(1, D), "b_sgu": b_sgu[0].reshape(1, D),
        "w_pa": w_pa[0].astype(BF16), "w_pb": w_pb[0].astype(BF16), "w_out": w_out[0].astype(BF16),
        "g_ffn": g_ffn[0].reshape(1, D),
        "w_router": jnp.pad(w_router[0], ((0, 0), (0, LANES - N_EXPERTS))),
        "b_router": jnp.pad(b_router[0], (0, LANES - N_EXPERTS)).reshape(1, LANES),
    }
    mix_prompt = dict(mix_p, w_s=w_s[0], b_s=b_s_prompt)
    mix_sample = dict(mix_p, w_s=w_s_sample, b_s=b_s_sample)

    mod = _ada(jnp.concatenate([c_prompt, c_sample], axis=0), w_ada[0], b_ada[0])
    mod_p = mod[:bp].reshape(bp, 1, N_MOD * D)
    mod_s = jnp.repeat(mod[bp:], ts, axis=0)

    xp = x_prompt.reshape(mp, D)
    xs = x_sample.reshape(ms, D)
    z_p, _, gt_p = _in_proj(xp, mod_p, False, tp, g_mix[0], w_main, w_if, b_if_p, 512, 2048, BF16)
    z_s, gc_s, _ = _in_proj(xs, mod_s, True, ts, g_mix[0], w_main, w_if, b_if_p, 256, 2048, F32)

    hm_p, C_p, n_p, m_p = _mlstm_prompt(z_p, gt_p, g_head[0], bp, tp)
    m0_tok = jnp.repeat(state_m[0], ts, axis=0)
    hm_s, C_s, n_s, m_s = _mlstm_sample(z_s, gc_s, m0_tok, state_C[0], state_n[0], g_head[0], bs, ts)

    x1_p, h2_p, lg_p, _ = _mix(xp, hm_p, z_p, mod_p, False, tp, 256, mix_prompt)
    x1_s, h2_s, lg_s, vg_s = _mix(xs, hm_s, z_s, mod_s, True, ts, 128, mix_sample)

    n_tok = mp + ms
    n_blocks = -(-(n_tok * TOP_K + N_EXPERTS * (MOE_ROWS - 1)) // MOE_ROWS)
    dest, gates, table, pend = _route(jnp.concatenate([lg_p, lg_s], axis=0), n_blocks)
    dest_kmajor = dest.T.reshape(TOP_K * n_tok)
    n_used = pend[0, N_EXPERTS - 1:N_EXPERTS].astype(jnp.int32) // MOE_ROWS
    xslots = _sc_scatter_rows(h2_p, h2_s, dest_kmajor, n_blocks * MOE_ROWS)
    yb = _moe(xslots, table[:, 0], table[:, 1], n_used, w_gu[0], b_gu[0], w_down[0], b_down[0])
    yk = _sc_gather_rows(yb, dest_kmajor).reshape(TOP_K, n_tok, D)

    y_p = _final(x1_p, yk, gates, 0, mod_p, False, tp, g_final)
    y_s = _final(x1_s, yk, gates, mp, mod_s, True, ts, g_final)

    return (y_p.reshape(bp, tp, D), y_s.reshape(bs, ts, D),
            C_p[None], n_p.reshape(1, bp, HEADS, HD), m_p[:, :, 0, 0][None],
            C_s[None], n_s.reshape(1, bs, HEADS, HD),
            m_s.reshape(bs, ts, HEADS, LANES)[:, 0, :, 0][None],
            vg_s.reshape(1, bs, ts, D))
```

```python
import functools

import jax
import jax.numpy as jnp
from jax import lax
from jax.experimental import pallas as pl
from jax.experimental.pallas import tpu as pltpu
from jax.experimental.pallas import tpu_sc as plsc

F32 = jnp.float32
BF16 = jnp.bfloat16

D = 1024
HEADS = 4
HD = D // HEADS
ML_CHUNK = 128
GROUPS = 4
GD = D // GROUPS
GM_CHUNK = 128
N_EXPERTS = 32
TOP_K = 4
D_FF = D
SWIGLU_LIMIT = 7.0
SWIGLU_ALPHA = 1.702
NORM_EPS = 1e-6
N_MOD = 6
PACKED = D // 2
LANES = 128
SC_CORES = 2
SC_SUBCORES = 16
SC_WORKERS = SC_CORES * SC_SUBCORES
MOE_ROWS = 256
SAMPLE_SEQS = 4
SC_SCATTER_CHUNK = 64
SC_GATHER_CHUNK = 96
VMEM_LIMIT = 48 * 1024 * 1024
MOE_VMEM_LIMIT = 56 * 1024 * 1024


def _dot(a, b):
    return jnp.dot(a, b, preferred_element_type=F32)


def _dot_nt(a, b):
    return lax.dot_general(a, b, (((1,), (1,)), ((), ())), preferred_element_type=F32)


def _dot_tn(a, b):
    return lax.dot_general(a, b, (((0,), (0,)), ((), ())), preferred_element_type=F32)


def _split_bf16(a):
    hi = a.astype(BF16)
    lo = (a - hi.astype(F32)).astype(BF16)
    return hi, lo


def _dot3(a, b):
    ah, al = _split_bf16(a)
    bh, bl = _split_bf16(b)
    return _dot(ah, bh) + (_dot(ah, bl) + _dot(al, bh))


def _log_sigmoid(x):
    return jnp.minimum(x, 0.0) - jnp.log1p(jnp.exp(-jnp.abs(x)))


def _rms(x):
    return x * lax.rsqrt(jnp.mean(x * x, axis=-1, keepdims=True) + NORM_EPS)


def _pack_rows(x):
    bits = lax.bitcast_convert_type(x.astype(BF16).astype(F32), jnp.uint32)
    word = (bits[:, :PACKED] & jnp.uint32(0xFFFF0000)) | (bits[:, PACKED:] >> 16)
    return lax.bitcast_convert_type(word, jnp.int32)


def _unpack_rows(w):
    bits = lax.bitcast_convert_type(w, jnp.uint32)
    left = lax.bitcast_convert_type(bits & jnp.uint32(0xFFFF0000), F32)
    right = lax.bitcast_convert_type(bits << 16, F32)
    return jnp.concatenate([left, right], axis=1)


def _mod_spec(per_token, tm, tokens_per_seq, col):
    if per_token:
        return pl.BlockSpec((tm, D), lambda i, *_: (i, col))
    return pl.BlockSpec((None, 1, D), lambda i, *_: ((i * tm) // tokens_per_seq, 0, col))


def _ada_kernel(c_ref, w_ref, b_ref, o_ref):
    c = c_ref[...]
    s = (c * jax.nn.sigmoid(c)).astype(BF16)
    o_ref[...] = _dot(s, w_ref[...].astype(BF16)) + b_ref[...]


def _ada(c, w, b):
    m, n = c.shape[0], w.shape[1]
    tn = 512
    return pl.pallas_call(
        _ada_kernel,
        grid=(n // tn,),
        in_specs=[pl.BlockSpec((m, D), lambda j: (0, 0)),
                  pl.BlockSpec((D, tn), lambda j: (0, j)),
                  pl.BlockSpec((1, tn), lambda j: (0, j))],
        out_specs=pl.BlockSpec((m, tn), lambda j: (0, j)),
        out_shape=jax.ShapeDtypeStruct((m, n), F32),
        name="ada",
    )(c, w, b.reshape(1, n))


def _in_kernel(x_ref, g_ref, sh_ref, sc_ref, w_ref, wif_ref, bif_ref, z_ref, gc_ref, gt_ref, h_scr):
    @pl.when(pl.program_id(1) == 0)
    def _():
        h = (_rms(x_ref[...]) * g_ref[...]) * (1.0 + sc_ref[...]) + sh_ref[...]
        h_scr[...] = h.astype(BF16)
        gates = _dot3(h, wif_ref[...]) + bif_ref[...]
        gc_ref[...] = gates
        gt_ref[...] = gates.T[:2 * HEADS, :]

    z_ref[...] = _dot(h_scr[...], w_ref[...]).astype(z_ref.dtype)


def _in_proj(x, mod, per_token, tokens_per_seq, g_mix, w_main, w_if, b_if, tm, tn, z_dtype):
    m = x.shape[0]
    n = w_main.shape[1]
    return pl.pallas_call(
        _in_kernel,
        grid=(m // tm, n // tn),
        in_specs=[pl.BlockSpec((tm, D), lambda i, j: (i, 0)),
                  pl.BlockSpec((1, D), lambda i, j: (0, 0)),
                  _mod_spec(per_token, tm, tokens_per_seq, 0),
                  _mod_spec(per_token, tm, tokens_per_seq, 1),
                  pl.BlockSpec((D, tn), lambda i, j: (0, j)),
                  pl.BlockSpec((D, LANES), lambda i, j: (0, 0)),
                  pl.BlockSpec((1, LANES), lambda i, j: (0, 0))],
        out_specs=[pl.BlockSpec((tm, tn), lambda i, j: (i, j)),
                   pl.BlockSpec((tm, LANES), lambda i, j: (i, 0)),
                   pl.BlockSpec((2 * HEADS, tm), lambda i, j: (0, i))],
        out_shape=[jax.ShapeDtypeStruct((m, n), z_dtype),
                   jax.ShapeDtypeStruct((m, LANES), F32),
                   jax.ShapeDtypeStruct((2 * HEADS, m), F32)],
        scratch_shapes=[pltpu.VMEM((tm, D), BF16)],
        compiler_params=pltpu.CompilerParams(
            dimension_semantics=("arbitrary", "arbitrary"), vmem_limit_bytes=VMEM_LIMIT),
        name="in_proj",
    )(x, g_mix.reshape(1, D), mod, mod, w_main, w_if, b_if)


def _mlstm_prompt_kernel(q_ref, k_ref, v_ref, o_ref, gt_ref, gh_ref, hm_ref, C_ref, n_ref, m_ref):
    L = q_ref.shape[0]

    @pl.when(pl.program_id(1) == 0)
    def _():
        C_ref[...] = jnp.zeros_like(C_ref)
        n_ref[...] = jnp.zeros_like(n_ref)
        m_ref[...] = jnp.zeros_like(m_ref)

    gates = gt_ref[...]
    r = lax.broadcasted_iota(jnp.int32, (L, L), 0)
    s = lax.broadcasted_iota(jnp.int32, (L, L), 1)
    eye = r == s
    causal = s <= r

    def to_col(x_row):
        return jnp.sum(jnp.where(eye, x_row, 0.0), axis=1, keepdims=True)

    for h in range(HEADS):
        cols = slice(h * HD, (h + 1) * HD)
        ig_row = gates[h:h + 1, :]
        lf_row = _log_sigmoid(gates[HEADS + h:HEADS + h + 1, :])
        lf_col = to_col(lf_row)
        b_row = jnp.sum(jnp.where(r <= s, lf_col, 0.0), axis=0, keepdims=True)
        b_col = to_col(b_row)
        m_prev = m_ref[h][:, :1]

        logD = jnp.where(causal, b_col - b_row + ig_row, -jnp.inf)
        inter = b_col + m_prev
        mt = jnp.maximum(jnp.max(logD, axis=1, keepdims=True), inter)
        q = q_ref[:, cols]
        ks = k_ref[:, cols] * (HD ** -0.5)
        v = v_ref[:, cols]
        S = _dot_nt(q, ks) * jnp.exp(logD - mt)
        w_int = jnp.exp(inter - mt)
        Cmat = C_ref[h]
        nvec = n_ref[h]
        num = _dot(S.astype(BF16), v) + w_int * _dot_nt(q, Cmat.astype(BF16))
        nq = jnp.sum(q.astype(F32) * nvec, axis=1, keepdims=True)
        den = jnp.sum(S, axis=1, keepdims=True) + w_int * nq
        hh = num / jnp.maximum(jnp.abs(den), jnp.exp(-mt))
        hg = jax.nn.sigmoid(o_ref[:, cols].astype(F32)) * hh
        hm_ref[:, cols] = (_rms(hg) * gh_ref[h]).astype(hm_ref.dtype)

        bL = b_row[:, L - 1:L]
        g_row = bL - b_row + ig_row
        m_new = jnp.maximum(bL + m_prev, jnp.max(g_row, axis=1, keepdims=True))
        w_old = jnp.exp(bL + m_prev - m_new)
        kw = ks.astype(F32) * to_col(jnp.exp(g_row - m_new))
        C_ref[h] = w_old * Cmat + _dot_tn(v, kw.astype(BF16))
        n_ref[h] = w_old * nvec + jnp.sum(kw, axis=0, keepdims=True)
        m_ref[h] = jnp.broadcast_to(m_new, (1, LANES))


def _mlstm_prompt(z, gates_t, g_head, batch, seq):
    nc = seq // ML_CHUNK
    m = batch * seq

    def zspec(col):
        return pl.BlockSpec((ML_CHUNK, D), lambda b, c: (b * nc + c, col))

    return pl.pallas_call(
        _mlstm_prompt_kernel,
        grid=(batch, nc),
        in_specs=[zspec(0), zspec(1), zspec(2), zspec(3),
                  pl.BlockSpec((2 * HEADS, ML_CHUNK), lambda b, c: (0, b * nc + c)),
                  pl.BlockSpec((HEADS, 1, HD), lambda b, c: (0, 0, 0))],
        out_specs=[pl.BlockSpec((ML_CHUNK, D), lambda b, c: (b * nc + c, 0)),
                   pl.BlockSpec((None, HEADS, HD, HD), lambda b, c: (b, 0, 0, 0)),
                   pl.BlockSpec((None, HEADS, 1, HD), lambda b, c: (b, 0, 0, 0)),
                   pl.BlockSpec((None, HEADS, 1, LANES), lambda b, c: (b, 0, 0, 0))],
        out_shape=[jax.ShapeDtypeStruct((m, D), BF16),
                   jax.ShapeDtypeStruct((batch, HEADS, HD, HD), F32),
                   jax.ShapeDtypeStruct((batch, HEADS, 1, HD), F32),
                   jax.ShapeDtypeStruct((batch, HEADS, 1, LANES), F32)],
        compiler_params=pltpu.CompilerParams(dimension_semantics=("arbitrary", "arbitrary")),
        name="mlstm_prompt",
    )(z, z, z, z, gates_t, g_head.reshape(HEADS, 1, HD))


def _mlstm_sample_kernel(seq_len, q_ref, k_ref, v_ref, o_ref, gc_ref, m0_ref, C0_ref, n0_ref, gh_ref,
                         hm_ref, C_ref, n_ref, m_ref):
    R = q_ref.shape[0]
    nseq = R // seq_len
    r = lax.broadcasted_iota(jnp.int32, (R, R), 0)
    s = lax.broadcasted_iota(jnp.int32, (R, R), 1)
    rseq = lax.broadcasted_iota(jnp.int32, (R, 1), 0) // seq_len
    eye = r == s
    same = (r // seq_len) == (s // seq_len)
    causal = same & (s <= r)

    def to_row(x_col):
        return jnp.sum(jnp.where(eye, x_col, 0.0), axis=0, keepdims=True)

    gc = gc_ref[...]
    for h in range(HEADS):
        cols = slice(h * HD, (h + 1) * HD)
        ig_col = gc[:, h:h + 1]
        lf_col = _log_sigmoid(gc[:, HEADS + h:HEADS + h + 1])
        lf_row = to_row(lf_col)
        b_col = jnp.sum(jnp.where(causal, lf_row, 0.0), axis=1, keepdims=True)
        bL_col = jnp.sum(jnp.where(same, lf_row, 0.0), axis=1, keepdims=True)
        b_row = to_row(b_col)
        ig_row = to_row(ig_col)
        m0_col = m0_ref[:, h:h + 1]

        logD = jnp.where(causal, b_col - b_row + ig_row, -jnp.inf)
        inter = b_col + m0_col
        mt = jnp.maximum(jnp.max(logD, axis=1, keepdims=True), inter)
        qf = q_ref[:, cols]
        q = qf.astype(BF16)
        ksf = k_ref[:, cols] * (HD ** -0.5)
        v = v_ref[:, cols].astype(BF16)
        S = _dot_nt(q, ksf.astype(BF16)) * jnp.exp(logD - mt)
        w_int = jnp.exp(inter - mt)

        Cq = jnp.zeros((R, HD), F32)
        nq = jnp.zeros((R, 1), F32)
        for g in range(nseq):
            Cq = jnp.where(rseq == g, _dot_nt(q, C0_ref[g, h].astype(BF16)), Cq)
            nq = jnp.where(rseq == g, jnp.sum(qf * n0_ref[g, h], axis=1, keepdims=True), nq)
        num = _dot(S.astype(BF16), v) + w_int * Cq
        den = jnp.sum(S, axis=1, keepdims=True) + w_int * nq
        hh = num / jnp.maximum(jnp.abs(den), jnp.exp(-mt))
        hg = jax.nn.sigmoid(o_ref[:, cols]) * hh
        hm_ref[:, cols] = (_rms(hg) * gh_ref[h]).astype(hm_ref.dtype)

        g_col = bL_col - b_col + ig_col
        gmax_col = jnp.max(jnp.where(same, to_row(g_col), -jnp.inf), axis=1, keepdims=True)
        m_new_col = jnp.maximum(bL_col + m0_col, gmax_col)
        w_old_col = jnp.exp(bL_col + m0_col - m_new_col)
        kw = ksf * jnp.exp(g_col - m_new_col)
        for g in range(nseq):
            kw_g = jnp.where(rseq == g, kw, 0.0)
            w_old = w_old_col[g * seq_len:g * seq_len + 1, :]
            C_ref[g, h] = w_old * C0_ref[g, h] + _dot_tn(v, kw_g.astype(BF16))
            n_ref[g, h] = w_old * n0_ref[g, h] + jnp.sum(kw_g, axis=0, keepdims=True)
        m_ref[:, h * LANES:(h + 1) * LANES] = jnp.broadcast_to(m_new_col, (R, LANES))


def _mlstm_sample(z, gates_c, m0_tok, C0, n0, g_head, batch, seq):
    rows = SAMPLE_SEQS * seq
    m = batch * seq

    def zspec(col):
        return pl.BlockSpec((rows, D), lambda i: (i, col))

    state_c = pl.BlockSpec((SAMPLE_SEQS, HEADS, HD, HD), lambda i: (i, 0, 0, 0))
    state_n = pl.BlockSpec((SAMPLE_SEQS, HEADS, 1, HD), lambda i: (i, 0, 0, 0))
    return pl.pallas_call(
        functools.partial(_mlstm_sample_kernel, seq),
        grid=(batch // SAMPLE_SEQS,),
        in_specs=[zspec(0), zspec(1), zspec(2), zspec(3),
                  pl.BlockSpec((rows, LANES), lambda i: (i, 0)),
                  pl.BlockSpec((rows, HEADS), lambda i: (i, 0)),
                  state_c, state_n,
                  pl.BlockSpec((HEADS, 1, HD), lambda i: (0, 0, 0))],
        out_specs=[pl.BlockSpec((rows, D), lambda i: (i, 0)),
                   state_c, state_n,
                   pl.BlockSpec((rows, HEADS * LANES), lambda i: (i, 0))],
        out_shape=[jax.ShapeDtypeStruct((m, D), BF16),
                   jax.ShapeDtypeStruct((batch, HEADS, HD, HD), F32),
                   jax.ShapeDtypeStruct((batch, HEADS, 1, HD), F32),
                   jax.ShapeDtypeStruct((m, HEADS * LANES), F32)],
        compiler_params=pltpu.CompilerParams(
            dimension_semantics=("arbitrary",), vmem_limit_bytes=VMEM_LIMIT),
        name="mlstm_sample",
    )(z, z, z, z, gates_c, m0_tok, C0, n0.reshape(batch, HEADS, 1, HD), g_head.reshape(HEADS, 1, HD))


def _mix_kernel(x_ref, hm_ref, u_ref, v_ref, ga_ref, gb_ref, gt1_ref, sh2_ref, sc2_ref,
                gsgu_ref, bsgu_ref, ws_ref, bs_ref, wpa_ref, wpb_ref, wout_ref, gffn_ref,
                wr_ref, br_ref, x1_ref, h2_ref, lg_ref, vg_ref, yg_scr):
    tm = x_ref.shape[0]
    u = jax.nn.gelu(u_ref[...].astype(F32))
    vv = jax.nn.gelu(v_ref[...].astype(F32))
    mu = jnp.mean(vv, axis=-1, keepdims=True)
    var = jnp.mean(jnp.square(vv - mu), axis=-1, keepdims=True)
    vg = (vv - mu) * lax.rsqrt(var + NORM_EPS) * gsgu_ref[...] + bsgu_ref[...]
    vg_ref[...] = vg
    vgb = vg.astype(BF16)

    r = lax.broadcasted_iota(jnp.int32, (GM_CHUNK, GM_CHUNK), 0)
    s = lax.broadcasted_iota(jnp.int32, (GM_CHUNK, GM_CHUNK), 1)
    for g in range(GROUPS):
        w = jnp.where(s <= r, ws_ref[g], 0.0).astype(BF16)
        bias = bs_ref[:, g:g + 1]
        for c in range(tm // GM_CHUNK):
            rows = slice(c * GM_CHUNK, (c + 1) * GM_CHUNK)
            cols = slice(g * GD, (g + 1) * GD)
            mixed = _dot(w, vgb[rows, cols]) + bias
            yg_scr[rows, cols] = (u[rows, cols] * mixed).astype(BF16)

    a = _dot(hm_ref[...], wpa_ref[...])
    b = _dot(yg_scr[...], wpb_ref[...])
    merged = (jax.nn.sigmoid(ga_ref[...].astype(F32)) * a
              + jax.nn.sigmoid(gb_ref[...].astype(F32)) * b)
    x1 = x_ref[...] + gt1_ref[...] * _dot(merged.astype(BF16), wout_ref[...])
    x1_ref[...] = x1
    h2 = (_rms(x1) * gffn_ref[...]) * (1.0 + sc2_ref[...]) + sh2_ref[...]
    h2_ref[...] = _pack_rows(h2)
    lg_ref[...] = _dot3(h2, wr_ref[...]) + br_ref[...]


def _mix(x, hm, z, mod, per_token, tokens_per_seq, tm, p):
    m = x.shape[0]
    zcol = 4

    def zspec(blk):
        return pl.BlockSpec((tm, D), lambda i: (i, blk))

    def full(shape):
        return pl.BlockSpec(shape, lambda i: (0,) * len(shape))

    row = pl.BlockSpec((tm, D), lambda i: (i, 0))
    return pl.pallas_call(
        _mix_kernel,
        grid=(m // tm,),
        in_specs=[row, row, zspec(zcol), zspec(zcol + 1), zspec(zcol + 2), zspec(zcol + 3),
                  _mod_spec(per_token, tm, tokens_per_seq, 2),
                  _mod_spec(per_token, tm, tokens_per_seq, 3),
                  _mod_spec(per_token, tm, tokens_per_seq, 4),
                  full((1, D)), full((1, D)),
                  full((GROUPS, GM_CHUNK, GM_CHUNK)), full((GM_CHUNK, GROUPS)),
                  full((D, D)), full((D, D)), full((D, D)), full((1, D)),
                  full((D, LANES)), full((1, LANES))],
        out_specs=[row, pl.BlockSpec((tm, PACKED), lambda i: (i, 0)),
                   pl.BlockSpec((tm, LANES), lambda i: (i, 0)), row],
        out_shape=[jax.ShapeDtypeStruct((m, D), F32),
                   jax.ShapeDtypeStruct((m, PACKED), jnp.int32),
                   jax.ShapeDtypeStruct((m, LANES), F32),
                   jax.ShapeDtypeStruct((m, D), F32)],
        scratch_shapes=[pltpu.VMEM((tm, D), BF16)],
        compiler_params=pltpu.CompilerParams(
            dimension_semantics=("arbitrary",), vmem_limit_bytes=VMEM_LIMIT),
        name="mix",
    )(x, hm, z, z, z, z, mod, mod, mod, p["g_sgu"], p["b_sgu"], p["w_s"], p["b_s"],
      p["w_pa"], p["w_pb"], p["w_out"], p["g_ffn"], p["w_router"], p["b_router"])


def _route_kernel(lg_ref, dest_ref, gate_ref, be_ref, pend_ref, cnt_scr, base_scr):
    pss = pl.program_id(0)
    i = pl.program_id(1)
    tm = lg_ref.shape[0]
    lane = lax.broadcasted_iota(jnp.int32, (tm, LANES), 1)
    lg = jnp.where(lane < N_EXPERTS, lg_ref[...], -jnp.inf)
    sels, vals = [], []
    for _ in range(TOP_K):
        mx = jnp.max(lg, axis=1, keepdims=True)
        idx = jnp.min(jnp.where(lg == mx, lane, LANES), axis=1, keepdims=True)
        sel = lane == idx
        sels.append(sel)
        vals.append(mx)
        lg = jnp.where(sel, -jnp.inf, lg)
    onehot = sum(s.astype(F32) for s in sels)
    tile_cnt = jnp.sum(onehot, axis=0, keepdims=True)

    @pl.when((pss == 0) & (i == 0))
    def _():
        cnt_scr[...] = jnp.zeros_like(cnt_scr)

    @pl.when(pss == 0)
    def _():
        cnt_scr[...] += tile_cnt

    @pl.when((pss == 1) & (i == 0))
    def _():
        cnt = cnt_scr[...]
        padded = jnp.floor((cnt + (MOE_ROWS - 1)) * (1.0 / MOE_ROWS)) * MOE_ROWS
        r = lax.broadcasted_iota(jnp.int32, (LANES, LANES), 0)
        s = lax.broadcasted_iota(jnp.int32, (LANES, LANES), 1)
        padded_col = jnp.sum(jnp.where(r == s, padded, 0.0), axis=1, keepdims=True)
        pstart = jnp.sum(jnp.where(r < s, padded_col, 0.0), axis=0, keepdims=True)
        pend = pstart + padded
        base_scr[...] = pstart
        pend_ref[...] = pend
        nbp = be_ref.shape[0]
        bstart = (lax.broadcasted_iota(jnp.int32, (nbp, LANES), 0) * MOE_ROWS).astype(F32)
        elane = lax.broadcasted_iota(jnp.int32, (nbp, LANES), 1)
        be = jnp.sum(jnp.where((elane < N_EXPERTS) & (pend <= bstart), 1.0, 0.0), axis=1, keepdims=True)
        be = jnp.minimum(be, N_EXPERTS - 1.0)
        owns = (elane < N_EXPERTS) & (pstart <= bstart) & (bstart < pend)
        last = jnp.sum(jnp.where(owns, pstart + cnt, 0.0), axis=1, keepdims=True)
        valid = jnp.clip(last - bstart[:, :1], 0.0, MOE_ROWS)
        be_ref[...] = jnp.where(elane == 0, be, jnp.where(elane == 1, valid, 0.0)).astype(jnp.int32)

    @pl.when(pss == 1)
    def _():
        r = lax.broadcasted_iota(jnp.int32, (tm, tm), 0)
        s = lax.broadcasted_iota(jnp.int32, (tm, tm), 1)
        before = _dot((s < r).astype(BF16), onehot.astype(BF16))
        slot = before + base_scr[...]
        dest = jnp.zeros((tm, LANES), F32)
        gates = jnp.zeros((tm, LANES), F32)
        ex = [jnp.exp(v - vals[0]) for v in vals]
        denom = sum(ex)
        for k in range(TOP_K):
            d_k = jnp.sum(jnp.where(sels[k], slot, 0.0), axis=1, keepdims=True)
            dest = jnp.where(lane == k, d_k, dest)
            gates = jnp.where(lane == k, ex[k] / denom, gates)
        dest_ref[...] = dest[:, :TOP_K].astype(jnp.int32)
        gate_ref[...] = gates[:, :TOP_K]
        base_scr[...] += tile_cnt


def _route(logits, n_blocks):
    n = logits.shape[0]
    tm = 256
    tok = lambda p, i: (i * p, 0)
    fixed = lambda p, i: (0, 0)
    return pl.pallas_call(
        _route_kernel,
        grid=(2, n // tm),
        in_specs=[pl.BlockSpec((tm, LANES), lambda p, i: (i, 0))],
        out_specs=[pl.BlockSpec((tm, TOP_K), tok), pl.BlockSpec((tm, TOP_K), tok),
                   pl.BlockSpec((n_blocks, LANES), fixed), pl.BlockSpec((1, LANES), fixed)],
        out_shape=[jax.ShapeDtypeStruct((n, TOP_K), jnp.int32),
                   jax.ShapeDtypeStruct((n, TOP_K), F32),
                   jax.ShapeDtypeStruct((n_blocks, LANES), jnp.int32),
                   jax.ShapeDtypeStruct((1, LANES), F32)],
        scratch_shapes=[pltpu.VMEM((1, LANES), F32), pltpu.VMEM((1, LANES), F32)],
        compiler_params=pltpu.CompilerParams(dimension_semantics=("arbitrary", "arbitrary")),
        name="route",
    )(logits)


def _sc_worker_base(per_worker):
    return (lax.axis_index("s") * SC_CORES + lax.axis_index("c")) * per_worker


def _sc_scatter_rows(rows_a, rows_b, idx_flat, n_out):
    na, nb = rows_a.shape[0], rows_b.shape[0]
    width, dtype = rows_a.shape[1], rows_a.dtype
    n = na + nb
    per_a, per_b = na // SC_WORKERS, nb // SC_WORKERS
    chunk = SC_SCATTER_CHUNK
    n_chunks = per_a // chunk
    assert per_a * SC_WORKERS == na and per_b * SC_WORKERS == nb and per_b % 8 == 0 and per_b <= chunk
    assert n_chunks * chunk == per_a and n_chunks % 2 == 0
    mesh = plsc.VectorSubcoreMesh(core_axis_name="c", subcore_axis_name="s")

    @functools.partial(
        pl.kernel, mesh=mesh,
        out_type=jax.ShapeDtypeStruct((n_out, width), dtype),
        scratch_types=[pltpu.VMEM((chunk,), jnp.int32)] * TOP_K + [pltpu.VMEM((per_b,), jnp.int32)]
                      + [pltpu.VMEM((chunk, width), dtype), pltpu.VMEM((chunk, width), dtype),
                         pltpu.VMEM((per_b, width), dtype)]
                      + [pltpu.SemaphoreType.DMA] * 3,
    )
    def scatter(a_hbm, b_hbm, idx_hbm, out_hbm, i0, i1, i2, i3, ib, rows0, rows1, rowsb, rsem0, rsem1, wsem):
        base = _sc_worker_base(per_a)
        idx_bufs = (i0, i1, i2, i3)
        bufs = ((rows0, rsem0), (rows1, rsem1))

        def off(j):
            return pl.multiple_of(base + j * chunk, 8)

        def read(j, buf):
            rows_v, sem = buf
            return pltpu.make_async_copy(a_hbm.at[pl.ds(off(j), chunk)], rows_v, sem)

        def spread(j, buf):
            rows_v, _ = buf
            read(j, buf).wait()
            for k in range(TOP_K):
                pltpu.sync_copy(idx_hbm.at[pl.ds(pl.multiple_of(k * n + off(j), 8), chunk)], idx_bufs[k])
            for k in range(TOP_K):
                pltpu.make_async_copy(rows_v, out_hbm.at[idx_bufs[k]], wsem).start()
            for k in range(TOP_K):
                pltpu.make_async_copy(rows_v, out_hbm.at[idx_bufs[k]], wsem).wait()

        read(0, bufs[0]).start()

        @pl.loop(0, n_chunks, step=2)
        def _(j):
            read(j + 1, bufs[1]).start()
            spread(j, bufs[0])

            @pl.when(j + 2 < n_chunks)
            def _():
                read(j + 2, bufs[0]).start()
            spread(j + 1, bufs[1])

        off_b = pl.multiple_of(_sc_worker_base(per_b), 8)
        pltpu.sync_copy(b_hbm.at[pl.ds(off_b, per_b)], rowsb)
        for k in range(TOP_K):
            pltpu.sync_copy(idx_hbm.at[pl.ds(pl.multiple_of(k * n + na + off_b, 8), per_b)], ib)
            pltpu.async_copy(rowsb, out_hbm.at[ib], wsem).wait()

    return scatter(rows_a, rows_b, idx_flat)


def _sc_gather_rows(table, idx_flat):
    b = idx_flat.shape[0]
    width, dtype = table.shape[1], table.dtype
    per_worker = b // SC_WORKERS
    chunk = SC_GATHER_CHUNK
    n_chunks = per_worker // chunk
    assert per_worker * SC_WORKERS == b and n_chunks * chunk == per_worker and n_chunks % 2 == 0
    mesh = plsc.VectorSubcoreMesh(core_axis_name="c", subcore_axis_name="s")

    @functools.partial(
        pl.kernel, mesh=mesh,
        out_type=jax.ShapeDtypeStruct((b, width), dtype),
        scratch_types=[pltpu.VMEM((chunk,), jnp.int32), pltpu.VMEM((chunk,), jnp.int32),
                       pltpu.VMEM((chunk, width), dtype), pltpu.VMEM((chunk, width), dtype),
                       pltpu.SemaphoreType.DMA, pltpu.SemaphoreType.DMA],
    )
    def gather(table_hbm, idx_hbm, out_hbm, idx0, idx1, rows0, rows1, sem0, sem1):
        base = _sc_worker_base(per_worker)
        bufs = ((idx0, rows0, sem0), (idx1, rows1, sem1))

        def off(j):
            return pl.multiple_of(base + j * chunk, 8)

        def start(j, buf):
            idx_v, rows_v, sem = buf
            pltpu.sync_copy(idx_hbm.at[pl.ds(off(j), chunk)], idx_v)
            pltpu.make_async_copy(table_hbm.at[idx_v], rows_v, sem).start()

        def finish(j, buf):
            idx_v, rows_v, sem = buf
            pltpu.make_async_copy(table_hbm.at[idx_v], rows_v, sem).wait()
            pltpu.sync_copy(rows_v, out_hbm.at[pl.ds(off(j), chunk)])

        start(0, bufs[0])

        @pl.loop(0, n_chunks, step=2)
        def _(j):
            start(j + 1, bufs[1])
            finish(j, bufs[0])

            @pl.when(j + 2 < n_chunks)
            def _():
                start(j + 2, bufs[0])
            finish(j + 1, bufs[1])

    return gather(table, idx_flat)


def _moe_kernel(be_ref, valid_ref, nused_ref, x_ref, wgu_ref, bgu_ref, wd_ref, bd_ref, o_ref, wgu_bf, wd_bf):
    i = pl.program_id(0)

    @pl.when((i == 0) | (be_ref[i] != be_ref[jnp.maximum(i - 1, 0)]))
    def _():
        wgu_bf[...] = wgu_ref[...].astype(BF16)
        wd_bf[...] = wd_ref[...].astype(BF16)

    @pl.when(i < nused_ref[0])
    def _():
        row = lax.broadcasted_iota(jnp.int32, (MOE_ROWS, 1), 0)
        x = _unpack_rows(jnp.where(row < valid_ref[i], x_ref[...], 0)).astype(BF16)
        gu = _dot(x, wgu_bf[...]) + bgu_ref[...]
        gate = jnp.minimum(gu[:, :D_FF], SWIGLU_LIMIT)
        up = jnp.clip(gu[:, D_FF:], -SWIGLU_LIMIT, SWIGLU_LIMIT)
        act = gate * jax.nn.sigmoid(SWIGLU_ALPHA * gate) * (up + 1.0)
        o_ref[...] = _pack_rows(_dot(act.astype(BF16), wd_bf[...]) + bd_ref[...])

    @pl.when(i >= nused_ref[0])
    def _():
        o_ref[...] = jnp.zeros_like(o_ref)


def _moe(xs, block_e, block_valid, n_used, w_gu, b_gu, w_down, b_down):
    rows = xs.shape[0]
    nb = rows // MOE_ROWS
    grid_spec = pltpu.PrefetchScalarGridSpec(
        num_scalar_prefetch=3,
        grid=(nb,),
        in_specs=[pl.BlockSpec((MOE_ROWS, PACKED), lambda i, be, va, nu: (jnp.minimum(i, nu[0] - 1), 0)),
                  pl.BlockSpec((None, D, 2 * D_FF), lambda i, be, va, nu: (be[i], 0, 0)),
                  pl.BlockSpec((None, 1, 2 * D_FF), lambda i, be, va, nu: (be[i], 0, 0)),
                  pl.BlockSpec((None, D_FF, D), lambda i, be, va, nu: (be[i], 0, 0)),
                  pl.BlockSpec((None, 1, D), lambda i, be, va, nu: (be[i], 0, 0))],
        out_specs=pl.BlockSpec((MOE_ROWS, PACKED), lambda i, be, va, nu: (i, 0)),
        scratch_shapes=[pltpu.VMEM((D, 2 * D_FF), BF16), pltpu.VMEM((D_FF, D), BF16)],
    )
    return pl.pallas_call(
        _moe_kernel,
        grid_spec=grid_spec,
        out_shape=jax.ShapeDtypeStruct((rows, PACKED), jnp.int32),
        compiler_params=pltpu.CompilerParams(
            dimension_semantics=("arbitrary",), vmem_limit_bytes=MOE_VMEM_LIMIT),
        name="moe",
    )(block_e, block_valid, n_used, xs, w_gu, b_gu.reshape(N_EXPERTS, 1, 2 * D_FF), w_down,
      b_down.reshape(N_EXPERTS, 1, D))


def _final_kernel(x1_ref, yk_ref, gate_ref, gt2_ref, g_ref, o_ref):
    y2 = gate_ref[:, 0:1] * _unpack_rows(yk_ref[0])
    for k in range(1, TOP_K):
        y2 = y2 + gate_ref[:, k:k + 1] * _unpack_rows(yk_ref[k])
    o_ref[...] = _rms(x1_ref[...] + gt2_ref[...] * y2) * g_ref[...]


def _final(x1, yk, gates, row_off, mod, per_token, tokens_per_seq, g_final):
    tm = 256
    m = x1.shape[0]
    off = row_off // tm
    return pl.pallas_call(
        _final_kernel,
        grid=(m // tm,),
        in_specs=[pl.BlockSpec((tm, D), lambda i: (i, 0)),
                  pl.BlockSpec((TOP_K, tm, PACKED), lambda i: (0, i + off, 0)),
                  pl.BlockSpec((tm, TOP_K), lambda i: (i + off, 0)),
                  _mod_spec(per_token, tm, tokens_per_seq, 5),
                  pl.BlockSpec((1, D), lambda i: (0, 0))],
        out_specs=pl.BlockSpec((tm, D), lambda i: (i, 0)),
        out_shape=jax.ShapeDtypeStruct((m, D), F32),
        compiler_params=pltpu.CompilerParams(
            dimension_semantics=("arbitrary",), vmem_limit_bytes=VMEM_LIMIT),
        name="final",
    )(x1, yk, gates, mod, g_final.reshape(1, D))


def kernel(x_prompt, x_sample, state_C, state_n, state_m, c_prompt, c_sample, w_ada, b_ada, g_mix, w_in,
           b_if, g_head, g_sgu, b_sgu, w_s, b_s, w_pa, w_pb, w_out, g_ffn, w_router, b_router, w_gu, b_gu,
           w_down, b_down, g_final):
    depth = w_ada.shape[0]
    assert depth == 1
    bp, tp, _ = x_prompt.shape
    bs, ts, _ = x_sample.shape
    mp, ms = bp * tp, bs * ts
    assert tp % ML_CHUNK == 0 and ts <= ML_CHUNK and GM_CHUNK % ts == 0

    w_in0 = w_in[0]
    nqkvo = 4 * D
    w_main = jnp.concatenate([w_in0[:, :nqkvo], w_in0[:, nqkvo + 2 * HEADS:]], axis=1).astype(BF16)
    w_if = jnp.pad(w_in0[:, nqkvo:nqkvo + 2 * HEADS], ((0, 0), (0, LANES - 2 * HEADS)))
    b_if_p = jnp.pad(b_if[0], (0, LANES - 2 * HEADS)).reshape(1, LANES)
    reps = GM_CHUNK // ts
    eye_r = jnp.eye(reps, dtype=F32)
    w_s_sample = jnp.einsum("ab,gts->gatbs", eye_r, w_s[0][:, :ts, :ts]).reshape(GROUPS, GM_CHUNK, GM_CHUNK)
    b_s_prompt = b_s[0].T
    b_s_sample = jnp.tile(b_s[0][:, :ts].T, (reps, 1))
    mix_p = {
        "g_sgu": g_sgu[0].reshape(1, D), "b_sgu": b_sgu[0].reshape(1, D),
        "w_pa": w_pa[0].astype(BF16), "w_pb": w_pb[0].astype(BF16), "w_out": w_out[0].astype(BF16),
        "g_ffn": g_ffn[0].reshape(1, D),
        "w_router": jnp.pad(w_router[0], ((0, 0), (0, LANES - N_EXPERTS))),
        "b_router": jnp.pad(b_router[0], (0, LANES - N_EXPERTS)).reshape(1, LANES),
    }
    mix_prompt = dict(mix_p, w_s=w_s[0], b_s=b_s_prompt)
    mix_sample = dict(mix_p, w_s=w_s_sample, b_s=b_s_sample)

    mod = _ada(jnp.concatenate([c_prompt, c_sample], axis=0), w_ada[0], b_ada[0])
    mod_p = mod[:bp].reshape(bp, 1, N_MOD * D)
    mod_s = jnp.repeat(mod[bp:], ts, axis=0)

    xp = x_prompt.reshape(mp, D)
    xs = x_sample.reshape(ms, D)
    z_p, _, gt_p = _in_proj(xp, mod_p, False, tp, g_mix[0], w_main, w_if, b_if_p, 512, 2048, BF16)
    z_s, gc_s, _ = _in_proj(xs, mod_s, True, ts, g_mix[0], w_main, w_if, b_if_p, 256, 2048, F32)

    hm_p, C_p, n_p, m_p = _mlstm_prompt(z_p, gt_p, g_head[0], bp, tp)
    m0_tok = jnp.repeat(state_m[0], ts, axis=0)
    hm_s, C_s, n_s, m_s = _mlstm_sample(z_s, gc_s, m0_tok, state_C[0], state_n[0], g_head[0], bs, ts)

    x1_p, h2_p, lg_p, _ = _mix(xp, hm_p, z_p, mod_p, False, tp, 256, mix_prompt)
    x1_s, h2_s, lg_s, vg_s = _mix(xs, hm_s, z_s, mod_s, True, ts, 128, mix_sample)

    n_tok = mp + ms
    n_blocks = -(-(n_tok * TOP_K + N_EXPERTS * (MOE_ROWS - 1)) // MOE_ROWS)
    dest, gates, table, pend = _route(jnp.concatenate([lg_p, lg_s], axis=0), n_blocks)
    dest_kmajor = dest.T.reshape(TOP_K * n_tok)
    n_used = pend[0, N_EXPERTS - 1:N_EXPERTS].astype(jnp.int32) // MOE_ROWS
    xslots = _sc_scatter_rows(h2_p, h2_s, dest_kmajor, n_blocks * MOE_ROWS)
    yb = _moe(xslots, table[:, 0], table[:, 1], n_used, w_gu[0], b_gu[0], w_down[0], b_down[0])
    yk = _sc_gather_rows(yb, dest_kmajor).reshape(TOP_K, n_tok, PACKED)

    y_p = _final(x1_p, yk, gates, 0, mod_p, False, tp, g_final)
    y_s = _final(x1_s, yk, gates, mp, mod_s, True, ts, g_final)

    return (y_p.reshape(bp, tp, D), y_s.reshape(bs, ts, D),
            C_p[None], n_p.reshape(1, bp, HEADS, HD), m_p[:, :, 0, 0][None],
            C_s[None], n_s.reshape(1, bs, HEADS, HD),
            m_s.reshape(bs, ts, HEADS, LANES)[:, 0, :, 0][None],
            vg_s.reshape(1, bs, ts, D))
```

```python
import functools

import jax
import jax.numpy as jnp
from jax import lax
from jax.experimental import pallas as pl
from jax.experimental.pallas import tpu as pltpu
from jax.experimental.pallas import tpu_sc as plsc

F32 = jnp.float32
BF16 = jnp.bfloat16

D = 1024
HEADS = 4
HD = D // HEADS
ML_CHUNK = 128
GROUPS = 4
GD = D // GROUPS
GM_CHUNK = 128
N_EXPERTS = 32
TOP_K = 4
D_FF = D
SWIGLU_LIMIT = 7.0
SWIGLU_ALPHA = 1.702
NORM_EPS = 1e-6
N_MOD = 6
PACKED = D // 2
LANES = 128
SC_CORES = 2
SC_SUBCORES = 16
SC_WORKERS = SC_CORES * SC_SUBCORES
MOE_ROWS = 256
SAMPLE_SEQS = 4
SC_SCATTER_CHUNK = 64
SC_GATHER_CHUNK = 96
VMEM_LIMIT = 48 * 1024 * 1024
MOE_VMEM_LIMIT = 56 * 1024 * 1024


def _dot(a, b):
    return jnp.dot(a, b, preferred_element_type=F32)


def _dot_nt(a, b):
    return lax.dot_general(a, b, (((1,), (1,)), ((), ())), preferred_element_type=F32)


def _dot_tn(a, b):
    return lax.dot_general(a, b, (((0,), (0,)), ((), ())), preferred_element_type=F32)


def _split_bf16(a):
    hi = a.astype(BF16)
    lo = (a - hi.astype(F32)).astype(BF16)
    return hi, lo


def _dot3(a, b):
    ah, al = _split_bf16(a)
    bh, bl = _split_bf16(b)
    return _dot(ah, bh) + (_dot(ah, bl) + _dot(al, bh))


def _log_sigmoid(x):
    return jnp.minimum(x, 0.0) - jnp.log1p(jnp.exp(-jnp.abs(x)))


def _rms(x):
    return x * lax.rsqrt(jnp.mean(x * x, axis=-1, keepdims=True) + NORM_EPS)


def _pack_rows(x):
    bits = lax.bitcast_convert_type(x.astype(BF16).astype(F32), jnp.uint32)
    word = (bits[:, :PACKED] & jnp.uint32(0xFFFF0000)) | (bits[:, PACKED:] >> 16)
    return lax.bitcast_convert_type(word, jnp.int32)


def _unpack_rows(w):
    bits = lax.bitcast_convert_type(w, jnp.uint32)
    left = lax.bitcast_convert_type(bits & jnp.uint32(0xFFFF0000), F32)
    right = lax.bitcast_convert_type(bits << 16, F32)
    return jnp.concatenate([left, right], axis=1)


def _mod_spec(per_token, tm, tokens_per_seq, col):
    if per_token:
        return pl.BlockSpec((tm, D), lambda i, *_: (i, col))
    return pl.BlockSpec((None, 1, D), lambda i, *_: ((i * tm) // tokens_per_seq, 0, col))


def _ada_kernel(c_ref, w_ref, b_ref, o_ref):
    c = c_ref[...]
    s = (c * jax.nn.sigmoid(c)).astype(BF16)
    o_ref[...] = _dot(s, w_ref[...].astype(BF16)) + b_ref[...]


def _ada(c, w, b):
    m, n = c.shape[0], w.shape[1]
    tn = 512
    return pl.pallas_call(
        _ada_kernel,
        grid=(n // tn,),
        in_specs=[pl.BlockSpec((m, D), lambda j: (0, 0)),
                  pl.BlockSpec((D, tn), lambda j: (0, j)),
                  pl.BlockSpec((1, tn), lambda j: (0, j))],
        out_specs=pl.BlockSpec((m, tn), lambda j: (0, j)),
        out_shape=jax.ShapeDtypeStruct((m, n), F32),
        name="ada",
    )(c, w, b.reshape(1, n))


def _in_kernel(x_ref, g_ref, sh_ref, sc_ref, w_ref, wif_ref, bif_ref, z_ref, gc_ref, gt_ref, h_scr):
    @pl.when(pl.program_id(1) == 0)
    def _():
        h = (_rms(x_ref[...]) * g_ref[...]) * (1.0 + sc_ref[...]) + sh_ref[...]
        h_scr[...] = h.astype(BF16)
        gates = _dot3(h, wif_ref[...]) + bif_ref[...]
        gc_ref[...] = gates
        gt_ref[...] = gates.T[:2 * HEADS, :]

    z_ref[...] = _dot(h_scr[...], w_ref[...]).astype(z_ref.dtype)


def _in_proj(x, mod, per_token, tokens_per_seq, g_mix, w_main, w_if, b_if, tm, tn, z_dtype):
    m = x.shape[0]
    n = w_main.shape[1]
    return pl.pallas_call(
        _in_kernel,
        grid=(m // tm, n // tn),
        in_specs=[pl.BlockSpec((tm, D), lambda i, j: (i, 0)),
                  pl.BlockSpec((1, D), lambda i, j: (0, 0)),
                  _mod_spec(per_token, tm, tokens_per_seq, 0),
                  _mod_spec(per_token, tm, tokens_per_seq, 1),
                  pl.BlockSpec((D, tn), lambda i, j: (0, j)),
                  pl.BlockSpec((D, LANES), lambda i, j: (0, 0)),
                  pl.BlockSpec((1, LANES), lambda i, j: (0, 0))],
        out_specs=[pl.BlockSpec((tm, tn), lambda i, j: (i, j)),
                   pl.BlockSpec((tm, LANES), lambda i, j: (i, 0)),
                   pl.BlockSpec((2 * HEADS, tm), lambda i, j: (0, i))],
        out_shape=[jax.ShapeDtypeStruct((m, n), z_dtype),
                   jax.ShapeDtypeStruct((m, LANES), F32),
                   jax.ShapeDtypeStruct((2 * HEADS, m), F32)],
        scratch_shapes=[pltpu.VMEM((tm, D), BF16)],
        compiler_params=pltpu.CompilerParams(
            dimension_semantics=("arbitrary", "arbitrary"), vmem_limit_bytes=VMEM_LIMIT),
        name="in_proj",
    )(x, g_mix.reshape(1, D), mod, mod, w_main, w_if, b_if)


def _mlstm_prompt_kernel(q_ref, k_ref, v_ref, o_ref, gt_ref, gh_ref, hm_ref, C_ref, n_ref, m_ref):
    L = q_ref.shape[0]

    @pl.when(pl.program_id(1) == 0)
    def _():
        C_ref[...] = jnp.zeros_like(C_ref)
        n_ref[...] = jnp.zeros_like(n_ref)
        m_ref[...] = jnp.zeros_like(m_ref)

    gates = gt_ref[...]
    r = lax.broadcasted_iota(jnp.int32, (L, L), 0)
    s = lax.broadcasted_iota(jnp.int32, (L, L), 1)
    eye = r == s
    causal = s <= r

    def to_col(x_row):
        return jnp.sum(jnp.where(eye, x_row, 0.0), axis=1, keepdims=True)

    for h in range(HEADS):
        cols = slice(h * HD, (h + 1) * HD)
        ig_row = gates[h:h + 1, :]
        lf_row = _log_sigmoid(gates[HEADS + h:HEADS + h + 1, :])
        lf_col = to_col(lf_row)
        b_row = jnp.sum(jnp.where(r <= s, lf_col, 0.0), axis=0, keepdims=True)
        b_col = to_col(b_row)
        m_prev = m_ref[h][:, :1]

        logD = jnp.where(causal, b_col - b_row + ig_row, -jnp.inf)
        inter = b_col + m_prev
        mt = jnp.maximum(jnp.max(logD, axis=1, keepdims=True), inter)
        q = q_ref[:, cols]
        ks = k_ref[:, cols] * (HD ** -0.5)
        v = v_ref[:, cols]
        S = _dot_nt(q, ks) * jnp.exp(logD - mt)
        w_int = jnp.exp(inter - mt)
        Cmat = C_ref[h]
        nvec = n_ref[h]
        num = _dot(S.astype(BF16), v) + w_int * _dot_nt(q, Cmat.astype(BF16))
        nq = jnp.sum(q.astype(F32) * nvec, axis=1, keepdims=True)
        den = jnp.sum(S, axis=1, keepdims=True) + w_int * nq
        hh = num / jnp.maximum(jnp.abs(den), jnp.exp(-mt))
        hg = jax.nn.sigmoid(o_ref[:, cols].astype(F32)) * hh
        hm_ref[:, cols] = (_rms(hg) * gh_ref[h]).astype(hm_ref.dtype)

        bL = b_row[:, L - 1:L]
        g_row = bL - b_row + ig_row
        m_new = jnp.maximum(bL + m_prev, jnp.max(g_row, axis=1, keepdims=True))
        w_old = jnp.exp(bL + m_prev - m_new)
        kw = ks.astype(F32) * to_col(jnp.exp(g_row - m_new))
        C_ref[h] = w_old * Cmat + _dot_tn(v, kw.astype(BF16))
        n_ref[h] = w_old * nvec + jnp.sum(kw, axis=0, keepdims=True)
        m_ref[h] = jnp.broadcast_to(m_new, (1, LANES))


def _mlstm_prompt(z, gates_t, g_head, batch, seq):
    nc = seq // ML_CHUNK
    m = batch * seq

    def zspec(col):
        return pl.BlockSpec((ML_CHUNK, D), lambda b, c: (b * nc + c, col))

    return pl.pallas_call(
        _mlstm_prompt_kernel,
        grid=(batch, nc),
        in_specs=[zspec(0), zspec(1), zspec(2), zspec(3),
                  pl.BlockSpec((2 * HEADS, ML_CHUNK), lambda b, c: (0, b * nc + c)),
                  pl.BlockSpec((HEADS, 1, HD), lambda b, c: (0, 0, 0))],
        out_specs=[pl.BlockSpec((ML_CHUNK, D), lambda b, c: (b * nc + c, 0)),
                   pl.BlockSpec((None, HEADS, HD, HD), lambda b, c: (b, 0, 0, 0)),
                   pl.BlockSpec((None, HEADS, 1, HD), lambda b, c: (b, 0, 0, 0)),
                   pl.BlockSpec((None, HEADS, 1, LANES), lambda b, c: (b, 0, 0, 0))],
        out_shape=[jax.ShapeDtypeStruct((m, D), BF16),
                   jax.ShapeDtypeStruct((batch, HEADS, HD, HD), F32),
                   jax.ShapeDtypeStruct((batch, HEADS, 1, HD), F32),
                   jax.ShapeDtypeStruct((batch, HEADS, 1, LANES), F32)],
        compiler_params=pltpu.CompilerParams(dimension_semantics=("arbitrary", "arbitrary")),
        name="mlstm_prompt",
    )(z, z, z, z, gates_t, g_head.reshape(HEADS, 1, HD))


def _mlstm_sample_kernel(seq_len, q_ref, k_ref, v_ref, o_ref, gc_ref, m0_ref, C0_ref, n0_ref, gh_ref,
                         hm_ref, C_ref, n_ref, m_ref):
    R = q_ref.shape[0]
    nseq = R // seq_len
    r = lax.broadcasted_iota(jnp.int32, (R, R), 0)
    s = lax.broadcasted_iota(jnp.int32, (R, R), 1)
    rseq = lax.broadcasted_iota(jnp.int32, (R, 1), 0) // seq_len
    eye = r == s
    same = (r // seq_len) == (s // seq_len)
    causal = same & (s <= r)

    def to_row(x_col):
        return jnp.sum(jnp.where(eye, x_col, 0.0), axis=0, keepdims=True)

    gc = gc_ref[...]
    for h in range(HEADS):
        cols = slice(h * HD, (h + 1) * HD)
        ig_col = gc[:, h:h + 1]
        lf_col = _log_sigmoid(gc[:, HEADS + h:HEADS + h + 1])
        lf_row = to_row(lf_col)
        b_col = jnp.sum(jnp.where(causal, lf_row, 0.0), axis=1, keepdims=True)
        bL_col = jnp.sum(jnp.where(same, lf_row, 0.0), axis=1, keepdims=True)
        b_row = to_row(b_col)
        ig_row = to_row(ig_col)
        m0_col = m0_ref[:, h:h + 1]

        logD = jnp.where(causal, b_col - b_row + ig_row, -jnp.inf)
        inter = b_col + m0_col
        mt = jnp.maximum(jnp.max(logD, axis=1, keepdims=True), inter)
        qf = q_ref[:, cols]
        q = qf.astype(BF16)
        ksf = k_ref[:, cols] * (HD ** -0.5)
        v = v_ref[:, cols].astype(BF16)
        S = _dot_nt(q, ksf.astype(BF16)) * jnp.exp(logD - mt)
        w_int = jnp.exp(inter - mt)

        Cq = jnp.zeros((R, HD), F32)
        nq = jnp.zeros((R, 1), F32)
        for g in range(nseq):
            Cq = jnp.where(rseq == g, _dot_nt(q, C0_ref[g, h].astype(BF16)), Cq)
            nq = jnp.where(rseq == g, jnp.sum(qf * n0_ref[g, h], axis=1, keepdims=True), nq)
        num = _dot(S.astype(BF16), v) + w_int * Cq
        den = jnp.sum(S, axis=1, keepdims=True) + w_int * nq
        hh = num / jnp.maximum(jnp.abs(den), jnp.exp(-mt))
        hg = jax.nn.sigmoid(o_ref[:, cols]) * hh
        hm_ref[:, cols] = (_rms(hg) * gh_ref[h]).astype(hm_ref.dtype)

        g_col = bL_col - b_col + ig_col
        gmax_col = jnp.max(jnp.where(same, to_row(g_col), -jnp.inf), axis=1, keepdims=True)
        m_new_col = jnp.maximum(bL_col + m0_col, gmax_col)
        w_old_col = jnp.exp(bL_col + m0_col - m_new_col)
        kw = ksf * jnp.exp(g_col - m_new_col)
        for g in range(nseq):
            kw_g = jnp.where(rseq == g, kw, 0.0)
            w_old = w_old_col[g * seq_len:g * seq_len + 1, :]
            C_ref[g, h] = w_old * C0_ref[g, h] + _dot_tn(v, kw_g.astype(BF16))
            n_ref[g, h] = w_old * n0_ref[g, h] + jnp.sum(kw_g, axis=0, keepdims=True)
        m_ref[:, h * LANES:(h + 1) * LANES] = jnp.broadcast_to(m_new_col, (R, LANES))


def _mlstm_sample(z, gates_c, m0_tok, C0, n0, g_head, batch, seq):
    rows = SAMPLE_SEQS * seq
    m = batch * seq

    def zspec(col):
        return pl.BlockSpec((rows, D), lambda i: (i, col))

    state_c = pl.BlockSpec((SAMPLE_SEQS, HEADS, HD, HD), lambda i: (i, 0, 0, 0))
    state_n = pl.BlockSpec((SAMPLE_SEQS, HEADS, 1, HD), lambda i: (i, 0, 0, 0))
    return pl.pallas_call(
        functools.partial(_mlstm_sample_kernel, seq),
        grid=(batch // SAMPLE_SEQS,),
        in_specs=[zspec(0), zspec(1), zspec(2), zspec(3),
                  pl.BlockSpec((rows, LANES), lambda i: (i, 0)),
                  pl.BlockSpec((rows, HEADS), lambda i: (i, 0)),
                  state_c, state_n,
                  pl.BlockSpec((HEADS, 1, HD), lambda i: (0, 0, 0))],
        out_specs=[pl.BlockSpec((rows, D), lambda i: (i, 0)),
                   state_c, state_n,
                   pl.BlockSpec((rows, HEADS * LANES), lambda i: (i, 0))],
        out_shape=[jax.ShapeDtypeStruct((m, D), BF16),
                   jax.ShapeDtypeStruct((batch, HEADS, HD, HD), F32),
                   jax.ShapeDtypeStruct((batch, HEADS, 1, HD), F32),
                   jax.ShapeDtypeStruct((m, HEADS * LANES), F32)],
        compiler_params=pltpu.CompilerParams(
            dimension_semantics=("arbitrary",), vmem_limit_bytes=VMEM_LIMIT),
        name="mlstm_sample",
    )(z, z, z, z, gates_c, m0_tok, C0, n0.reshape(batch, HEADS, 1, HD), g_head.reshape(HEADS, 1, HD))


def _mix_kernel(x_ref, hm_ref, u_ref, v_ref, ga_ref, gb_ref, gt1_ref, sh2_ref, sc2_ref,
                gsgu_ref, bsgu_ref, ws_ref, bs_ref, wpa_ref, wpb_ref, wout_ref, gffn_ref,
                wr_ref, br_ref, x1_ref, h2_ref, choice_ref, gate_ref, vg_ref, cnt_ref, yg_scr):
    tm = x_ref.shape[0]
    u = jax.nn.gelu(u_ref[...].astype(F32))
    vv = jax.nn.gelu(v_ref[...].astype(F32))
    mu = jnp.mean(vv, axis=-1, keepdims=True)
    var = jnp.mean(jnp.square(vv - mu), axis=-1, keepdims=True)
    vg = (vv - mu) * lax.rsqrt(var + NORM_EPS) * gsgu_ref[...] + bsgu_ref[...]
    vg_ref[...] = vg
    vgb = vg.astype(BF16)

    r = lax.broadcasted_iota(jnp.int32, (GM_CHUNK, GM_CHUNK), 0)
    s = lax.broadcasted_iota(jnp.int32, (GM_CHUNK, GM_CHUNK), 1)
    for g in range(GROUPS):
        w = jnp.where(s <= r, ws_ref[g], 0.0).astype(BF16)
        bias = bs_ref[:, g:g + 1]
        for c in range(tm // GM_CHUNK):
            rows = slice(c * GM_CHUNK, (c + 1) * GM_CHUNK)
            cols = slice(g * GD, (g + 1) * GD)
            mixed = _dot(w, vgb[rows, cols]) + bias
            yg_scr[rows, cols] = (u[rows, cols] * mixed).astype(BF16)

    a = _dot(hm_ref[...], wpa_ref[...])
    b = _dot(yg_scr[...], wpb_ref[...])
    merged = (jax.nn.sigmoid(ga_ref[...].astype(F32)) * a
              + jax.nn.sigmoid(gb_ref[...].astype(F32)) * b)
    x1 = x_ref[...] + gt1_ref[...] * _dot(merged.astype(BF16), wout_ref[...])
    x1_ref[...] = x1
    h2 = (_rms(x1) * gffn_ref[...]) * (1.0 + sc2_ref[...]) + sh2_ref[...]
    h2_ref[...] = _pack_rows(h2)

    lane = lax.broadcasted_iota(jnp.int32, (tm, LANES), 1)
    lg = jnp.where(lane < N_EXPERTS, _dot3(h2, wr_ref[...]) + br_ref[...], -jnp.inf)
    choice = jnp.zeros((tm, LANES), F32)
    vals = []
    for k in range(TOP_K):
        mx = jnp.max(lg, axis=1, keepdims=True)
        idx = jnp.min(jnp.where(lg == mx, lane, LANES), axis=1, keepdims=True)
        sel = lane == idx
        choice = jnp.where(sel, k + 1.0, choice)
        vals.append(mx)
        lg = jnp.where(sel, -jnp.inf, lg)
    choice_ref[...] = choice
    ex = [jnp.exp(v - vals[0]) for v in vals]
    denom = sum(ex)
    gates = jnp.zeros((tm, LANES), F32)
    for k in range(TOP_K):
        gates = jnp.where(lane == k, ex[k] / denom, gates)
    gate_ref[...] = gates[:, :TOP_K]

    @pl.when(pl.program_id(0) == 0)
    def _():
        cnt_ref[...] = jnp.zeros_like(cnt_ref)

    cnt_ref[...] += jnp.sum(jnp.where(choice > 0.0, 1.0, 0.0), axis=0, keepdims=True)


def _mix(x, hm, z, mod, per_token, tokens_per_seq, tm, p):
    m = x.shape[0]
    zcol = 4

    def zspec(blk):
        return pl.BlockSpec((tm, D), lambda i: (i, blk))

    def full(shape):
        return pl.BlockSpec(shape, lambda i: (0,) * len(shape))

    row = pl.BlockSpec((tm, D), lambda i: (i, 0))
    return pl.pallas_call(
        _mix_kernel,
        grid=(m // tm,),
        in_specs=[row, row, zspec(zcol), zspec(zcol + 1), zspec(zcol + 2), zspec(zcol + 3),
                  _mod_spec(per_token, tm, tokens_per_seq, 2),
                  _mod_spec(per_token, tm, tokens_per_seq, 3),
                  _mod_spec(per_token, tm, tokens_per_seq, 4),
                  full((1, D)), full((1, D)),
                  full((GROUPS, GM_CHUNK, GM_CHUNK)), full((GM_CHUNK, GROUPS)),
                  full((D, D)), full((D, D)), full((D, D)), full((1, D)),
                  full((D, LANES)), full((1, LANES))],
        out_specs=[row, pl.BlockSpec((tm, PACKED), lambda i: (i, 0)),
                   pl.BlockSpec((tm, LANES), lambda i: (i, 0)), pl.BlockSpec((tm, TOP_K), lambda i: (i, 0)),
                   row, full((1, LANES))],
        out_shape=[jax.ShapeDtypeStruct((m, D), F32),
                   jax.ShapeDtypeStruct((m, PACKED), jnp.int32),
                   jax.ShapeDtypeStruct((m, LANES), F32),
                   jax.ShapeDtypeStruct((m, TOP_K), F32),
                   jax.ShapeDtypeStruct((m, D), F32),
                   jax.ShapeDtypeStruct((1, LANES), F32)],
        scratch_shapes=[pltpu.VMEM((tm, D), BF16)],
        compiler_params=pltpu.CompilerParams(
            dimension_semantics=("arbitrary",), vmem_limit_bytes=VMEM_LIMIT),
        name="mix",
    )(x, hm, z, z, z, z, mod, mod, mod, p["g_sgu"], p["b_sgu"], p["w_s"], p["b_s"],
      p["w_pa"], p["w_pb"], p["w_out"], p["g_ffn"], p["w_router"], p["b_router"])


def _route_kernel(choice_ref, cnta_ref, cntb_ref, dest_ref, be_ref, pend_ref, base_scr):
    tm = choice_ref.shape[0]
    lane = lax.broadcasted_iota(jnp.int32, (tm, LANES), 1)
    choice = choice_ref[...]
    onehot = jnp.where(choice > 0.0, 1.0, 0.0)

    @pl.when(pl.program_id(0) == 0)
    def _():
        cnt = cnta_ref[...] + cntb_ref[...]
        padded = jnp.floor((cnt + (MOE_ROWS - 1)) * (1.0 / MOE_ROWS)) * MOE_ROWS
        r = lax.broadcasted_iota(jnp.int32, (LANES, LANES), 0)
        s = lax.broadcasted_iota(jnp.int32, (LANES, LANES), 1)
        padded_col = jnp.sum(jnp.where(r == s, padded, 0.0), axis=1, keepdims=True)
        pstart = jnp.sum(jnp.where(r < s, padded_col, 0.0), axis=0, keepdims=True)
        pend = pstart + padded
        base_scr[...] = pstart
        pend_ref[...] = pend
        nbp = be_ref.shape[0]
        bstart = (lax.broadcasted_iota(jnp.int32, (nbp, LANES), 0) * MOE_ROWS).astype(F32)
        elane = lax.broadcasted_iota(jnp.int32, (nbp, LANES), 1)
        be = jnp.sum(jnp.where((elane < N_EXPERTS) & (pend <= bstart), 1.0, 0.0), axis=1, keepdims=True)
        be = jnp.minimum(be, N_EXPERTS - 1.0)
        owns = (elane < N_EXPERTS) & (pstart <= bstart) & (bstart < pend)
        last = jnp.sum(jnp.where(owns, pstart + cnt, 0.0), axis=1, keepdims=True)
        valid = jnp.clip(last - bstart[:, :1], 0.0, MOE_ROWS)
        be_ref[...] = jnp.where(elane == 0, be, jnp.where(elane == 1, valid, 0.0)).astype(jnp.int32)

    r = lax.broadcasted_iota(jnp.int32, (tm, tm), 0)
    s = lax.broadcasted_iota(jnp.int32, (tm, tm), 1)
    before = _dot((s < r).astype(BF16), onehot.astype(BF16))
    slot = before + base_scr[...]
    dest = jnp.zeros((tm, LANES), F32)
    for k in range(TOP_K):
        d_k = jnp.sum(jnp.where(choice == k + 1.0, slot, 0.0), axis=1, keepdims=True)
        dest = jnp.where(lane == k, d_k, dest)
    dest_ref[...] = dest[:, :TOP_K].astype(jnp.int32)
    base_scr[...] += jnp.sum(onehot, axis=0, keepdims=True)


def _route(choice, cnt_a, cnt_b, n_blocks):
    n = choice.shape[0]
    tm = 256
    fixed = lambda i: (0, 0)
    return pl.pallas_call(
        _route_kernel,
        grid=(n // tm,),
        in_specs=[pl.BlockSpec((tm, LANES), lambda i: (i, 0)),
                  pl.BlockSpec((1, LANES), fixed), pl.BlockSpec((1, LANES), fixed)],
        out_specs=[pl.BlockSpec((tm, TOP_K), lambda i: (i, 0)),
                   pl.BlockSpec((n_blocks, LANES), fixed), pl.BlockSpec((1, LANES), fixed)],
        out_shape=[jax.ShapeDtypeStruct((n, TOP_K), jnp.int32),
                   jax.ShapeDtypeStruct((n_blocks, LANES), jnp.int32),
                   jax.ShapeDtypeStruct((1, LANES), F32)],
        scratch_shapes=[pltpu.VMEM((1, LANES), F32)],
        compiler_params=pltpu.CompilerParams(dimension_semantics=("arbitrary",)),
        name="route",
    )(choice, cnt_a, cnt_b)


def _sc_worker_base(per_worker):
    return (lax.axis_index("s") * SC_CORES + lax.axis_index("c")) * per_worker


def _sc_scatter_rows(rows_a, rows_b, idx_flat, n_out):
    na, nb = rows_a.shape[0], rows_b.shape[0]
    width, dtype = rows_a.shape[1], rows_a.dtype
    n = na + nb
    per_a, per_b = na // SC_WORKERS, nb // SC_WORKERS
    chunk = SC_SCATTER_CHUNK
    n_chunks = per_a // chunk
    assert per_a * SC_WORKERS == na and per_b * SC_WORKERS == nb and per_b % 8 == 0 and per_b <= chunk
    assert n_chunks * chunk == per_a and n_chunks % 2 == 0
    mesh = plsc.VectorSubcoreMesh(core_axis_name="c", subcore_axis_name="s")

    @functools.partial(
        pl.kernel, mesh=mesh,
        out_type=jax.ShapeDtypeStruct((n_out, width), dtype),
        scratch_types=[pltpu.VMEM((chunk,), jnp.int32)] * TOP_K + [pltpu.VMEM((per_b,), jnp.int32)]
                      + [pltpu.VMEM((chunk, width), dtype), pltpu.VMEM((chunk, width), dtype),
                         pltpu.VMEM((per_b, width), dtype)]
                      + [pltpu.SemaphoreType.DMA] * 3,
    )
    def scatter(a_hbm, b_hbm, idx_hbm, out_hbm, i0, i1, i2, i3, ib, rows0, rows1, rowsb, rsem0, rsem1, wsem):
        base = _sc_worker_base(per_a)
        idx_bufs = (i0, i1, i2, i3)
        bufs = ((rows0, rsem0), (rows1, rsem1))

        def off(j):
            return pl.multiple_of(base + j * chunk, 8)

        def read(j, buf):
            rows_v, sem = buf
            return pltpu.make_async_copy(a_hbm.at[pl.ds(off(j), chunk)], rows_v, sem)

        def spread(j, buf):
            rows_v, _ = buf
            read(j, buf).wait()
            for k in range(TOP_K):
                pltpu.sync_copy(idx_hbm.at[pl.ds(pl.multiple_of(k * n + off(j), 8), chunk)], idx_bufs[k])
            for k in range(TOP_K):
                pltpu.make_async_copy(rows_v, out_hbm.at[idx_bufs[k]], wsem).start()
            for k in range(TOP_K):
                pltpu.make_async_copy(rows_v, out_hbm.at[idx_bufs[k]], wsem).wait()

        read(0, bufs[0]).start()

        @pl.loop(0, n_chunks, step=2)
        def _(j):
            read(j + 1, bufs[1]).start()
            spread(j, bufs[0])

            @pl.when(j + 2 < n_chunks)
            def _():
                read(j + 2, bufs[0]).start()
            spread(j + 1, bufs[1])

        off_b = pl.multiple_of(_sc_worker_base(per_b), 8)
        pltpu.sync_copy(b_hbm.at[pl.ds(off_b, per_b)], rowsb)
        for k in range(TOP_K):
            pltpu.sync_copy(idx_hbm.at[pl.ds(pl.multiple_of(k * n + na + off_b, 8), per_b)], ib)
            pltpu.async_copy(rowsb, out_hbm.at[ib], wsem).wait()

    return scatter(rows_a, rows_b, idx_flat)


def _sc_gather_rows(table, idx_flat):
    b = idx_flat.shape[0]
    width, dtype = table.shape[1], table.dtype
    per_worker = b // SC_WORKERS
    chunk = SC_GATHER_CHUNK
    n_chunks = per_worker // chunk
    assert per_worker * SC_WORKERS == b and n_chunks * chunk == per_worker and n_chunks % 2 == 0
    mesh = plsc.VectorSubcoreMesh(core_axis_name="c", subcore_axis_name="s")

    @functools.partial(
        pl.kernel, mesh=mesh,
        out_type=jax.ShapeDtypeStruct((b, width), dtype),
        scratch_types=[pltpu.VMEM((chunk,), jnp.int32), pltpu.VMEM((chunk,), jnp.int32),
                       pltpu.VMEM((chunk, width), dtype), pltpu.VMEM((chunk, width), dtype),
                       pltpu.SemaphoreType.DMA, pltpu.SemaphoreType.DMA],
    )
    def gather(table_hbm, idx_hbm, out_hbm, idx0, idx1, rows0, rows1, sem0, sem1):
        base = _sc_worker_base(per_worker)
        bufs = ((idx0, rows0, sem0), (idx1, rows1, sem1))

        def off(j):
            return pl.multiple_of(base + j * chunk, 8)

        def start(j, buf):
            idx_v, rows_v, sem = buf
            pltpu.sync_copy(idx_hbm.at[pl.ds(off(j), chunk)], idx_v)
            pltpu.make_async_copy(table_hbm.at[idx_v], rows_v, sem).start()

        def finish(j, buf):
            idx_v, rows_v, sem = buf
            pltpu.make_async_copy(table_hbm.at[idx_v], rows_v, sem).wait()
            pltpu.sync_copy(rows_v, out_hbm.at[pl.ds(off(j), chunk)])

        start(0, bufs[0])

        @pl.loop(0, n_chunks, step=2)
        def _(j):
            start(j + 1, bufs[1])
            finish(j, bufs[0])

            @pl.when(j + 2 < n_chunks)
            def _():
                start(j + 2, bufs[0])
            finish(j + 1, bufs[1])

    return gather(table, idx_flat)


def _moe_kernel(be_ref, valid_ref, nused_ref, x_ref, wgu_ref, bgu_ref, wd_ref, bd_ref, o_ref, wgu_bf, wd_bf):
    i = pl.program_id(0)

    @pl.when((i == 0) | (be_ref[i] != be_ref[jnp.maximum(i - 1, 0)]))
    def _():
        wgu_bf[...] = wgu_ref[...].astype(BF16)
        wd_bf[...] = wd_ref[...].astype(BF16)

    @pl.when(i < nused_ref[0])
    def _():
        row = lax.broadcasted_iota(jnp.int32, (MOE_ROWS, 1), 0)
        x = _unpack_rows(jnp.where(row < valid_ref[i], x_ref[...], 0)).astype(BF16)
        gu = _dot(x, wgu_bf[...]) + bgu_ref[...]
        gate = jnp.minimum(gu[:, :D_FF], SWIGLU_LIMIT)
        up = jnp.clip(gu[:, D_FF:], -SWIGLU_LIMIT, SWIGLU_LIMIT)
        act = gate * jax.nn.sigmoid(SWIGLU_ALPHA * gate) * (up + 1.0)
        o_ref[...] = _pack_rows(_dot(act.astype(BF16), wd_bf[...]) + bd_ref[...])

    @pl.when(i >= nused_ref[0])
    def _():
        o_ref[...] = jnp.zeros_like(o_ref)


def _moe(xs, block_e, block_valid, n_used, w_gu, b_gu, w_down, b_down):
    rows = xs.shape[0]
    nb = rows // MOE_ROWS
    grid_spec = pltpu.PrefetchScalarGridSpec(
        num_scalar_prefetch=3,
        grid=(nb,),
        in_specs=[pl.BlockSpec((MOE_ROWS, PACKED), lambda i, be, va, nu: (jnp.minimum(i, nu[0] - 1), 0)),
                  pl.BlockSpec((None, D, 2 * D_FF), lambda i, be, va, nu: (be[i], 0, 0)),
                  pl.BlockSpec((None, 1, 2 * D_FF), lambda i, be, va, nu: (be[i], 0, 0)),
                  pl.BlockSpec((None, D_FF, D), lambda i, be, va, nu: (be[i], 0, 0)),
                  pl.BlockSpec((None, 1, D), lambda i, be, va, nu: (be[i], 0, 0))],
        out_specs=pl.BlockSpec((MOE_ROWS, PACKED), lambda i, be, va, nu: (i, 0)),
        scratch_shapes=[pltpu.VMEM((D, 2 * D_FF), BF16), pltpu.VMEM((D_FF, D), BF16)],
    )
    return pl.pallas_call(
        _moe_kernel,
        grid_spec=grid_spec,
        out_shape=jax.ShapeDtypeStruct((rows, PACKED), jnp.int32),
        compiler_params=pltpu.CompilerParams(
            dimension_semantics=("arbitrary",), vmem_limit_bytes=MOE_VMEM_LIMIT),
        name="moe",
    )(block_e, block_valid, n_used, xs, w_gu, b_gu.reshape(N_EXPERTS, 1, 2 * D_FF), w_down,
      b_down.reshape(N_EXPERTS, 1, D))


def _final_kernel(x1_ref, yk_ref, gate_ref, gt2_ref, g_ref, o_ref):
    y2 = gate_ref[:, 0:1] * _unpack_rows(yk_ref[0])
    for k in range(1, TOP_K):
        y2 = y2 + gate_ref[:, k:k + 1] * _unpack_rows(yk_ref[k])
    o_ref[...] = _rms(x1_ref[...] + gt2_ref[...] * y2) * g_ref[...]


def _final(x1, yk, gates, row_off, mod, per_token, tokens_per_seq, g_final):
    tm = 256
    m = x1.shape[0]
    off = row_off // tm
    return pl.pallas_call(
        _final_kernel,
        grid=(m // tm,),
        in_specs=[pl.BlockSpec((tm, D), lambda i: (i, 0)),
                  pl.BlockSpec((TOP_K, tm, PACKED), lambda i: (0, i + off, 0)),
                  pl.BlockSpec((tm, TOP_K), lambda i: (i, 0)),
                  _mod_spec(per_token, tm, tokens_per_seq, 5),
                  pl.BlockSpec((1, D), lambda i: (0, 0))],
        out_specs=pl.BlockSpec((tm, D), lambda i: (i, 0)),
        out_shape=jax.ShapeDtypeStruct((m, D), F32),
        compiler_params=pltpu.CompilerParams(
            dimension_semantics=("arbitrary",), vmem_limit_bytes=VMEM_LIMIT),
        name="final",
    )(x1, yk, gates, mod, g_final.reshape(1, D))


def kernel(x_prompt, x_sample, state_C, state_n, state_m, c_prompt, c_sample, w_ada, b_ada, g_mix, w_in,
           b_if, g_head, g_sgu, b_sgu, w_s, b_s, w_pa, w_pb, w_out, g_ffn, w_router, b_router, w_gu, b_gu,
           w_down, b_down, g_final):
    depth = w_ada.shape[0]
    assert depth == 1
    bp, tp, _ = x_prompt.shape
    bs, ts, _ = x_sample.shape
    mp, ms = bp * tp, bs * ts
    assert tp % ML_CHUNK == 0 and ts <= ML_CHUNK and GM_CHUNK % ts == 0

    w_in0 = w_in[0]
    nqkvo = 4 * D
    w_main = jnp.concatenate([w_in0[:, :nqkvo], w_in0[:, nqkvo + 2 * HEADS:]], axis=1).astype(BF16)
    w_if = jnp.pad(w_in0[:, nqkvo:nqkvo + 2 * HEADS], ((0, 0), (0, LANES - 2 * HEADS)))
    b_if_p = jnp.pad(b_if[0], (0, LANES - 2 * HEADS)).reshape(1, LANES)
    reps = GM_CHUNK // ts
    eye_r = jnp.eye(reps, dtype=F32)
    w_s_sample = jnp.einsum("ab,gts->gatbs", eye_r, w_s[0][:, :ts, :ts]).reshape(GROUPS, GM_CHUNK, GM_CHUNK)
    b_s_prompt = b_s[0].T
    b_s_sample = jnp.tile(b_s[0][:, :ts].T, (reps, 1))
    mix_p = {
        "g_sgu": g_sgu[0].reshape(1, D), "b_sgu": b_sgu[0].reshape(1, D),
        "w_pa": w_pa[0].astype(BF16), "w_pb": w_pb[0].astype(BF16), "w_out": w_out[0].astype(BF16),
        "g_ffn": g_ffn[0].reshape(1, D),
        "w_router": jnp.pad(w_router[0], ((0, 0), (0, LANES - N_EXPERTS))),
        "b_router": jnp.pad(b_router[0], (0, LANES - N_EXPERTS)).reshape(1, LANES),
    }
    mix_prompt = dict(mix_p, w_s=w_s[0], b_s=b_s_prompt)
    mix_sample = dict(mix_p, w_s=w_s_sample, b_s=b_s_sample)

    mod = _ada(jnp.concatenate([c_prompt, c_sample], axis=0), w_ada[0], b_ada[0])
    mod_p = mod[:bp].reshape(bp, 1, N_MOD * D)
    mod_s = jnp.repeat(mod[bp:], ts, axis=0)

    xp = x_prompt.reshape(mp, D)
    xs = x_sample.reshape(ms, D)
    z_p, _, gt_p = _in_proj(xp, mod_p, False, tp, g_mix[0], w_main, w_if, b_if_p, 512, 2048, BF16)
    z_s, gc_s, _ = _in_proj(xs, mod_s, True, ts, g_mix[0], w_main, w_if, b_if_p, 256, 2048, F32)

    hm_p, C_p, n_p, m_p = _mlstm_prompt(z_p, gt_p, g_head[0], bp, tp)
    m0_tok = jnp.repeat(state_m[0], ts, axis=0)
    hm_s, C_s, n_s, m_s = _mlstm_sample(z_s, gc_s, m0_tok, state_C[0], state_n[0], g_head[0], bs, ts)

    x1_p, h2_p, ch_p, gates_p, _, cnt_p = _mix(xp, hm_p, z_p, mod_p, False, tp, 256, mix_prompt)
    x1_s, h2_s, ch_s, gates_s, vg_s, cnt_s = _mix(xs, hm_s, z_s, mod_s, True, ts, 128, mix_sample)

    n_tok = mp + ms
    n_blocks = -(-(n_tok * TOP_K + N_EXPERTS * (MOE_ROWS - 1)) // MOE_ROWS)
    dest, table, pend = _route(jnp.concatenate([ch_p, ch_s], axis=0), cnt_p, cnt_s, n_blocks)
    dest_kmajor = dest.T.reshape(TOP_K * n_tok)
    n_used = pend[0, N_EXPERTS - 1:N_EXPERTS].astype(jnp.int32) // MOE_ROWS
    xslots = _sc_scatter_rows(h2_p, h2_s, dest_kmajor, n_blocks * MOE_ROWS)
    yb = _moe(xslots, table[:, 0], table[:, 1], n_used, w_gu[0], b_gu[0], w_down[0], b_down[0])
    yk = _sc_gather_rows(yb, dest_kmajor).reshape(TOP_K, n_tok, PACKED)

    y_p = _final(x1_p, yk, gates_p, 0, mod_p, False, tp, g_final)
    y_s = _final(x1_s, yk, gates_s, mp, mod_s, True, ts, g_final)

    return (y_p.reshape(bp, tp, D), y_s.reshape(bs, ts, D),
            C_p[None], n_p.reshape(1, bp, HEADS, HD), m_p[:, :, 0, 0][None],
            C_s[None], n_s.reshape(1, bs, HEADS, HD),
            m_s.reshape(bs, ts, HEADS, LANES)[:, 0, :, 0][None],
            vg_s.reshape(1, bs, ts, D))
```

```python
import functools

import jax
import jax.numpy as jnp
from jax import lax
from jax.experimental import pallas as pl
from jax.experimental.pallas import tpu as pltpu
from jax.experimental.pallas import tpu_sc as plsc

F32 = jnp.float32
BF16 = jnp.bfloat16

D = 1024
HEADS = 4
HD = D // HEADS
ML_CHUNK = 128
GROUPS = 4
GD = D // GROUPS
GM_CHUNK = 128
N_EXPERTS = 32
TOP_K = 4
D_FF = D
SWIGLU_LIMIT = 7.0
SWIGLU_ALPHA = 1.702
NORM_EPS = 1e-6
N_MOD = 6
PACKED = D // 2
LANES = 128
SC_CORES = 2
SC_SUBCORES = 16
SC_WORKERS = SC_CORES * SC_SUBCORES
IN_COLS = 2048
MOE_ROWS = 256
SAMPLE_SEQS = 4
SC_SCATTER_CHUNK = 64
SC_GATHER_CHUNK = 96
VMEM_LIMIT = 48 * 1024 * 1024
MOE_VMEM_LIMIT = 56 * 1024 * 1024


def _dot(a, b):
    return jnp.dot(a, b, preferred_element_type=F32)


def _dot_nt(a, b):
    return lax.dot_general(a, b, (((1,), (1,)), ((), ())), preferred_element_type=F32)


def _dot_tn(a, b):
    return lax.dot_general(a, b, (((0,), (0,)), ((), ())), preferred_element_type=F32)


def _split_bf16(a):
    hi = a.astype(BF16)
    lo = (a - hi.astype(F32)).astype(BF16)
    return hi, lo


def _dot3(a, b):
    ah, al = _split_bf16(a)
    bh, bl = _split_bf16(b)
    return _dot(ah, bh) + (_dot(ah, bl) + _dot(al, bh))


def _log_sigmoid(x):
    return jnp.minimum(x, 0.0) - jnp.log1p(jnp.exp(-jnp.abs(x)))


def _rms(x):
    return x * lax.rsqrt(jnp.mean(x * x, axis=-1, keepdims=True) + NORM_EPS)


def _pack_rows(x):
    bits = lax.bitcast_convert_type(x.astype(BF16).astype(F32), jnp.uint32)
    word = (bits[:, :PACKED] & jnp.uint32(0xFFFF0000)) | (bits[:, PACKED:] >> 16)
    return lax.bitcast_convert_type(word, jnp.int32)


def _unpack_rows(w):
    bits = lax.bitcast_convert_type(w, jnp.uint32)
    left = lax.bitcast_convert_type(bits & jnp.uint32(0xFFFF0000), F32)
    right = lax.bitcast_convert_type(bits << 16, F32)
    return jnp.concatenate([left, right], axis=1)


def _mod_spec(per_token, tm, tokens_per_seq, col):
    if per_token:
        return pl.BlockSpec((tm, D), lambda i, *_: (i, col))
    return pl.BlockSpec((None, 1, D), lambda i, *_: ((i * tm) // tokens_per_seq, 0, col))


def _ada_kernel(c_ref, w_ref, b_ref, o_ref):
    c = c_ref[...]
    s = (c * jax.nn.sigmoid(c)).astype(BF16)
    o_ref[...] = _dot(s, w_ref[...].astype(BF16)) + b_ref[...]


def _ada(c, w, b):
    m, n = c.shape[0], w.shape[1]
    tn = 512
    return pl.pallas_call(
        _ada_kernel,
        grid=(n // tn,),
        in_specs=[pl.BlockSpec((m, D), lambda j: (0, 0)),
                  pl.BlockSpec((D, tn), lambda j: (0, j)),
                  pl.BlockSpec((1, tn), lambda j: (0, j))],
        out_specs=pl.BlockSpec((m, tn), lambda j: (0, j)),
        out_shape=jax.ShapeDtypeStruct((m, n), F32),
        name="ada",
    )(c, w, b.reshape(1, n))


def _in_kernel(x_ref, g_ref, sh_ref, sc_ref, wa_ref, wb_ref, wif_ref, bif_ref, z_ref, gc_ref, gt_ref):
    h = (_rms(x_ref[...]) * g_ref[...]) * (1.0 + sc_ref[...]) + sh_ref[...]
    hb = h.astype(BF16)
    gates = _dot3(h, wif_ref[...]) + bif_ref[...]
    gc_ref[...] = gates
    gt_ref[...] = gates.T[:2 * HEADS, :]
    half = wa_ref.shape[1]
    for w_ref, col0 in ((wa_ref, 0), (wb_ref, half)):
        for c in range(half // IN_COLS):
            cols = slice(c * IN_COLS, (c + 1) * IN_COLS)
            z_ref[:, col0 + c * IN_COLS:col0 + (c + 1) * IN_COLS] = _dot(hb, w_ref[:, cols]).astype(z_ref.dtype)


def _in_proj(x, mod, per_token, tokens_per_seq, g_mix, w_a, w_b, w_if, b_if, tm, z_dtype):
    m = x.shape[0]
    half = w_a.shape[1]
    resident = functools.partial(pl.BlockSpec, index_map=lambda i: (0, 0), pipeline_mode=pl.Buffered(1))
    return pl.pallas_call(
        _in_kernel,
        grid=(m // tm,),
        in_specs=[pl.BlockSpec((tm, D), lambda i: (i, 0)),
                  pl.BlockSpec((1, D), lambda i: (0, 0)),
                  _mod_spec(per_token, tm, tokens_per_seq, 0),
                  _mod_spec(per_token, tm, tokens_per_seq, 1),
                  resident((D, half)), resident((D, half)),
                  pl.BlockSpec((D, LANES), lambda i: (0, 0)),
                  pl.BlockSpec((1, LANES), lambda i: (0, 0))],
        out_specs=[pl.BlockSpec((tm, 2 * half), lambda i: (i, 0)),
                   pl.BlockSpec((tm, LANES), lambda i: (i, 0)),
                   pl.BlockSpec((2 * HEADS, tm), lambda i: (0, i))],
        out_shape=[jax.ShapeDtypeStruct((m, 2 * half), z_dtype),
                   jax.ShapeDtypeStruct((m, LANES), F32),
                   jax.ShapeDtypeStruct((2 * HEADS, m), F32)],
        compiler_params=pltpu.CompilerParams(
            dimension_semantics=("arbitrary",), vmem_limit_bytes=MOE_VMEM_LIMIT),
        name="in_proj",
    )(x, g_mix.reshape(1, D), mod, mod, w_a, w_b, w_if, b_if)


def _mlstm_prompt_kernel(q_ref, k_ref, v_ref, o_ref, gt_ref, gh_ref, hm_ref, C_ref, n_ref, m_ref):
    L = q_ref.shape[0]

    @pl.when(pl.program_id(1) == 0)
    def _():
        C_ref[...] = jnp.zeros_like(C_ref)
        n_ref[...] = jnp.zeros_like(n_ref)
        m_ref[...] = jnp.zeros_like(m_ref)

    gates = gt_ref[...]
    r = lax.broadcasted_iota(jnp.int32, (L, L), 0)
    s = lax.broadcasted_iota(jnp.int32, (L, L), 1)
    eye = r == s
    causal = s <= r

    def to_col(x_row):
        return jnp.sum(jnp.where(eye, x_row, 0.0), axis=1, keepdims=True)

    for h in range(HEADS):
        cols = slice(h * HD, (h + 1) * HD)
        ig_row = gates[h:h + 1, :]
        lf_row = _log_sigmoid(gates[HEADS + h:HEADS + h + 1, :])
        lf_col = to_col(lf_row)
        b_row = jnp.sum(jnp.where(r <= s, lf_col, 0.0), axis=0, keepdims=True)
        b_col = to_col(b_row)
        m_prev = m_ref[h][:, :1]

        logD = jnp.where(causal, b_col - b_row + ig_row, -jnp.inf)
        inter = b_col + m_prev
        mt = jnp.maximum(jnp.max(logD, axis=1, keepdims=True), inter)
        q = q_ref[:, cols]
        ks = k_ref[:, cols] * (HD ** -0.5)
        v = v_ref[:, cols]
        S = _dot_nt(q, ks) * jnp.exp(logD - mt)
        w_int = jnp.exp(inter - mt)
        Cmat = C_ref[h]
        nvec = n_ref[h]
        num = _dot(S.astype(BF16), v) + w_int * _dot_nt(q, Cmat.astype(BF16))
        nq = jnp.sum(q.astype(F32) * nvec, axis=1, keepdims=True)
        den = jnp.sum(S, axis=1, keepdims=True) + w_int * nq
        hh = num / jnp.maximum(jnp.abs(den), jnp.exp(-mt))
        hg = jax.nn.sigmoid(o_ref[:, cols].astype(F32)) * hh
        hm_ref[:, cols] = (_rms(hg) * gh_ref[h]).astype(hm_ref.dtype)

        bL = b_row[:, L - 1:L]
        g_row = bL - b_row + ig_row
        m_new = jnp.maximum(bL + m_prev, jnp.max(g_row, axis=1, keepdims=True))
        w_old = jnp.exp(bL + m_prev - m_new)
        kw = ks.astype(F32) * to_col(jnp.exp(g_row - m_new))
        C_ref[h] = w_old * Cmat + _dot_tn(v, kw.astype(BF16))
        n_ref[h] = w_old * nvec + jnp.sum(kw, axis=0, keepdims=True)
        m_ref[h] = jnp.broadcast_to(m_new, (1, LANES))


def _mlstm_prompt(z, gates_t, g_head, batch, seq):
    nc = seq // ML_CHUNK
    m = batch * seq

    def zspec(col):
        return pl.BlockSpec((ML_CHUNK, D), lambda b, c: (b * nc + c, col))

    return pl.pallas_call(
        _mlstm_prompt_kernel,
        grid=(batch, nc),
        in_specs=[zspec(0), zspec(1), zspec(2), zspec(3),
                  pl.BlockSpec((2 * HEADS, ML_CHUNK), lambda b, c: (0, b * nc + c)),
                  pl.BlockSpec((HEADS, 1, HD), lambda b, c: (0, 0, 0))],
        out_specs=[pl.BlockSpec((ML_CHUNK, D), lambda b, c: (b * nc + c, 0)),
                   pl.BlockSpec((None, HEADS, HD, HD), lambda b, c: (b, 0, 0, 0)),
                   pl.BlockSpec((None, HEADS, 1, HD), lambda b, c: (b, 0, 0, 0)),
                   pl.BlockSpec((None, HEADS, 1, LANES), lambda b, c: (b, 0, 0, 0))],
        out_shape=[jax.ShapeDtypeStruct((m, D), BF16),
                   jax.ShapeDtypeStruct((batch, HEADS, HD, HD), F32),
                   jax.ShapeDtypeStruct((batch, HEADS, 1, HD), F32),
                   jax.ShapeDtypeStruct((batch, HEADS, 1, LANES), F32)],
        compiler_params=pltpu.CompilerParams(dimension_semantics=("arbitrary", "arbitrary")),
        name="mlstm_prompt",
    )(z, z, z, z, gates_t, g_head.reshape(HEADS, 1, HD))


def _mlstm_sample_kernel(seq_len, q_ref, k_ref, v_ref, o_ref, gc_ref, m0_ref, C0_ref, n0_ref, gh_ref,
                         hm_ref, C_ref, n_ref, m_ref):
    R = q_ref.shape[0]
    nseq = R // seq_len
    r = lax.broadcasted_iota(jnp.int32, (R, R), 0)
    s = lax.broadcasted_iota(jnp.int32, (R, R), 1)
    rseq = lax.broadcasted_iota(jnp.int32, (R, 1), 0) // seq_len
    eye = r == s
    same = (r // seq_len) == (s // seq_len)
    causal = same & (s <= r)

    def to_row(x_col):
        return jnp.sum(jnp.where(eye, x_col, 0.0), axis=0, keepdims=True)

    gc = gc_ref[...]
    for h in range(HEADS):
        cols = slice(h * HD, (h + 1) * HD)
        ig_col = gc[:, h:h + 1]
        lf_col = _log_sigmoid(gc[:, HEADS + h:HEADS + h + 1])
        lf_row = to_row(lf_col)
        b_col = jnp.sum(jnp.where(causal, lf_row, 0.0), axis=1, keepdims=True)
        bL_col = jnp.sum(jnp.where(same, lf_row, 0.0), axis=1, keepdims=True)
        b_row = to_row(b_col)
        ig_row = to_row(ig_col)
        m0_col = m0_ref[:, h:h + 1]

        logD = jnp.where(causal, b_col - b_row + ig_row, -jnp.inf)
        inter = b_col + m0_col
        mt = jnp.maximum(jnp.max(logD, axis=1, keepdims=True), inter)
        qf = q_ref[:, cols]
        q = qf.astype(BF16)
        ksf = k_ref[:, cols] * (HD ** -0.5)
        v = v_ref[:, cols].astype(BF16)
        S = _dot_nt(q, ksf.astype(BF16)) * jnp.exp(logD - mt)
        w_int = jnp.exp(inter - mt)

        Cq = jnp.zeros((R, HD), F32)
        nq = jnp.zeros((R, 1), F32)
        for g in range(nseq):
            Cq = jnp.where(rseq == g, _dot_nt(q, C0_ref[g, h].astype(BF16)), Cq)
            nq = jnp.where(rseq == g, jnp.sum(qf * n0_ref[g, h], axis=1, keepdims=True), nq)
        num = _dot(S.astype(BF16), v) + w_int * Cq
        den = jnp.sum(S, axis=1, keepdims=True) + w_int * nq
        hh = num / jnp.maximum(jnp.abs(den), jnp.exp(-mt))
        hg = jax.nn.sigmoid(o_ref[:, cols]) * hh
        hm_ref[:, cols] = (_rms(hg) * gh_ref[h]).astype(hm_ref.dtype)

        g_col = bL_col - b_col + ig_col
        gmax_col = jnp.max(jnp.where(same, to_row(g_col), -jnp.inf), axis=1, keepdims=True)
        m_new_col = jnp.maximum(bL_col + m0_col, gmax_col)
        w_old_col = jnp.exp(bL_col + m0_col - m_new_col)
        kw = ksf * jnp.exp(g_col - m_new_col)
        for g in range(nseq):
            kw_g = jnp.where(rseq == g, kw, 0.0)
            w_old = w_old_col[g * seq_len:g * seq_len + 1, :]
            C_ref[g, h] = w_old * C0_ref[g, h] + _dot_tn(v, kw_g.astype(BF16))
            n_ref[g, h] = w_old * n0_ref[g, h] + jnp.sum(kw_g, axis=0, keepdims=True)
        m_ref[:, h * LANES:(h + 1) * LANES] = jnp.broadcast_to(m_new_col, (R, LANES))


def _mlstm_sample(z, gates_c, m0_tok, C0, n0, g_head, batch, seq):
    rows = SAMPLE_SEQS * seq
    m = batch * seq

    def zspec(col):
        return pl.BlockSpec((rows, D), lambda i: (i, col))

    state_c = pl.BlockSpec((SAMPLE_SEQS, HEADS, HD, HD), lambda i: (i, 0, 0, 0))
    state_n = pl.BlockSpec((SAMPLE_SEQS, HEADS, 1, HD), lambda i: (i, 0, 0, 0))
    return pl.pallas_call(
        functools.partial(_mlstm_sample_kernel, seq),
        grid=(batch // SAMPLE_SEQS,),
        in_specs=[zspec(0), zspec(1), zspec(2), zspec(3),
                  pl.BlockSpec((rows, LANES), lambda i: (i, 0)),
                  pl.BlockSpec((rows, HEADS), lambda i: (i, 0)),
                  state_c, state_n,
                  pl.BlockSpec((HEADS, 1, HD), lambda i: (0, 0, 0))],
        out_specs=[pl.BlockSpec((rows, D), lambda i: (i, 0)),
                   state_c, state_n,
                   pl.BlockSpec((rows, HEADS * LANES), lambda i: (i, 0))],
        out_shape=[jax.ShapeDtypeStruct((m, D), BF16),
                   jax.ShapeDtypeStruct((batch, HEADS, HD, HD), F32),
                   jax.ShapeDtypeStruct((batch, HEADS, 1, HD), F32),
                   jax.ShapeDtypeStruct((m, HEADS * LANES), F32)],
        compiler_params=pltpu.CompilerParams(
            dimension_semantics=("arbitrary",), vmem_limit_bytes=VMEM_LIMIT),
        name="mlstm_sample",
    )(z, z, z, z, gates_c, m0_tok, C0, n0.reshape(batch, HEADS, 1, HD), g_head.reshape(HEADS, 1, HD))


def _mix_kernel(x_ref, hm_ref, u_ref, v_ref, ga_ref, gb_ref, gt1_ref, sh2_ref, sc2_ref,
                gsgu_ref, bsgu_ref, ws_ref, bs_ref, wpa_ref, wpb_ref, wout_ref, gffn_ref,
                wr_ref, br_ref, x1_ref, h2_ref, choice_ref, gate_ref, vg_ref, cnt_ref, yg_scr):
    tm = x_ref.shape[0]
    u = jax.nn.gelu(u_ref[...].astype(F32))
    vv = jax.nn.gelu(v_ref[...].astype(F32))
    mu = jnp.mean(vv, axis=-1, keepdims=True)
    var = jnp.mean(jnp.square(vv - mu), axis=-1, keepdims=True)
    vg = (vv - mu) * lax.rsqrt(var + NORM_EPS) * gsgu_ref[...] + bsgu_ref[...]
    vg_ref[...] = vg
    vgb = vg.astype(BF16)

    r = lax.broadcasted_iota(jnp.int32, (GM_CHUNK, GM_CHUNK), 0)
    s = lax.broadcasted_iota(jnp.int32, (GM_CHUNK, GM_CHUNK), 1)
    for g in range(GROUPS):
        w = jnp.where(s <= r, ws_ref[g], 0.0).astype(BF16)
        bias = bs_ref[:, g:g + 1]
        for c in range(tm // GM_CHUNK):
            rows = slice(c * GM_CHUNK, (c + 1) * GM_CHUNK)
            cols = slice(g * GD, (g + 1) * GD)
            mixed = _dot(w, vgb[rows, cols]) + bias
            yg_scr[rows, cols] = (u[rows, cols] * mixed).astype(BF16)

    a = _dot(hm_ref[...], wpa_ref[...])
    b = _dot(yg_scr[...], wpb_ref[...])
    merged = (jax.nn.sigmoid(ga_ref[...].astype(F32)) * a
              + jax.nn.sigmoid(gb_ref[...].astype(F32)) * b)
    x1 = x_ref[...] + gt1_ref[...] * _dot(merged.astype(BF16), wout_ref[...])
    x1_ref[...] = x1
    h2 = (_rms(x1) * gffn_ref[...]) * (1.0 + sc2_ref[...]) + sh2_ref[...]
    h2_ref[...] = _pack_rows(h2)

    lane = lax.broadcasted_iota(jnp.int32, (tm, LANES), 1)
    lg = jnp.where(lane < N_EXPERTS, _dot3(h2, wr_ref[...]) + br_ref[...], -jnp.inf)
    choice = jnp.zeros((tm, LANES), F32)
    vals = []
    for k in range(TOP_K):
        mx = jnp.max(lg, axis=1, keepdims=True)
        idx = jnp.min(jnp.where(lg == mx, lane, LANES), axis=1, keepdims=True)
        sel = lane == idx
        choice = jnp.where(sel, k + 1.0, choice)
        vals.append(mx)
        lg = jnp.where(sel, -jnp.inf, lg)
    choice_ref[...] = choice
    ex = [jnp.exp(v - vals[0]) for v in vals]
    denom = sum(ex)
    gates = jnp.zeros((tm, LANES), F32)
    for k in range(TOP_K):
        gates = jnp.where(lane == k, ex[k] / denom, gates)
    gate_ref[...] = gates[:, :TOP_K]

    @pl.when(pl.program_id(0) == 0)
    def _():
        cnt_ref[...] = jnp.zeros_like(cnt_ref)

    cnt_ref[...] += jnp.sum(jnp.where(choice > 0.0, 1.0, 0.0), axis=0, keepdims=True)


def _mix(x, hm, z, mod, per_token, tokens_per_seq, tm, p):
    m = x.shape[0]
    zcol = 4

    def zspec(blk):
        return pl.BlockSpec((tm, D), lambda i: (i, blk))

    def full(shape):
        return pl.BlockSpec(shape, lambda i: (0,) * len(shape))

    row = pl.BlockSpec((tm, D), lambda i: (i, 0))
    return pl.pallas_call(
        _mix_kernel,
        grid=(m // tm,),
        in_specs=[row, row, zspec(zcol), zspec(zcol + 1), zspec(zcol + 2), zspec(zcol + 3),
                  _mod_spec(per_token, tm, tokens_per_seq, 2),
                  _mod_spec(per_token, tm, tokens_per_seq, 3),
                  _mod_spec(per_token, tm, tokens_per_seq, 4),
                  full((1, D)), full((1, D)),
                  full((GROUPS, GM_CHUNK, GM_CHUNK)), full((GM_CHUNK, GROUPS)),
                  full((D, D)), full((D, D)), full((D, D)), full((1, D)),
                  full((D, LANES)), full((1, LANES))],
        out_specs=[row, pl.BlockSpec((tm, PACKED), lambda i: (i, 0)),
                   pl.BlockSpec((tm, LANES), lambda i: (i, 0)), pl.BlockSpec((tm, TOP_K), lambda i: (i, 0)),
                   row, full((1, LANES))],
        out_shape=[jax.ShapeDtypeStruct((m, D), F32),
                   jax.ShapeDtypeStruct((m, PACKED), jnp.int32),
                   jax.ShapeDtypeStruct((m, LANES), F32),
                   jax.ShapeDtypeStruct((m, TOP_K), F32),
                   jax.ShapeDtypeStruct((m, D), F32),
                   jax.ShapeDtypeStruct((1, LANES), F32)],
        scratch_shapes=[pltpu.VMEM((tm, D), BF16)],
        compiler_params=pltpu.CompilerParams(
            dimension_semantics=("arbitrary",), vmem_limit_bytes=VMEM_LIMIT),
        name="mix",
    )(x, hm, z, z, z, z, mod, mod, mod, p["g_sgu"], p["b_sgu"], p["w_s"], p["b_s"],
      p["w_pa"], p["w_pb"], p["w_out"], p["g_ffn"], p["w_router"], p["b_router"])


def _route_kernel(choice_ref, cnta_ref, cntb_ref, dest_ref, ex_ref, base_scr):
    tm = choice_ref.shape[0]
    lane = lax.broadcasted_iota(jnp.int32, (tm, LANES), 1)
    choice = choice_ref[...]
    onehot = jnp.where(choice > 0.0, 1.0, 0.0)

    @pl.when(pl.program_id(0) == 0)
    def _():
        cnt = cnta_ref[...] + cntb_ref[...]
        padded = jnp.floor((cnt + (MOE_ROWS - 1)) * (1.0 / MOE_ROWS)) * MOE_ROWS
        r = lax.broadcasted_iota(jnp.int32, (LANES, LANES), 0)
        s = lax.broadcasted_iota(jnp.int32, (LANES, LANES), 1)
        padded_col = jnp.sum(jnp.where(r == s, padded, 0.0), axis=1, keepdims=True)
        pstart = jnp.sum(jnp.where(r < s, padded_col, 0.0), axis=0, keepdims=True)
        base_scr[...] = pstart
        trow = lax.broadcasted_iota(jnp.int32, ex_ref.shape, 0)
        table = jnp.where(trow == 0, pstart * (1.0 / MOE_ROWS),
                          jnp.where(trow == 1, padded * (1.0 / MOE_ROWS), jnp.where(trow == 2, cnt, 0.0)))
        ex_ref[...] = table.astype(jnp.int32)

    r = lax.broadcasted_iota(jnp.int32, (tm, tm), 0)
    s = lax.broadcasted_iota(jnp.int32, (tm, tm), 1)
    before = _dot((s < r).astype(BF16), onehot.astype(BF16))
    slot = before + base_scr[...]
    dest = jnp.zeros((tm, LANES), F32)
    for k in range(TOP_K):
        d_k = jnp.sum(jnp.where(choice == k + 1.0, slot, 0.0), axis=1, keepdims=True)
        dest = jnp.where(lane == k, d_k, dest)
    dest_ref[...] = dest[:, :TOP_K].astype(jnp.int32)
    base_scr[...] += jnp.sum(onehot, axis=0, keepdims=True)


def _route(choice, cnt_a, cnt_b):
    n = choice.shape[0]
    tm = 256
    fixed = lambda i: (0, 0)
    return pl.pallas_call(
        _route_kernel,
        grid=(n // tm,),
        in_specs=[pl.BlockSpec((tm, LANES), lambda i: (i, 0)),
                  pl.BlockSpec((1, LANES), fixed), pl.BlockSpec((1, LANES), fixed)],
        out_specs=[pl.BlockSpec((tm, TOP_K), lambda i: (i, 0)), pl.BlockSpec((8, LANES), fixed)],
        out_shape=[jax.ShapeDtypeStruct((n, TOP_K), jnp.int32),
                   jax.ShapeDtypeStruct((8, LANES), jnp.int32)],
        scratch_shapes=[pltpu.VMEM((1, LANES), F32)],
        compiler_params=pltpu.CompilerParams(dimension_semantics=("arbitrary",)),
        name="route",
    )(choice, cnt_a, cnt_b)


def _sc_worker_base(per_worker):
    return (lax.axis_index("s") * SC_CORES + lax.axis_index("c")) * per_worker


def _sc_scatter_rows(rows_a, rows_b, idx_flat, n_out):
    na, nb = rows_a.shape[0], rows_b.shape[0]
    width, dtype = rows_a.shape[1], rows_a.dtype
    n = na + nb
    per_a, per_b = na // SC_WORKERS, nb // SC_WORKERS
    chunk = SC_SCATTER_CHUNK
    n_chunks = per_a // chunk
    assert per_a * SC_WORKERS == na and per_b * SC_WORKERS == nb and per_b % 8 == 0 and per_b <= chunk
    assert n_chunks * chunk == per_a and n_chunks % 2 == 0
    mesh = plsc.VectorSubcoreMesh(core_axis_name="c", subcore_axis_name="s")

    @functools.partial(
        pl.kernel, mesh=mesh,
        out_type=jax.ShapeDtypeStruct((n_out, width), dtype),
        scratch_types=[pltpu.VMEM((chunk,), jnp.int32)] * TOP_K + [pltpu.VMEM((per_b,), jnp.int32)]
                      + [pltpu.VMEM((chunk, width), dtype), pltpu.VMEM((chunk, width), dtype),
                         pltpu.VMEM((per_b, width), dtype)]
                      + [pltpu.SemaphoreType.DMA] * 3,
    )
    def scatter(a_hbm, b_hbm, idx_hbm, out_hbm, i0, i1, i2, i3, ib, rows0, rows1, rowsb, rsem0, rsem1, wsem):
        base = _sc_worker_base(per_a)
        idx_bufs = (i0, i1, i2, i3)
        bufs = ((rows0, rsem0), (rows1, rsem1))

        def off(j):
            return pl.multiple_of(base + j * chunk, 8)

        def read(j, buf):
            rows_v, sem = buf
            return pltpu.make_async_copy(a_hbm.at[pl.ds(off(j), chunk)], rows_v, sem)

        def spread(j, buf):
            rows_v, _ = buf
            read(j, buf).wait()
            for k in range(TOP_K):
                pltpu.sync_copy(idx_hbm.at[pl.ds(pl.multiple_of(k * n + off(j), 8), chunk)], idx_bufs[k])
            for k in range(TOP_K):
                pltpu.make_async_copy(rows_v, out_hbm.at[idx_bufs[k]], wsem).start()
            for k in range(TOP_K):
                pltpu.make_async_copy(rows_v, out_hbm.at[idx_bufs[k]], wsem).wait()

        read(0, bufs[0]).start()

        @pl.loop(0, n_chunks, step=2)
        def _(j):
            read(j + 1, bufs[1]).start()
            spread(j, bufs[0])

            @pl.when(j + 2 < n_chunks)
            def _():
                read(j + 2, bufs[0]).start()
            spread(j + 1, bufs[1])

        off_b = pl.multiple_of(_sc_worker_base(per_b), 8)
        pltpu.sync_copy(b_hbm.at[pl.ds(off_b, per_b)], rowsb)
        for k in range(TOP_K):
            pltpu.sync_copy(idx_hbm.at[pl.ds(pl.multiple_of(k * n + na + off_b, 8), per_b)], ib)
            pltpu.async_copy(rowsb, out_hbm.at[ib], wsem).wait()

    return scatter(rows_a, rows_b, idx_flat)


def _sc_gather_rows(table, idx_flat):
    b = idx_flat.shape[0]
    width, dtype = table.shape[1], table.dtype
    per_worker = b // SC_WORKERS
    chunk = SC_GATHER_CHUNK
    n_chunks = per_worker // chunk
    assert per_worker * SC_WORKERS == b and n_chunks * chunk == per_worker and n_chunks % 2 == 0
    mesh = plsc.VectorSubcoreMesh(core_axis_name="c", subcore_axis_name="s")

    @functools.partial(
        pl.kernel, mesh=mesh,
        out_type=jax.ShapeDtypeStruct((b, width), dtype),
        scratch_types=[pltpu.VMEM((chunk,), jnp.int32), pltpu.VMEM((chunk,), jnp.int32),
                       pltpu.VMEM((chunk, width), dtype), pltpu.VMEM((chunk, width), dtype),
                       pltpu.SemaphoreType.DMA, pltpu.SemaphoreType.DMA],
    )
    def gather(table_hbm, idx_hbm, out_hbm, idx0, idx1, rows0, rows1, sem0, sem1):
        base = _sc_worker_base(per_worker)
        bufs = ((idx0, rows0, sem0), (idx1, rows1, sem1))

        def off(j):
            return pl.multiple_of(base + j * chunk, 8)

        def start(j, buf):
            idx_v, rows_v, sem = buf
            pltpu.sync_copy(idx_hbm.at[pl.ds(off(j), chunk)], idx_v)
            pltpu.make_async_copy(table_hbm.at[idx_v], rows_v, sem).start()

        def finish(j, buf):
            idx_v, rows_v, sem = buf
            pltpu.make_async_copy(table_hbm.at[idx_v], rows_v, sem).wait()
            pltpu.sync_copy(rows_v, out_hbm.at[pl.ds(off(j), chunk)])

        start(0, bufs[0])

        @pl.loop(0, n_chunks, step=2)
        def _(j):
            start(j + 1, bufs[1])
            finish(j, bufs[0])

            @pl.when(j + 2 < n_chunks)
            def _():
                start(j + 2, bufs[0])
            finish(j + 1, bufs[1])

    return gather(table, idx_flat)


def _moe_kernel(first_ref, nblk_ref, cnt_ref, xs_hbm, wgu_ref, bgu_ref, wd_ref, bd_ref, out_hbm,
                xbuf, obuf, wgu_bf, wd_bf, xsem, osem):
    e = pl.program_id(0)
    first, nblk, cnt = first_ref[e], nblk_ref[e], cnt_ref[e]

    def rows(j):
        return pl.ds(pl.multiple_of((first + j) * MOE_ROWS, MOE_ROWS), MOE_ROWS)

    def x_copy(j, slot):
        return pltpu.make_async_copy(xs_hbm.at[rows(j)], xbuf.at[slot], xsem.at[slot])

    def o_copy(j, slot):
        return pltpu.make_async_copy(obuf.at[slot], out_hbm.at[rows(j)], osem.at[slot])

    @pl.when(nblk > 0)
    def _():
        x_copy(0, 0).start()
        wgu_bf[...] = wgu_ref[...].astype(BF16)
        wd_bf[...] = wd_ref[...].astype(BF16)

        def block(j, carry):
            slot = j % 2
            x_copy(j, slot).wait()

            @pl.when(j + 1 < nblk)
            def _():
                x_copy(j + 1, 1 - slot).start()

            @pl.when(j >= 2)
            def _():
                o_copy(j - 2, slot).wait()

            row = lax.broadcasted_iota(jnp.int32, (MOE_ROWS, 1), 0)
            x = _unpack_rows(jnp.where(row < cnt - j * MOE_ROWS, xbuf[slot], 0)).astype(BF16)
            gu = _dot(x, wgu_bf[...]) + bgu_ref[...]
            gate = jnp.minimum(gu[:, :D_FF], SWIGLU_LIMIT)
            up = jnp.clip(gu[:, D_FF:], -SWIGLU_LIMIT, SWIGLU_LIMIT)
            act = gate * jax.nn.sigmoid(SWIGLU_ALPHA * gate) * (up + 1.0)
            obuf[slot] = _pack_rows(_dot(act.astype(BF16), wd_bf[...]) + bd_ref[...])
            o_copy(j, slot).start()
            return carry

        lax.fori_loop(0, nblk, block, 0)

        @pl.when(nblk >= 2)
        def _():
            o_copy(nblk - 2, nblk % 2).wait()
        o_copy(nblk - 1, (nblk - 1) % 2).wait()


def _moe(xs, first_block, n_blocks, counts, w_gu, b_gu, w_down, b_down):
    grid_spec = pltpu.PrefetchScalarGridSpec(
        num_scalar_prefetch=3,
        grid=(N_EXPERTS,),
        in_specs=[pl.BlockSpec(memory_space=pl.ANY),
                  pl.BlockSpec((None, D, 2 * D_FF), lambda e, *_: (e, 0, 0)),
                  pl.BlockSpec((None, 1, 2 * D_FF), lambda e, *_: (e, 0, 0)),
                  pl.BlockSpec((None, D_FF, D), lambda e, *_: (e, 0, 0)),
                  pl.BlockSpec((None, 1, D), lambda e, *_: (e, 0, 0))],
        out_specs=pl.BlockSpec(memory_space=pl.ANY),
        scratch_shapes=[pltpu.VMEM((2, MOE_ROWS, PACKED), jnp.int32), pltpu.VMEM((2, MOE_ROWS, PACKED), jnp.int32),
                        pltpu.VMEM((D, 2 * D_FF), BF16), pltpu.VMEM((D_FF, D), BF16),
                        pltpu.SemaphoreType.DMA((2,)), pltpu.SemaphoreType.DMA((2,))],
    )
    return pl.pallas_call(
        _moe_kernel,
        grid_spec=grid_spec,
        out_shape=jax.ShapeDtypeStruct(xs.shape, jnp.int32),
        compiler_params=pltpu.CompilerParams(
            dimension_semantics=("arbitrary",), vmem_limit_bytes=MOE_VMEM_LIMIT),
        name="moe",
    )(first_block, n_blocks, counts, xs, w_gu, b_gu.reshape(N_EXPERTS, 1, 2 * D_FF), w_down,
      b_down.reshape(N_EXPERTS, 1, D))


def _final_kernel(x1_ref, yk_ref, gate_ref, gt2_ref, g_ref, o_ref):
    y2 = gate_ref[:, 0:1] * _unpack_rows(yk_ref[0])
    for k in range(1, TOP_K):
        y2 = y2 + gate_ref[:, k:k + 1] * _unpack_rows(yk_ref[k])
    o_ref[...] = _rms(x1_ref[...] + gt2_ref[...] * y2) * g_ref[...]


def _final(x1, yk, gates, row_off, mod, per_token, tokens_per_seq, g_final):
    tm = 256
    m = x1.shape[0]
    off = row_off // tm
    return pl.pallas_call(
        _final_kernel,
        grid=(m // tm,),
        in_specs=[pl.BlockSpec((tm, D), lambda i: (i, 0)),
                  pl.BlockSpec((TOP_K, tm, PACKED), lambda i: (0, i + off, 0)),
                  pl.BlockSpec((tm, TOP_K), lambda i: (i, 0)),
                  _mod_spec(per_token, tm, tokens_per_seq, 5),
                  pl.BlockSpec((1, D), lambda i: (0, 0))],
        out_specs=pl.BlockSpec((tm, D), lambda i: (i, 0)),
        out_shape=jax.ShapeDtypeStruct((m, D), F32),
        compiler_params=pltpu.CompilerParams(
            dimension_semantics=("arbitrary",), vmem_limit_bytes=VMEM_LIMIT),
        name="final",
    )(x1, yk, gates, mod, g_final.reshape(1, D))


def kernel(x_prompt, x_sample, state_C, state_n, state_m, c_prompt, c_sample, w_ada, b_ada, g_mix, w_in,
           b_if, g_head, g_sgu, b_sgu, w_s, b_s, w_pa, w_pb, w_out, g_ffn, w_router, b_router, w_gu, b_gu,
           w_down, b_down, g_final):
    depth = w_ada.shape[0]
    assert depth == 1
    bp, tp, _ = x_prompt.shape
    bs, ts, _ = x_sample.shape
    mp, ms = bp * tp, bs * ts
    assert tp % ML_CHUNK == 0 and ts <= ML_CHUNK and GM_CHUNK % ts == 0

    w_in0 = w_in[0]
    nqkvo = 4 * D
    w_qkvo = w_in0[:, :nqkvo].astype(BF16)
    w_gate = w_in0[:, nqkvo + 2 * HEADS:].astype(BF16)
    w_if = jnp.pad(w_in0[:, nqkvo:nqkvo + 2 * HEADS], ((0, 0), (0, LANES - 2 * HEADS)))
    b_if_p = jnp.pad(b_if[0], (0, LANES - 2 * HEADS)).reshape(1, LANES)
    reps = GM_CHUNK // ts
    eye_r = jnp.eye(reps, dtype=F32)
    w_s_sample = jnp.einsum("ab,gts->gatbs", eye_r, w_s[0][:, :ts, :ts]).reshape(GROUPS, GM_CHUNK, GM_CHUNK)
    b_s_prompt = b_s[0].T
    b_s_sample = jnp.tile(b_s[0][:, :ts].T, (reps, 1))
    mix_p = {
        "g_sgu": g_sgu[0].reshape(1, D), "b_sgu": b_sgu[0].reshape(1, D),
        "w_pa": w_pa[0].astype(BF16), "w_pb": w_pb[0].astype(BF16), "w_out": w_out[0].astype(BF16),
        "g_ffn": g_ffn[0].reshape(1, D),
        "w_router": jnp.pad(w_router[0], ((0, 0), (0, LANES - N_EXPERTS))),
        "b_router": jnp.pad(b_router[0], (0, LANES - N_EXPERTS)).reshape(1, LANES),
    }
    mix_prompt = dict(mix_p, w_s=w_s[0], b_s=b_s_prompt)
    mix_sample = dict(mix_p, w_s=w_s_sample, b_s=b_s_sample)

    mod = _ada(jnp.concatenate([c_prompt, c_sample], axis=0), w_ada[0], b_ada[0])
    mod_p = mod[:bp].reshape(bp, 1, N_MOD * D)
    mod_s = jnp.repeat(mod[bp:], ts, axis=0)

    xp = x_prompt.reshape(mp, D)
    xs = x_sample.reshape(ms, D)
    z_p, _, gt_p = _in_proj(xp, mod_p, False, tp, g_mix[0], w_qkvo, w_gate, w_if, b_if_p, 512, BF16)
    z_s, gc_s, _ = _in_proj(xs, mod_s, True, ts, g_mix[0], w_qkvo, w_gate, w_if, b_if_p, 128, F32)

    hm_p, C_p, n_p, m_p = _mlstm_prompt(z_p, gt_p, g_head[0], bp, tp)
    m0_tok = jnp.repeat(state_m[0], ts, axis=0)
    hm_s, C_s, n_s, m_s = _mlstm_sample(z_s, gc_s, m0_tok, state_C[0], state_n[0], g_head[0], bs, ts)

    x1_p, h2_p, ch_p, gates_p, _, cnt_p = _mix(xp, hm_p, z_p, mod_p, False, tp, 256, mix_prompt)
    x1_s, h2_s, ch_s, gates_s, vg_s, cnt_s = _mix(xs, hm_s, z_s, mod_s, True, ts, 128, mix_sample)

    n_tok = mp + ms
    n_blocks = -(-(n_tok * TOP_K + N_EXPERTS * (MOE_ROWS - 1)) // MOE_ROWS)
    dest, ex_table = _route(jnp.concatenate([ch_p, ch_s], axis=0), cnt_p, cnt_s)
    dest_kmajor = dest.T.reshape(TOP_K * n_tok)
    xslots = _sc_scatter_rows(h2_p, h2_s, dest_kmajor, n_blocks * MOE_ROWS)
    yb = _moe(xslots, ex_table[0, :N_EXPERTS], ex_table[1, :N_EXPERTS], ex_table[2, :N_EXPERTS],
              w_gu[0], b_gu[0], w_down[0], b_down[0])
    yk = _sc_gather_rows(yb, dest_kmajor).reshape(TOP_K, n_tok, PACKED)

    y_p = _final(x1_p, yk, gates_p, 0, mod_p, False, tp, g_final)
    y_s = _final(x1_s, yk, gates_s, mp, mod_s, True, ts, g_final)

    return (y_p.reshape(bp, tp, D), y_s.reshape(bs, ts, D),
            C_p[None], n_p.reshape(1, bp, HEADS, HD), m_p[:, :, 0, 0][None],
            C_s[None], n_s.reshape(1, bs, HEADS, HD),
            m_s.reshape(bs, ts, HEADS, LANES)[:, 0, :, 0][None],
            vg_s.reshape(1, bs, ts, D))
```

```python
import functools

import jax
import jax.numpy as jnp
from jax import lax
from jax.experimental import pallas as pl
from jax.experimental.pallas import tpu as pltpu
from jax.experimental.pallas import tpu_sc as plsc

F32 = jnp.float32
BF16 = jnp.bfloat16

D = 1024
HEADS = 4
HD = D // HEADS
ML_CHUNK = 128
GROUPS = 4
GD = D // GROUPS
GM_CHUNK = 128
N_EXPERTS = 32
TOP_K = 4
D_FF = D
SWIGLU_LIMIT = 7.0
SWIGLU_ALPHA = 1.702
NORM_EPS = 1e-6
N_MOD = 6
PACKED = D // 2
LANES = 128
SC_CORES = 2
SC_SUBCORES = 16
SC_WORKERS = SC_CORES * SC_SUBCORES
IN_COLS = 2048
MOE_ROWS = 512
SAMPLE_SEQS = 4
SC_SCATTER_CHUNK = 64
SC_GATHER_CHUNK = 96
VMEM_LIMIT = 48 * 1024 * 1024
MOE_VMEM_LIMIT = 56 * 1024 * 1024


def _dot(a, b):
    return jnp.dot(a, b, preferred_element_type=F32)


def _dot_nt(a, b):
    return lax.dot_general(a, b, (((1,), (1,)), ((), ())), preferred_element_type=F32)


def _dot_tn(a, b):
    return lax.dot_general(a, b, (((0,), (0,)), ((), ())), preferred_element_type=F32)


def _split_bf16(a):
    hi = a.astype(BF16)
    lo = (a - hi.astype(F32)).astype(BF16)
    return hi, lo


def _dot3(a, b):
    ah, al = _split_bf16(a)
    bh, bl = _split_bf16(b)
    return _dot(ah, bh) + (_dot(ah, bl) + _dot(al, bh))


def _log_sigmoid(x):
    return jnp.minimum(x, 0.0) - jnp.log1p(jnp.exp(-jnp.abs(x)))


def _rms(x):
    return x * lax.rsqrt(jnp.mean(x * x, axis=-1, keepdims=True) + NORM_EPS)


def _pack_rows(x):
    bits = lax.bitcast_convert_type(x.astype(BF16).astype(F32), jnp.uint32)
    word = (bits[:, :PACKED] & jnp.uint32(0xFFFF0000)) | (bits[:, PACKED:] >> 16)
    return lax.bitcast_convert_type(word, jnp.int32)


def _unpack_rows(w):
    bits = lax.bitcast_convert_type(w, jnp.uint32)
    left = lax.bitcast_convert_type(bits & jnp.uint32(0xFFFF0000), F32)
    right = lax.bitcast_convert_type(bits << 16, F32)
    return jnp.concatenate([left, right], axis=1)


def _mod_spec(per_token, tm, tokens_per_seq, col):
    if per_token:
        return pl.BlockSpec((tm, D), lambda i, *_: (i, col))
    return pl.BlockSpec((None, 1, D), lambda i, *_: ((i * tm) // tokens_per_seq, 0, col))


def _ada_kernel(c_ref, w_ref, b_ref, o_ref):
    c = c_ref[...]
    s = (c * jax.nn.sigmoid(c)).astype(BF16)
    o_ref[...] = _dot(s, w_ref[...].astype(BF16)) + b_ref[...]


def _ada(c, w, b):
    m, n = c.shape[0], w.shape[1]
    tn = 512
    return pl.pallas_call(
        _ada_kernel,
        grid=(n // tn,),
        in_specs=[pl.BlockSpec((m, D), lambda j: (0, 0)),
                  pl.BlockSpec((D, tn), lambda j: (0, j)),
                  pl.BlockSpec((1, tn), lambda j: (0, j))],
        out_specs=pl.BlockSpec((m, tn), lambda j: (0, j)),
        out_shape=jax.ShapeDtypeStruct((m, n), F32),
        name="ada",
    )(c, w, b.reshape(1, n))


def _in_kernel(x_ref, g_ref, sh_ref, sc_ref, wa_ref, wb_ref, wif_ref, bif_ref, z_ref, gc_ref, gt_ref):
    h = (_rms(x_ref[...]) * g_ref[...]) * (1.0 + sc_ref[...]) + sh_ref[...]
    hb = h.astype(BF16)
    gates = _dot3(h, wif_ref[...]) + bif_ref[...]
    gc_ref[...] = gates
    gt_ref[...] = gates.T[:2 * HEADS, :]
    half = wa_ref.shape[1]
    for w_ref, col0 in ((wa_ref, 0), (wb_ref, half)):
        for c in range(half // IN_COLS):
            cols = slice(c * IN_COLS, (c + 1) * IN_COLS)
            z_ref[:, col0 + c * IN_COLS:col0 + (c + 1) * IN_COLS] = _dot(hb, w_ref[:, cols]).astype(z_ref.dtype)


def _in_proj(x, mod, per_token, tokens_per_seq, g_mix, w_a, w_b, w_if, b_if, tm, z_dtype):
    m = x.shape[0]
    half = w_a.shape[1]
    resident = functools.partial(pl.BlockSpec, index_map=lambda i: (0, 0), pipeline_mode=pl.Buffered(1))
    return pl.pallas_call(
        _in_kernel,
        grid=(m // tm,),
        in_specs=[pl.BlockSpec((tm, D), lambda i: (i, 0)),
                  pl.BlockSpec((1, D), lambda i: (0, 0)),
                  _mod_spec(per_token, tm, tokens_per_seq, 0),
                  _mod_spec(per_token, tm, tokens_per_seq, 1),
                  resident((D, half)), resident((D, half)),
                  pl.BlockSpec((D, LANES), lambda i: (0, 0)),
                  pl.BlockSpec((1, LANES), lambda i: (0, 0))],
        out_specs=[pl.BlockSpec((tm, 2 * half), lambda i: (i, 0)),
                   pl.BlockSpec((tm, LANES), lambda i: (i, 0)),
                   pl.BlockSpec((2 * HEADS, tm), lambda i: (0, i))],
        out_shape=[jax.ShapeDtypeStruct((m, 2 * half), z_dtype),
                   jax.ShapeDtypeStruct((m, LANES), F32),
                   jax.ShapeDtypeStruct((2 * HEADS, m), F32)],
        compiler_params=pltpu.CompilerParams(
            dimension_semantics=("arbitrary",), vmem_limit_bytes=MOE_VMEM_LIMIT),
        name="in_proj",
    )(x, g_mix.reshape(1, D), mod, mod, w_a, w_b, w_if, b_if)


def _mlstm_prompt_kernel(q_ref, k_ref, v_ref, o_ref, gt_ref, gh_ref, hm_ref, C_ref, n_ref, m_ref):
    L = q_ref.shape[0]

    @pl.when(pl.program_id(1) == 0)
    def _():
        C_ref[...] = jnp.zeros_like(C_ref)
        n_ref[...] = jnp.zeros_like(n_ref)
        m_ref[...] = jnp.zeros_like(m_ref)

    gates = gt_ref[...]
    r = lax.broadcasted_iota(jnp.int32, (L, L), 0)
    s = lax.broadcasted_iota(jnp.int32, (L, L), 1)
    eye = r == s
    causal = s <= r

    def to_col(x_row):
        return jnp.sum(jnp.where(eye, x_row, 0.0), axis=1, keepdims=True)

    for h in range(HEADS):
        cols = slice(h * HD, (h + 1) * HD)
        ig_row = gates[h:h + 1, :]
        lf_row = _log_sigmoid(gates[HEADS + h:HEADS + h + 1, :])
        lf_col = to_col(lf_row)
        b_row = jnp.sum(jnp.where(r <= s, lf_col, 0.0), axis=0, keepdims=True)
        b_col = to_col(b_row)
        m_prev = m_ref[h][:, :1]

        logD = jnp.where(causal, b_col - b_row + ig_row, -jnp.inf)
        inter = b_col + m_prev
        mt = jnp.maximum(jnp.max(logD, axis=1, keepdims=True), inter)
        q = q_ref[:, cols]
        ks = k_ref[:, cols] * (HD ** -0.5)
        v = v_ref[:, cols]
        S = _dot_nt(q, ks) * jnp.exp(logD - mt)
        w_int = jnp.exp(inter - mt)
        Cmat = C_ref[h]
        nvec = n_ref[h]
        num = _dot(S.astype(BF16), v) + w_int * _dot_nt(q, Cmat.astype(BF16))
        nq = jnp.sum(q.astype(F32) * nvec, axis=1, keepdims=True)
        den = jnp.sum(S, axis=1, keepdims=True) + w_int * nq
        hh = num / jnp.maximum(jnp.abs(den), jnp.exp(-mt))
        hg = jax.nn.sigmoid(o_ref[:, cols].astype(F32)) * hh
        hm_ref[:, cols] = (_rms(hg) * gh_ref[h]).astype(hm_ref.dtype)

        bL = b_row[:, L - 1:L]
        g_row = bL - b_row + ig_row
        m_new = jnp.maximum(bL + m_prev, jnp.max(g_row, axis=1, keepdims=True))
        w_old = jnp.exp(bL + m_prev - m_new)
        kw = ks.astype(F32) * to_col(jnp.exp(g_row - m_new))
        C_ref[h] = w_old * Cmat + _dot_tn(v, kw.astype(BF16))
        n_ref[h] = w_old * nvec + jnp.sum(kw, axis=0, keepdims=True)
        m_ref[h] = jnp.broadcast_to(m_new, (1, LANES))


def _mlstm_prompt(z, gates_t, g_head, batch, seq):
    nc = seq // ML_CHUNK
    m = batch * seq

    def zspec(col):
        return pl.BlockSpec((ML_CHUNK, D), lambda b, c: (b * nc + c, col))

    return pl.pallas_call(
        _mlstm_prompt_kernel,
        grid=(batch, nc),
        in_specs=[zspec(0), zspec(1), zspec(2), zspec(3),
                  pl.BlockSpec((2 * HEADS, ML_CHUNK), lambda b, c: (0, b * nc + c)),
                  pl.BlockSpec((HEADS, 1, HD), lambda b, c: (0, 0, 0))],
        out_specs=[pl.BlockSpec((ML_CHUNK, D), lambda b, c: (b * nc + c, 0)),
                   pl.BlockSpec((None, HEADS, HD, HD), lambda b, c: (b, 0, 0, 0)),
                   pl.BlockSpec((None, HEADS, 1, HD), lambda b, c: (b, 0, 0, 0)),
                   pl.BlockSpec((None, HEADS, 1, LANES), lambda b, c: (b, 0, 0, 0))],
        out_shape=[jax.ShapeDtypeStruct((m, D), BF16),
                   jax.ShapeDtypeStruct((batch, HEADS, HD, HD), F32),
                   jax.ShapeDtypeStruct((batch, HEADS, 1, HD), F32),
                   jax.ShapeDtypeStruct((batch, HEADS, 1, LANES), F32)],
        compiler_params=pltpu.CompilerParams(dimension_semantics=("arbitrary", "arbitrary")),
        name="mlstm_prompt",
    )(z, z, z, z, gates_t, g_head.reshape(HEADS, 1, HD))


def _mlstm_sample_kernel(seq_len, q_ref, k_ref, v_ref, o_ref, gc_ref, m0_ref, C0_ref, n0_ref, gh_ref,
                         hm_ref, C_ref, n_ref, m_ref):
    R = q_ref.shape[0]
    nseq = R // seq_len
    r = lax.broadcasted_iota(jnp.int32, (R, R), 0)
    s = lax.broadcasted_iota(jnp.int32, (R, R), 1)
    rseq = lax.broadcasted_iota(jnp.int32, (R, 1), 0) // seq_len
    eye = r == s
    same = (r // seq_len) == (s // seq_len)
    causal = same & (s <= r)

    def to_row(x_col):
        return jnp.sum(jnp.where(eye, x_col, 0.0), axis=0, keepdims=True)

    gc = gc_ref[...]
    for h in range(HEADS):
        cols = slice(h * HD, (h + 1) * HD)
        ig_col = gc[:, h:h + 1]
        lf_col = _log_sigmoid(gc[:, HEADS + h:HEADS + h + 1])
        lf_row = to_row(lf_col)
        b_col = jnp.sum(jnp.where(causal, lf_row, 0.0), axis=1, keepdims=True)
        bL_col = jnp.sum(jnp.where(same, lf_row, 0.0), axis=1, keepdims=True)
        b_row = to_row(b_col)
        ig_row = to_row(ig_col)
        m0_col = m0_ref[:, h:h + 1]

        logD = jnp.where(causal, b_col - b_row + ig_row, -jnp.inf)
        inter = b_col + m0_col
        mt = jnp.maximum(jnp.max(logD, axis=1, keepdims=True), inter)
        qf = q_ref[:, cols]
        q = qf.astype(BF16)
        ksf = k_ref[:, cols] * (HD ** -0.5)
        v = v_ref[:, cols].astype(BF16)
        S = _dot_nt(q, ksf.astype(BF16)) * jnp.exp(logD - mt)
        w_int = jnp.exp(inter - mt)

        Cq = jnp.zeros((R, HD), F32)
        nq = jnp.zeros((R, 1), F32)
        for g in range(nseq):
            Cq = jnp.where(rseq == g, _dot_nt(q, C0_ref[g, h].astype(BF16)), Cq)
            nq = jnp.where(rseq == g, jnp.sum(qf * n0_ref[g, h], axis=1, keepdims=True), nq)
        num = _dot(S.astype(BF16), v) + w_int * Cq
        den = jnp.sum(S, axis=1, keepdims=True) + w_int * nq
        hh = num / jnp.maximum(jnp.abs(den), jnp.exp(-mt))
        hg = jax.nn.sigmoid(o_ref[:, cols]) * hh
        hm_ref[:, cols] = (_rms(hg) * gh_ref[h]).astype(hm_ref.dtype)

        g_col = bL_col - b_col + ig_col
        gmax_col = jnp.max(jnp.where(same, to_row(g_col), -jnp.inf), axis=1, keepdims=True)
        m_new_col = jnp.maximum(bL_col + m0_col, gmax_col)
        w_old_col = jnp.exp(bL_col + m0_col - m_new_col)
        kw = ksf * jnp.exp(g_col - m_new_col)
        for g in range(nseq):
            kw_g = jnp.where(rseq == g, kw, 0.0)
            w_old = w_old_col[g * seq_len:g * seq_len + 1, :]
            C_ref[g, h] = w_old * C0_ref[g, h] + _dot_tn(v, kw_g.astype(BF16))
            n_ref[g, h] = w_old * n0_ref[g, h] + jnp.sum(kw_g, axis=0, keepdims=True)
        m_ref[:, h * LANES:(h + 1) * LANES] = jnp.broadcast_to(m_new_col, (R, LANES))


def _mlstm_sample(z, gates_c, m0_tok, C0, n0, g_head, batch, seq):
    rows = SAMPLE_SEQS * seq
    m = batch * seq

    def zspec(col):
        return pl.BlockSpec((rows, D), lambda i: (i, col))

    state_c = pl.BlockSpec((SAMPLE_SEQS, HEADS, HD, HD), lambda i: (i, 0, 0, 0))
    state_n = pl.BlockSpec((SAMPLE_SEQS, HEADS, 1, HD), lambda i: (i, 0, 0, 0))
    return pl.pallas_call(
        functools.partial(_mlstm_sample_kernel, seq),
        grid=(batch // SAMPLE_SEQS,),
        in_specs=[zspec(0), zspec(1), zspec(2), zspec(3),
                  pl.BlockSpec((rows, LANES), lambda i: (i, 0)),
                  pl.BlockSpec((rows, HEADS), lambda i: (i, 0)),
                  state_c, state_n,
                  pl.BlockSpec((HEADS, 1, HD), lambda i: (0, 0, 0))],
        out_specs=[pl.BlockSpec((rows, D), lambda i: (i, 0)),
                   state_c, state_n,
                   pl.BlockSpec((rows, HEADS * LANES), lambda i: (i, 0))],
        out_shape=[jax.ShapeDtypeStruct((m, D), BF16),
                   jax.ShapeDtypeStruct((batch, HEADS, HD, HD), F32),
                   jax.ShapeDtypeStruct((batch, HEADS, 1, HD), F32),
                   jax.ShapeDtypeStruct((m, HEADS * LANES), F32)],
        compiler_params=pltpu.CompilerParams(
            dimension_semantics=("arbitrary",), vmem_limit_bytes=VMEM_LIMIT),
        name="mlstm_sample",
    )(z, z, z, z, gates_c, m0_tok, C0, n0.reshape(batch, HEADS, 1, HD), g_head.reshape(HEADS, 1, HD))


def _mix_kernel(x_ref, hm_ref, u_ref, v_ref, ga_ref, gb_ref, gt1_ref, sh2_ref, sc2_ref,
                gsgu_ref, bsgu_ref, ws_ref, bs_ref, wpa_ref, wpb_ref, wout_ref, gffn_ref,
                wr_ref, br_ref, x1_ref, h2_ref, choice_ref, gate_ref, vg_ref, cnt_ref, yg_scr):
    tm = x_ref.shape[0]
    u = jax.nn.gelu(u_ref[...].astype(F32))
    vv = jax.nn.gelu(v_ref[...].astype(F32))
    mu = jnp.mean(vv, axis=-1, keepdims=True)
    var = jnp.mean(jnp.square(vv - mu), axis=-1, keepdims=True)
    vg = (vv - mu) * lax.rsqrt(var + NORM_EPS) * gsgu_ref[...] + bsgu_ref[...]
    vg_ref[...] = vg
    vgb = vg.astype(BF16)

    r = lax.broadcasted_iota(jnp.int32, (GM_CHUNK, GM_CHUNK), 0)
    s = lax.broadcasted_iota(jnp.int32, (GM_CHUNK, GM_CHUNK), 1)
    for g in range(GROUPS):
        w = jnp.where(s <= r, ws_ref[g], 0.0).astype(BF16)
        bias = bs_ref[:, g:g + 1]
        for c in range(tm // GM_CHUNK):
            rows = slice(c * GM_CHUNK, (c + 1) * GM_CHUNK)
            cols = slice(g * GD, (g + 1) * GD)
            mixed = _dot(w, vgb[rows, cols]) + bias
            yg_scr[rows, cols] = (u[rows, cols] * mixed).astype(BF16)

    a = _dot(hm_ref[...], wpa_ref[...])
    b = _dot(yg_scr[...], wpb_ref[...])
    merged = (jax.nn.sigmoid(ga_ref[...].astype(F32)) * a
              + jax.nn.sigmoid(gb_ref[...].astype(F32)) * b)
    x1 = x_ref[...] + gt1_ref[...] * _dot(merged.astype(BF16), wout_ref[...])
    x1_ref[...] = x1
    h2 = (_rms(x1) * gffn_ref[...]) * (1.0 + sc2_ref[...]) + sh2_ref[...]
    h2_ref[...] = _pack_rows(h2)

    lane = lax.broadcasted_iota(jnp.int32, (tm, LANES), 1)
    lg = jnp.where(lane < N_EXPERTS, _dot3(h2, wr_ref[...]) + br_ref[...], -jnp.inf)
    choice = jnp.zeros((tm, LANES), F32)
    vals = []
    for k in range(TOP_K):
        mx = jnp.max(lg, axis=1, keepdims=True)
        idx = jnp.min(jnp.where(lg == mx, lane, LANES), axis=1, keepdims=True)
        sel = lane == idx
        choice = jnp.where(sel, k + 1.0, choice)
        vals.append(mx)
        lg = jnp.where(sel, -jnp.inf, lg)
    choice_ref[...] = choice
    ex = [jnp.exp(v - vals[0]) for v in vals]
    denom = sum(ex)
    gates = jnp.zeros((tm, LANES), F32)
    for k in range(TOP_K):
        gates = jnp.where(lane == k, ex[k] / denom, gates)
    gate_ref[...] = gates[:, :TOP_K]

    @pl.when(pl.program_id(0) == 0)
    def _():
        cnt_ref[...] = jnp.zeros_like(cnt_ref)

    cnt_ref[...] += jnp.sum(jnp.where(choice > 0.0, 1.0, 0.0), axis=0, keepdims=True)


def _mix(x, hm, z, mod, per_token, tokens_per_seq, tm, p):
    m = x.shape[0]
    zcol = 4

    def zspec(blk):
        return pl.BlockSpec((tm, D), lambda i: (i, blk))

    def full(shape):
        return pl.BlockSpec(shape, lambda i: (0,) * len(shape))

    row = pl.BlockSpec((tm, D), lambda i: (i, 0))
    return pl.pallas_call(
        _mix_kernel,
        grid=(m // tm,),
        in_specs=[row, row, zspec(zcol), zspec(zcol + 1), zspec(zcol + 2), zspec(zcol + 3),
                  _mod_spec(per_token, tm, tokens_per_seq, 2),
                  _mod_spec(per_token, tm, tokens_per_seq, 3),
                  _mod_spec(per_token, tm, tokens_per_seq, 4),
                  full((1, D)), full((1, D)),
                  full((GROUPS, GM_CHUNK, GM_CHUNK)), full((GM_CHUNK, GROUPS)),
                  full((D, D)), full((D, D)), full((D, D)), full((1, D)),
                  full((D, LANES)), full((1, LANES))],
        out_specs=[row, pl.BlockSpec((tm, PACKED), lambda i: (i, 0)),
                   pl.BlockSpec((tm, LANES), lambda i: (i, 0)), pl.BlockSpec((tm, TOP_K), lambda i: (i, 0)),
                   row, full((1, LANES))],
        out_shape=[jax.ShapeDtypeStruct((m, D), F32),
                   jax.ShapeDtypeStruct((m, PACKED), jnp.int32),
                   jax.ShapeDtypeStruct((m, LANES), F32),
                   jax.ShapeDtypeStruct((m, TOP_K), F32),
                   jax.ShapeDtypeStruct((m, D), F32),
                   jax.ShapeDtypeStruct((1, LANES), F32)],
        scratch_shapes=[pltpu.VMEM((tm, D), BF16)],
        compiler_params=pltpu.CompilerParams(
            dimension_semantics=("arbitrary",), vmem_limit_bytes=VMEM_LIMIT),
        name="mix",
    )(x, hm, z, z, z, z, mod, mod, mod, p["g_sgu"], p["b_sgu"], p["w_s"], p["b_s"],
      p["w_pa"], p["w_pb"], p["w_out"], p["g_ffn"], p["w_router"], p["b_router"])


def _route_kernel(choice_ref, cnta_ref, cntb_ref, dest_ref, ex_ref, base_scr):
    tm = choice_ref.shape[0]
    lane = lax.broadcasted_iota(jnp.int32, (tm, LANES), 1)
    choice = choice_ref[...]
    onehot = jnp.where(choice > 0.0, 1.0, 0.0)

    @pl.when(pl.program_id(0) == 0)
    def _():
        cnt = cnta_ref[...] + cntb_ref[...]
        padded = jnp.floor((cnt + (MOE_ROWS - 1)) * (1.0 / MOE_ROWS)) * MOE_ROWS
        r = lax.broadcasted_iota(jnp.int32, (LANES, LANES), 0)
        s = lax.broadcasted_iota(jnp.int32, (LANES, LANES), 1)
        padded_col = jnp.sum(jnp.where(r == s, padded, 0.0), axis=1, keepdims=True)
        pstart = jnp.sum(jnp.where(r < s, padded_col, 0.0), axis=0, keepdims=True)
        base_scr[...] = pstart
        trow = lax.broadcasted_iota(jnp.int32, ex_ref.shape, 0)
        table = jnp.where(trow == 0, pstart * (1.0 / MOE_ROWS),
                          jnp.where(trow == 1, padded * (1.0 / MOE_ROWS), jnp.where(trow == 2, cnt, 0.0)))
        ex_ref[...] = table.astype(jnp.int32)

    r = lax.broadcasted_iota(jnp.int32, (tm, tm), 0)
    s = lax.broadcasted_iota(jnp.int32, (tm, tm), 1)
    before = _dot((s < r).astype(BF16), onehot.astype(BF16))
    slot = before + base_scr[...]
    dest = jnp.zeros((tm, LANES), F32)
    for k in range(TOP_K):
        d_k = jnp.sum(jnp.where(choice == k + 1.0, slot, 0.0), axis=1, keepdims=True)
        dest = jnp.where(lane == k, d_k, dest)
    dest_ref[...] = dest[:, :TOP_K].astype(jnp.int32)
    base_scr[...] += jnp.sum(onehot, axis=0, keepdims=True)


def _route(choice, cnt_a, cnt_b):
    n = choice.shape[0]
    tm = 256
    fixed = lambda i: (0, 0)
    return pl.pallas_call(
        _route_kernel,
        grid=(n // tm,),
        in_specs=[pl.BlockSpec((tm, LANES), lambda i: (i, 0)),
                  pl.BlockSpec((1, LANES), fixed), pl.BlockSpec((1, LANES), fixed)],
        out_specs=[pl.BlockSpec((tm, TOP_K), lambda i: (i, 0)), pl.BlockSpec((8, LANES), fixed)],
        out_shape=[jax.ShapeDtypeStruct((n, TOP_K), jnp.int32),
                   jax.ShapeDtypeStruct((8, LANES), jnp.int32)],
        scratch_shapes=[pltpu.VMEM((1, LANES), F32)],
        compiler_params=pltpu.CompilerParams(dimension_semantics=("arbitrary",)),
        name="route",
    )(choice, cnt_a, cnt_b)


def _sc_worker_base(per_worker):
    return (lax.axis_index("s") * SC_CORES + lax.axis_index("c")) * per_worker


def _sc_scatter_rows(rows_a, rows_b, idx_flat, n_out):
    na, nb = rows_a.shape[0], rows_b.shape[0]
    width, dtype = rows_a.shape[1], rows_a.dtype
    n = na + nb
    per_a, per_b = na // SC_WORKERS, nb // SC_WORKERS
    chunk = SC_SCATTER_CHUNK
    n_chunks = per_a // chunk
    assert per_a * SC_WORKERS == na and per_b * SC_WORKERS == nb and per_b % 8 == 0 and per_b <= chunk
    assert n_chunks * chunk == per_a and n_chunks % 2 == 0
    mesh = plsc.VectorSubcoreMesh(core_axis_name="c", subcore_axis_name="s")

    @functools.partial(
        pl.kernel, mesh=mesh,
        out_type=jax.ShapeDtypeStruct((n_out, width), dtype),
        scratch_types=[pltpu.VMEM((chunk,), jnp.int32)] * TOP_K + [pltpu.VMEM((per_b,), jnp.int32)]
                      + [pltpu.VMEM((chunk, width), dtype), pltpu.VMEM((chunk, width), dtype),
                         pltpu.VMEM((per_b, width), dtype)]
                      + [pltpu.SemaphoreType.DMA] * 3,
    )
    def scatter(a_hbm, b_hbm, idx_hbm, out_hbm, i0, i1, i2, i3, ib, rows0, rows1, rowsb, rsem0, rsem1, wsem):
        base = _sc_worker_base(per_a)
        idx_bufs = (i0, i1, i2, i3)
        bufs = ((rows0, rsem0), (rows1, rsem1))

        def off(j):
            return pl.multiple_of(base + j * chunk, 8)

        def read(j, buf):
            rows_v, sem = buf
            return pltpu.make_async_copy(a_hbm.at[pl.ds(off(j), chunk)], rows_v, sem)

        def spread(j, buf):
            rows_v, _ = buf
            read(j, buf).wait()
            for k in range(TOP_K):
                pltpu.sync_copy(idx_hbm.at[pl.ds(pl.multiple_of(k * n + off(j), 8), chunk)], idx_bufs[k])
            for k in range(TOP_K):
                pltpu.make_async_copy(rows_v, out_hbm.at[idx_bufs[k]], wsem).start()
            for k in range(TOP_K):
                pltpu.make_async_copy(rows_v, out_hbm.at[idx_bufs[k]], wsem).wait()

        read(0, bufs[0]).start()

        @pl.loop(0, n_chunks, step=2)
        def _(j):
            read(j + 1, bufs[1]).start()
            spread(j, bufs[0])

            @pl.when(j + 2 < n_chunks)
            def _():
                read(j + 2, bufs[0]).start()
            spread(j + 1, bufs[1])

        off_b = pl.multiple_of(_sc_worker_base(per_b), 8)
        pltpu.sync_copy(b_hbm.at[pl.ds(off_b, per_b)], rowsb)
        for k in range(TOP_K):
            pltpu.sync_copy(idx_hbm.at[pl.ds(pl.multiple_of(k * n + na + off_b, 8), per_b)], ib)
            pltpu.async_copy(rowsb, out_hbm.at[ib], wsem).wait()

    return scatter(rows_a, rows_b, idx_flat)


def _sc_gather_rows(table, idx_flat):
    b = idx_flat.shape[0]
    width, dtype = table.shape[1], table.dtype
    per_worker = b // SC_WORKERS
    chunk = SC_GATHER_CHUNK
    n_chunks = per_worker // chunk
    assert per_worker * SC_WORKERS == b and n_chunks * chunk == per_worker and n_chunks % 2 == 0
    mesh = plsc.VectorSubcoreMesh(core_axis_name="c", subcore_axis_name="s")

    @functools.partial(
        pl.kernel, mesh=mesh,
        out_type=jax.ShapeDtypeStruct((b, width), dtype),
        scratch_types=[pltpu.VMEM((chunk,), jnp.int32), pltpu.VMEM((chunk,), jnp.int32),
                       pltpu.VMEM((chunk, width), dtype), pltpu.VMEM((chunk, width), dtype),
                       pltpu.SemaphoreType.DMA, pltpu.SemaphoreType.DMA],
    )
    def gather(table_hbm, idx_hbm, out_hbm, idx0, idx1, rows0, rows1, sem0, sem1):
        base = _sc_worker_base(per_worker)
        bufs = ((idx0, rows0, sem0), (idx1, rows1, sem1))

        def off(j):
            return pl.multiple_of(base + j * chunk, 8)

        def start(j, buf):
            idx_v, rows_v, sem = buf
            pltpu.sync_copy(idx_hbm.at[pl.ds(off(j), chunk)], idx_v)
            pltpu.make_async_copy(table_hbm.at[idx_v], rows_v, sem).start()

        def finish(j, buf):
            idx_v, rows_v, sem = buf
            pltpu.make_async_copy(table_hbm.at[idx_v], rows_v, sem).wait()
            pltpu.sync_copy(rows_v, out_hbm.at[pl.ds(off(j), chunk)])

        start(0, bufs[0])

        @pl.loop(0, n_chunks, step=2)
        def _(j):
            start(j + 1, bufs[1])
            finish(j, bufs[0])

            @pl.when(j + 2 < n_chunks)
            def _():
                start(j + 2, bufs[0])
            finish(j + 1, bufs[1])

    return gather(table, idx_flat)


def _moe_kernel(first_ref, nblk_ref, cnt_ref, xs_hbm, wgu_ref, bgu_ref, wd_ref, bd_ref, out_hbm,
                xbuf, obuf, wgu_bf, wd_bf, xsem, osem):
    e = pl.program_id(0)
    first, nblk, cnt = first_ref[e], nblk_ref[e], cnt_ref[e]

    def rows(j):
        return pl.ds(pl.multiple_of((first + j) * MOE_ROWS, MOE_ROWS), MOE_ROWS)

    def x_copy(j, slot):
        return pltpu.make_async_copy(xs_hbm.at[rows(j)], xbuf.at[slot], xsem.at[slot])

    def o_copy(j, slot):
        return pltpu.make_async_copy(obuf.at[slot], out_hbm.at[rows(j)], osem.at[slot])

    @pl.when(nblk > 0)
    def _():
        x_copy(0, 0).start()
        wgu_bf[...] = wgu_ref[...].astype(BF16)
        wd_bf[...] = wd_ref[...].astype(BF16)

        def block(j, carry):
            slot = j % 2
            x_copy(j, slot).wait()

            @pl.when(j + 1 < nblk)
            def _():
                x_copy(j + 1, 1 - slot).start()

            @pl.when(j >= 2)
            def _():
                o_copy(j - 2, slot).wait()

            row = lax.broadcasted_iota(jnp.int32, (MOE_ROWS, 1), 0)
            x = _unpack_rows(jnp.where(row < cnt - j * MOE_ROWS, xbuf[slot], 0)).astype(BF16)
            gu = _dot(x, wgu_bf[...]) + bgu_ref[...]
            gate = jnp.minimum(gu[:, :D_FF], SWIGLU_LIMIT)
            up = jnp.clip(gu[:, D_FF:], -SWIGLU_LIMIT, SWIGLU_LIMIT)
            act = gate * jax.nn.sigmoid(SWIGLU_ALPHA * gate) * (up + 1.0)
            obuf[slot] = _pack_rows(_dot(act.astype(BF16), wd_bf[...]) + bd_ref[...])
            o_copy(j, slot).start()
            return carry

        lax.fori_loop(0, nblk, block, 0)

        @pl.when(nblk >= 2)
        def _():
            o_copy(nblk - 2, nblk % 2).wait()
        o_copy(nblk - 1, (nblk - 1) % 2).wait()


def _moe(xs, first_block, n_blocks, counts, w_gu, b_gu, w_down, b_down):
    grid_spec = pltpu.PrefetchScalarGridSpec(
        num_scalar_prefetch=3,
        grid=(N_EXPERTS,),
        in_specs=[pl.BlockSpec(memory_space=pl.ANY),
                  pl.BlockSpec((None, D, 2 * D_FF), lambda e, *_: (e, 0, 0)),
                  pl.BlockSpec((None, 1, 2 * D_FF), lambda e, *_: (e, 0, 0)),
                  pl.BlockSpec((None, D_FF, D), lambda e, *_: (e, 0, 0)),
                  pl.BlockSpec((None, 1, D), lambda e, *_: (e, 0, 0))],
        out_specs=pl.BlockSpec(memory_space=pl.ANY),
        scratch_shapes=[pltpu.VMEM((2, MOE_ROWS, PACKED), jnp.int32), pltpu.VMEM((2, MOE_ROWS, PACKED), jnp.int32),
                        pltpu.VMEM((D, 2 * D_FF), BF16), pltpu.VMEM((D_FF, D), BF16),
                        pltpu.SemaphoreType.DMA((2,)), pltpu.SemaphoreType.DMA((2,))],
    )
    return pl.pallas_call(
        _moe_kernel,
        grid_spec=grid_spec,
        out_shape=jax.ShapeDtypeStruct(xs.shape, jnp.int32),
        compiler_params=pltpu.CompilerParams(
            dimension_semantics=("arbitrary",), vmem_limit_bytes=MOE_VMEM_LIMIT),
        name="moe",
    )(first_block, n_blocks, counts, xs, w_gu, b_gu.reshape(N_EXPERTS, 1, 2 * D_FF), w_down,
      b_down.reshape(N_EXPERTS, 1, D))


def _final_kernel(x1_ref, yk_ref, gate_ref, gt2_ref, g_ref, o_ref):
    y2 = gate_ref[:, 0:1] * _unpack_rows(yk_ref[0])
    for k in range(1, TOP_K):
        y2 = y2 + gate_ref[:, k:k + 1] * _unpack_rows(yk_ref[k])
    o_ref[...] = _rms(x1_ref[...] + gt2_ref[...] * y2) * g_ref[...]


def _final(x1, yk, gates, row_off, mod, per_token, tokens_per_seq, g_final):
    tm = 256
    m = x1.shape[0]
    off = row_off // tm
    return pl.pallas_call(
        _final_kernel,
        grid=(m // tm,),
        in_specs=[pl.BlockSpec((tm, D), lambda i: (i, 0)),
                  pl.BlockSpec((TOP_K, tm, PACKED), lambda i: (0, i + off, 0)),
                  pl.BlockSpec((tm, TOP_K), lambda i: (i, 0)),
                  _mod_spec(per_token, tm, tokens_per_seq, 5),
                  pl.BlockSpec((1, D), lambda i: (0, 0))],
        out_specs=pl.BlockSpec((tm, D), lambda i: (i, 0)),
        out_shape=jax.ShapeDtypeStruct((m, D), F32),
        compiler_params=pltpu.CompilerParams(
            dimension_semantics=("arbitrary",), vmem_limit_bytes=VMEM_LIMIT),
        name="final",
    )(x1, yk, gates, mod, g_final.reshape(1, D))


def kernel(x_prompt, x_sample, state_C, state_n, state_m, c_prompt, c_sample, w_ada, b_ada, g_mix, w_in,
           b_if, g_head, g_sgu, b_sgu, w_s, b_s, w_pa, w_pb, w_out, g_ffn, w_router, b_router, w_gu, b_gu,
           w_down, b_down, g_final):
    depth = w_ada.shape[0]
    assert depth == 1
    bp, tp, _ = x_prompt.shape
    bs, ts, _ = x_sample.shape
    mp, ms = bp * tp, bs * ts
    assert tp % ML_CHUNK == 0 and ts <= ML_CHUNK and GM_CHUNK % ts == 0

    w_in0 = w_in[0]
    nqkvo = 4 * D
    w_qkvo = w_in0[:, :nqkvo].astype(BF16)
    w_gate = w_in0[:, nqkvo + 2 * HEADS:].astype(BF16)
    w_if = jnp.pad(w_in0[:, nqkvo:nqkvo + 2 * HEADS], ((0, 0), (0, LANES - 2 * HEADS)))
    b_if_p = jnp.pad(b_if[0], (0, LANES - 2 * HEADS)).reshape(1, LANES)
    reps = GM_CHUNK // ts
    eye_r = jnp.eye(reps, dtype=F32)
    w_s_sample = jnp.einsum("ab,gts->gatbs", eye_r, w_s[0][:, :ts, :ts]).reshape(GROUPS, GM_CHUNK, GM_CHUNK)
    b_s_prompt = b_s[0].T
    b_s_sample = jnp.tile(b_s[0][:, :ts].T, (reps, 1))
    mix_p = {
        "g_sgu": g_sgu[0].reshape(1, D), "b_sgu": b_sgu[0].reshape(1, D),
        "w_pa": w_pa[0].astype(BF16), "w_pb": w_pb[0].astype(BF16), "w_out": w_out[0].astype(BF16),
        "g_ffn": g_ffn[0].reshape(1, D),
        "w_router": jnp.pad(w_router[0], ((0, 0), (0, LANES - N_EXPERTS))),
        "b_router": jnp.pad(b_router[0], (0, LANES - N_EXPERTS)).reshape(1, LANES),
    }
    mix_prompt = dict(mix_p, w_s=w_s[0], b_s=b_s_prompt)
    mix_sample = dict(mix_p, w_s=w_s_sample, b_s=b_s_sample)

    mod = _ada(jnp.concatenate([c_prompt, c_sample], axis=0), w_ada[0], b_ada[0])
    mod_p = mod[:bp].reshape(bp, 1, N_MOD * D)
    mod_s = jnp.repeat(mod[bp:], ts, axis=0)

    xp = x_prompt.reshape(mp, D)
    xs = x_sample.reshape(ms, D)
    z_p, _, gt_p = _in_proj(xp, mod_p, False, tp, g_mix[0], w_qkvo, w_gate, w_if, b_if_p, 512, BF16)
    z_s, gc_s, _ = _in_proj(xs, mod_s, True, ts, g_mix[0], w_qkvo, w_gate, w_if, b_if_p, 128, F32)

    hm_p, C_p, n_p, m_p = _mlstm_prompt(z_p, gt_p, g_head[0], bp, tp)
    m0_tok = jnp.repeat(state_m[0], ts, axis=0)
    hm_s, C_s, n_s, m_s = _mlstm_sample(z_s, gc_s, m0_tok, state_C[0], state_n[0], g_head[0], bs, ts)

    x1_p, h2_p, ch_p, gates_p, _, cnt_p = _mix(xp, hm_p, z_p, mod_p, False, tp, 256, mix_prompt)
    x1_s, h2_s, ch_s, gates_s, vg_s, cnt_s = _mix(xs, hm_s, z_s, mod_s, True, ts, 128, mix_sample)

    n_tok = mp + ms
    n_blocks = -(-(n_tok * TOP_K + N_EXPERTS * (MOE_ROWS - 1)) // MOE_ROWS)
    dest, ex_table = _route(jnp.concatenate([ch_p, ch_s], axis=0), cnt_p, cnt_s)
    dest_kmajor = dest.T.reshape(TOP_K * n_tok)
    xslots = _sc_scatter_rows(h2_p, h2_s, dest_kmajor, n_blocks * MOE_ROWS)
    yb = _moe(xslots, ex_table[0, :N_EXPERTS], ex_table[1, :N_EXPERTS], ex_table[2, :N_EXPERTS],
              w_gu[0], b_gu[0], w_down[0], b_down[0])
    yk = _sc_gather_rows(yb, dest_kmajor).reshape(TOP_K, n_tok, PACKED)

    y_p = _final(x1_p, yk, gates_p, 0, mod_p, False, tp, g_final)
    y_s = _final(x1_s, yk, gates_s, mp, mod_s, True, ts, g_final)

    return (y_p.reshape(bp, tp, D), y_s.reshape(bs, ts, D),
            C_p[None], n_p.reshape(1, bp, HEADS, HD), m_p[:, :, 0, 0][None],
            C_s[None], n_s.reshape(1, bs, HEADS, HD),
            m_s.reshape(bs, ts, HEADS, LANES)[:, 0, :, 0][None],
            vg_s.reshape(1, bs, ts, D))
```

```python
import functools

import jax
import jax.numpy as jnp
from jax import lax
from jax.experimental import pallas as pl
from jax.experimental.pallas import tpu as pltpu
from jax.experimental.pallas import tpu_sc as plsc

F32 = jnp.float32
BF16 = jnp.bfloat16

D = 1024
HEADS = 4
HD = D // HEADS
ML_CHUNK = 128
GROUPS = 4
GD = D // GROUPS
GM_CHUNK = 128
N_EXPERTS = 32
TOP_K = 4
D_FF = D
SWIGLU_LIMIT = 7.0
SWIGLU_ALPHA = 1.702
NORM_EPS = 1e-6
N_MOD = 6
PACKED = D // 2
LANES = 128
SC_CORES = 2
SC_SUBCORES = 16
SC_WORKERS = SC_CORES * SC_SUBCORES
IN_COLS = 2048
MOE_ROWS = 256
ROW_DMA_PRIORITY = 1
SAMPLE_SEQS = 4
SC_SCATTER_CHUNK = 64
SC_GATHER_CHUNK = 96
VMEM_LIMIT = 48 * 1024 * 1024
MOE_VMEM_LIMIT = 56 * 1024 * 1024


def _dot(a, b):
    return jnp.dot(a, b, preferred_element_type=F32)


def _dot_nt(a, b):
    return lax.dot_general(a, b, (((1,), (1,)), ((), ())), preferred_element_type=F32)


def _dot_tn(a, b):
    return lax.dot_general(a, b, (((0,), (0,)), ((), ())), preferred_element_type=F32)


def _split_bf16(a):
    hi = a.astype(BF16)
    lo = (a - hi.astype(F32)).astype(BF16)
    return hi, lo


def _dot3(a, b):
    ah, al = _split_bf16(a)
    bh, bl = _split_bf16(b)
    return _dot(ah, bh) + (_dot(ah, bl) + _dot(al, bh))


def _log_sigmoid(x):
    return jnp.minimum(x, 0.0) - jnp.log1p(jnp.exp(-jnp.abs(x)))


def _rms(x):
    return x * lax.rsqrt(jnp.mean(x * x, axis=-1, keepdims=True) + NORM_EPS)


def _pack_rows(x):
    bits = lax.bitcast_convert_type(x.astype(BF16).astype(F32), jnp.uint32)
    word = (bits[:, :PACKED] & jnp.uint32(0xFFFF0000)) | (bits[:, PACKED:] >> 16)
    return lax.bitcast_convert_type(word, jnp.int32)


def _unpack_rows(w):
    bits = lax.bitcast_convert_type(w, jnp.uint32)
    left = lax.bitcast_convert_type(bits & jnp.uint32(0xFFFF0000), F32)
    right = lax.bitcast_convert_type(bits << 16, F32)
    return jnp.concatenate([left, right], axis=1)


def _mod_spec(per_token, tm, tokens_per_seq, col):
    if per_token:
        return pl.BlockSpec((tm, D), lambda i, *_: (i, col))
    return pl.BlockSpec((None, 1, D), lambda i, *_: ((i * tm) // tokens_per_seq, 0, col))


def _ada_kernel(c_ref, w_ref, b_ref, o_ref):
    c = c_ref[...]
    s = (c * jax.nn.sigmoid(c)).astype(BF16)
    o_ref[...] = _dot(s, w_ref[...].astype(BF16)) + b_ref[...]


def _ada(c, w, b):
    m, n = c.shape[0], w.shape[1]
    tn = 512
    return pl.pallas_call(
        _ada_kernel,
        grid=(n // tn,),
        in_specs=[pl.BlockSpec((m, D), lambda j: (0, 0)),
                  pl.BlockSpec((D, tn), lambda j: (0, j)),
                  pl.BlockSpec((1, tn), lambda j: (0, j))],
        out_specs=pl.BlockSpec((m, tn), lambda j: (0, j)),
        out_shape=jax.ShapeDtypeStruct((m, n), F32),
        name="ada",
    )(c, w, b.reshape(1, n))


def _in_kernel(x_ref, g_ref, sh_ref, sc_ref, wa_ref, wb_ref, wif_ref, bif_ref, z_ref, gc_ref, gt_ref):
    h = (_rms(x_ref[...]) * g_ref[...]) * (1.0 + sc_ref[...]) + sh_ref[...]
    hb = h.astype(BF16)
    gates = _dot3(h, wif_ref[...]) + bif_ref[...]
    gc_ref[...] = gates
    gt_ref[...] = gates.T[:2 * HEADS, :]
    half = wa_ref.shape[1]
    for w_ref, col0 in ((wa_ref, 0), (wb_ref, half)):
        for c in range(half // IN_COLS):
            cols = slice(c * IN_COLS, (c + 1) * IN_COLS)
            z_ref[:, col0 + c * IN_COLS:col0 + (c + 1) * IN_COLS] = _dot(hb, w_ref[:, cols]).astype(z_ref.dtype)


def _in_proj(x, mod, per_token, tokens_per_seq, g_mix, w_a, w_b, w_if, b_if, tm, z_dtype):
    m = x.shape[0]
    half = w_a.shape[1]
    resident = functools.partial(pl.BlockSpec, index_map=lambda i: (0, 0), pipeline_mode=pl.Buffered(1))
    return pl.pallas_call(
        _in_kernel,
        grid=(m // tm,),
        in_specs=[pl.BlockSpec((tm, D), lambda i: (i, 0)),
                  pl.BlockSpec((1, D), lambda i: (0, 0)),
                  _mod_spec(per_token, tm, tokens_per_seq, 0),
                  _mod_spec(per_token, tm, tokens_per_seq, 1),
                  resident((D, half)), resident((D, half)),
                  pl.BlockSpec((D, LANES), lambda i: (0, 0)),
                  pl.BlockSpec((1, LANES), lambda i: (0, 0))],
        out_specs=[pl.BlockSpec((tm, 2 * half), lambda i: (i, 0)),
                   pl.BlockSpec((tm, LANES), lambda i: (i, 0)),
                   pl.BlockSpec((2 * HEADS, tm), lambda i: (0, i))],
        out_shape=[jax.ShapeDtypeStruct((m, 2 * half), z_dtype),
                   jax.ShapeDtypeStruct((m, LANES), F32),
                   jax.ShapeDtypeStruct((2 * HEADS, m), F32)],
        compiler_params=pltpu.CompilerParams(
            dimension_semantics=("arbitrary",), vmem_limit_bytes=MOE_VMEM_LIMIT),
        name="in_proj",
    )(x, g_mix.reshape(1, D), mod, mod, w_a, w_b, w_if, b_if)


def _mlstm_prompt_kernel(q_ref, k_ref, v_ref, o_ref, gt_ref, gh_ref, hm_ref, C_ref, n_ref, m_ref):
    L = q_ref.shape[0]

    @pl.when(pl.program_id(1) == 0)
    def _():
        C_ref[...] = jnp.zeros_like(C_ref)
        n_ref[...] = jnp.zeros_like(n_ref)
        m_ref[...] = jnp.zeros_like(m_ref)

    gates = gt_ref[...]
    r = lax.broadcasted_iota(jnp.int32, (L, L), 0)
    s = lax.broadcasted_iota(jnp.int32, (L, L), 1)
    eye = r == s
    causal = s <= r

    def to_col(x_row):
        return jnp.sum(jnp.where(eye, x_row, 0.0), axis=1, keepdims=True)

    for h in range(HEADS):
        cols = slice(h * HD, (h + 1) * HD)
        ig_row = gates[h:h + 1, :]
        lf_row = _log_sigmoid(gates[HEADS + h:HEADS + h + 1, :])
        lf_col = to_col(lf_row)
        b_row = jnp.sum(jnp.where(r <= s, lf_col, 0.0), axis=0, keepdims=True)
        b_col = to_col(b_row)
        m_prev = m_ref[h][:, :1]

        logD = jnp.where(causal, b_col - b_row + ig_row, -jnp.inf)
        inter = b_col + m_prev
        mt = jnp.maximum(jnp.max(logD, axis=1, keepdims=True), inter)
        q = q_ref[:, cols]
        ks = k_ref[:, cols] * (HD ** -0.5)
        v = v_ref[:, cols]
        S = _dot_nt(q, ks) * jnp.exp(logD - mt)
        w_int = jnp.exp(inter - mt)
        Cmat = C_ref[h]
        nvec = n_ref[h]
        num = _dot(S.astype(BF16), v) + w_int * _dot_nt(q, Cmat.astype(BF16))
        nq = jnp.sum(q.astype(F32) * nvec, axis=1, keepdims=True)
        den = jnp.sum(S, axis=1, keepdims=True) + w_int * nq
        hh = num / jnp.maximum(jnp.abs(den), jnp.exp(-mt))
        hg = jax.nn.sigmoid(o_ref[:, cols].astype(F32)) * hh
        hm_ref[:, cols] = (_rms(hg) * gh_ref[h]).astype(hm_ref.dtype)

        bL = b_row[:, L - 1:L]
        g_row = bL - b_row + ig_row
        m_new = jnp.maximum(bL + m_prev, jnp.max(g_row, axis=1, keepdims=True))
        w_old = jnp.exp(bL + m_prev - m_new)
        kw = ks.astype(F32) * to_col(jnp.exp(g_row - m_new))
        C_ref[h] = w_old * Cmat + _dot_tn(v, kw.astype(BF16))
        n_ref[h] = w_old * nvec + jnp.sum(kw, axis=0, keepdims=True)
        m_ref[h] = jnp.broadcast_to(m_new, (1, LANES))


def _mlstm_prompt(z, gates_t, g_head, batch, seq):
    nc = seq // ML_CHUNK
    m = batch * seq

    def zspec(col):
        return pl.BlockSpec((ML_CHUNK, D), lambda b, c: (b * nc + c, col))

    return pl.pallas_call(
        _mlstm_prompt_kernel,
        grid=(batch, nc),
        in_specs=[zspec(0), zspec(1), zspec(2), zspec(3),
                  pl.BlockSpec((2 * HEADS, ML_CHUNK), lambda b, c: (0, b * nc + c)),
                  pl.BlockSpec((HEADS, 1, HD), lambda b, c: (0, 0, 0))],
        out_specs=[pl.BlockSpec((ML_CHUNK, D), lambda b, c: (b * nc + c, 0)),
                   pl.BlockSpec((None, HEADS, HD, HD), lambda b, c: (b, 0, 0, 0)),
                   pl.BlockSpec((None, HEADS, 1, HD), lambda b, c: (b, 0, 0, 0)),
                   pl.BlockSpec((None, HEADS, 1, LANES), lambda b, c: (b, 0, 0, 0))],
        out_shape=[jax.ShapeDtypeStruct((m, D), BF16),
                   jax.ShapeDtypeStruct((batch, HEADS, HD, HD), F32),
                   jax.ShapeDtypeStruct((batch, HEADS, 1, HD), F32),
                   jax.ShapeDtypeStruct((batch, HEADS, 1, LANES), F32)],
        compiler_params=pltpu.CompilerParams(dimension_semantics=("arbitrary", "arbitrary")),
        name="mlstm_prompt",
    )(z, z, z, z, gates_t, g_head.reshape(HEADS, 1, HD))


def _mlstm_sample_kernel(seq_len, q_ref, k_ref, v_ref, o_ref, gc_ref, m0_ref, C0_ref, n0_ref, gh_ref,
                         hm_ref, C_ref, n_ref, m_ref):
    R = q_ref.shape[0]
    nseq = R // seq_len
    r = lax.broadcasted_iota(jnp.int32, (R, R), 0)
    s = lax.broadcasted_iota(jnp.int32, (R, R), 1)
    rseq = lax.broadcasted_iota(jnp.int32, (R, 1), 0) // seq_len
    eye = r == s
    same = (r // seq_len) == (s // seq_len)
    causal = same & (s <= r)

    def to_row(x_col):
        return jnp.sum(jnp.where(eye, x_col, 0.0), axis=0, keepdims=True)

    gc = gc_ref[...]
    for h in range(HEADS):
        cols = slice(h * HD, (h + 1) * HD)
        ig_col = gc[:, h:h + 1]
        lf_col = _log_sigmoid(gc[:, HEADS + h:HEADS + h + 1])
        lf_row = to_row(lf_col)
        b_col = jnp.sum(jnp.where(causal, lf_row, 0.0), axis=1, keepdims=True)
        bL_col = jnp.sum(jnp.where(same, lf_row, 0.0), axis=1, keepdims=True)
        b_row = to_row(b_col)
        ig_row = to_row(ig_col)
        m0_col = m0_ref[:, h:h + 1]

        logD = jnp.where(causal, b_col - b_row + ig_row, -jnp.inf)
        inter = b_col + m0_col
        mt = jnp.maximum(jnp.max(logD, axis=1, keepdims=True), inter)
        qf = q_ref[:, cols]
        q = qf.astype(BF16)
        ksf = k_ref[:, cols] * (HD ** -0.5)
        v = v_ref[:, cols].astype(BF16)
        S = _dot_nt(q, ksf.astype(BF16)) * jnp.exp(logD - mt)
        w_int = jnp.exp(inter - mt)

        Cq = jnp.zeros((R, HD), F32)
        nq = jnp.zeros((R, 1), F32)
        for g in range(nseq):
            Cq = jnp.where(rseq == g, _dot_nt(q, C0_ref[g, h].astype(BF16)), Cq)
            nq = jnp.where(rseq == g, jnp.sum(qf * n0_ref[g, h], axis=1, keepdims=True), nq)
        num = _dot(S.astype(BF16), v) + w_int * Cq
        den = jnp.sum(S, axis=1, keepdims=True) + w_int * nq
        hh = num / jnp.maximum(jnp.abs(den), jnp.exp(-mt))
        hg = jax.nn.sigmoid(o_ref[:, cols]) * hh
        hm_ref[:, cols] = (_rms(hg) * gh_ref[h]).astype(hm_ref.dtype)

        g_col = bL_col - b_col + ig_col
        gmax_col = jnp.max(jnp.where(same, to_row(g_col), -jnp.inf), axis=1, keepdims=True)
        m_new_col = jnp.maximum(bL_col + m0_col, gmax_col)
        w_old_col = jnp.exp(bL_col + m0_col - m_new_col)
        kw = ksf * jnp.exp(g_col - m_new_col)
        for g in range(nseq):
            kw_g = jnp.where(rseq == g, kw, 0.0)
            w_old = w_old_col[g * seq_len:g * seq_len + 1, :]
            C_ref[g, h] = w_old * C0_ref[g, h] + _dot_tn(v, kw_g.astype(BF16))
            n_ref[g, h] = w_old * n0_ref[g, h] + jnp.sum(kw_g, axis=0, keepdims=True)
        m_ref[:, h * LANES:(h + 1) * LANES] = jnp.broadcast_to(m_new_col, (R, LANES))


def _mlstm_sample(z, gates_c, m0_tok, C0, n0, g_head, batch, seq):
    rows = SAMPLE_SEQS * seq
    m = batch * seq

    def zspec(col):
        return pl.BlockSpec((rows, D), lambda i: (i, col))

    state_c = pl.BlockSpec((SAMPLE_SEQS, HEADS, HD, HD), lambda i: (i, 0, 0, 0))
    state_n = pl.BlockSpec((SAMPLE_SEQS, HEADS, 1, HD), lambda i: (i, 0, 0, 0))
    return pl.pallas_call(
        functools.partial(_mlstm_sample_kernel, seq),
        grid=(batch // SAMPLE_SEQS,),
        in_specs=[zspec(0), zspec(1), zspec(2), zspec(3),
                  pl.BlockSpec((rows, LANES), lambda i: (i, 0)),
                  pl.BlockSpec((rows, HEADS), lambda i: (i, 0)),
                  state_c, state_n,
                  pl.BlockSpec((HEADS, 1, HD), lambda i: (0, 0, 0))],
        out_specs=[pl.BlockSpec((rows, D), lambda i: (i, 0)),
                   state_c, state_n,
                   pl.BlockSpec((rows, HEADS * LANES), lambda i: (i, 0))],
        out_shape=[jax.ShapeDtypeStruct((m, D), BF16),
                   jax.ShapeDtypeStruct((batch, HEADS, HD, HD), F32),
                   jax.ShapeDtypeStruct((batch, HEADS, 1, HD), F32),
                   jax.ShapeDtypeStruct((m, HEADS * LANES), F32)],
        compiler_params=pltpu.CompilerParams(
            dimension_semantics=("arbitrary",), vmem_limit_bytes=VMEM_LIMIT),
        name="mlstm_sample",
    )(z, z, z, z, gates_c, m0_tok, C0, n0.reshape(batch, HEADS, 1, HD), g_head.reshape(HEADS, 1, HD))


def _mix_kernel(x_ref, hm_ref, u_ref, v_ref, ga_ref, gb_ref, gt1_ref, sh2_ref, sc2_ref,
                gsgu_ref, bsgu_ref, ws_ref, bs_ref, wpa_ref, wpb_ref, wout_ref, gffn_ref,
                wr_ref, br_ref, x1_ref, h2_ref, choice_ref, gate_ref, vg_ref, cnt_ref, yg_scr):
    tm = x_ref.shape[0]
    u = jax.nn.gelu(u_ref[...].astype(F32))
    vv = jax.nn.gelu(v_ref[...].astype(F32))
    mu = jnp.mean(vv, axis=-1, keepdims=True)
    var = jnp.mean(jnp.square(vv - mu), axis=-1, keepdims=True)
    vg = (vv - mu) * lax.rsqrt(var + NORM_EPS) * gsgu_ref[...] + bsgu_ref[...]
    vg_ref[...] = vg
    vgb = vg.astype(BF16)

    r = lax.broadcasted_iota(jnp.int32, (GM_CHUNK, GM_CHUNK), 0)
    s = lax.broadcasted_iota(jnp.int32, (GM_CHUNK, GM_CHUNK), 1)
    for g in range(GROUPS):
        w = jnp.where(s <= r, ws_ref[g], 0.0).astype(BF16)
        bias = bs_ref[:, g:g + 1]
        for c in range(tm // GM_CHUNK):
            rows = slice(c * GM_CHUNK, (c + 1) * GM_CHUNK)
            cols = slice(g * GD, (g + 1) * GD)
            mixed = _dot(w, vgb[rows, cols]) + bias
            yg_scr[rows, cols] = (u[rows, cols] * mixed).astype(BF16)

    a = _dot(hm_ref[...], wpa_ref[...])
    b = _dot(yg_scr[...], wpb_ref[...])
    merged = (jax.nn.sigmoid(ga_ref[...].astype(F32)) * a
              + jax.nn.sigmoid(gb_ref[...].astype(F32)) * b)
    x1 = x_ref[...] + gt1_ref[...] * _dot(merged.astype(BF16), wout_ref[...])
    x1_ref[...] = x1
    h2 = (_rms(x1) * gffn_ref[...]) * (1.0 + sc2_ref[...]) + sh2_ref[...]
    h2_ref[...] = _pack_rows(h2)

    lane = lax.broadcasted_iota(jnp.int32, (tm, LANES), 1)
    lg = jnp.where(lane < N_EXPERTS, _dot3(h2, wr_ref[...]) + br_ref[...], -jnp.inf)
    choice = jnp.zeros((tm, LANES), F32)
    vals = []
    for k in range(TOP_K):
        mx = jnp.max(lg, axis=1, keepdims=True)
        idx = jnp.min(jnp.where(lg == mx, lane, LANES), axis=1, keepdims=True)
        sel = lane == idx
        choice = jnp.where(sel, k + 1.0, choice)
        vals.append(mx)
        lg = jnp.where(sel, -jnp.inf, lg)
    choice_ref[...] = choice
    ex = [jnp.exp(v - vals[0]) for v in vals]
    denom = sum(ex)
    gates = jnp.zeros((tm, LANES), F32)
    for k in range(TOP_K):
        gates = jnp.where(lane == k, ex[k] / denom, gates)
    gate_ref[...] = gates[:, :TOP_K]

    @pl.when(pl.program_id(0) == 0)
    def _():
        cnt_ref[...] = jnp.zeros_like(cnt_ref)

    cnt_ref[...] += jnp.sum(jnp.where(choice > 0.0, 1.0, 0.0), axis=0, keepdims=True)


def _mix(x, hm, z, mod, per_token, tokens_per_seq, tm, p):
    m = x.shape[0]
    zcol = 4

    def zspec(blk):
        return pl.BlockSpec((tm, D), lambda i: (i, blk))

    def full(shape):
        return pl.BlockSpec(shape, lambda i: (0,) * len(shape))

    row = pl.BlockSpec((tm, D), lambda i: (i, 0))
    return pl.pallas_call(
        _mix_kernel,
        grid=(m // tm,),
        in_specs=[row, row, zspec(zcol), zspec(zcol + 1), zspec(zcol + 2), zspec(zcol + 3),
                  _mod_spec(per_token, tm, tokens_per_seq, 2),
                  _mod_spec(per_token, tm, tokens_per_seq, 3),
                  _mod_spec(per_token, tm, tokens_per_seq, 4),
                  full((1, D)), full((1, D)),
                  full((GROUPS, GM_CHUNK, GM_CHUNK)), full((GM_CHUNK, GROUPS)),
                  full((D, D)), full((D, D)), full((D, D)), full((1, D)),
                  full((D, LANES)), full((1, LANES))],
        out_specs=[row, pl.BlockSpec((tm, PACKED), lambda i: (i, 0)),
                   pl.BlockSpec((tm, LANES), lambda i: (i, 0)), pl.BlockSpec((tm, TOP_K), lambda i: (i, 0)),
                   row, full((1, LANES))],
        out_shape=[jax.ShapeDtypeStruct((m, D), F32),
                   jax.ShapeDtypeStruct((m, PACKED), jnp.int32),
                   jax.ShapeDtypeStruct((m, LANES), F32),
                   jax.ShapeDtypeStruct((m, TOP_K), F32),
                   jax.ShapeDtypeStruct((m, D), F32),
                   jax.ShapeDtypeStruct((1, LANES), F32)],
        scratch_shapes=[pltpu.VMEM((tm, D), BF16)],
        compiler_params=pltpu.CompilerParams(
            dimension_semantics=("arbitrary",), vmem_limit_bytes=VMEM_LIMIT),
        name="mix",
    )(x, hm, z, z, z, z, mod, mod, mod, p["g_sgu"], p["b_sgu"], p["w_s"], p["b_s"],
      p["w_pa"], p["w_pb"], p["w_out"], p["g_ffn"], p["w_router"], p["b_router"])


def _route_kernel(choice_ref, cnta_ref, cntb_ref, dest_ref, ex_ref, base_scr):
    tm = choice_ref.shape[0]
    lane = lax.broadcasted_iota(jnp.int32, (tm, LANES), 1)
    choice = choice_ref[...]
    onehot = jnp.where(choice > 0.0, 1.0, 0.0)

    @pl.when(pl.program_id(0) == 0)
    def _():
        cnt = cnta_ref[...] + cntb_ref[...]
        padded = jnp.floor((cnt + (MOE_ROWS - 1)) * (1.0 / MOE_ROWS)) * MOE_ROWS
        r = lax.broadcasted_iota(jnp.int32, (LANES, LANES), 0)
        s = lax.broadcasted_iota(jnp.int32, (LANES, LANES), 1)
        padded_col = jnp.sum(jnp.where(r == s, padded, 0.0), axis=1, keepdims=True)
        pstart = jnp.sum(jnp.where(r < s, padded_col, 0.0), axis=0, keepdims=True)
        base_scr[...] = pstart
        trow = lax.broadcasted_iota(jnp.int32, ex_ref.shape, 0)
        table = jnp.where(trow == 0, pstart * (1.0 / MOE_ROWS),
                          jnp.where(trow == 1, padded * (1.0 / MOE_ROWS), jnp.where(trow == 2, cnt, 0.0)))
        ex_ref[...] = table.astype(jnp.int32)

    r = lax.broadcasted_iota(jnp.int32, (tm, tm), 0)
    s = lax.broadcasted_iota(jnp.int32, (tm, tm), 1)
    before = _dot((s < r).astype(BF16), onehot.astype(BF16))
    slot = before + base_scr[...]
    dest = jnp.zeros((tm, LANES), F32)
    for k in range(TOP_K):
        d_k = jnp.sum(jnp.where(choice == k + 1.0, slot, 0.0), axis=1, keepdims=True)
        dest = jnp.where(lane == k, d_k, dest)
    dest_ref[...] = dest[:, :TOP_K].astype(jnp.int32)
    base_scr[...] += jnp.sum(onehot, axis=0, keepdims=True)


def _route(choice, cnt_a, cnt_b):
    n = choice.shape[0]
    tm = 256
    fixed = lambda i: (0, 0)
    return pl.pallas_call(
        _route_kernel,
        grid=(n // tm,),
        in_specs=[pl.BlockSpec((tm, LANES), lambda i: (i, 0)),
                  pl.BlockSpec((1, LANES), fixed), pl.BlockSpec((1, LANES), fixed)],
        out_specs=[pl.BlockSpec((tm, TOP_K), lambda i: (i, 0)), pl.BlockSpec((8, LANES), fixed)],
        out_shape=[jax.ShapeDtypeStruct((n, TOP_K), jnp.int32),
                   jax.ShapeDtypeStruct((8, LANES), jnp.int32)],
        scratch_shapes=[pltpu.VMEM((1, LANES), F32)],
        compiler_params=pltpu.CompilerParams(dimension_semantics=("arbitrary",)),
        name="route",
    )(choice, cnt_a, cnt_b)


def _sc_worker_base(per_worker):
    return (lax.axis_index("s") * SC_CORES + lax.axis_index("c")) * per_worker


def _sc_scatter_rows(rows_a, rows_b, idx_flat, n_out):
    na, nb = rows_a.shape[0], rows_b.shape[0]
    width, dtype = rows_a.shape[1], rows_a.dtype
    n = na + nb
    per_a, per_b = na // SC_WORKERS, nb // SC_WORKERS
    chunk = SC_SCATTER_CHUNK
    n_chunks = per_a // chunk
    assert per_a * SC_WORKERS == na and per_b * SC_WORKERS == nb and per_b % 8 == 0 and per_b <= chunk
    assert n_chunks * chunk == per_a and n_chunks % 2 == 0
    mesh = plsc.VectorSubcoreMesh(core_axis_name="c", subcore_axis_name="s")

    @functools.partial(
        pl.kernel, mesh=mesh,
        out_type=jax.ShapeDtypeStruct((n_out, width), dtype),
        scratch_types=[pltpu.VMEM((chunk,), jnp.int32)] * TOP_K + [pltpu.VMEM((per_b,), jnp.int32)]
                      + [pltpu.VMEM((chunk, width), dtype), pltpu.VMEM((chunk, width), dtype),
                         pltpu.VMEM((per_b, width), dtype)]
                      + [pltpu.SemaphoreType.DMA] * 3,
    )
    def scatter(a_hbm, b_hbm, idx_hbm, out_hbm, i0, i1, i2, i3, ib, rows0, rows1, rowsb, rsem0, rsem1, wsem):
        base = _sc_worker_base(per_a)
        idx_bufs = (i0, i1, i2, i3)
        bufs = ((rows0, rsem0), (rows1, rsem1))

        def off(j):
            return pl.multiple_of(base + j * chunk, 8)

        def read(j, buf):
            rows_v, sem = buf
            return pltpu.make_async_copy(a_hbm.at[pl.ds(off(j), chunk)], rows_v, sem)

        def spread(j, buf):
            rows_v, _ = buf
            read(j, buf).wait()
            for k in range(TOP_K):
                pltpu.sync_copy(idx_hbm.at[pl.ds(pl.multiple_of(k * n + off(j), 8), chunk)], idx_bufs[k])
            for k in range(TOP_K):
                pltpu.make_async_copy(rows_v, out_hbm.at[idx_bufs[k]], wsem).start()
            for k in range(TOP_K):
                pltpu.make_async_copy(rows_v, out_hbm.at[idx_bufs[k]], wsem).wait()

        read(0, bufs[0]).start()

        @pl.loop(0, n_chunks, step=2)
        def _(j):
            read(j + 1, bufs[1]).start()
            spread(j, bufs[0])

            @pl.when(j + 2 < n_chunks)
            def _():
                read(j + 2, bufs[0]).start()
            spread(j + 1, bufs[1])

        off_b = pl.multiple_of(_sc_worker_base(per_b), 8)
        pltpu.sync_copy(b_hbm.at[pl.ds(off_b, per_b)], rowsb)
        for k in range(TOP_K):
            pltpu.sync_copy(idx_hbm.at[pl.ds(pl.multiple_of(k * n + na + off_b, 8), per_b)], ib)
            pltpu.async_copy(rowsb, out_hbm.at[ib], wsem).wait()

    return scatter(rows_a, rows_b, idx_flat)


def _sc_gather_rows(table, idx_flat):
    b = idx_flat.shape[0]
    width, dtype = table.shape[1], table.dtype
    per_worker = b // SC_WORKERS
    chunk = SC_GATHER_CHUNK
    n_chunks = per_worker // chunk
    assert per_worker * SC_WORKERS == b and n_chunks * chunk == per_worker and n_chunks % 2 == 0
    mesh = plsc.VectorSubcoreMesh(core_axis_name="c", subcore_axis_name="s")

    @functools.partial(
        pl.kernel, mesh=mesh,
        out_type=jax.ShapeDtypeStruct((b, width), dtype),
        scratch_types=[pltpu.VMEM((chunk,), jnp.int32), pltpu.VMEM((chunk,), jnp.int32),
                       pltpu.VMEM((chunk, width), dtype), pltpu.VMEM((chunk, width), dtype),
                       pltpu.SemaphoreType.DMA, pltpu.SemaphoreType.DMA],
    )
    def gather(table_hbm, idx_hbm, out_hbm, idx0, idx1, rows0, rows1, sem0, sem1):
        base = _sc_worker_base(per_worker)
        bufs = ((idx0, rows0, sem0), (idx1, rows1, sem1))

        def off(j):
            return pl.multiple_of(base + j * chunk, 8)

        def start(j, buf):
            idx_v, rows_v, sem = buf
            pltpu.sync_copy(idx_hbm.at[pl.ds(off(j), chunk)], idx_v)
            pltpu.make_async_copy(table_hbm.at[idx_v], rows_v, sem).start()

        def finish(j, buf):
            idx_v, rows_v, sem = buf
            pltpu.make_async_copy(table_hbm.at[idx_v], rows_v, sem).wait()
            pltpu.sync_copy(rows_v, out_hbm.at[pl.ds(off(j), chunk)])

        start(0, bufs[0])

        @pl.loop(0, n_chunks, step=2)
        def _(j):
            start(j + 1, bufs[1])
            finish(j, bufs[0])

            @pl.when(j + 2 < n_chunks)
            def _():
                start(j + 2, bufs[0])
            finish(j + 1, bufs[1])

    return gather(table, idx_flat)


def _moe_kernel(first_ref, nblk_ref, cnt_ref, xs_hbm, wgu_ref, bgu_ref, wd_ref, bd_ref, out_hbm,
                xbuf, obuf, wgu_bf, wd_bf, xsem, osem):
    e = pl.program_id(0)
    first, nblk, cnt = first_ref[e], nblk_ref[e], cnt_ref[e]

    def rows(j):
        return pl.ds(pl.multiple_of((first + j) * MOE_ROWS, MOE_ROWS), MOE_ROWS)

    def x_copy(j, slot):
        return pltpu.make_async_copy(xs_hbm.at[rows(j)], xbuf.at[slot], xsem.at[slot])

    def o_copy(j, slot):
        return pltpu.make_async_copy(obuf.at[slot], out_hbm.at[rows(j)], osem.at[slot])

    @pl.when(nblk > 0)
    def _():
        x_copy(0, 0).start(priority=ROW_DMA_PRIORITY)
        wgu_bf[...] = wgu_ref[...].astype(BF16)
        wd_bf[...] = wd_ref[...].astype(BF16)

        def block(j, carry):
            slot = j % 2
            x_copy(j, slot).wait()

            @pl.when(j + 1 < nblk)
            def _():
                x_copy(j + 1, 1 - slot).start(priority=ROW_DMA_PRIORITY)

            @pl.when(j >= 2)
            def _():
                o_copy(j - 2, slot).wait()

            row = lax.broadcasted_iota(jnp.int32, (MOE_ROWS, 1), 0)
            x = _unpack_rows(jnp.where(row < cnt - j * MOE_ROWS, xbuf[slot], 0)).astype(BF16)
            gu = _dot(x, wgu_bf[...]) + bgu_ref[...]
            gate = jnp.minimum(gu[:, :D_FF], SWIGLU_LIMIT)
            up = jnp.clip(gu[:, D_FF:], -SWIGLU_LIMIT, SWIGLU_LIMIT)
            act = gate * jax.nn.sigmoid(SWIGLU_ALPHA * gate) * (up + 1.0)
            obuf[slot] = _pack_rows(_dot(act.astype(BF16), wd_bf[...]) + bd_ref[...])
            o_copy(j, slot).start(priority=ROW_DMA_PRIORITY)
            return carry

        lax.fori_loop(0, nblk, block, 0)

        @pl.when(nblk >= 2)
        def _():
            o_copy(nblk - 2, nblk % 2).wait()
        o_copy(nblk - 1, (nblk - 1) % 2).wait()


def _moe(xs, first_block, n_blocks, counts, w_gu, b_gu, w_down, b_down):
    grid_spec = pltpu.PrefetchScalarGridSpec(
        num_scalar_prefetch=3,
        grid=(N_EXPERTS,),
        in_specs=[pl.BlockSpec(memory_space=pl.ANY),
                  pl.BlockSpec((None, D, 2 * D_FF), lambda e, *_: (e, 0, 0)),
                  pl.BlockSpec((None, 1, 2 * D_FF), lambda e, *_: (e, 0, 0)),
                  pl.BlockSpec((None, D_FF, D), lambda e, *_: (e, 0, 0)),
                  pl.BlockSpec((None, 1, D), lambda e, *_: (e, 0, 0))],
        out_specs=pl.BlockSpec(memory_space=pl.ANY),
        scratch_shapes=[pltpu.VMEM((2, MOE_ROWS, PACKED), jnp.int32), pltpu.VMEM((2, MOE_ROWS, PACKED), jnp.int32),
                        pltpu.VMEM((D, 2 * D_FF), BF16), pltpu.VMEM((D_FF, D), BF16),
                        pltpu.SemaphoreType.DMA((2,)), pltpu.SemaphoreType.DMA((2,))],
    )
    return pl.pallas_call(
        _moe_kernel,
        grid_spec=grid_spec,
        out_shape=jax.ShapeDtypeStruct(xs.shape, jnp.int32),
        compiler_params=pltpu.CompilerParams(
            dimension_semantics=("arbitrary",), vmem_limit_bytes=MOE_VMEM_LIMIT),
        name="moe",
    )(first_block, n_blocks, counts, xs, w_gu, b_gu.reshape(N_EXPERTS, 1, 2 * D_FF), w_down,
      b_down.reshape(N_EXPERTS, 1, D))


def _final_kernel(x1_ref, yk_ref, gate_ref, gt2_ref, g_ref, o_ref):
    y2 = gate_ref[:, 0:1] * _unpack_rows(yk_ref[0])
    for k in range(1, TOP_K):
        y2 = y2 + gate_ref[:, k:k + 1] * _unpack_rows(yk_ref[k])
    o_ref[...] = _rms(x1_ref[...] + gt2_ref[...] * y2) * g_ref[...]


def _final(x1, yk, gates, row_off, mod, per_token, tokens_per_seq, g_final):
    tm = 256
    m = x1.shape[0]
    off = row_off // tm
    return pl.pallas_call(
        _final_kernel,
        grid=(m // tm,),
        in_specs=[pl.BlockSpec((tm, D), lambda i: (i, 0)),
                  pl.BlockSpec((TOP_K, tm, PACKED), lambda i: (0, i + off, 0)),
                  pl.BlockSpec((tm, TOP_K), lambda i: (i, 0)),
                  _mod_spec(per_token, tm, tokens_per_seq, 5),
                  pl.BlockSpec((1, D), lambda i: (0, 0))],
        out_specs=pl.BlockSpec((tm, D), lambda i: (i, 0)),
        out_shape=jax.ShapeDtypeStruct((m, D), F32),
        compiler_params=pltpu.CompilerParams(
            dimension_semantics=("arbitrary",), vmem_limit_bytes=VMEM_LIMIT),
        name="final",
    )(x1, yk, gates, mod, g_final.reshape(1, D))


def kernel(x_prompt, x_sample, state_C, state_n, state_m, c_prompt, c_sample, w_ada, b_ada, g_mix, w_in,
           b_if, g_head, g_sgu, b_sgu, w_s, b_s, w_pa, w_pb, w_out, g_ffn, w_router, b_router, w_gu, b_gu,
           w_down, b_down, g_final):
    depth = w_ada.shape[0]
    assert depth == 1
    bp, tp, _ = x_prompt.shape
    bs, ts, _ = x_sample.shape
    mp, ms = bp * tp, bs * ts
    assert tp % ML_CHUNK == 0 and ts <= ML_CHUNK and GM_CHUNK % ts == 0

    w_in0 = w_in[0]
    nqkvo = 4 * D
    w_qkvo = w_in0[:, :nqkvo].astype(BF16)
    w_gate = w_in0[:, nqkvo + 2 * HEADS:].astype(BF16)
    w_if = jnp.pad(w_in0[:, nqkvo:nqkvo + 2 * HEADS], ((0, 0), (0, LANES - 2 * HEADS)))
    b_if_p = jnp.pad(b_if[0], (0, LANES - 2 * HEADS)).reshape(1, LANES)
    reps = GM_CHUNK // ts
    eye_r = jnp.eye(reps, dtype=F32)
    w_s_sample = jnp.einsum("ab,gts->gatbs", eye_r, w_s[0][:, :ts, :ts]).reshape(GROUPS, GM_CHUNK, GM_CHUNK)
    b_s_prompt = b_s[0].T
    b_s_sample = jnp.tile(b_s[0][:, :ts].T, (reps, 1))
    mix_p = {
        "g_sgu": g_sgu[0].reshape(1, D), "b_sgu": b_sgu[0].reshape(1, D),
        "w_pa": w_pa[0].astype(BF16), "w_pb": w_pb[0].astype(BF16), "w_out": w_out[0].astype(BF16),
        "g_ffn": g_ffn[0].reshape(1, D),
        "w_router": jnp.pad(w_router[0], ((0, 0), (0, LANES - N_EXPERTS))),
        "b_router": jnp.pad(b_router[0], (0, LANES - N_EXPERTS)).reshape(1, LANES),
    }
    mix_prompt = dict(mix_p, w_s=w_s[0], b_s=b_s_prompt)
    mix_sample = dict(mix_p, w_s=w_s_sample, b_s=b_s_sample)

    mod = _ada(jnp.concatenate([c_prompt, c_sample], axis=0), w_ada[0], b_ada[0])
    mod_p = mod[:bp].reshape(bp, 1, N_MOD * D)
    mod_s = jnp.repeat(mod[bp:], ts, axis=0)

    xp = x_prompt.reshape(mp, D)
    xs = x_sample.reshape(ms, D)
    z_p, _, gt_p = _in_proj(xp, mod_p, False, tp, g_mix[0], w_qkvo, w_gate, w_if, b_if_p, 512, BF16)
    z_s, gc_s, _ = _in_proj(xs, mod_s, True, ts, g_mix[0], w_qkvo, w_gate, w_if, b_if_p, 128, F32)

    hm_p, C_p, n_p, m_p = _mlstm_prompt(z_p, gt_p, g_head[0], bp, tp)
    m0_tok = jnp.repeat(state_m[0], ts, axis=0)
    hm_s, C_s, n_s, m_s = _mlstm_sample(z_s, gc_s, m0_tok, state_C[0], state_n[0], g_head[0], bs, ts)

    x1_p, h2_p, ch_p, gates_p, _, cnt_p = _mix(xp, hm_p, z_p, mod_p, False, tp, 256, mix_prompt)
    x1_s, h2_s, ch_s, gates_s, vg_s, cnt_s = _mix(xs, hm_s, z_s, mod_s, True, ts, 128, mix_sample)

    n_tok = mp + ms
    n_blocks = -(-(n_tok * TOP_K + N_EXPERTS * (MOE_ROWS - 1)) // MOE_ROWS)
    dest, ex_table = _route(jnp.concatenate([ch_p, ch_s], axis=0), cnt_p, cnt_s)
    dest_kmajor = dest.T.reshape(TOP_K * n_tok)
    xslots = _sc_scatter_rows(h2_p, h2_s, dest_kmajor, n_blocks * MOE_ROWS)
    yb = _moe(xslots, ex_table[0, :N_EXPERTS], ex_table[1, :N_EXPERTS], ex_table[2, :N_EXPERTS],
              w_gu[0], b_gu[0], w_down[0], b_down[0])
    yk = _sc_gather_rows(yb, dest_kmajor).reshape(TOP_K, n_tok, PACKED)

    y_p = _final(x1_p, yk, gates_p, 0, mod_p, False, tp, g_final)
    y_s = _final(x1_s, yk, gates_s, mp, mod_s, True, ts, g_final)

    return (y_p.reshape(bp, tp, D), y_s.reshape(bs, ts, D),
            C_p[None], n_p.reshape(1, bp, HEADS, HD), m_p[:, :, 0, 0][None],
            C_s[None], n_s.reshape(1, bs, HEADS, HD),
            m_s.reshape(bs, ts, HEADS, LANES)[:, 0, :, 0][None],
            vg_s.reshape(1, bs, ts, D))
```

```python
import functools

import jax
import jax.numpy as jnp
from jax import lax
from jax.experimental import pallas as pl
from jax.experimental.pallas import tpu as pltpu
from jax.experimental.pallas import tpu_sc as plsc

F32 = jnp.float32
BF16 = jnp.bfloat16

D = 1024
HEADS = 4
HD = D // HEADS
ML_CHUNK = 128
GROUPS = 4
GD = D // GROUPS
GM_CHUNK = 128
N_EXPERTS = 32
TOP_K = 4
D_FF = D
SWIGLU_LIMIT = 7.0
SWIGLU_ALPHA = 1.702
NORM_EPS = 1e-6
N_MOD = 6
PACKED = D // 2
LANES = 128
SC_CORES = 2
SC_SUBCORES = 16
SC_WORKERS = SC_CORES * SC_SUBCORES
IN_COLS = 2048
MOE_ROWS = 256
ROW_DMA_PRIORITY = 1
MLSTM_SEQS = 1
SAMPLE_SEQS = 4
SC_SCATTER_CHUNK = 64
SC_GATHER_CHUNK = 96
VMEM_LIMIT = 48 * 1024 * 1024
MOE_VMEM_LIMIT = 56 * 1024 * 1024


def _dot(a, b):
    return jnp.dot(a, b, preferred_element_type=F32)


def _dot_nt(a, b):
    return lax.dot_general(a, b, (((1,), (1,)), ((), ())), preferred_element_type=F32)


def _dot_tn(a, b):
    return lax.dot_general(a, b, (((0,), (0,)), ((), ())), preferred_element_type=F32)


def _split_bf16(a):
    hi = a.astype(BF16)
    lo = (a - hi.astype(F32)).astype(BF16)
    return hi, lo


def _dot3(a, b):
    ah, al = _split_bf16(a)
    bh, bl = _split_bf16(b)
    return _dot(ah, bh) + (_dot(ah, bl) + _dot(al, bh))


def _log_sigmoid(x):
    return jnp.minimum(x, 0.0) - jnp.log1p(jnp.exp(-jnp.abs(x)))


def _rms(x):
    return x * lax.rsqrt(jnp.mean(x * x, axis=-1, keepdims=True) + NORM_EPS)


def _pack_rows(x):
    bits = lax.bitcast_convert_type(x.astype(BF16).astype(F32), jnp.uint32)
    word = (bits[:, :PACKED] & jnp.uint32(0xFFFF0000)) | (bits[:, PACKED:] >> 16)
    return lax.bitcast_convert_type(word, jnp.int32)


def _unpack_rows(w):
    bits = lax.bitcast_convert_type(w, jnp.uint32)
    left = lax.bitcast_convert_type(bits & jnp.uint32(0xFFFF0000), F32)
    right = lax.bitcast_convert_type(bits << 16, F32)
    return jnp.concatenate([left, right], axis=1)


def _mod_spec(per_token, tm, tokens_per_seq, col):
    if per_token:
        return pl.BlockSpec((tm, D), lambda i, *_: (i, col))
    return pl.BlockSpec((None, 1, D), lambda i, *_: ((i * tm) // tokens_per_seq, 0, col))


def _ada_kernel(c_ref, w_ref, b_ref, o_ref):
    c = c_ref[...]
    s = (c * jax.nn.sigmoid(c)).astype(BF16)
    o_ref[...] = _dot(s, w_ref[...].astype(BF16)) + b_ref[...]


def _ada(c, w, b):
    m, n = c.shape[0], w.shape[1]
    tn = 512
    return pl.pallas_call(
        _ada_kernel,
        grid=(n // tn,),
        in_specs=[pl.BlockSpec((m, D), lambda j: (0, 0)),
                  pl.BlockSpec((D, tn), lambda j: (0, j)),
                  pl.BlockSpec((1, tn), lambda j: (0, j))],
        out_specs=pl.BlockSpec((m, tn), lambda j: (0, j)),
        out_shape=jax.ShapeDtypeStruct((m, n), F32),
        name="ada",
    )(c, w, b.reshape(1, n))


def _in_kernel(x_ref, g_ref, sh_ref, sc_ref, wa_ref, wb_ref, wif_ref, bif_ref, z_ref, gc_ref, gt_ref):
    h = (_rms(x_ref[...]) * g_ref[...]) * (1.0 + sc_ref[...]) + sh_ref[...]
    hb = h.astype(BF16)
    gates = _dot3(h, wif_ref[...]) + bif_ref[...]
    gc_ref[...] = gates
    gt_ref[...] = gates.T[:2 * HEADS, :]
    half = wa_ref.shape[1]
    for w_ref, col0 in ((wa_ref, 0), (wb_ref, half)):
        for c in range(half // IN_COLS):
            cols = slice(c * IN_COLS, (c + 1) * IN_COLS)
            z_ref[:, col0 + c * IN_COLS:col0 + (c + 1) * IN_COLS] = _dot(hb, w_ref[:, cols]).astype(z_ref.dtype)


def _in_proj(x, mod, per_token, tokens_per_seq, g_mix, w_a, w_b, w_if, b_if, tm, z_dtype):
    m = x.shape[0]
    half = w_a.shape[1]
    gt_len = tokens_per_seq if tokens_per_seq % tm == 0 else m
    gt_tiles = gt_len // tm
    resident = functools.partial(pl.BlockSpec, index_map=lambda i: (0, 0), pipeline_mode=pl.Buffered(1))
    return pl.pallas_call(
        _in_kernel,
        grid=(m // tm,),
        in_specs=[pl.BlockSpec((tm, D), lambda i: (i, 0)),
                  pl.BlockSpec((1, D), lambda i: (0, 0)),
                  _mod_spec(per_token, tm, tokens_per_seq, 0),
                  _mod_spec(per_token, tm, tokens_per_seq, 1),
                  resident((D, half)), resident((D, half)),
                  pl.BlockSpec((D, LANES), lambda i: (0, 0)),
                  pl.BlockSpec((1, LANES), lambda i: (0, 0))],
        out_specs=[pl.BlockSpec((tm, 2 * half), lambda i: (i, 0)),
                   pl.BlockSpec((tm, LANES), lambda i: (i, 0)),
                   pl.BlockSpec((None, 2 * HEADS, tm), lambda i: (i // gt_tiles, 0, i % gt_tiles))],
        out_shape=[jax.ShapeDtypeStruct((m, 2 * half), z_dtype),
                   jax.ShapeDtypeStruct((m, LANES), F32),
                   jax.ShapeDtypeStruct((m // gt_len, 2 * HEADS, gt_len), F32)],
        compiler_params=pltpu.CompilerParams(
            dimension_semantics=("arbitrary",), vmem_limit_bytes=MOE_VMEM_LIMIT),
        name="in_proj",
    )(x, g_mix.reshape(1, D), mod, mod, w_a, w_b, w_if, b_if)


def _mlstm_prompt_kernel(q_ref, k_ref, v_ref, o_ref, gt_ref, gh_ref, hm_ref, C_ref, n_ref, m_ref):
    nseq, L = q_ref.shape[0], q_ref.shape[1]

    @pl.when(pl.program_id(1) == 0)
    def _():
        C_ref[...] = jnp.zeros_like(C_ref)
        n_ref[...] = jnp.zeros_like(n_ref)
        m_ref[...] = jnp.zeros_like(m_ref)

    r = lax.broadcasted_iota(jnp.int32, (L, L), 0)
    s = lax.broadcasted_iota(jnp.int32, (L, L), 1)
    eye = r == s
    causal = s <= r

    def to_col(x_row):
        return jnp.sum(jnp.where(eye, x_row, 0.0), axis=1, keepdims=True)

    for b in range(nseq):
        gates = gt_ref[b]
        for h in range(HEADS):
            cols = slice(h * HD, (h + 1) * HD)
            ig_row = gates[h:h + 1, :]
            lf_row = _log_sigmoid(gates[HEADS + h:HEADS + h + 1, :])
            lf_col = to_col(lf_row)
            b_row = jnp.sum(jnp.where(r <= s, lf_col, 0.0), axis=0, keepdims=True)
            b_col = to_col(b_row)
            m_prev = m_ref[b, h][:, :1]

            logD = jnp.where(causal, b_col - b_row + ig_row, -jnp.inf)
            inter = b_col + m_prev
            mt = jnp.maximum(jnp.max(logD, axis=1, keepdims=True), inter)
            q = q_ref[b, :, cols]
            ks = k_ref[b, :, cols] * (HD ** -0.5)
            v = v_ref[b, :, cols]
            S = _dot_nt(q, ks) * jnp.exp(logD - mt)
            w_int = jnp.exp(inter - mt)
            Cmat = C_ref[b, h]
            nvec = n_ref[b, h]
            num = _dot(S.astype(BF16), v) + w_int * _dot_nt(q, Cmat.astype(BF16))
            nq = jnp.sum(q.astype(F32) * nvec, axis=1, keepdims=True)
            den = jnp.sum(S, axis=1, keepdims=True) + w_int * nq
            hh = num / jnp.maximum(jnp.abs(den), jnp.exp(-mt))
            hg = jax.nn.sigmoid(o_ref[b, :, cols].astype(F32)) * hh
            hm_ref[b, :, cols] = (_rms(hg) * gh_ref[h]).astype(hm_ref.dtype)

            bL = b_row[:, L - 1:L]
            g_row = bL - b_row + ig_row
            m_new = jnp.maximum(bL + m_prev, jnp.max(g_row, axis=1, keepdims=True))
            w_old = jnp.exp(bL + m_prev - m_new)
            kw = ks.astype(F32) * to_col(jnp.exp(g_row - m_new))
            C_ref[b, h] = w_old * Cmat + _dot_tn(v, kw.astype(BF16))
            n_ref[b, h] = w_old * nvec + jnp.sum(kw, axis=0, keepdims=True)
            m_ref[b, h] = jnp.broadcast_to(m_new, (1, LANES))


def _mlstm_prompt(z, gates_t, g_head, batch, seq):
    nc = seq // ML_CHUNK
    nb = MLSTM_SEQS
    z3 = z.reshape(batch, seq, z.shape[1])

    def zspec(col):
        return pl.BlockSpec((nb, ML_CHUNK, D), lambda b, c: (b, c, col))

    def state(last):
        return pl.BlockSpec((nb, HEADS) + last, lambda b, c: (b, 0, 0, 0))

    return pl.pallas_call(
        _mlstm_prompt_kernel,
        grid=(batch // nb, nc),
        in_specs=[zspec(0), zspec(1), zspec(2), zspec(3),
                  pl.BlockSpec((nb, 2 * HEADS, ML_CHUNK), lambda b, c: (b, 0, c)),
                  pl.BlockSpec((HEADS, 1, HD), lambda b, c: (0, 0, 0))],
        out_specs=[pl.BlockSpec((nb, ML_CHUNK, D), lambda b, c: (b, c, 0)),
                   state((HD, HD)), state((1, HD)), state((1, LANES))],
        out_shape=[jax.ShapeDtypeStruct((batch, seq, D), BF16),
                   jax.ShapeDtypeStruct((batch, HEADS, HD, HD), F32),
                   jax.ShapeDtypeStruct((batch, HEADS, 1, HD), F32),
                   jax.ShapeDtypeStruct((batch, HEADS, 1, LANES), F32)],
        compiler_params=pltpu.CompilerParams(dimension_semantics=("arbitrary", "arbitrary")),
        name="mlstm_prompt",
    )(z3, z3, z3, z3, gates_t, g_head.reshape(HEADS, 1, HD))


def _mlstm_sample_kernel(seq_len, q_ref, k_ref, v_ref, o_ref, gc_ref, m0_ref, C0_ref, n0_ref, gh_ref,
                         hm_ref, C_ref, n_ref, m_ref):
    R = q_ref.shape[0]
    nseq = R // seq_len
    r = lax.broadcasted_iota(jnp.int32, (R, R), 0)
    s = lax.broadcasted_iota(jnp.int32, (R, R), 1)
    rseq = lax.broadcasted_iota(jnp.int32, (R, 1), 0) // seq_len
    eye = r == s
    same = (r // seq_len) == (s // seq_len)
    causal = same & (s <= r)

    def to_row(x_col):
        return jnp.sum(jnp.where(eye, x_col, 0.0), axis=0, keepdims=True)

    gc = gc_ref[...]
    for h in range(HEADS):
        cols = slice(h * HD, (h + 1) * HD)
        ig_col = gc[:, h:h + 1]
        lf_col = _log_sigmoid(gc[:, HEADS + h:HEADS + h + 1])
        lf_row = to_row(lf_col)
        b_col = jnp.sum(jnp.where(causal, lf_row, 0.0), axis=1, keepdims=True)
        bL_col = jnp.sum(jnp.where(same, lf_row, 0.0), axis=1, keepdims=True)
        b_row = to_row(b_col)
        ig_row = to_row(ig_col)
        m0_col = m0_ref[:, h:h + 1]

        logD = jnp.where(causal, b_col - b_row + ig_row, -jnp.inf)
        inter = b_col + m0_col
        mt = jnp.maximum(jnp.max(logD, axis=1, keepdims=True), inter)
        qf = q_ref[:, cols]
        q = qf.astype(BF16)
        ksf = k_ref[:, cols] * (HD ** -0.5)
        v = v_ref[:, cols].astype(BF16)
        S = _dot_nt(q, ksf.astype(BF16)) * jnp.exp(logD - mt)
        w_int = jnp.exp(inter - mt)

        Cq = jnp.zeros((R, HD), F32)
        nq = jnp.zeros((R, 1), F32)
        for g in range(nseq):
            Cq = jnp.where(rseq == g, _dot_nt(q, C0_ref[g, h].astype(BF16)), Cq)
            nq = jnp.where(rseq == g, jnp.sum(qf * n0_ref[g, h], axis=1, keepdims=True), nq)
        num = _dot(S.astype(BF16), v) + w_int * Cq
        den = jnp.sum(S, axis=1, keepdims=True) + w_int * nq
        hh = num / jnp.maximum(jnp.abs(den), jnp.exp(-mt))
        hg = jax.nn.sigmoid(o_ref[:, cols]) * hh
        hm_ref[:, cols] = (_rms(hg) * gh_ref[h]).astype(hm_ref.dtype)

        g_col = bL_col - b_col + ig_col
        gmax_col = jnp.max(jnp.where(same, to_row(g_col), -jnp.inf), axis=1, keepdims=True)
        m_new_col = jnp.maximum(bL_col + m0_col, gmax_col)
        w_old_col = jnp.exp(bL_col + m0_col - m_new_col)
        kw = ksf * jnp.exp(g_col - m_new_col)
        for g in range(nseq):
            kw_g = jnp.where(rseq == g, kw, 0.0)
            w_old = w_old_col[g * seq_len:g * seq_len + 1, :]
            C_ref[g, h] = w_old * C0_ref[g, h] + _dot_tn(v, kw_g.astype(BF16))
            n_ref[g, h] = w_old * n0_ref[g, h] + jnp.sum(kw_g, axis=0, keepdims=True)
        m_ref[:, h * LANES:(h + 1) * LANES] = jnp.broadcast_to(m_new_col, (R, LANES))


def _mlstm_sample(z, gates_c, m0_tok, C0, n0, g_head, batch, seq):
    rows = SAMPLE_SEQS * seq
    m = batch * seq

    def zspec(col):
        return pl.BlockSpec((rows, D), lambda i: (i, col))

    state_c = pl.BlockSpec((SAMPLE_SEQS, HEADS, HD, HD), lambda i: (i, 0, 0, 0))
    state_n = pl.BlockSpec((SAMPLE_SEQS, HEADS, 1, HD), lambda i: (i, 0, 0, 0))
    return pl.pallas_call(
        functools.partial(_mlstm_sample_kernel, seq),
        grid=(batch // SAMPLE_SEQS,),
        in_specs=[zspec(0), zspec(1), zspec(2), zspec(3),
                  pl.BlockSpec((rows, LANES), lambda i: (i, 0)),
                  pl.BlockSpec((rows, HEADS), lambda i: (i, 0)),
                  state_c, state_n,
                  pl.BlockSpec((HEADS, 1, HD), lambda i: (0, 0, 0))],
        out_specs=[pl.BlockSpec((rows, D), lambda i: (i, 0)),
                   state_c, state_n,
                   pl.BlockSpec((rows, HEADS * LANES), lambda i: (i, 0))],
        out_shape=[jax.ShapeDtypeStruct((m, D), BF16),
                   jax.ShapeDtypeStruct((batch, HEADS, HD, HD), F32),
                   jax.ShapeDtypeStruct((batch, HEADS, 1, HD), F32),
                   jax.ShapeDtypeStruct((m, HEADS * LANES), F32)],
        compiler_params=pltpu.CompilerParams(
            dimension_semantics=("arbitrary",), vmem_limit_bytes=VMEM_LIMIT),
        name="mlstm_sample",
    )(z, z, z, z, gates_c, m0_tok, C0, n0.reshape(batch, HEADS, 1, HD), g_head.reshape(HEADS, 1, HD))


def _mix_kernel(x_ref, hm_ref, u_ref, v_ref, ga_ref, gb_ref, gt1_ref, sh2_ref, sc2_ref,
                gsgu_ref, bsgu_ref, ws_ref, bs_ref, wpa_ref, wpb_ref, wout_ref, gffn_ref,
                wr_ref, br_ref, x1_ref, h2_ref, choice_ref, gate_ref, vg_ref, cnt_ref, yg_scr):
    tm = x_ref.shape[0]
    u = jax.nn.gelu(u_ref[...].astype(F32))
    vv = jax.nn.gelu(v_ref[...].astype(F32))
    mu = jnp.mean(vv, axis=-1, keepdims=True)
    var = jnp.mean(jnp.square(vv - mu), axis=-1, keepdims=True)
    vg = (vv - mu) * lax.rsqrt(var + NORM_EPS) * gsgu_ref[...] + bsgu_ref[...]
    vg_ref[...] = vg
    vgb = vg.astype(BF16)

    r = lax.broadcasted_iota(jnp.int32, (GM_CHUNK, GM_CHUNK), 0)
    s = lax.broadcasted_iota(jnp.int32, (GM_CHUNK, GM_CHUNK), 1)
    for g in range(GROUPS):
        w = jnp.where(s <= r, ws_ref[g], 0.0).astype(BF16)
        bias = bs_ref[:, g:g + 1]
        for c in range(tm // GM_CHUNK):
            rows = slice(c * GM_CHUNK, (c + 1) * GM_CHUNK)
            cols = slice(g * GD, (g + 1) * GD)
            mixed = _dot(w, vgb[rows, cols]) + bias
            yg_scr[rows, cols] = (u[rows, cols] * mixed).astype(BF16)

    a = _dot(hm_ref[...], wpa_ref[...])
    b = _dot(yg_scr[...], wpb_ref[...])
    merged = (jax.nn.sigmoid(ga_ref[...].astype(F32)) * a
              + jax.nn.sigmoid(gb_ref[...].astype(F32)) * b)
    x1 = x_ref[...] + gt1_ref[...] * _dot(merged.astype(BF16), wout_ref[...])
    x1_ref[...] = x1
    h2 = (_rms(x1) * gffn_ref[...]) * (1.0 + sc2_ref[...]) + sh2_ref[...]
    h2_ref[...] = _pack_rows(h2)

    lane = lax.broadcasted_iota(jnp.int32, (tm, LANES), 1)
    lane_f = lane.astype(F32)
    lg = jnp.where(lane < N_EXPERTS, _dot3(h2, wr_ref[...]) + br_ref[...], -jnp.inf)
    choice = jnp.zeros((tm, LANES), F32)
    vals = []
    for k in range(TOP_K):
        mx = jnp.max(lg, axis=1, keepdims=True)
        sel = lane_f == jnp.min(jnp.where(lg == mx, lane_f, float(LANES)), axis=1, keepdims=True)
        choice = jnp.where(sel, k + 1.0, choice)
        vals.append(mx)
        lg = jnp.where(sel, -jnp.inf, lg)
    choice_ref[...] = choice
    ex = [jnp.exp(v - vals[0]) for v in vals]
    denom = sum(ex)
    gates = jnp.zeros((tm, LANES), F32)
    for k in range(TOP_K):
        gates = jnp.where(lane == k, ex[k] / denom, gates)
    gate_ref[...] = gates[:, :TOP_K]

    @pl.when(pl.program_id(0) == 0)
    def _():
        cnt_ref[...] = jnp.zeros_like(cnt_ref)

    cnt_ref[...] += jnp.sum(jnp.where(choice > 0.0, 1.0, 0.0), axis=0, keepdims=True)


def _mix(x, hm, z, mod, per_token, tokens_per_seq, tm, p):
    m = x.shape[0]
    zcol = 4

    def zspec(blk):
        return pl.BlockSpec((tm, D), lambda i: (i, blk))

    def full(shape):
        return pl.BlockSpec(shape, lambda i: (0,) * len(shape))

    row = pl.BlockSpec((tm, D), lambda i: (i, 0))
    return pl.pallas_call(
        _mix_kernel,
        grid=(m // tm,),
        in_specs=[row, row, zspec(zcol), zspec(zcol + 1), zspec(zcol + 2), zspec(zcol + 3),
                  _mod_spec(per_token, tm, tokens_per_seq, 2),
                  _mod_spec(per_token, tm, tokens_per_seq, 3),
                  _mod_spec(per_token, tm, tokens_per_seq, 4),
                  full((1, D)), full((1, D)),
                  full((GROUPS, GM_CHUNK, GM_CHUNK)), full((GM_CHUNK, GROUPS)),
                  full((D, D)), full((D, D)), full((D, D)), full((1, D)),
                  full((D, LANES)), full((1, LANES))],
        out_specs=[row, pl.BlockSpec((tm, PACKED), lambda i: (i, 0)),
                   pl.BlockSpec((tm, LANES), lambda i: (i, 0)), pl.BlockSpec((tm, TOP_K), lambda i: (i, 0)),
                   row, full((1, LANES))],
        out_shape=[jax.ShapeDtypeStruct((m, D), F32),
                   jax.ShapeDtypeStruct((m, PACKED), jnp.int32),
                   jax.ShapeDtypeStruct((m, LANES), F32),
                   jax.ShapeDtypeStruct((m, TOP_K), F32),
                   jax.ShapeDtypeStruct((m, D), F32),
                   jax.ShapeDtypeStruct((1, LANES), F32)],
        scratch_shapes=[pltpu.VMEM((tm, D), BF16)],
        compiler_params=pltpu.CompilerParams(
            dimension_semantics=("arbitrary",), vmem_limit_bytes=VMEM_LIMIT),
        name="mix",
    )(x, hm, z, z, z, z, mod, mod, mod, p["g_sgu"], p["b_sgu"], p["w_s"], p["b_s"],
      p["w_pa"], p["w_pb"], p["w_out"], p["g_ffn"], p["w_router"], p["b_router"])


def _route_kernel(choice_ref, cnta_ref, cntb_ref, dest_ref, ex_ref, base_scr):
    tm = choice_ref.shape[0]
    lane = lax.broadcasted_iota(jnp.int32, (tm, LANES), 1)
    choice = choice_ref[...]
    onehot = jnp.where(choice > 0.0, 1.0, 0.0)

    @pl.when(pl.program_id(0) == 0)
    def _():
        cnt = cnta_ref[...] + cntb_ref[...]
        padded = jnp.floor((cnt + (MOE_ROWS - 1)) * (1.0 / MOE_ROWS)) * MOE_ROWS
        r = lax.broadcasted_iota(jnp.int32, (LANES, LANES), 0)
        s = lax.broadcasted_iota(jnp.int32, (LANES, LANES), 1)
        padded_col = jnp.sum(jnp.where(r == s, padded, 0.0), axis=1, keepdims=True)
        pstart = jnp.sum(jnp.where(r < s, padded_col, 0.0), axis=0, keepdims=True)
        base_scr[...] = pstart
        trow = lax.broadcasted_iota(jnp.int32, ex_ref.shape, 0)
        table = jnp.where(trow == 0, pstart * (1.0 / MOE_ROWS),
                          jnp.where(trow == 1, padded * (1.0 / MOE_ROWS), jnp.where(trow == 2, cnt, 0.0)))
        ex_ref[...] = table.astype(jnp.int32)

    r = lax.broadcasted_iota(jnp.int32, (tm, tm), 0)
    s = lax.broadcasted_iota(jnp.int32, (tm, tm), 1)
    before = _dot((s < r).astype(BF16), onehot.astype(BF16))
    slot = before + base_scr[...]
    dest = jnp.zeros((tm, LANES), F32)
    for k in range(TOP_K):
        d_k = jnp.sum(jnp.where(choice == k + 1.0, slot, 0.0), axis=1, keepdims=True)
        dest = jnp.where(lane == k, d_k, dest)
    dest_ref[...] = dest[:, :TOP_K].astype(jnp.int32)
    base_scr[...] += jnp.sum(onehot, axis=0, keepdims=True)


def _route(choice, cnt_a, cnt_b):
    n = choice.shape[0]
    tm = 256
    fixed = lambda i: (0, 0)
    return pl.pallas_call(
        _route_kernel,
        grid=(n // tm,),
        in_specs=[pl.BlockSpec((tm, LANES), lambda i: (i, 0)),
                  pl.BlockSpec((1, LANES), fixed), pl.BlockSpec((1, LANES), fixed)],
        out_specs=[pl.BlockSpec((tm, TOP_K), lambda i: (i, 0)), pl.BlockSpec((8, LANES), fixed)],
        out_shape=[jax.ShapeDtypeStruct((n, TOP_K), jnp.int32),
                   jax.ShapeDtypeStruct((8, LANES), jnp.int32)],
        scratch_shapes=[pltpu.VMEM((1, LANES), F32)],
        compiler_params=pltpu.CompilerParams(dimension_semantics=("arbitrary",)),
        name="route",
    )(choice, cnt_a, cnt_b)


def _sc_worker_base(per_worker):
    return (lax.axis_index("s") * SC_CORES + lax.axis_index("c")) * per_worker


def _sc_scatter_rows(rows_a, rows_b, idx_flat, n_out):
    na, nb = rows_a.shape[0], rows_b.shape[0]
    width, dtype = rows_a.shape[1], rows_a.dtype
    n = na + nb
    per_a, per_b = na // SC_WORKERS, nb // SC_WORKERS
    chunk = SC_SCATTER_CHUNK
    n_chunks = per_a // chunk
    assert per_a * SC_WORKERS == na and per_b * SC_WORKERS == nb and per_b % 8 == 0 and per_b <= chunk
    assert n_chunks * chunk == per_a and n_chunks % 2 == 0
    mesh = plsc.VectorSubcoreMesh(core_axis_name="c", subcore_axis_name="s")

    @functools.partial(
        pl.kernel, mesh=mesh,
        out_type=jax.ShapeDtypeStruct((n_out, width), dtype),
        scratch_types=[pltpu.VMEM((chunk,), jnp.int32)] * TOP_K + [pltpu.VMEM((per_b,), jnp.int32)]
                      + [pltpu.VMEM((chunk, width), dtype), pltpu.VMEM((chunk, width), dtype),
                         pltpu.VMEM((per_b, width), dtype)]
                      + [pltpu.SemaphoreType.DMA] * 3,
    )
    def scatter(a_hbm, b_hbm, idx_hbm, out_hbm, i0, i1, i2, i3, ib, rows0, rows1, rowsb, rsem0, rsem1, wsem):
        base = _sc_worker_base(per_a)
        idx_bufs = (i0, i1, i2, i3)
        bufs = ((rows0, rsem0), (rows1, rsem1))

        def off(j):
            return pl.multiple_of(base + j * chunk, 8)

        def read(j, buf):
            rows_v, sem = buf
            return pltpu.make_async_copy(a_hbm.at[pl.ds(off(j), chunk)], rows_v, sem)

        def spread(j, buf):
            rows_v, _ = buf
            read(j, buf).wait()
            for k in range(TOP_K):
                pltpu.sync_copy(idx_hbm.at[pl.ds(pl.multiple_of(k * n + off(j), 8), chunk)], idx_bufs[k])
            for k in range(TOP_K):
                pltpu.make_async_copy(rows_v, out_hbm.at[idx_bufs[k]], wsem).start()
            for k in range(TOP_K):
                pltpu.make_async_copy(rows_v, out_hbm.at[idx_bufs[k]], wsem).wait()

        read(0, bufs[0]).start()

        @pl.loop(0, n_chunks, step=2)
        def _(j):
            read(j + 1, bufs[1]).start()
            spread(j, bufs[0])

            @pl.when(j + 2 < n_chunks)
            def _():
                read(j + 2, bufs[0]).start()
            spread(j + 1, bufs[1])

        off_b = pl.multiple_of(_sc_worker_base(per_b), 8)
        pltpu.sync_copy(b_hbm.at[pl.ds(off_b, per_b)], rowsb)
        for k in range(TOP_K):
            pltpu.sync_copy(idx_hbm.at[pl.ds(pl.multiple_of(k * n + na + off_b, 8), per_b)], ib)
            pltpu.async_copy(rowsb, out_hbm.at[ib], wsem).wait()

    return scatter(rows_a, rows_b, idx_flat)


def _sc_gather_rows(table, idx_flat):
    b = idx_flat.shape[0]
    width, dtype = table.shape[1], table.dtype
    per_worker = b // SC_WORKERS
    chunk = SC_GATHER_CHUNK
    n_chunks = per_worker // chunk
    assert per_worker * SC_WORKERS == b and n_chunks * chunk == per_worker and n_chunks % 2 == 0
    mesh = plsc.VectorSubcoreMesh(core_axis_name="c", subcore_axis_name="s")

    @functools.partial(
        pl.kernel, mesh=mesh,
        out_type=jax.ShapeDtypeStruct((b, width), dtype),
        scratch_types=[pltpu.VMEM((chunk,), jnp.int32), pltpu.VMEM((chunk,), jnp.int32),
                       pltpu.VMEM((chunk, width), dtype), pltpu.VMEM((chunk, width), dtype),
                       pltpu.SemaphoreType.DMA, pltpu.SemaphoreType.DMA],
    )
    def gather(table_hbm, idx_hbm, out_hbm, idx0, idx1, rows0, rows1, sem0, sem1):
        base = _sc_worker_base(per_worker)
        bufs = ((idx0, rows0, sem0), (idx1, rows1, sem1))

        def off(j):
            return pl.multiple_of(base + j * chunk, 8)

        def start(j, buf):
            idx_v, rows_v, sem = buf
            pltpu.sync_copy(idx_hbm.at[pl.ds(off(j), chunk)], idx_v)
            pltpu.make_async_copy(table_hbm.at[idx_v], rows_v, sem).start()

        def finish(j, buf):
            idx_v, rows_v, sem = buf
            pltpu.make_async_copy(table_hbm.at[idx_v], rows_v, sem).wait()
            pltpu.sync_copy(rows_v, out_hbm.at[pl.ds(off(j), chunk)])

        start(0, bufs[0])

        @pl.loop(0, n_chunks, step=2)
        def _(j):
            start(j + 1, bufs[1])
            finish(j, bufs[0])

            @pl.when(j + 2 < n_chunks)
            def _():
                start(j + 2, bufs[0])
            finish(j + 1, bufs[1])

    return gather(table, idx_flat)


def _moe_kernel(first_ref, nblk_ref, cnt_ref, xs_hbm, wgu_ref, bgu_ref, wd_ref, bd_ref, out_hbm,
                xbuf, obuf, wgu_bf, wd_bf, xsem, osem):
    e = pl.program_id(0)
    first, nblk, cnt = first_ref[e], nblk_ref[e], cnt_ref[e]

    def rows(j):
        return pl.ds(pl.multiple_of((first + j) * MOE_ROWS, MOE_ROWS), MOE_ROWS)

    def x_copy(j, slot):
        return pltpu.make_async_copy(xs_hbm.at[rows(j)], xbuf.at[slot], xsem.at[slot])

    def o_copy(j, slot):
        return pltpu.make_async_copy(obuf.at[slot], out_hbm.at[rows(j)], osem.at[slot])

    @pl.when(nblk > 0)
    def _():
        x_copy(0, 0).start(priority=ROW_DMA_PRIORITY)
        wgu_bf[...] = wgu_ref[...].astype(BF16)
        wd_bf[...] = wd_ref[...].astype(BF16)

        def block(j, carry):
            slot = j % 2
            x_copy(j, slot).wait()

            @pl.when(j + 1 < nblk)
            def _():
                x_copy(j + 1, 1 - slot).start(priority=ROW_DMA_PRIORITY)

            @pl.when(j >= 2)
            def _():
                o_copy(j - 2, slot).wait()

            row = lax.broadcasted_iota(jnp.int32, (MOE_ROWS, 1), 0)
            x = _unpack_rows(jnp.where(row < cnt - j * MOE_ROWS, xbuf[slot], 0)).astype(BF16)
            gu = _dot(x, wgu_bf[...]) + bgu_ref[...]
            gate = jnp.minimum(gu[:, :D_FF], SWIGLU_LIMIT)
            up = jnp.clip(gu[:, D_FF:], -SWIGLU_LIMIT, SWIGLU_LIMIT)
            act = gate * jax.nn.sigmoid(SWIGLU_ALPHA * gate) * (up + 1.0)
            obuf[slot] = _pack_rows(_dot(act.astype(BF16), wd_bf[...]) + bd_ref[...])
            o_copy(j, slot).start(priority=ROW_DMA_PRIORITY)
            return carry

        lax.fori_loop(0, nblk, block, 0)

        @pl.when(nblk >= 2)
        def _():
            o_copy(nblk - 2, nblk % 2).wait()
        o_copy(nblk - 1, (nblk - 1) % 2).wait()


def _moe(xs, first_block, n_blocks, counts, w_gu, b_gu, w_down, b_down):
    grid_spec = pltpu.PrefetchScalarGridSpec(
        num_scalar_prefetch=3,
        grid=(N_EXPERTS,),
        in_specs=[pl.BlockSpec(memory_space=pl.ANY),
                  pl.BlockSpec((None, D, 2 * D_FF), lambda e, *_: (e, 0, 0)),
                  pl.BlockSpec((None, 1, 2 * D_FF), lambda e, *_: (e, 0, 0)),
                  pl.BlockSpec((None, D_FF, D), lambda e, *_: (e, 0, 0)),
                  pl.BlockSpec((None, 1, D), lambda e, *_: (e, 0, 0))],
        out_specs=pl.BlockSpec(memory_space=pl.ANY),
        scratch_shapes=[pltpu.VMEM((2, MOE_ROWS, PACKED), jnp.int32), pltpu.VMEM((2, MOE_ROWS, PACKED), jnp.int32),
                        pltpu.VMEM((D, 2 * D_FF), BF16), pltpu.VMEM((D_FF, D), BF16),
                        pltpu.SemaphoreType.DMA((2,)), pltpu.SemaphoreType.DMA((2,))],
    )
    return pl.pallas_call(
        _moe_kernel,
        grid_spec=grid_spec,
        out_shape=jax.ShapeDtypeStruct(xs.shape, jnp.int32),
        compiler_params=pltpu.CompilerParams(
            dimension_semantics=("arbitrary",), vmem_limit_bytes=MOE_VMEM_LIMIT),
        name="moe",
    )(first_block, n_blocks, counts, xs, w_gu, b_gu.reshape(N_EXPERTS, 1, 2 * D_FF), w_down,
      b_down.reshape(N_EXPERTS, 1, D))


def _final_kernel(x1_ref, yk_ref, gate_ref, gt2_ref, g_ref, o_ref):
    y2 = gate_ref[:, 0:1] * _unpack_rows(yk_ref[0])
    for k in range(1, TOP_K):
        y2 = y2 + gate_ref[:, k:k + 1] * _unpack_rows(yk_ref[k])
    o_ref[...] = _rms(x1_ref[...] + gt2_ref[...] * y2) * g_ref[...]


def _final(x1, yk, gates, row_off, mod, per_token, tokens_per_seq, g_final):
    tm = 256
    m = x1.shape[0]
    off = row_off // tm
    return pl.pallas_call(
        _final_kernel,
        grid=(m // tm,),
        in_specs=[pl.BlockSpec((tm, D), lambda i: (i, 0)),
                  pl.BlockSpec((TOP_K, tm, PACKED), lambda i: (0, i + off, 0)),
                  pl.BlockSpec((tm, TOP_K), lambda i: (i, 0)),
                  _mod_spec(per_token, tm, tokens_per_seq, 5),
                  pl.BlockSpec((1, D), lambda i: (0, 0))],
        out_specs=pl.BlockSpec((tm, D), lambda i: (i, 0)),
        out_shape=jax.ShapeDtypeStruct((m, D), F32),
        compiler_params=pltpu.CompilerParams(
            dimension_semantics=("arbitrary",), vmem_limit_bytes=VMEM_LIMIT),
        name="final",
    )(x1, yk, gates, mod, g_final.reshape(1, D))


def kernel(x_prompt, x_sample, state_C, state_n, state_m, c_prompt, c_sample, w_ada, b_ada, g_mix, w_in,
           b_if, g_head, g_sgu, b_sgu, w_s, b_s, w_pa, w_pb, w_out, g_ffn, w_router, b_router, w_gu, b_gu,
           w_down, b_down, g_final):
    depth = w_ada.shape[0]
    assert depth == 1
    bp, tp, _ = x_prompt.shape
    bs, ts, _ = x_sample.shape
    mp, ms = bp * tp, bs * ts
    assert tp % ML_CHUNK == 0 and ts <= ML_CHUNK and GM_CHUNK % ts == 0

    w_in0 = w_in[0]
    nqkvo = 4 * D
    w_qkvo = w_in0[:, :nqkvo].astype(BF16)
    w_gate = w_in0[:, nqkvo + 2 * HEADS:].astype(BF16)
    w_if = jnp.pad(w_in0[:, nqkvo:nqkvo + 2 * HEADS], ((0, 0), (0, LANES - 2 * HEADS)))
    b_if_p = jnp.pad(b_if[0], (0, LANES - 2 * HEADS)).reshape(1, LANES)
    reps = GM_CHUNK // ts
    eye_r = jnp.eye(reps, dtype=F32)
    w_s_sample = jnp.einsum("ab,gts->gatbs", eye_r, w_s[0][:, :ts, :ts]).reshape(GROUPS, GM_CHUNK, GM_CHUNK)
    b_s_prompt = b_s[0].T
    b_s_sample = jnp.tile(b_s[0][:, :ts].T, (reps, 1))
    mix_p = {
        "g_sgu": g_sgu[0].reshape(1, D), "b_sgu": b_sgu[0].reshape(1, D),
        "w_pa": w_pa[0].astype(BF16), "w_pb": w_pb[0].astype(BF16), "w_out": w_out[0].astype(BF16),
        "g_ffn": g_ffn[0].reshape(1, D),
        "w_router": jnp.pad(w_router[0], ((0, 0), (0, LANES - N_EXPERTS))),
        "b_router": jnp.pad(b_router[0], (0, LANES - N_EXPERTS)).reshape(1, LANES),
    }
    mix_prompt = dict(mix_p, w_s=w_s[0], b_s=b_s_prompt)
    mix_sample = dict(mix_p, w_s=w_s_sample, b_s=b_s_sample)

    mod = _ada(jnp.concatenate([c_prompt, c_sample], axis=0), w_ada[0], b_ada[0])
    mod_p = mod[:bp].reshape(bp, 1, N_MOD * D)
    mod_s = jnp.repeat(mod[bp:], ts, axis=0)

    xp = x_prompt.reshape(mp, D)
    xs = x_sample.reshape(ms, D)
    z_p, _, gt_p = _in_proj(xp, mod_p, False, tp, g_mix[0], w_qkvo, w_gate, w_if, b_if_p, 512, BF16)
    z_s, gc_s, _ = _in_proj(xs, mod_s, True, ts, g_mix[0], w_qkvo, w_gate, w_if, b_if_p, 128, F32)

    hm_p, C_p, n_p, m_p = _mlstm_prompt(z_p, gt_p, g_head[0], bp, tp)
    hm_p = hm_p.reshape(mp, D)
    m0_tok = jnp.repeat(state_m[0], ts, axis=0)
    hm_s, C_s, n_s, m_s = _mlstm_sample(z_s, gc_s, m0_tok, state_C[0], state_n[0], g_head[0], bs, ts)

    x1_p, h2_p, ch_p, gates_p, _, cnt_p = _mix(xp, hm_p, z_p, mod_p, False, tp, 256, mix_prompt)
    x1_s, h2_s, ch_s, gates_s, vg_s, cnt_s = _mix(xs, hm_s, z_s, mod_s, True, ts, 128, mix_sample)

    n_tok = mp + ms
    n_blocks = -(-(n_tok * TOP_K + N_EXPERTS * (MOE_ROWS - 1)) // MOE_ROWS)
    dest, ex_table = _route(jnp.concatenate([ch_p, ch_s], axis=0), cnt_p, cnt_s)
    dest_kmajor = dest.T.reshape(TOP_K * n_tok)
    xslots = _sc_scatter_rows(h2_p, h2_s, dest_kmajor, n_blocks * MOE_ROWS)
    yb = _moe(xslots, ex_table[0, :N_EXPERTS], ex_table[1, :N_EXPERTS], ex_table[2, :N_EXPERTS],
              w_gu[0], b_gu[0], w_down[0], b_down[0])
    yk = _sc_gather_rows(yb, dest_kmajor).reshape(TOP_K, n_tok, PACKED)

    y_p = _final(x1_p, yk, gates_p, 0, mod_p, False, tp, g_final)
    y_s = _final(x1_s, yk, gates_s, mp, mod_s, True, ts, g_final)

    return (y_p.reshape(bp, tp, D), y_s.reshape(bs, ts, D),
            C_p[None], n_p.reshape(1, bp, HEADS, HD), m_p[:, :, 0, 0][None],
            C_s[None], n_s.reshape(1, bs, HEADS, HD),
            m_s.reshape(bs, ts, HEADS, LANES)[:, 0, :, 0][None],
            vg_s.reshape(1, bs, ts, D))
```

```python
import functools

import jax
import jax.numpy as jnp
from jax import lax
from jax.experimental import pallas as pl
from jax.experimental.pallas import tpu as pltpu
from jax.experimental.pallas import tpu_sc as plsc

F32 = jnp.float32
BF16 = jnp.bfloat16

D = 1024
HEADS = 4
HD = D // HEADS
ML_CHUNK = 128
GROUPS = 4
GD = D // GROUPS
GM_CHUNK = 128
N_EXPERTS = 32
TOP_K = 4
D_FF = D
SWIGLU_LIMIT = 7.0
SWIGLU_ALPHA = 1.702
NORM_EPS = 1e-6
N_MOD = 6
PACKED = D // 2
LANES = 128
SC_CORES = 2
SC_SUBCORES = 16
SC_WORKERS = SC_CORES * SC_SUBCORES
IN_COLS = 2048
MOE_ROWS = 256
ROW_DMA_PRIORITY = 1
MLSTM_SEQS = 1
SAMPLE_SEQS = 4
SC_SCATTER_CHUNK = 64
SC_GATHER_CHUNK = 96
VMEM_LIMIT = 48 * 1024 * 1024
MOE_VMEM_LIMIT = 56 * 1024 * 1024


def _dot(a, b):
    return jnp.dot(a, b, preferred_element_type=F32)


def _dot_nt(a, b):
    return lax.dot_general(a, b, (((1,), (1,)), ((), ())), preferred_element_type=F32)


def _dot_tn(a, b):
    return lax.dot_general(a, b, (((0,), (0,)), ((), ())), preferred_element_type=F32)


def _split_bf16(a):
    hi = a.astype(BF16)
    lo = (a - hi.astype(F32)).astype(BF16)
    return hi, lo


def _dot3(a, b):
    ah, al = _split_bf16(a)
    bh, bl = _split_bf16(b)
    return _dot(ah, bh) + (_dot(ah, bl) + _dot(al, bh))


def _log_sigmoid(x):
    return jnp.minimum(x, 0.0) - jnp.log1p(jnp.exp(-jnp.abs(x)))


def _rms(x):
    return x * lax.rsqrt(jnp.mean(x * x, axis=-1, keepdims=True) + NORM_EPS)


def _pack_rows(x):
    bits = lax.bitcast_convert_type(x.astype(BF16).astype(F32), jnp.uint32)
    word = (bits[:, :PACKED] & jnp.uint32(0xFFFF0000)) | (bits[:, PACKED:] >> 16)
    return lax.bitcast_convert_type(word, jnp.int32)


def _unpack_rows(w):
    bits = lax.bitcast_convert_type(w, jnp.uint32)
    left = lax.bitcast_convert_type(bits & jnp.uint32(0xFFFF0000), F32)
    right = lax.bitcast_convert_type(bits << 16, F32)
    return jnp.concatenate([left, right], axis=1)


def _mod_spec(per_token, tm, tokens_per_seq, col):
    if per_token:
        return pl.BlockSpec((tm, D), lambda i, *_: (i, col))
    return pl.BlockSpec((None, 1, D), lambda i, *_: ((i * tm) // tokens_per_seq, 0, col))


def _ada_kernel(c_ref, w_ref, b_ref, o_ref):
    c = c_ref[...]
    s = (c * jax.nn.sigmoid(c)).astype(BF16)
    o_ref[...] = _dot(s, w_ref[...].astype(BF16)) + b_ref[...]


def _ada(c, w, b):
    m, n = c.shape[0], w.shape[1]
    tn = 512
    return pl.pallas_call(
        _ada_kernel,
        grid=(n // tn,),
        in_specs=[pl.BlockSpec((m, D), lambda j: (0, 0)),
                  pl.BlockSpec((D, tn), lambda j: (0, j)),
                  pl.BlockSpec((1, tn), lambda j: (0, j))],
        out_specs=pl.BlockSpec((m, tn), lambda j: (0, j)),
        out_shape=jax.ShapeDtypeStruct((m, n), F32),
        name="ada",
    )(c, w, b.reshape(1, n))


def _in_kernel(x_ref, g_ref, sh_ref, sc_ref, wa_ref, wb_ref, wif_ref, bif_ref, z_ref, gc_ref, gt_ref):
    h = (_rms(x_ref[...]) * g_ref[...]) * (1.0 + sc_ref[...]) + sh_ref[...]
    hb = h.astype(BF16)
    gates = _dot3(h, wif_ref[...]) + bif_ref[...]
    gc_ref[...] = gates
    gt_ref[...] = gates.T[:2 * HEADS, :]
    half = wa_ref.shape[1]
    for w_ref, col0 in ((wa_ref, 0), (wb_ref, half)):
        for c in range(half // IN_COLS):
            cols = slice(c * IN_COLS, (c + 1) * IN_COLS)
            z_ref[:, col0 + c * IN_COLS:col0 + (c + 1) * IN_COLS] = _dot(hb, w_ref[:, cols]).astype(z_ref.dtype)


def _in_proj(x, mod, per_token, tokens_per_seq, g_mix, w_a, w_b, w_if, b_if, tm, z_dtype):
    m = x.shape[0]
    half = w_a.shape[1]
    gt_len = tokens_per_seq if tokens_per_seq % tm == 0 else m
    gt_tiles = gt_len // tm
    resident = functools.partial(pl.BlockSpec, index_map=lambda i: (0, 0), pipeline_mode=pl.Buffered(1))
    return pl.pallas_call(
        _in_kernel,
        grid=(m // tm,),
        in_specs=[pl.BlockSpec((tm, D), lambda i: (i, 0)),
                  pl.BlockSpec((1, D), lambda i: (0, 0)),
                  _mod_spec(per_token, tm, tokens_per_seq, 0),
                  _mod_spec(per_token, tm, tokens_per_seq, 1),
                  resident((D, half)), resident((D, half)),
                  pl.BlockSpec((D, LANES), lambda i: (0, 0)),
                  pl.BlockSpec((1, LANES), lambda i: (0, 0))],
        out_specs=[pl.BlockSpec((tm, 2 * half), lambda i: (i, 0)),
                   pl.BlockSpec((tm, LANES), lambda i: (i, 0)),
                   pl.BlockSpec((None, 2 * HEADS, tm), lambda i: (i // gt_tiles, 0, i % gt_tiles))],
        out_shape=[jax.ShapeDtypeStruct((m, 2 * half), z_dtype),
                   jax.ShapeDtypeStruct((m, LANES), F32),
                   jax.ShapeDtypeStruct((m // gt_len, 2 * HEADS, gt_len), F32)],
        compiler_params=pltpu.CompilerParams(
            dimension_semantics=("arbitrary",), vmem_limit_bytes=MOE_VMEM_LIMIT),
        name="in_proj",
    )(x, g_mix.reshape(1, D), mod, mod, w_a, w_b, w_if, b_if)


def _mlstm_prompt_kernel(q_ref, k_ref, v_ref, o_ref, gt_ref, gh_ref, hm_ref, C_ref, n_ref, m_ref):
    nseq, L = q_ref.shape[0], q_ref.shape[1]

    @pl.when(pl.program_id(1) == 0)
    def _():
        C_ref[...] = jnp.zeros_like(C_ref)
        n_ref[...] = jnp.zeros_like(n_ref)
        m_ref[...] = jnp.zeros_like(m_ref)

    r = lax.broadcasted_iota(jnp.int32, (L, L), 0)
    s = lax.broadcasted_iota(jnp.int32, (L, L), 1)
    eye = r == s
    causal = s <= r

    def to_col(x_row):
        return jnp.sum(jnp.where(eye, x_row, 0.0), axis=1, keepdims=True)

    for b in range(nseq):
        gates = gt_ref[b]
        for h in range(HEADS):
            cols = slice(h * HD, (h + 1) * HD)
            ig_row = gates[h:h + 1, :]
            lf_row = _log_sigmoid(gates[HEADS + h:HEADS + h + 1, :])
            lf_col = to_col(lf_row)
            b_row = jnp.sum(jnp.where(r <= s, lf_col, 0.0), axis=0, keepdims=True)
            b_col = to_col(b_row)
            m_prev = m_ref[b, h][:, :1]

            logD = jnp.where(causal, b_col - b_row + ig_row, -jnp.inf)
            inter = b_col + m_prev
            mt = jnp.maximum(jnp.max(logD, axis=1, keepdims=True), inter)
            q = q_ref[b, :, cols]
            ks = k_ref[b, :, cols] * (HD ** -0.5)
            v = v_ref[b, :, cols]
            S = _dot_nt(q, ks) * jnp.exp(logD - mt)
            w_int = jnp.exp(inter - mt)
            Cmat = C_ref[b, h]
            nvec = n_ref[b, h]
            num = _dot(S.astype(BF16), v) + w_int * _dot_nt(q, Cmat.astype(BF16))
            nq = jnp.sum(q.astype(F32) * nvec, axis=1, keepdims=True)
            den = jnp.sum(S, axis=1, keepdims=True) + w_int * nq
            hh = num / jnp.maximum(jnp.abs(den), jnp.exp(-mt))
            hg = jax.nn.sigmoid(o_ref[b, :, cols].astype(F32)) * hh
            hm_ref[b, :, cols] = (_rms(hg) * gh_ref[h]).astype(hm_ref.dtype)

            bL = b_row[:, L - 1:L]
            g_row = bL - b_row + ig_row
            m_new = jnp.maximum(bL + m_prev, jnp.max(g_row, axis=1, keepdims=True))
            w_old = jnp.exp(bL + m_prev - m_new)
            kw = ks.astype(F32) * to_col(jnp.exp(g_row - m_new))
            C_ref[b, h] = w_old * Cmat + _dot_tn(v, kw.astype(BF16))
            n_ref[b, h] = w_old * nvec + jnp.sum(kw, axis=0, keepdims=True)
            m_ref[b, h] = jnp.broadcast_to(m_new, (1, LANES))


def _mlstm_prompt(z, gates_t, g_head, batch, seq):
    nc = seq // ML_CHUNK
    nb = MLSTM_SEQS
    z3 = z.reshape(batch, seq, z.shape[1])

    def zspec(col):
        return pl.BlockSpec((nb, ML_CHUNK, D), lambda b, c: (b, c, col))

    def state(last):
        return pl.BlockSpec((nb, HEADS) + last, lambda b, c: (b, 0, 0, 0))

    return pl.pallas_call(
        _mlstm_prompt_kernel,
        grid=(batch // nb, nc),
        in_specs=[zspec(0), zspec(1), zspec(2), zspec(3),
                  pl.BlockSpec((nb, 2 * HEADS, ML_CHUNK), lambda b, c: (b, 0, c)),
                  pl.BlockSpec((HEADS, 1, HD), lambda b, c: (0, 0, 0))],
        out_specs=[pl.BlockSpec((nb, ML_CHUNK, D), lambda b, c: (b, c, 0)),
                   state((HD, HD)), state((1, HD)), state((1, LANES))],
        out_shape=[jax.ShapeDtypeStruct((batch, seq, D), BF16),
                   jax.ShapeDtypeStruct((batch, HEADS, HD, HD), F32),
                   jax.ShapeDtypeStruct((batch, HEADS, 1, HD), F32),
                   jax.ShapeDtypeStruct((batch, HEADS, 1, LANES), F32)],
        compiler_params=pltpu.CompilerParams(dimension_semantics=("arbitrary", "arbitrary")),
        name="mlstm_prompt",
    )(z3, z3, z3, z3, gates_t, g_head.reshape(HEADS, 1, HD))


def _mlstm_sample_kernel(seq_len, q_ref, k_ref, v_ref, o_ref, gc_ref, m0_ref, C0_ref, n0_ref, gh_ref,
                         hm_ref, C_ref, n_ref, m_ref):
    R = q_ref.shape[0]
    nseq = R // seq_len
    r = lax.broadcasted_iota(jnp.int32, (R, R), 0)
    s = lax.broadcasted_iota(jnp.int32, (R, R), 1)
    rseq = lax.broadcasted_iota(jnp.int32, (R, 1), 0) // seq_len
    eye = r == s
    same = (r // seq_len) == (s // seq_len)
    causal = same & (s <= r)

    def to_row(x_col):
        return jnp.sum(jnp.where(eye, x_col, 0.0), axis=0, keepdims=True)

    gc = gc_ref[...]
    for h in range(HEADS):
        cols = slice(h * HD, (h + 1) * HD)
        ig_col = gc[:, h:h + 1]
        lf_col = _log_sigmoid(gc[:, HEADS + h:HEADS + h + 1])
        lf_row = to_row(lf_col)
        b_col = jnp.sum(jnp.where(causal, lf_row, 0.0), axis=1, keepdims=True)
        bL_col = jnp.sum(jnp.where(same, lf_row, 0.0), axis=1, keepdims=True)
        b_row = to_row(b_col)
        ig_row = to_row(ig_col)
        m0_col = m0_ref[:, h:h + 1]

        logD = jnp.where(causal, b_col - b_row + ig_row, -jnp.inf)
        inter = b_col + m0_col
        mt = jnp.maximum(jnp.max(logD, axis=1, keepdims=True), inter)
        qf = q_ref[:, cols]
        q = qf.astype(BF16)
        ksf = k_ref[:, cols] * (HD ** -0.5)
        v = v_ref[:, cols].astype(BF16)
        S = _dot_nt(q, ksf.astype(BF16)) * jnp.exp(logD - mt)
        w_int = jnp.exp(inter - mt)

        Cq = jnp.zeros((R, HD), F32)
        nq = jnp.zeros((R, 1), F32)
        for g in range(nseq):
            Cq = jnp.where(rseq == g, _dot_nt(q, C0_ref[g, h].astype(BF16)), Cq)
            nq = jnp.where(rseq == g, jnp.sum(qf * n0_ref[g, h], axis=1, keepdims=True), nq)
        num = _dot(S.astype(BF16), v) + w_int * Cq
        den = jnp.sum(S, axis=1, keepdims=True) + w_int * nq
        hh = num / jnp.maximum(jnp.abs(den), jnp.exp(-mt))
        hg = jax.nn.sigmoid(o_ref[:, cols]) * hh
        hm_ref[:, cols] = (_rms(hg) * gh_ref[h]).astype(hm_ref.dtype)

        g_col = bL_col - b_col + ig_col
        gmax_col = jnp.max(jnp.where(same, to_row(g_col), -jnp.inf), axis=1, keepdims=True)
        m_new_col = jnp.maximum(bL_col + m0_col, gmax_col)
        w_old_col = jnp.exp(bL_col + m0_col - m_new_col)
        kw = ksf * jnp.exp(g_col - m_new_col)
        for g in range(nseq):
            kw_g = jnp.where(rseq == g, kw, 0.0)
            w_old = w_old_col[g * seq_len:g * seq_len + 1, :]
            C_ref[g, h] = w_old * C0_ref[g, h] + _dot_tn(v, kw_g.astype(BF16))
            n_ref[g, h] = w_old * n0_ref[g, h] + jnp.sum(kw_g, axis=0, keepdims=True)
        m_ref[:, h * LANES:(h + 1) * LANES] = jnp.broadcast_to(m_new_col, (R, LANES))


def _mlstm_sample(z, gates_c, m0_tok, C0, n0, g_head, batch, seq):
    rows = SAMPLE_SEQS * seq
    m = batch * seq

    def zspec(col):
        return pl.BlockSpec((rows, D), lambda i: (i, col))

    state_c = pl.BlockSpec((SAMPLE_SEQS, HEADS, HD, HD), lambda i: (i, 0, 0, 0))
    state_n = pl.BlockSpec((SAMPLE_SEQS, HEADS, 1, HD), lambda i: (i, 0, 0, 0))
    return pl.pallas_call(
        functools.partial(_mlstm_sample_kernel, seq),
        grid=(batch // SAMPLE_SEQS,),
        in_specs=[zspec(0), zspec(1), zspec(2), zspec(3),
                  pl.BlockSpec((rows, LANES), lambda i: (i, 0)),
                  pl.BlockSpec((rows, HEADS), lambda i: (i, 0)),
                  state_c, state_n,
                  pl.BlockSpec((HEADS, 1, HD), lambda i: (0, 0, 0))],
        out_specs=[pl.BlockSpec((rows, D), lambda i: (i, 0)),
                   state_c, state_n,
                   pl.BlockSpec((rows, HEADS * LANES), lambda i: (i, 0))],
        out_shape=[jax.ShapeDtypeStruct((m, D), BF16),
                   jax.ShapeDtypeStruct((batch, HEADS, HD, HD), F32),
                   jax.ShapeDtypeStruct((batch, HEADS, 1, HD), F32),
                   jax.ShapeDtypeStruct((m, HEADS * LANES), F32)],
        compiler_params=pltpu.CompilerParams(
            dimension_semantics=("arbitrary",), vmem_limit_bytes=VMEM_LIMIT),
        name="mlstm_sample",
    )(z, z, z, z, gates_c, m0_tok, C0, n0.reshape(batch, HEADS, 1, HD), g_head.reshape(HEADS, 1, HD))


def _mix_kernel(x_ref, hm_ref, u_ref, v_ref, ga_ref, gb_ref, gt1_ref, sh2_ref, sc2_ref,
                gsgu_ref, bsgu_ref, ws_ref, bs_ref, wpa_ref, wpb_ref, wout_ref, gffn_ref,
                wr_ref, br_ref, x1_ref, h2_ref, choice_ref, gate_ref, vg_ref, cnt_ref, yg_scr):
    tm = x_ref.shape[0]
    u = jax.nn.gelu(u_ref[...].astype(F32))
    vv = jax.nn.gelu(v_ref[...].astype(F32))
    mu = jnp.mean(vv, axis=-1, keepdims=True)
    var = jnp.mean(jnp.square(vv - mu), axis=-1, keepdims=True)
    vg = (vv - mu) * lax.rsqrt(var + NORM_EPS) * gsgu_ref[...] + bsgu_ref[...]
    vg_ref[...] = vg
    vgb = vg.astype(BF16)

    r = lax.broadcasted_iota(jnp.int32, (GM_CHUNK, GM_CHUNK), 0)
    s = lax.broadcasted_iota(jnp.int32, (GM_CHUNK, GM_CHUNK), 1)
    for g in range(GROUPS):
        w = jnp.where(s <= r, ws_ref[g], 0.0).astype(BF16)
        bias = bs_ref[:, g:g + 1]
        for c in range(tm // GM_CHUNK):
            rows = slice(c * GM_CHUNK, (c + 1) * GM_CHUNK)
            cols = slice(g * GD, (g + 1) * GD)
            mixed = _dot(w, vgb[rows, cols]) + bias
            yg_scr[rows, cols] = (u[rows, cols] * mixed).astype(BF16)

    a = _dot(hm_ref[...], wpa_ref[...])
    b = _dot(yg_scr[...], wpb_ref[...])
    merged = (jax.nn.sigmoid(ga_ref[...].astype(F32)) * a
              + jax.nn.sigmoid(gb_ref[...].astype(F32)) * b)
    x1 = x_ref[...] + gt1_ref[...] * _dot(merged.astype(BF16), wout_ref[...])
    x1_ref[...] = x1
    h2 = (_rms(x1) * gffn_ref[...]) * (1.0 + sc2_ref[...]) + sh2_ref[...]
    h2_ref[...] = _pack_rows(h2)

    lane = lax.broadcasted_iota(jnp.int32, (tm, LANES), 1)
    lane_f = lane.astype(F32)
    lg = jnp.where(lane < N_EXPERTS, _dot3(h2, wr_ref[...]) + br_ref[...], -jnp.inf)
    choice = jnp.zeros((tm, LANES), F32)
    vals = []
    for k in range(TOP_K):
        mx = jnp.max(lg, axis=1, keepdims=True)
        sel = lane_f == jnp.min(jnp.where(lg == mx, lane_f, float(LANES)), axis=1, keepdims=True)
        choice = jnp.where(sel, k + 1.0, choice)
        vals.append(mx)
        lg = jnp.where(sel, -jnp.inf, lg)
    choice_ref[...] = choice
    ex = [jnp.exp(v - vals[0]) for v in vals]
    denom = sum(ex)
    gates = jnp.zeros((tm, LANES), F32)
    for k in range(TOP_K):
        gates = jnp.where(lane == k, ex[k] / denom, gates)
    gate_ref[...] = gates[:, :TOP_K]

    @pl.when(pl.program_id(0) == 0)
    def _():
        cnt_ref[...] = jnp.zeros_like(cnt_ref)

    cnt_ref[...] += jnp.sum(jnp.where(choice > 0.0, 1.0, 0.0), axis=0, keepdims=True)


def _mix(x, hm, z, mod, per_token, tokens_per_seq, tm, p):
    m = x.shape[0]
    zcol = 4

    def zspec(blk):
        return pl.BlockSpec((tm, D), lambda i: (i, blk))

    def full(shape):
        return pl.BlockSpec(shape, lambda i: (0,) * len(shape))

    row = pl.BlockSpec((tm, D), lambda i: (i, 0))
    return pl.pallas_call(
        _mix_kernel,
        grid=(m // tm,),
        in_specs=[row, row, zspec(zcol), zspec(zcol + 1), zspec(zcol + 2), zspec(zcol + 3),
                  _mod_spec(per_token, tm, tokens_per_seq, 2),
                  _mod_spec(per_token, tm, tokens_per_seq, 3),
                  _mod_spec(per_token, tm, tokens_per_seq, 4),
                  full((1, D)), full((1, D)),
                  full((GROUPS, GM_CHUNK, GM_CHUNK)), full((GM_CHUNK, GROUPS)),
                  full((D, D)), full((D, D)), full((D, D)), full((1, D)),
                  full((D, LANES)), full((1, LANES))],
        out_specs=[row, pl.BlockSpec((tm, PACKED), lambda i: (i, 0)),
                   pl.BlockSpec((tm, LANES), lambda i: (i, 0)), pl.BlockSpec((tm, TOP_K), lambda i: (i, 0)),
                   row, full((1, LANES))],
        out_shape=[jax.ShapeDtypeStruct((m, D), F32),
                   jax.ShapeDtypeStruct((m, PACKED), jnp.int32),
                   jax.ShapeDtypeStruct((m, LANES), F32),
                   jax.ShapeDtypeStruct((m, TOP_K), F32),
                   jax.ShapeDtypeStruct((m, D), F32),
                   jax.ShapeDtypeStruct((1, LANES), F32)],
        scratch_shapes=[pltpu.VMEM((tm, D), BF16)],
        compiler_params=pltpu.CompilerParams(
            dimension_semantics=("arbitrary",), vmem_limit_bytes=VMEM_LIMIT),
        name="mix",
    )(x, hm, z, z, z, z, mod, mod, mod, p["g_sgu"], p["b_sgu"], p["w_s"], p["b_s"],
      p["w_pa"], p["w_pb"], p["w_out"], p["g_ffn"], p["w_router"], p["b_router"])


def _route_kernel(tiles_a, cha_ref, chb_ref, cnta_ref, cntb_ref, dest_ref, ex_ref, base_scr):
    tm = cha_ref.shape[0]
    lane = lax.broadcasted_iota(jnp.int32, (tm, LANES), 1)
    choice = jnp.where(pl.program_id(0) < tiles_a, cha_ref[...], chb_ref[...])
    onehot = jnp.where(choice > 0.0, 1.0, 0.0)

    @pl.when(pl.program_id(0) == 0)
    def _():
        cnt = cnta_ref[...] + cntb_ref[...]
        padded = jnp.floor((cnt + (MOE_ROWS - 1)) * (1.0 / MOE_ROWS)) * MOE_ROWS
        r = lax.broadcasted_iota(jnp.int32, (LANES, LANES), 0)
        s = lax.broadcasted_iota(jnp.int32, (LANES, LANES), 1)
        padded_col = jnp.sum(jnp.where(r == s, padded, 0.0), axis=1, keepdims=True)
        pstart = jnp.sum(jnp.where(r < s, padded_col, 0.0), axis=0, keepdims=True)
        base_scr[...] = pstart
        trow = lax.broadcasted_iota(jnp.int32, ex_ref.shape, 0)
        table = jnp.where(trow == 0, pstart * (1.0 / MOE_ROWS),
                          jnp.where(trow == 1, padded * (1.0 / MOE_ROWS), jnp.where(trow == 2, cnt, 0.0)))
        ex_ref[...] = table.astype(jnp.int32)

    r = lax.broadcasted_iota(jnp.int32, (tm, tm), 0)
    s = lax.broadcasted_iota(jnp.int32, (tm, tm), 1)
    before = _dot((s < r).astype(BF16), onehot.astype(BF16))
    slot = before + base_scr[...]
    dest = jnp.zeros((tm, LANES), F32)
    for k in range(TOP_K):
        d_k = jnp.sum(jnp.where(choice == k + 1.0, slot, 0.0), axis=1, keepdims=True)
        dest = jnp.where(lane == k, d_k, dest)
    dest_ref[...] = dest[:, :TOP_K].astype(jnp.int32)
    base_scr[...] += jnp.sum(onehot, axis=0, keepdims=True)


def _route(choice_a, choice_b, cnt_a, cnt_b):
    tm = 256
    tiles_a, tiles_b = choice_a.shape[0] // tm, choice_b.shape[0] // tm
    n = (tiles_a + tiles_b) * tm
    fixed = lambda i: (0, 0)
    return pl.pallas_call(
        functools.partial(_route_kernel, tiles_a),
        grid=(tiles_a + tiles_b,),
        in_specs=[pl.BlockSpec((tm, LANES), lambda i: (jnp.minimum(i, tiles_a - 1), 0)),
                  pl.BlockSpec((tm, LANES), lambda i: (jnp.maximum(i - tiles_a, 0), 0)),
                  pl.BlockSpec((1, LANES), fixed), pl.BlockSpec((1, LANES), fixed)],
        out_specs=[pl.BlockSpec((tm, TOP_K), lambda i: (i, 0)), pl.BlockSpec((8, LANES), fixed)],
        out_shape=[jax.ShapeDtypeStruct((n, TOP_K), jnp.int32),
                   jax.ShapeDtypeStruct((8, LANES), jnp.int32)],
        scratch_shapes=[pltpu.VMEM((1, LANES), F32)],
        compiler_params=pltpu.CompilerParams(dimension_semantics=("arbitrary",)),
        name="route",
    )(choice_a, choice_b, cnt_a, cnt_b)


def _sc_worker_base(per_worker):
    return (lax.axis_index("s") * SC_CORES + lax.axis_index("c")) * per_worker


def _sc_scatter_rows(rows_a, rows_b, idx_flat, n_out):
    na, nb = rows_a.shape[0], rows_b.shape[0]
    width, dtype = rows_a.shape[1], rows_a.dtype
    n = na + nb
    per_a, per_b = na // SC_WORKERS, nb // SC_WORKERS
    chunk = SC_SCATTER_CHUNK
    n_chunks = per_a // chunk
    assert per_a * SC_WORKERS == na and per_b * SC_WORKERS == nb and per_b % 8 == 0 and per_b <= chunk
    assert n_chunks * chunk == per_a and n_chunks % 2 == 0
    mesh = plsc.VectorSubcoreMesh(core_axis_name="c", subcore_axis_name="s")

    @functools.partial(
        pl.kernel, mesh=mesh,
        out_type=jax.ShapeDtypeStruct((n_out, width), dtype),
        scratch_types=[pltpu.VMEM((chunk,), jnp.int32)] * TOP_K + [pltpu.VMEM((per_b,), jnp.int32)]
                      + [pltpu.VMEM((chunk, width), dtype), pltpu.VMEM((chunk, width), dtype),
                         pltpu.VMEM((per_b, width), dtype)]
                      + [pltpu.SemaphoreType.DMA] * 3,
    )
    def scatter(a_hbm, b_hbm, idx_hbm, out_hbm, i0, i1, i2, i3, ib, rows0, rows1, rowsb, rsem0, rsem1, wsem):
        base = _sc_worker_base(per_a)
        idx_bufs = (i0, i1, i2, i3)
        bufs = ((rows0, rsem0), (rows1, rsem1))

        def off(j):
            return pl.multiple_of(base + j * chunk, 8)

        def read(j, buf):
            rows_v, sem = buf
            return pltpu.make_async_copy(a_hbm.at[pl.ds(off(j), chunk)], rows_v, sem)

        def spread(j, buf):
            rows_v, _ = buf
            read(j, buf).wait()
            for k in range(TOP_K):
                pltpu.sync_copy(idx_hbm.at[pl.ds(pl.multiple_of(k * n + off(j), 8), chunk)], idx_bufs[k])
            for k in range(TOP_K):
                pltpu.make_async_copy(rows_v, out_hbm.at[idx_bufs[k]], wsem).start()
            for k in range(TOP_K):
                pltpu.make_async_copy(rows_v, out_hbm.at[idx_bufs[k]], wsem).wait()

        read(0, bufs[0]).start()

        @pl.loop(0, n_chunks, step=2)
        def _(j):
            read(j + 1, bufs[1]).start()
            spread(j, bufs[0])

            @pl.when(j + 2 < n_chunks)
            def _():
                read(j + 2, bufs[0]).start()
            spread(j + 1, bufs[1])

        off_b = pl.multiple_of(_sc_worker_base(per_b), 8)
        pltpu.sync_copy(b_hbm.at[pl.ds(off_b, per_b)], rowsb)
        for k in range(TOP_K):
            pltpu.sync_copy(idx_hbm.at[pl.ds(pl.multiple_of(k * n + na + off_b, 8), per_b)], ib)
            pltpu.async_copy(rowsb, out_hbm.at[ib], wsem).wait()

    return scatter(rows_a, rows_b, idx_flat)


def _sc_gather_rows(table, idx_flat):
    b = idx_flat.shape[0]
    width, dtype = table.shape[1], table.dtype
    per_worker = b // SC_WORKERS
    chunk = SC_GATHER_CHUNK
    n_chunks = per_worker // chunk
    assert per_worker * SC_WORKERS == b and n_chunks * chunk == per_worker and n_chunks % 2 == 0
    mesh = plsc.VectorSubcoreMesh(core_axis_name="c", subcore_axis_name="s")

    @functools.partial(
        pl.kernel, mesh=mesh,
        out_type=jax.ShapeDtypeStruct((b, width), dtype),
        scratch_types=[pltpu.VMEM((chunk,), jnp.int32), pltpu.VMEM((chunk,), jnp.int32),
                       pltpu.VMEM((chunk, width), dtype), pltpu.VMEM((chunk, width), dtype),
                       pltpu.SemaphoreType.DMA, pltpu.SemaphoreType.DMA],
    )
    def gather(table_hbm, idx_hbm, out_hbm, idx0, idx1, rows0, rows1, sem0, sem1):
        base = _sc_worker_base(per_worker)
        bufs = ((idx0, rows0, sem0), (idx1, rows1, sem1))

        def off(j):
            return pl.multiple_of(base + j * chunk, 8)

        def start(j, buf):
            idx_v, rows_v, sem = buf
            pltpu.sync_copy(idx_hbm.at[pl.ds(off(j), chunk)], idx_v)
            pltpu.make_async_copy(table_hbm.at[idx_v], rows_v, sem).start()

        def finish(j, buf):
            idx_v, rows_v, sem = buf
            pltpu.make_async_copy(table_hbm.at[idx_v], rows_v, sem).wait()
            pltpu.sync_copy(rows_v, out_hbm.at[pl.ds(off(j), chunk)])

        start(0, bufs[0])

        @pl.loop(0, n_chunks, step=2)
        def _(j):
            start(j + 1, bufs[1])
            finish(j, bufs[0])

            @pl.when(j + 2 < n_chunks)
            def _():
                start(j + 2, bufs[0])
            finish(j + 1, bufs[1])

    return gather(table, idx_flat)


def _moe_kernel(first_ref, nblk_ref, cnt_ref, xs_hbm, wgu_ref, bgu_ref, wd_ref, bd_ref, out_hbm,
                xbuf, obuf, wgu_bf, wd_bf, xsem, osem):
    e = pl.program_id(0)
    first, nblk, cnt = first_ref[e], nblk_ref[e], cnt_ref[e]

    def rows(j):
        return pl.ds(pl.multiple_of((first + j) * MOE_ROWS, MOE_ROWS), MOE_ROWS)

    def x_copy(j, slot):
        return pltpu.make_async_copy(xs_hbm.at[rows(j)], xbuf.at[slot], xsem.at[slot])

    def o_copy(j, slot):
        return pltpu.make_async_copy(obuf.at[slot], out_hbm.at[rows(j)], osem.at[slot])

    @pl.when(nblk > 0)
    def _():
        x_copy(0, 0).start(priority=ROW_DMA_PRIORITY)
        wgu_bf[...] = wgu_ref[...].astype(BF16)
        wd_bf[...] = wd_ref[...].astype(BF16)

        def block(j, carry):
            slot = j % 2
            x_copy(j, slot).wait()

            @pl.when(j + 1 < nblk)
            def _():
                x_copy(j + 1, 1 - slot).start(priority=ROW_DMA_PRIORITY)

            @pl.when(j >= 2)
            def _():
                o_copy(j - 2, slot).wait()

            row = lax.broadcasted_iota(jnp.int32, (MOE_ROWS, 1), 0)
            x = _unpack_rows(jnp.where(row < cnt - j * MOE_ROWS, xbuf[slot], 0)).astype(BF16)
            gu = _dot(x, wgu_bf[...]) + bgu_ref[...]
            gate = jnp.minimum(gu[:, :D_FF], SWIGLU_LIMIT)
            up = jnp.clip(gu[:, D_FF:], -SWIGLU_LIMIT, SWIGLU_LIMIT)
            act = gate * jax.nn.sigmoid(SWIGLU_ALPHA * gate) * (up + 1.0)
            obuf[slot] = _pack_rows(_dot(act.astype(BF16), wd_bf[...]) + bd_ref[...])
            o_copy(j, slot).start(priority=ROW_DMA_PRIORITY)
            return carry

        lax.fori_loop(0, nblk, block, 0)

        @pl.when(nblk >= 2)
        def _():
            o_copy(nblk - 2, nblk % 2).wait()
        o_copy(nblk - 1, (nblk - 1) % 2).wait()


def _moe(xs, first_block, n_blocks, counts, w_gu, b_gu, w_down, b_down):
    grid_spec = pltpu.PrefetchScalarGridSpec(
        num_scalar_prefetch=3,
        grid=(N_EXPERTS,),
        in_specs=[pl.BlockSpec(memory_space=pl.ANY),
                  pl.BlockSpec((None, D, 2 * D_FF), lambda e, *_: (e, 0, 0)),
                  pl.BlockSpec((None, 1, 2 * D_FF), lambda e, *_: (e, 0, 0)),
                  pl.BlockSpec((None, D_FF, D), lambda e, *_: (e, 0, 0)),
                  pl.BlockSpec((None, 1, D), lambda e, *_: (e, 0, 0))],
        out_specs=pl.BlockSpec(memory_space=pl.ANY),
        scratch_shapes=[pltpu.VMEM((2, MOE_ROWS, PACKED), jnp.int32), pltpu.VMEM((2, MOE_ROWS, PACKED), jnp.int32),
                        pltpu.VMEM((D, 2 * D_FF), BF16), pltpu.VMEM((D_FF, D), BF16),
                        pltpu.SemaphoreType.DMA((2,)), pltpu.SemaphoreType.DMA((2,))],
    )
    return pl.pallas_call(
        _moe_kernel,
        grid_spec=grid_spec,
        out_shape=jax.ShapeDtypeStruct(xs.shape, jnp.int32),
        compiler_params=pltpu.CompilerParams(
            dimension_semantics=("arbitrary",), vmem_limit_bytes=MOE_VMEM_LIMIT),
        name="moe",
    )(first_block, n_blocks, counts, xs, w_gu, b_gu.reshape(N_EXPERTS, 1, 2 * D_FF), w_down,
      b_down.reshape(N_EXPERTS, 1, D))


def _final_kernel(x1_ref, yk_ref, gate_ref, gt2_ref, g_ref, o_ref):
    y2 = gate_ref[:, 0:1] * _unpack_rows(yk_ref[0])
    for k in range(1, TOP_K):
        y2 = y2 + gate_ref[:, k:k + 1] * _unpack_rows(yk_ref[k])
    o_ref[...] = _rms(x1_ref[...] + gt2_ref[...] * y2) * g_ref[...]


def _final(x1, yk, gates, row_off, mod, per_token, tokens_per_seq, g_final):
    tm = 256
    m = x1.shape[0]
    off = row_off // tm
    return pl.pallas_call(
        _final_kernel,
        grid=(m // tm,),
        in_specs=[pl.BlockSpec((tm, D), lambda i: (i, 0)),
                  pl.BlockSpec((TOP_K, tm, PACKED), lambda i: (0, i + off, 0)),
                  pl.BlockSpec((tm, TOP_K), lambda i: (i, 0)),
                  _mod_spec(per_token, tm, tokens_per_seq, 5),
                  pl.BlockSpec((1, D), lambda i: (0, 0))],
        out_specs=pl.BlockSpec((tm, D), lambda i: (i, 0)),
        out_shape=jax.ShapeDtypeStruct((m, D), F32),
        compiler_params=pltpu.CompilerParams(
            dimension_semantics=("arbitrary",), vmem_limit_bytes=VMEM_LIMIT),
        name="final",
    )(x1, yk, gates, mod, g_final.reshape(1, D))


def kernel(x_prompt, x_sample, state_C, state_n, state_m, c_prompt, c_sample, w_ada, b_ada, g_mix, w_in,
           b_if, g_head, g_sgu, b_sgu, w_s, b_s, w_pa, w_pb, w_out, g_ffn, w_router, b_router, w_gu, b_gu,
           w_down, b_down, g_final):
    depth = w_ada.shape[0]
    assert depth == 1
    bp, tp, _ = x_prompt.shape
    bs, ts, _ = x_sample.shape
    mp, ms = bp * tp, bs * ts
    assert tp % ML_CHUNK == 0 and ts <= ML_CHUNK and GM_CHUNK % ts == 0

    w_in0 = w_in[0]
    nqkvo = 4 * D
    w_qkvo = w_in0[:, :nqkvo].astype(BF16)
    w_gate = w_in0[:, nqkvo + 2 * HEADS:].astype(BF16)
    w_if = jnp.pad(w_in0[:, nqkvo:nqkvo + 2 * HEADS], ((0, 0), (0, LANES - 2 * HEADS)))
    b_if_p = jnp.pad(b_if[0], (0, LANES - 2 * HEADS)).reshape(1, LANES)
    reps = GM_CHUNK // ts
    eye_r = jnp.eye(reps, dtype=F32)
    w_s_sample = jnp.einsum("ab,gts->gatbs", eye_r, w_s[0][:, :ts, :ts]).reshape(GROUPS, GM_CHUNK, GM_CHUNK)
    b_s_prompt = b_s[0].T
    b_s_sample = jnp.tile(b_s[0][:, :ts].T, (reps, 1))
    mix_p = {
        "g_sgu": g_sgu[0].reshape(1, D), "b_sgu": b_sgu[0].reshape(1, D),
        "w_pa": w_pa[0].astype(BF16), "w_pb": w_pb[0].astype(BF16), "w_out": w_out[0].astype(BF16),
        "g_ffn": g_ffn[0].reshape(1, D),
        "w_router": jnp.pad(w_router[0], ((0, 0), (0, LANES - N_EXPERTS))),
        "b_router": jnp.pad(b_router[0], (0, LANES - N_EXPERTS)).reshape(1, LANES),
    }
    mix_prompt = dict(mix_p, w_s=w_s[0], b_s=b_s_prompt)
    mix_sample = dict(mix_p, w_s=w_s_sample, b_s=b_s_sample)

    mod = _ada(jnp.concatenate([jnp.repeat(c_sample, ts, axis=0), c_prompt], axis=0), w_ada[0], b_ada[0])
    mod_s = mod
    mod_p = mod[ms:].reshape(bp, 1, N_MOD * D)

    xp = x_prompt.reshape(mp, D)
    xs = x_sample.reshape(ms, D)
    z_p, _, gt_p = _in_proj(xp, mod_p, False, tp, g_mix[0], w_qkvo, w_gate, w_if, b_if_p, 512, BF16)
    z_s, gc_s, _ = _in_proj(xs, mod_s, True, ts, g_mix[0], w_qkvo, w_gate, w_if, b_if_p, 128, F32)

    hm_p, C_p, n_p, m_p = _mlstm_prompt(z_p, gt_p, g_head[0], bp, tp)
    hm_p = hm_p.reshape(mp, D)
    m0_tok = jnp.repeat(state_m[0], ts, axis=0)
    hm_s, C_s, n_s, m_s = _mlstm_sample(z_s, gc_s, m0_tok, state_C[0], state_n[0], g_head[0], bs, ts)

    x1_p, h2_p, ch_p, gates_p, _, cnt_p = _mix(xp, hm_p, z_p, mod_p, False, tp, 256, mix_prompt)
    x1_s, h2_s, ch_s, gates_s, vg_s, cnt_s = _mix(xs, hm_s, z_s, mod_s, True, ts, 128, mix_sample)

    n_tok = mp + ms
    n_blocks = -(-(n_tok * TOP_K + N_EXPERTS * (MOE_ROWS - 1)) // MOE_ROWS)
    dest, ex_table = _route(ch_p, ch_s, cnt_p, cnt_s)
    dest_kmajor = dest.T.reshape(TOP_K * n_tok)
    xslots = _sc_scatter_rows(h2_p, h2_s, dest_kmajor, n_blocks * MOE_ROWS)
    yb = _moe(xslots, ex_table[0, :N_EXPERTS], ex_table[1, :N_EXPERTS], ex_table[2, :N_EXPERTS],
              w_gu[0], b_gu[0], w_down[0], b_down[0])
    yk = _sc_gather_rows(yb, dest_kmajor).reshape(TOP_K, n_tok, PACKED)

    y_p = _final(x1_p, yk, gates_p, 0, mod_p, False, tp, g_final)
    y_s = _final(x1_s, yk, gates_s, mp, mod_s, True, ts, g_final)

    return (y_p.reshape(bp, tp, D), y_s.reshape(bs, ts, D),
            C_p[None], n_p.reshape(1, bp, HEADS, HD), m_p[:, :, 0, 0][None],
            C_s[None], n_s.reshape(1, bs, HEADS, HD),
            m_s.reshape(bs, ts, HEADS, LANES)[:, 0, :, 0][None],
            vg_s.reshape(1, bs, ts, D))
```

```python
import functools

import jax
import jax.numpy as jnp
from jax import lax
from jax.experimental import pallas as pl
from jax.experimental.pallas import tpu as pltpu
from jax.experimental.pallas import tpu_sc as plsc

F32 = jnp.float32
BF16 = jnp.bfloat16

D = 1024
HEADS = 4
HD = D // HEADS
ML_CHUNK = 512
GROUPS = 4
GD = D // GROUPS
GM_CHUNK = 128
N_EXPERTS = 32
TOP_K = 4
D_FF = D
SWIGLU_LIMIT = 7.0
SWIGLU_ALPHA = 1.702
NORM_EPS = 1e-6
N_MOD = 6
PACKED = D // 2
LANES = 128
SC_CORES = 2
SC_SUBCORES = 16
SC_WORKERS = SC_CORES * SC_SUBCORES
IN_COLS = 2048
MOE_ROWS = 256
ROW_DMA_PRIORITY = 1
MLSTM_SEQS = 1
SAMPLE_SEQS = 4
SC_SCATTER_CHUNK = 64
SC_GATHER_CHUNK = 96
VMEM_LIMIT = 48 * 1024 * 1024
MOE_VMEM_LIMIT = 56 * 1024 * 1024


def _dot(a, b):
    return jnp.dot(a, b, preferred_element_type=F32)


def _dot_nt(a, b):
    return lax.dot_general(a, b, (((1,), (1,)), ((), ())), preferred_element_type=F32)


def _dot_tn(a, b):
    return lax.dot_general(a, b, (((0,), (0,)), ((), ())), preferred_element_type=F32)


def _split_bf16(a):
    hi = a.astype(BF16)
    lo = (a - hi.astype(F32)).astype(BF16)
    return hi, lo


def _dot3(a, b):
    ah, al = _split_bf16(a)
    bh, bl = _split_bf16(b)
    return _dot(ah, bh) + (_dot(ah, bl) + _dot(al, bh))


def _log_sigmoid(x):
    return jnp.minimum(x, 0.0) - jnp.log1p(jnp.exp(-jnp.abs(x)))


def _rms(x):
    return x * lax.rsqrt(jnp.mean(x * x, axis=-1, keepdims=True) + NORM_EPS)


def _pack_rows(x):
    bits = lax.bitcast_convert_type(x.astype(BF16).astype(F32), jnp.uint32)
    word = (bits[:, :PACKED] & jnp.uint32(0xFFFF0000)) | (bits[:, PACKED:] >> 16)
    return lax.bitcast_convert_type(word, jnp.int32)


def _unpack_rows(w):
    bits = lax.bitcast_convert_type(w, jnp.uint32)
    left = lax.bitcast_convert_type(bits & jnp.uint32(0xFFFF0000), F32)
    right = lax.bitcast_convert_type(bits << 16, F32)
    return jnp.concatenate([left, right], axis=1)


def _mod_spec(per_token, tm, tokens_per_seq, col):
    if per_token:
        return pl.BlockSpec((tm, D), lambda i, *_: (i, col))
    return pl.BlockSpec((None, 1, D), lambda i, *_: ((i * tm) // tokens_per_seq, 0, col))


def _ada_kernel(c_ref, w_ref, b_ref, o_ref):
    c = c_ref[...]
    s = (c * jax.nn.sigmoid(c)).astype(BF16)
    o_ref[...] = _dot(s, w_ref[...].astype(BF16)) + b_ref[...]


def _ada(c, w, b):
    m, n = c.shape[0], w.shape[1]
    tn = 512
    return pl.pallas_call(
        _ada_kernel,
        grid=(n // tn,),
        in_specs=[pl.BlockSpec((m, D), lambda j: (0, 0)),
                  pl.BlockSpec((D, tn), lambda j: (0, j)),
                  pl.BlockSpec((1, tn), lambda j: (0, j))],
        out_specs=pl.BlockSpec((m, tn), lambda j: (0, j)),
        out_shape=jax.ShapeDtypeStruct((m, n), F32),
        name="ada",
    )(c, w, b.reshape(1, n))


def _in_kernel(x_ref, g_ref, sh_ref, sc_ref, wa_ref, wb_ref, wif_ref, bif_ref, z_ref, gc_ref, gt_ref):
    h = (_rms(x_ref[...]) * g_ref[...]) * (1.0 + sc_ref[...]) + sh_ref[...]
    hb = h.astype(BF16)
    gates = _dot3(h, wif_ref[...]) + bif_ref[...]
    gc_ref[...] = gates
    gt_ref[...] = gates.T[:2 * HEADS, :]
    half = wa_ref.shape[1]
    for w_ref, col0 in ((wa_ref, 0), (wb_ref, half)):
        for c in range(half // IN_COLS):
            cols = slice(c * IN_COLS, (c + 1) * IN_COLS)
            z_ref[:, col0 + c * IN_COLS:col0 + (c + 1) * IN_COLS] = _dot(hb, w_ref[:, cols]).astype(z_ref.dtype)


def _in_proj(x, mod, per_token, tokens_per_seq, g_mix, w_a, w_b, w_if, b_if, tm, z_dtype):
    m = x.shape[0]
    half = w_a.shape[1]
    gt_len = tokens_per_seq if tokens_per_seq % tm == 0 else m
    gt_tiles = gt_len // tm
    resident = functools.partial(pl.BlockSpec, index_map=lambda i: (0, 0), pipeline_mode=pl.Buffered(1))
    return pl.pallas_call(
        _in_kernel,
        grid=(m // tm,),
        in_specs=[pl.BlockSpec((tm, D), lambda i: (i, 0)),
                  pl.BlockSpec((1, D), lambda i: (0, 0)),
                  _mod_spec(per_token, tm, tokens_per_seq, 0),
                  _mod_spec(per_token, tm, tokens_per_seq, 1),
                  resident((D, half)), resident((D, half)),
                  pl.BlockSpec((D, LANES), lambda i: (0, 0)),
                  pl.BlockSpec((1, LANES), lambda i: (0, 0))],
        out_specs=[pl.BlockSpec((tm, 2 * half), lambda i: (i, 0)),
                   pl.BlockSpec((tm, LANES), lambda i: (i, 0)),
                   pl.BlockSpec((None, 2 * HEADS, tm), lambda i: (i // gt_tiles, 0, i % gt_tiles))],
        out_shape=[jax.ShapeDtypeStruct((m, 2 * half), z_dtype),
                   jax.ShapeDtypeStruct((m, LANES), F32),
                   jax.ShapeDtypeStruct((m // gt_len, 2 * HEADS, gt_len), F32)],
        compiler_params=pltpu.CompilerParams(
            dimension_semantics=("arbitrary",), vmem_limit_bytes=MOE_VMEM_LIMIT),
        name="in_proj",
    )(x, g_mix.reshape(1, D), mod, mod, w_a, w_b, w_if, b_if)


def _mlstm_prompt_kernel(q_ref, k_ref, v_ref, o_ref, gt_ref, gh_ref, hm_ref, C_ref, n_ref, m_ref):
    nseq, L = q_ref.shape[0], q_ref.shape[1]

    @pl.when(pl.program_id(1) == 0)
    def _():
        C_ref[...] = jnp.zeros_like(C_ref)
        n_ref[...] = jnp.zeros_like(n_ref)
        m_ref[...] = jnp.zeros_like(m_ref)

    r = lax.broadcasted_iota(jnp.int32, (L, L), 0)
    s = lax.broadcasted_iota(jnp.int32, (L, L), 1)
    eye = r == s
    causal = s <= r

    def to_col(x_row):
        return jnp.sum(jnp.where(eye, x_row, 0.0), axis=1, keepdims=True)

    for b in range(nseq):
        gates = gt_ref[b]
        for h in range(HEADS):
            cols = slice(h * HD, (h + 1) * HD)
            ig_row = gates[h:h + 1, :]
            lf_row = _log_sigmoid(gates[HEADS + h:HEADS + h + 1, :])
            lf_col = to_col(lf_row)
            b_row = jnp.sum(jnp.where(r <= s, lf_col, 0.0), axis=0, keepdims=True)
            b_col = to_col(b_row)
            m_prev = m_ref[b, h][:, :1]

            logD = jnp.where(causal, b_col - b_row + ig_row, -jnp.inf)
            inter = b_col + m_prev
            mt = jnp.maximum(jnp.max(logD, axis=1, keepdims=True), inter)
            q = q_ref[b, :, cols]
            ks = k_ref[b, :, cols] * (HD ** -0.5)
            v = v_ref[b, :, cols]
            S = _dot_nt(q, ks) * jnp.exp(logD - mt)
            w_int = jnp.exp(inter - mt)
            Cmat = C_ref[b, h]
            nvec = n_ref[b, h]
            num = _dot(S.astype(BF16), v) + w_int * _dot_nt(q, Cmat.astype(BF16))
            nq = jnp.sum(q.astype(F32) * nvec, axis=1, keepdims=True)
            den = jnp.sum(S, axis=1, keepdims=True) + w_int * nq
            hh = num / jnp.maximum(jnp.abs(den), jnp.exp(-mt))
            hg = jax.nn.sigmoid(o_ref[b, :, cols].astype(F32)) * hh
            hm_ref[b, :, cols] = (_rms(hg) * gh_ref[h]).astype(hm_ref.dtype)

            bL = b_row[:, L - 1:L]
            g_row = bL - b_row + ig_row
            m_new = jnp.maximum(bL + m_prev, jnp.max(g_row, axis=1, keepdims=True))
            w_old = jnp.exp(bL + m_prev - m_new)
            kw = ks.astype(F32) * to_col(jnp.exp(g_row - m_new))
            C_ref[b, h] = w_old * Cmat + _dot_tn(v, kw.astype(BF16))
            n_ref[b, h] = w_old * nvec + jnp.sum(kw, axis=0, keepdims=True)
            m_ref[b, h] = jnp.broadcast_to(m_new, (1, LANES))


def _mlstm_prompt(z, gates_t, g_head, batch, seq):
    nc = seq // ML_CHUNK
    nb = MLSTM_SEQS
    z3 = z.reshape(batch, seq, z.shape[1])

    def zspec(col):
        return pl.BlockSpec((nb, ML_CHUNK, D), lambda b, c: (b, c, col))

    def state(last):
        return pl.BlockSpec((nb, HEADS) + last, lambda b, c: (b, 0, 0, 0))

    return pl.pallas_call(
        _mlstm_prompt_kernel,
        grid=(batch // nb, nc),
        in_specs=[zspec(0), zspec(1), zspec(2), zspec(3),
                  pl.BlockSpec((nb, 2 * HEADS, ML_CHUNK), lambda b, c: (b, 0, c)),
                  pl.BlockSpec((HEADS, 1, HD), lambda b, c: (0, 0, 0))],
        out_specs=[pl.BlockSpec((nb, ML_CHUNK, D), lambda b, c: (b, c, 0)),
                   state((HD, HD)), state((1, HD)), state((1, LANES))],
        out_shape=[jax.ShapeDtypeStruct((batch, seq, D), BF16),
                   jax.ShapeDtypeStruct((batch, HEADS, HD, HD), F32),
                   jax.ShapeDtypeStruct((batch, HEADS, 1, HD), F32),
                   jax.ShapeDtypeStruct((batch, HEADS, 1, LANES), F32)],
        compiler_params=pltpu.CompilerParams(dimension_semantics=("arbitrary", "arbitrary")),
        name="mlstm_prompt",
    )(z3, z3, z3, z3, gates_t, g_head.reshape(HEADS, 1, HD))


def _mlstm_sample_kernel(seq_len, q_ref, k_ref, v_ref, o_ref, gc_ref, m0_ref, C0_ref, n0_ref, gh_ref,
                         hm_ref, C_ref, n_ref, m_ref):
    R = q_ref.shape[0]
    nseq = R // seq_len
    r = lax.broadcasted_iota(jnp.int32, (R, R), 0)
    s = lax.broadcasted_iota(jnp.int32, (R, R), 1)
    rseq = lax.broadcasted_iota(jnp.int32, (R, 1), 0) // seq_len
    eye = r == s
    same = (r // seq_len) == (s // seq_len)
    causal = same & (s <= r)

    def to_row(x_col):
        return jnp.sum(jnp.where(eye, x_col, 0.0), axis=0, keepdims=True)

    gc = gc_ref[...]
    for h in range(HEADS):
        cols = slice(h * HD, (h + 1) * HD)
        ig_col = gc[:, h:h + 1]
        lf_col = _log_sigmoid(gc[:, HEADS + h:HEADS + h + 1])
        lf_row = to_row(lf_col)
        b_col = jnp.sum(jnp.where(causal, lf_row, 0.0), axis=1, keepdims=True)
        bL_col = jnp.sum(jnp.where(same, lf_row, 0.0), axis=1, keepdims=True)
        b_row = to_row(b_col)
        ig_row = to_row(ig_col)
        m0_col = m0_ref[:, h:h + 1]

        logD = jnp.where(causal, b_col - b_row + ig_row, -jnp.inf)
        inter = b_col + m0_col
        mt = jnp.maximum(jnp.max(logD, axis=1, keepdims=True), inter)
        qf = q_ref[:, cols]
        q = qf.astype(BF16)
        ksf = k_ref[:, cols] * (HD ** -0.5)
        v = v_ref[:, cols].astype(BF16)
        S = _dot_nt(q, ksf.astype(BF16)) * jnp.exp(logD - mt)
        w_int = jnp.exp(inter - mt)

        Cq = jnp.zeros((R, HD), F32)
        nq = jnp.zeros((R, 1), F32)
        for g in range(nseq):
            Cq = jnp.where(rseq == g, _dot_nt(q, C0_ref[g, h].astype(BF16)), Cq)
            nq = jnp.where(rseq == g, jnp.sum(qf * n0_ref[g, h], axis=1, keepdims=True), nq)
        num = _dot(S.astype(BF16), v) + w_int * Cq
        den = jnp.sum(S, axis=1, keepdims=True) + w_int * nq
        hh = num / jnp.maximum(jnp.abs(den), jnp.exp(-mt))
        hg = jax.nn.sigmoid(o_ref[:, cols]) * hh
        hm_ref[:, cols] = (_rms(hg) * gh_ref[h]).astype(hm_ref.dtype)

        g_col = bL_col - b_col + ig_col
        gmax_col = jnp.max(jnp.where(same, to_row(g_col), -jnp.inf), axis=1, keepdims=True)
        m_new_col = jnp.maximum(bL_col + m0_col, gmax_col)
        w_old_col = jnp.exp(bL_col + m0_col - m_new_col)
        kw = ksf * jnp.exp(g_col - m_new_col)
        for g in range(nseq):
            kw_g = jnp.where(rseq == g, kw, 0.0)
            w_old = w_old_col[g * seq_len:g * seq_len + 1, :]
            C_ref[g, h] = w_old * C0_ref[g, h] + _dot_tn(v, kw_g.astype(BF16))
            n_ref[g, h] = w_old * n0_ref[g, h] + jnp.sum(kw_g, axis=0, keepdims=True)
        m_ref[:, h * LANES:(h + 1) * LANES] = jnp.broadcast_to(m_new_col, (R, LANES))


def _mlstm_sample(z, gates_c, m0_tok, C0, n0, g_head, batch, seq):
    rows = SAMPLE_SEQS * seq
    m = batch * seq

    def zspec(col):
        return pl.BlockSpec((rows, D), lambda i: (i, col))

    state_c = pl.BlockSpec((SAMPLE_SEQS, HEADS, HD, HD), lambda i: (i, 0, 0, 0))
    state_n = pl.BlockSpec((SAMPLE_SEQS, HEADS, 1, HD), lambda i: (i, 0, 0, 0))
    return pl.pallas_call(
        functools.partial(_mlstm_sample_kernel, seq),
        grid=(batch // SAMPLE_SEQS,),
        in_specs=[zspec(0), zspec(1), zspec(2), zspec(3),
                  pl.BlockSpec((rows, LANES), lambda i: (i, 0)),
                  pl.BlockSpec((rows, HEADS), lambda i: (i, 0)),
                  state_c, state_n,
                  pl.BlockSpec((HEADS, 1, HD), lambda i: (0, 0, 0))],
        out_specs=[pl.BlockSpec((rows, D), lambda i: (i, 0)),
                   state_c, state_n,
                   pl.BlockSpec((rows, HEADS * LANES), lambda i: (i, 0))],
        out_shape=[jax.ShapeDtypeStruct((m, D), BF16),
                   jax.ShapeDtypeStruct((batch, HEADS, HD, HD), F32),
                   jax.ShapeDtypeStruct((batch, HEADS, 1, HD), F32),
                   jax.ShapeDtypeStruct((m, HEADS * LANES), F32)],
        compiler_params=pltpu.CompilerParams(
            dimension_semantics=("arbitrary",), vmem_limit_bytes=VMEM_LIMIT),
        name="mlstm_sample",
    )(z, z, z, z, gates_c, m0_tok, C0, n0.reshape(batch, HEADS, 1, HD), g_head.reshape(HEADS, 1, HD))


def _mix_kernel(x_ref, hm_ref, u_ref, v_ref, ga_ref, gb_ref, gt1_ref, sh2_ref, sc2_ref,
                gsgu_ref, bsgu_ref, ws_ref, bs_ref, wpa_ref, wpb_ref, wout_ref, gffn_ref,
                wr_ref, br_ref, x1_ref, h2_ref, choice_ref, gate_ref, vg_ref, cnt_ref, yg_scr):
    tm = x_ref.shape[0]
    u = jax.nn.gelu(u_ref[...].astype(F32))
    vv = jax.nn.gelu(v_ref[...].astype(F32))
    mu = jnp.mean(vv, axis=-1, keepdims=True)
    var = jnp.mean(jnp.square(vv - mu), axis=-1, keepdims=True)
    vg = (vv - mu) * lax.rsqrt(var + NORM_EPS) * gsgu_ref[...] + bsgu_ref[...]
    vg_ref[...] = vg
    vgb = vg.astype(BF16)

    r = lax.broadcasted_iota(jnp.int32, (GM_CHUNK, GM_CHUNK), 0)
    s = lax.broadcasted_iota(jnp.int32, (GM_CHUNK, GM_CHUNK), 1)
    for g in range(GROUPS):
        w = jnp.where(s <= r, ws_ref[g], 0.0).astype(BF16)
        bias = bs_ref[:, g:g + 1]
        for c in range(tm // GM_CHUNK):
            rows = slice(c * GM_CHUNK, (c + 1) * GM_CHUNK)
            cols = slice(g * GD, (g + 1) * GD)
            mixed = _dot(w, vgb[rows, cols]) + bias
            yg_scr[rows, cols] = (u[rows, cols] * mixed).astype(BF16)

    a = _dot(hm_ref[...], wpa_ref[...])
    b = _dot(yg_scr[...], wpb_ref[...])
    merged = (jax.nn.sigmoid(ga_ref[...].astype(F32)) * a
              + jax.nn.sigmoid(gb_ref[...].astype(F32)) * b)
    x1 = x_ref[...] + gt1_ref[...] * _dot(merged.astype(BF16), wout_ref[...])
    x1_ref[...] = x1
    h2 = (_rms(x1) * gffn_ref[...]) * (1.0 + sc2_ref[...]) + sh2_ref[...]
    h2_ref[...] = _pack_rows(h2)

    lane = lax.broadcasted_iota(jnp.int32, (tm, LANES), 1)
    lane_f = lane.astype(F32)
    lg = jnp.where(lane < N_EXPERTS, _dot3(h2, wr_ref[...]) + br_ref[...], -jnp.inf)
    choice = jnp.zeros((tm, LANES), F32)
    vals = []
    for k in range(TOP_K):
        mx = jnp.max(lg, axis=1, keepdims=True)
        sel = lane_f == jnp.min(jnp.where(lg == mx, lane_f, float(LANES)), axis=1, keepdims=True)
        choice = jnp.where(sel, k + 1.0, choice)
        vals.append(mx)
        lg = jnp.where(sel, -jnp.inf, lg)
    choice_ref[...] = choice
    ex = [jnp.exp(v - vals[0]) for v in vals]
    denom = sum(ex)
    gates = jnp.zeros((tm, LANES), F32)
    for k in range(TOP_K):
        gates = jnp.where(lane == k, ex[k] / denom, gates)
    gate_ref[...] = gates[:, :TOP_K]

    @pl.when(pl.program_id(0) == 0)
    def _():
        cnt_ref[...] = jnp.zeros_like(cnt_ref)

    cnt_ref[...] += jnp.sum(jnp.where(choice > 0.0, 1.0, 0.0), axis=0, keepdims=True)


def _mix(x, hm, z, mod, per_token, tokens_per_seq, tm, p):
    m = x.shape[0]
    zcol = 4

    def zspec(blk):
        return pl.BlockSpec((tm, D), lambda i: (i, blk))

    def full(shape):
        return pl.BlockSpec(shape, lambda i: (0,) * len(shape))

    row = pl.BlockSpec((tm, D), lambda i: (i, 0))
    return pl.pallas_call(
        _mix_kernel,
        grid=(m // tm,),
        in_specs=[row, row, zspec(zcol), zspec(zcol + 1), zspec(zcol + 2), zspec(zcol + 3),
                  _mod_spec(per_token, tm, tokens_per_seq, 2),
                  _mod_spec(per_token, tm, tokens_per_seq, 3),
                  _mod_spec(per_token, tm, tokens_per_seq, 4),
                  full((1, D)), full((1, D)),
                  full((GROUPS, GM_CHUNK, GM_CHUNK)), full((GM_CHUNK, GROUPS)),
                  full((D, D)), full((D, D)), full((D, D)), full((1, D)),
                  full((D, LANES)), full((1, LANES))],
        out_specs=[row, pl.BlockSpec((tm, PACKED), lambda i: (i, 0)),
                   pl.BlockSpec((tm, LANES), lambda i: (i, 0)), pl.BlockSpec((tm, TOP_K), lambda i: (i, 0)),
                   row, full((1, LANES))],
        out_shape=[jax.ShapeDtypeStruct((m, D), F32),
                   jax.ShapeDtypeStruct((m, PACKED), jnp.int32),
                   jax.ShapeDtypeStruct((m, LANES), F32),
                   jax.ShapeDtypeStruct((m, TOP_K), F32),
                   jax.ShapeDtypeStruct((m, D), F32),
                   jax.ShapeDtypeStruct((1, LANES), F32)],
        scratch_shapes=[pltpu.VMEM((tm, D), BF16)],
        compiler_params=pltpu.CompilerParams(
            dimension_semantics=("arbitrary",), vmem_limit_bytes=VMEM_LIMIT),
        name="mix",
    )(x, hm, z, z, z, z, mod, mod, mod, p["g_sgu"], p["b_sgu"], p["w_s"], p["b_s"],
      p["w_pa"], p["w_pb"], p["w_out"], p["g_ffn"], p["w_router"], p["b_router"])


def _route_kernel(tiles_a, cha_ref, chb_ref, cnta_ref, cntb_ref, dest_ref, ex_ref, base_scr):
    tm = cha_ref.shape[0]
    lane = lax.broadcasted_iota(jnp.int32, (tm, LANES), 1)
    choice = jnp.where(pl.program_id(0) < tiles_a, cha_ref[...], chb_ref[...])
    onehot = jnp.where(choice > 0.0, 1.0, 0.0)

    @pl.when(pl.program_id(0) == 0)
    def _():
        cnt = cnta_ref[...] + cntb_ref[...]
        padded = jnp.floor((cnt + (MOE_ROWS - 1)) * (1.0 / MOE_ROWS)) * MOE_ROWS
        r = lax.broadcasted_iota(jnp.int32, (LANES, LANES), 0)
        s = lax.broadcasted_iota(jnp.int32, (LANES, LANES), 1)
        padded_col = jnp.sum(jnp.where(r == s, padded, 0.0), axis=1, keepdims=True)
        pstart = jnp.sum(jnp.where(r < s, padded_col, 0.0), axis=0, keepdims=True)
        base_scr[...] = pstart
        trow = lax.broadcasted_iota(jnp.int32, ex_ref.shape, 0)
        table = jnp.where(trow == 0, pstart * (1.0 / MOE_ROWS),
                          jnp.where(trow == 1, padded * (1.0 / MOE_ROWS), jnp.where(trow == 2, cnt, 0.0)))
        ex_ref[...] = table.astype(jnp.int32)

    r = lax.broadcasted_iota(jnp.int32, (tm, tm), 0)
    s = lax.broadcasted_iota(jnp.int32, (tm, tm), 1)
    before = _dot((s < r).astype(BF16), onehot.astype(BF16))
    slot = before + base_scr[...]
    dest = jnp.zeros((tm, LANES), F32)
    for k in range(TOP_K):
        d_k = jnp.sum(jnp.where(choice == k + 1.0, slot, 0.0), axis=1, keepdims=True)
        dest = jnp.where(lane == k, d_k, dest)
    dest_ref[...] = dest[:, :TOP_K].astype(jnp.int32)
    base_scr[...] += jnp.sum(onehot, axis=0, keepdims=True)


def _route(choice_a, choice_b, cnt_a, cnt_b):
    tm = 256
    tiles_a, tiles_b = choice_a.shape[0] // tm, choice_b.shape[0] // tm
    n = (tiles_a + tiles_b) * tm
    fixed = lambda i: (0, 0)
    return pl.pallas_call(
        functools.partial(_route_kernel, tiles_a),
        grid=(tiles_a + tiles_b,),
        in_specs=[pl.BlockSpec((tm, LANES), lambda i: (jnp.minimum(i, tiles_a - 1), 0)),
                  pl.BlockSpec((tm, LANES), lambda i: (jnp.maximum(i - tiles_a, 0), 0)),
                  pl.BlockSpec((1, LANES), fixed), pl.BlockSpec((1, LANES), fixed)],
        out_specs=[pl.BlockSpec((tm, TOP_K), lambda i: (i, 0)), pl.BlockSpec((8, LANES), fixed)],
        out_shape=[jax.ShapeDtypeStruct((n, TOP_K), jnp.int32),
                   jax.ShapeDtypeStruct((8, LANES), jnp.int32)],
        scratch_shapes=[pltpu.VMEM((1, LANES), F32)],
        compiler_params=pltpu.CompilerParams(dimension_semantics=("arbitrary",)),
        name="route",
    )(choice_a, choice_b, cnt_a, cnt_b)


def _sc_worker_base(per_worker):
    return (lax.axis_index("s") * SC_CORES + lax.axis_index("c")) * per_worker


def _sc_scatter_rows(rows_a, rows_b, idx_flat, n_out):
    na, nb = rows_a.shape[0], rows_b.shape[0]
    width, dtype = rows_a.shape[1], rows_a.dtype
    n = na + nb
    per_a, per_b = na // SC_WORKERS, nb // SC_WORKERS
    chunk = SC_SCATTER_CHUNK
    n_chunks = per_a // chunk
    assert per_a * SC_WORKERS == na and per_b * SC_WORKERS == nb and per_b % 8 == 0 and per_b <= chunk
    assert n_chunks * chunk == per_a and n_chunks % 2 == 0
    mesh = plsc.VectorSubcoreMesh(core_axis_name="c", subcore_axis_name="s")

    @functools.partial(
        pl.kernel, mesh=mesh,
        out_type=jax.ShapeDtypeStruct((n_out, width), dtype),
        scratch_types=[pltpu.VMEM((chunk,), jnp.int32)] * TOP_K + [pltpu.VMEM((per_b,), jnp.int32)]
                      + [pltpu.VMEM((chunk, width), dtype), pltpu.VMEM((chunk, width), dtype),
                         pltpu.VMEM((per_b, width), dtype)]
                      + [pltpu.SemaphoreType.DMA] * 3,
    )
    def scatter(a_hbm, b_hbm, idx_hbm, out_hbm, i0, i1, i2, i3, ib, rows0, rows1, rowsb, rsem0, rsem1, wsem):
        base = _sc_worker_base(per_a)
        idx_bufs = (i0, i1, i2, i3)
        bufs = ((rows0, rsem0), (rows1, rsem1))

        def off(j):
            return pl.multiple_of(base + j * chunk, 8)

        def read(j, buf):
            rows_v, sem = buf
            return pltpu.make_async_copy(a_hbm.at[pl.ds(off(j), chunk)], rows_v, sem)

        def spread(j, buf):
            rows_v, _ = buf
            read(j, buf).wait()
            for k in range(TOP_K):
                pltpu.sync_copy(idx_hbm.at[pl.ds(pl.multiple_of(k * n + off(j), 8), chunk)], idx_bufs[k])
            for k in range(TOP_K):
                pltpu.make_async_copy(rows_v, out_hbm.at[idx_bufs[k]], wsem).start()
            for k in range(TOP_K):
                pltpu.make_async_copy(rows_v, out_hbm.at[idx_bufs[k]], wsem).wait()

        read(0, bufs[0]).start()

        @pl.loop(0, n_chunks, step=2)
        def _(j):
            read(j + 1, bufs[1]).start()
            spread(j, bufs[0])

            @pl.when(j + 2 < n_chunks)
            def _():
                read(j + 2, bufs[0]).start()
            spread(j + 1, bufs[1])

        off_b = pl.multiple_of(_sc_worker_base(per_b), 8)
        pltpu.sync_copy(b_hbm.at[pl.ds(off_b, per_b)], rowsb)
        for k in range(TOP_K):
            pltpu.sync_copy(idx_hbm.at[pl.ds(pl.multiple_of(k * n + na + off_b, 8), per_b)], ib)
            pltpu.async_copy(rowsb, out_hbm.at[ib], wsem).wait()

    return scatter(rows_a, rows_b, idx_flat)


def _sc_gather_rows(table, idx_flat):
    b = idx_flat.shape[0]
    width, dtype = table.shape[1], table.dtype
    per_worker = b // SC_WORKERS
    chunk = SC_GATHER_CHUNK
    n_chunks = per_worker // chunk
    assert per_worker * SC_WORKERS == b and n_chunks * chunk == per_worker and n_chunks % 2 == 0
    mesh = plsc.VectorSubcoreMesh(core_axis_name="c", subcore_axis_name="s")

    @functools.partial(
        pl.kernel, mesh=mesh,
        out_type=jax.ShapeDtypeStruct((b, width), dtype),
        scratch_types=[pltpu.VMEM((chunk,), jnp.int32), pltpu.VMEM((chunk,), jnp.int32),
                       pltpu.VMEM((chunk, width), dtype), pltpu.VMEM((chunk, width), dtype),
                       pltpu.SemaphoreType.DMA, pltpu.SemaphoreType.DMA],
    )
    def gather(table_hbm, idx_hbm, out_hbm, idx0, idx1, rows0, rows1, sem0, sem1):
        base = _sc_worker_base(per_worker)
        bufs = ((idx0, rows0, sem0), (idx1, rows1, sem1))

        def off(j):
            return pl.multiple_of(base + j * chunk, 8)

        def start(j, buf):
            idx_v, rows_v, sem = buf
            pltpu.sync_copy(idx_hbm.at[pl.ds(off(j), chunk)], idx_v)
            pltpu.make_async_copy(table_hbm.at[idx_v], rows_v, sem).start()

        def finish(j, buf):
            idx_v, rows_v, sem = buf
            pltpu.make_async_copy(table_hbm.at[idx_v], rows_v, sem).wait()
            pltpu.sync_copy(rows_v, out_hbm.at[pl.ds(off(j), chunk)])

        start(0, bufs[0])

        @pl.loop(0, n_chunks, step=2)
        def _(j):
            start(j + 1, bufs[1])
            finish(j, bufs[0])

            @pl.when(j + 2 < n_chunks)
            def _():
                start(j + 2, bufs[0])
            finish(j + 1, bufs[1])

    return gather(table, idx_flat)


def _moe_kernel(first_ref, nblk_ref, cnt_ref, xs_hbm, wgu_ref, bgu_ref, wd_ref, bd_ref, out_hbm,
                xbuf, obuf, wgu_bf, wd_bf, xsem, osem):
    e = pl.program_id(0)
    first, nblk, cnt = first_ref[e], nblk_ref[e], cnt_ref[e]

    def rows(j):
        return pl.ds(pl.multiple_of((first + j) * MOE_ROWS, MOE_ROWS), MOE_ROWS)

    def x_copy(j, slot):
        return pltpu.make_async_copy(xs_hbm.at[rows(j)], xbuf.at[slot], xsem.at[slot])

    def o_copy(j, slot):
        return pltpu.make_async_copy(obuf.at[slot], out_hbm.at[rows(j)], osem.at[slot])

    @pl.when(nblk > 0)
    def _():
        x_copy(0, 0).start(priority=ROW_DMA_PRIORITY)
        wgu_bf[...] = wgu_ref[...].astype(BF16)
        wd_bf[...] = wd_ref[...].astype(BF16)

        def block(j, carry):
            slot = j % 2
            x_copy(j, slot).wait()

            @pl.when(j + 1 < nblk)
            def _():
                x_copy(j + 1, 1 - slot).start(priority=ROW_DMA_PRIORITY)

            @pl.when(j >= 2)
            def _():
                o_copy(j - 2, slot).wait()

            row = lax.broadcasted_iota(jnp.int32, (MOE_ROWS, 1), 0)
            x = _unpack_rows(jnp.where(row < cnt - j * MOE_ROWS, xbuf[slot], 0)).astype(BF16)
            gu = _dot(x, wgu_bf[...]) + bgu_ref[...]
            gate = jnp.minimum(gu[:, :D_FF], SWIGLU_LIMIT)
            up = jnp.clip(gu[:, D_FF:], -SWIGLU_LIMIT, SWIGLU_LIMIT)
            act = gate * jax.nn.sigmoid(SWIGLU_ALPHA * gate) * (up + 1.0)
            obuf[slot] = _pack_rows(_dot(act.astype(BF16), wd_bf[...]) + bd_ref[...])
            o_copy(j, slot).start(priority=ROW_DMA_PRIORITY)
            return carry

        lax.fori_loop(0, nblk, block, 0)

        @pl.when(nblk >= 2)
        def _():
            o_copy(nblk - 2, nblk % 2).wait()
        o_copy(nblk - 1, (nblk - 1) % 2).wait()


def _moe(xs, first_block, n_blocks, counts, w_gu, b_gu, w_down, b_down):
    grid_spec = pltpu.PrefetchScalarGridSpec(
        num_scalar_prefetch=3,
        grid=(N_EXPERTS,),
        in_specs=[pl.BlockSpec(memory_space=pl.ANY),
                  pl.BlockSpec((None, D, 2 * D_FF), lambda e, *_: (e, 0, 0)),
                  pl.BlockSpec((None, 1, 2 * D_FF), lambda e, *_: (e, 0, 0)),
                  pl.BlockSpec((None, D_FF, D), lambda e, *_: (e, 0, 0)),
                  pl.BlockSpec((None, 1, D), lambda e, *_: (e, 0, 0))],
        out_specs=pl.BlockSpec(memory_space=pl.ANY),
        scratch_shapes=[pltpu.VMEM((2, MOE_ROWS, PACKED), jnp.int32), pltpu.VMEM((2, MOE_ROWS, PACKED), jnp.int32),
                        pltpu.VMEM((D, 2 * D_FF), BF16), pltpu.VMEM((D_FF, D), BF16),
                        pltpu.SemaphoreType.DMA((2,)), pltpu.SemaphoreType.DMA((2,))],
    )
    return pl.pallas_call(
        _moe_kernel,
        grid_spec=grid_spec,
        out_shape=jax.ShapeDtypeStruct(xs.shape, jnp.int32),
        compiler_params=pltpu.CompilerParams(
            dimension_semantics=("arbitrary",), vmem_limit_bytes=MOE_VMEM_LIMIT),
        name="moe",
    )(first_block, n_blocks, counts, xs, w_gu, b_gu.reshape(N_EXPERTS, 1, 2 * D_FF), w_down,
      b_down.reshape(N_EXPERTS, 1, D))


def _final_kernel(x1_ref, yk_ref, gate_ref, gt2_ref, g_ref, o_ref):
    y2 = gate_ref[:, 0:1] * _unpack_rows(yk_ref[0])
    for k in range(1, TOP_K):
        y2 = y2 + gate_ref[:, k:k + 1] * _unpack_rows(yk_ref[k])
    o_ref[...] = _rms(x1_ref[...] + gt2_ref[...] * y2) * g_ref[...]


def _final(x1, yk, gates, row_off, mod, per_token, tokens_per_seq, g_final):
    tm = 256
    m = x1.shape[0]
    off = row_off // tm
    return pl.pallas_call(
        _final_kernel,
        grid=(m // tm,),
        in_specs=[pl.BlockSpec((tm, D), lambda i: (i, 0)),
                  pl.BlockSpec((TOP_K, tm, PACKED), lambda i: (0, i + off, 0)),
                  pl.BlockSpec((tm, TOP_K), lambda i: (i, 0)),
                  _mod_spec(per_token, tm, tokens_per_seq, 5),
                  pl.BlockSpec((1, D), lambda i: (0, 0))],
        out_specs=pl.BlockSpec((tm, D), lambda i: (i, 0)),
        out_shape=jax.ShapeDtypeStruct((m, D), F32),
        compiler_params=pltpu.CompilerParams(
            dimension_semantics=("arbitrary",), vmem_limit_bytes=VMEM_LIMIT),
        name="final",
    )(x1, yk, gates, mod, g_final.reshape(1, D))


def kernel(x_prompt, x_sample, state_C, state_n, state_m, c_prompt, c_sample, w_ada, b_ada, g_mix, w_in,
           b_if, g_head, g_sgu, b_sgu, w_s, b_s, w_pa, w_pb, w_out, g_ffn, w_router, b_router, w_gu, b_gu,
           w_down, b_down, g_final):
    depth = w_ada.shape[0]
    assert depth == 1
    bp, tp, _ = x_prompt.shape
    bs, ts, _ = x_sample.shape
    mp, ms = bp * tp, bs * ts
    assert tp % ML_CHUNK == 0 and ts <= ML_CHUNK and GM_CHUNK % ts == 0

    w_in0 = w_in[0]
    nqkvo = 4 * D
    w_qkvo = w_in0[:, :nqkvo].astype(BF16)
    w_gate = w_in0[:, nqkvo + 2 * HEADS:].astype(BF16)
    w_if = jnp.pad(w_in0[:, nqkvo:nqkvo + 2 * HEADS], ((0, 0), (0, LANES - 2 * HEADS)))
    b_if_p = jnp.pad(b_if[0], (0, LANES - 2 * HEADS)).reshape(1, LANES)
    reps = GM_CHUNK // ts
    eye_r = jnp.eye(reps, dtype=F32)
    w_s_sample = jnp.einsum("ab,gts->gatbs", eye_r, w_s[0][:, :ts, :ts]).reshape(GROUPS, GM_CHUNK, GM_CHUNK)
    b_s_prompt = b_s[0].T
    b_s_sample = jnp.tile(b_s[0][:, :ts].T, (reps, 1))
    mix_p = {
        "g_sgu": g_sgu[0].reshape(1, D), "b_sgu": b_sgu[0].reshape(1, D),
        "w_pa": w_pa[0].astype(BF16), "w_pb": w_pb[0].astype(BF16), "w_out": w_out[0].astype(BF16),
        "g_ffn": g_ffn[0].reshape(1, D),
        "w_router": jnp.pad(w_router[0], ((0, 0), (0, LANES - N_EXPERTS))),
        "b_router": jnp.pad(b_router[0], (0, LANES - N_EXPERTS)).reshape(1, LANES),
    }
    mix_prompt = dict(mix_p, w_s=w_s[0], b_s=b_s_prompt)
    mix_sample = dict(mix_p, w_s=w_s_sample, b_s=b_s_sample)

    mod = _ada(jnp.concatenate([jnp.repeat(c_sample, ts, axis=0), c_prompt], axis=0), w_ada[0], b_ada[0])
    mod_s = mod
    mod_p = mod[ms:].reshape(bp, 1, N_MOD * D)

    xp = x_prompt.reshape(mp, D)
    xs = x_sample.reshape(ms, D)
    z_p, _, gt_p = _in_proj(xp, mod_p, False, tp, g_mix[0], w_qkvo, w_gate, w_if, b_if_p, 512, BF16)
    z_s, gc_s, _ = _in_proj(xs, mod_s, True, ts, g_mix[0], w_qkvo, w_gate, w_if, b_if_p, 128, F32)

    hm_p, C_p, n_p, m_p = _mlstm_prompt(z_p, gt_p, g_head[0], bp, tp)
    hm_p = hm_p.reshape(mp, D)
    m0_tok = jnp.repeat(state_m[0], ts, axis=0)
    hm_s, C_s, n_s, m_s = _mlstm_sample(z_s, gc_s, m0_tok, state_C[0], state_n[0], g_head[0], bs, ts)

    x1_p, h2_p, ch_p, gates_p, _, cnt_p = _mix(xp, hm_p, z_p, mod_p, False, tp, 256, mix_prompt)
    x1_s, h2_s, ch_s, gates_s, vg_s, cnt_s = _mix(xs, hm_s, z_s, mod_s, True, ts, 128, mix_sample)

    n_tok = mp + ms
    n_blocks = -(-(n_tok * TOP_K + N_EXPERTS * (MOE_ROWS - 1)) // MOE_ROWS)
    dest, ex_table = _route(ch_p, ch_s, cnt_p, cnt_s)
    dest_kmajor = dest.T.reshape(TOP_K * n_tok)
    xslots = _sc_scatter_rows(h2_p, h2_s, dest_kmajor, n_blocks * MOE_ROWS)
    yb = _moe(xslots, ex_table[0, :N_EXPERTS], ex_table[1, :N_EXPERTS], ex_table[2, :N_EXPERTS],
              w_gu[0], b_gu[0], w_down[0], b_down[0])
    yk = _sc_gather_rows(yb, dest_kmajor).reshape(TOP_K, n_tok, PACKED)

    y_p = _final(x1_p, yk, gates_p, 0, mod_p, False, tp, g_final)
    y_s = _final(x1_s, yk, gates_s, mp, mod_s, True, ts, g_final)

    return (y_p.reshape(bp, tp, D), y_s.reshape(bs, ts, D),
            C_p[None], n_p.reshape(1, bp, HEADS, HD), m_p[:, :, 0, 0][None],
            C_s[None], n_s.reshape(1, bs, HEADS, HD),
            m_s.reshape(bs, ts, HEADS, LANES)[:, 0, :, 0][None],
            vg_s.reshape(1, bs, ts, D))
```

```python
import functools

import jax
import jax.numpy as jnp
from jax import lax
from jax.experimental import pallas as pl
from jax.experimental.pallas import tpu as pltpu
from jax.experimental.pallas import tpu_sc as plsc

F32 = jnp.float32
BF16 = jnp.bfloat16

D = 1024
HEADS = 4
HD = D // HEADS
ML_CHUNK = 512
GROUPS = 4
GD = D // GROUPS
GM_CHUNK = 128
N_EXPERTS = 32
TOP_K = 4
D_FF = D
SWIGLU_LIMIT = 7.0
SWIGLU_ALPHA = 1.702
NORM_EPS = 1e-6
N_MOD = 6
PACKED = D // 2
LANES = 128
SC_CORES = 2
SC_SUBCORES = 16
SC_WORKERS = SC_CORES * SC_SUBCORES
IN_COLS = 2048
ROW_TILE = 512
MOE_ROWS = 256
ROW_DMA_PRIORITY = 1
MLSTM_SEQS = 1
SAMPLE_SEQS = 4
SC_SCATTER_CHUNK = 64
SC_GATHER_CHUNK = 96
VMEM_LIMIT = 48 * 1024 * 1024
MOE_VMEM_LIMIT = 56 * 1024 * 1024


def _dot(a, b):
    return jnp.dot(a, b, preferred_element_type=F32)


def _dot_nt(a, b):
    return lax.dot_general(a, b, (((1,), (1,)), ((), ())), preferred_element_type=F32)


def _dot_tn(a, b):
    return lax.dot_general(a, b, (((0,), (0,)), ((), ())), preferred_element_type=F32)


def _split_bf16(a):
    hi = a.astype(BF16)
    lo = (a - hi.astype(F32)).astype(BF16)
    return hi, lo


def _dot3(a, b):
    ah, al = _split_bf16(a)
    bh, bl = _split_bf16(b)
    return _dot(ah, bh) + (_dot(ah, bl) + _dot(al, bh))


def _log_sigmoid(x):
    return jnp.minimum(x, 0.0) - jnp.log1p(jnp.exp(-jnp.abs(x)))


def _rms(x):
    return x * lax.rsqrt(jnp.mean(x * x, axis=-1, keepdims=True) + NORM_EPS)


def _pack_rows(x):
    bits = lax.bitcast_convert_type(x.astype(BF16).astype(F32), jnp.uint32)
    word = (bits[:, :PACKED] & jnp.uint32(0xFFFF0000)) | (bits[:, PACKED:] >> 16)
    return lax.bitcast_convert_type(word, jnp.int32)


def _unpack_rows(w):
    bits = lax.bitcast_convert_type(w, jnp.uint32)
    left = lax.bitcast_convert_type(bits & jnp.uint32(0xFFFF0000), F32)
    right = lax.bitcast_convert_type(bits << 16, F32)
    return jnp.concatenate([left, right], axis=1)


def _mod_spec(per_token, tm, tokens_per_seq, col):
    if per_token:
        return pl.BlockSpec((tm, D), lambda i, *_: (i, col))
    return pl.BlockSpec((None, 1, D), lambda i, *_: ((i * tm) // tokens_per_seq, 0, col))


def _ada_kernel(c_ref, w_ref, b_ref, o_ref):
    c = c_ref[...]
    s = (c * jax.nn.sigmoid(c)).astype(BF16)
    o_ref[...] = _dot(s, w_ref[...].astype(BF16)) + b_ref[...]


def _ada(c, w, b):
    m, n = c.shape[0], w.shape[1]
    tn = 512
    return pl.pallas_call(
        _ada_kernel,
        grid=(n // tn,),
        in_specs=[pl.BlockSpec((m, D), lambda j: (0, 0)),
                  pl.BlockSpec((D, tn), lambda j: (0, j)),
                  pl.BlockSpec((1, tn), lambda j: (0, j))],
        out_specs=pl.BlockSpec((m, tn), lambda j: (0, j)),
        out_shape=jax.ShapeDtypeStruct((m, n), F32),
        name="ada",
    )(c, w, b.reshape(1, n))


def _in_kernel(x_ref, g_ref, sh_ref, sc_ref, wa_ref, wb_ref, wif_ref, bif_ref, z_ref, gc_ref, gt_ref):
    h = (_rms(x_ref[...]) * g_ref[...]) * (1.0 + sc_ref[...]) + sh_ref[...]
    hb = h.astype(BF16)
    gates = _dot3(h, wif_ref[...]) + bif_ref[...]
    gc_ref[...] = gates
    gt_ref[...] = gates.T[:2 * HEADS, :]
    half = wa_ref.shape[1]
    for w_ref, col0 in ((wa_ref, 0), (wb_ref, half)):
        for c in range(half // IN_COLS):
            cols = slice(c * IN_COLS, (c + 1) * IN_COLS)
            z_ref[:, col0 + c * IN_COLS:col0 + (c + 1) * IN_COLS] = _dot(hb, w_ref[:, cols]).astype(z_ref.dtype)


def _in_proj(x, mod, per_token, tokens_per_seq, g_mix, w_a, w_b, w_if, b_if, tm, z_dtype):
    m = x.shape[0]
    half = w_a.shape[1]
    gt_len = tokens_per_seq if tokens_per_seq % tm == 0 else m
    gt_tiles = gt_len // tm
    resident = functools.partial(pl.BlockSpec, index_map=lambda i: (0, 0), pipeline_mode=pl.Buffered(1))
    return pl.pallas_call(
        _in_kernel,
        grid=(m // tm,),
        in_specs=[pl.BlockSpec((tm, D), lambda i: (i, 0)),
                  pl.BlockSpec((1, D), lambda i: (0, 0)),
                  _mod_spec(per_token, tm, tokens_per_seq, 0),
                  _mod_spec(per_token, tm, tokens_per_seq, 1),
                  resident((D, half)), resident((D, half)),
                  pl.BlockSpec((D, LANES), lambda i: (0, 0)),
                  pl.BlockSpec((1, LANES), lambda i: (0, 0))],
        out_specs=[pl.BlockSpec((tm, 2 * half), lambda i: (i, 0)),
                   pl.BlockSpec((tm, LANES), lambda i: (i, 0)),
                   pl.BlockSpec((None, 2 * HEADS, tm), lambda i: (i // gt_tiles, 0, i % gt_tiles))],
        out_shape=[jax.ShapeDtypeStruct((m, 2 * half), z_dtype),
                   jax.ShapeDtypeStruct((m, LANES), F32),
                   jax.ShapeDtypeStruct((m // gt_len, 2 * HEADS, gt_len), F32)],
        compiler_params=pltpu.CompilerParams(
            dimension_semantics=("arbitrary",), vmem_limit_bytes=MOE_VMEM_LIMIT),
        name="in_proj",
    )(x, g_mix.reshape(1, D), mod, mod, w_a, w_b, w_if, b_if)


def _mlstm_prompt_kernel(q_ref, k_ref, v_ref, o_ref, gt_ref, gh_ref, hm_ref, C_ref, n_ref, m_ref):
    nseq, L = q_ref.shape[0], q_ref.shape[1]

    @pl.when(pl.program_id(1) == 0)
    def _():
        C_ref[...] = jnp.zeros_like(C_ref)
        n_ref[...] = jnp.zeros_like(n_ref)
        m_ref[...] = jnp.zeros_like(m_ref)

    r = lax.broadcasted_iota(jnp.int32, (L, L), 0)
    s = lax.broadcasted_iota(jnp.int32, (L, L), 1)
    eye = r == s
    causal = s <= r

    def to_col(x_row):
        return jnp.sum(jnp.where(eye, x_row, 0.0), axis=1, keepdims=True)

    for b in range(nseq):
        gates = gt_ref[b]
        for h in range(HEADS):
            cols = slice(h * HD, (h + 1) * HD)
            ig_row = gates[h:h + 1, :]
            lf_row = _log_sigmoid(gates[HEADS + h:HEADS + h + 1, :])
            lf_col = to_col(lf_row)
            b_row = jnp.sum(jnp.where(r <= s, lf_col, 0.0), axis=0, keepdims=True)
            b_col = to_col(b_row)
            m_prev = m_ref[b, h][:, :1]

            logD = jnp.where(causal, b_col - b_row + ig_row, -jnp.inf)
            inter = b_col + m_prev
            mt = jnp.maximum(jnp.max(logD, axis=1, keepdims=True), inter)
            q = q_ref[b, :, cols]
            ks = k_ref[b, :, cols] * (HD ** -0.5)
            v = v_ref[b, :, cols]
            S = _dot_nt(q, ks) * jnp.exp(logD - mt)
            w_int = jnp.exp(inter - mt)
            Cmat = C_ref[b, h]
            nvec = n_ref[b, h]
            num = _dot(S.astype(BF16), v) + w_int * _dot_nt(q, Cmat.astype(BF16))
            nq = jnp.sum(q.astype(F32) * nvec, axis=1, keepdims=True)
            den = jnp.sum(S, axis=1, keepdims=True) + w_int * nq
            hh = num / jnp.maximum(jnp.abs(den), jnp.exp(-mt))
            hg = jax.nn.sigmoid(o_ref[b, :, cols].astype(F32)) * hh
            hm_ref[b, :, cols] = (_rms(hg) * gh_ref[h]).astype(hm_ref.dtype)

            bL = b_row[:, L - 1:L]
            g_row = bL - b_row + ig_row
            m_new = jnp.maximum(bL + m_prev, jnp.max(g_row, axis=1, keepdims=True))
            w_old = jnp.exp(bL + m_prev - m_new)
            kw = ks.astype(F32) * to_col(jnp.exp(g_row - m_new))
            C_ref[b, h] = w_old * Cmat + _dot_tn(v, kw.astype(BF16))
            n_ref[b, h] = w_old * nvec + jnp.sum(kw, axis=0, keepdims=True)
            m_ref[b, h] = jnp.broadcast_to(m_new, (1, LANES))


def _mlstm_prompt(z, gates_t, g_head, batch, seq):
    nc = seq // ML_CHUNK
    nb = MLSTM_SEQS
    z3 = z.reshape(batch, seq, z.shape[1])

    def zspec(col):
        return pl.BlockSpec((nb, ML_CHUNK, D), lambda b, c: (b, c, col))

    def state(last):
        return pl.BlockSpec((nb, HEADS) + last, lambda b, c: (b, 0, 0, 0))

    return pl.pallas_call(
        _mlstm_prompt_kernel,
        grid=(batch // nb, nc),
        in_specs=[zspec(0), zspec(1), zspec(2), zspec(3),
                  pl.BlockSpec((nb, 2 * HEADS, ML_CHUNK), lambda b, c: (b, 0, c)),
                  pl.BlockSpec((HEADS, 1, HD), lambda b, c: (0, 0, 0))],
        out_specs=[pl.BlockSpec((nb, ML_CHUNK, D), lambda b, c: (b, c, 0)),
                   state((HD, HD)), state((1, HD)), state((1, LANES))],
        out_shape=[jax.ShapeDtypeStruct((batch, seq, D), BF16),
                   jax.ShapeDtypeStruct((batch, HEADS, HD, HD), F32),
                   jax.ShapeDtypeStruct((batch, HEADS, 1, HD), F32),
                   jax.ShapeDtypeStruct((batch, HEADS, 1, LANES), F32)],
        compiler_params=pltpu.CompilerParams(dimension_semantics=("arbitrary", "arbitrary")),
        name="mlstm_prompt",
    )(z3, z3, z3, z3, gates_t, g_head.reshape(HEADS, 1, HD))


def _mlstm_sample_kernel(seq_len, q_ref, k_ref, v_ref, o_ref, gc_ref, m0_ref, C0_ref, n0_ref, gh_ref,
                         hm_ref, C_ref, n_ref, m_ref):
    R = q_ref.shape[0]
    nseq = R // seq_len
    r = lax.broadcasted_iota(jnp.int32, (R, R), 0)
    s = lax.broadcasted_iota(jnp.int32, (R, R), 1)
    rseq = lax.broadcasted_iota(jnp.int32, (R, 1), 0) // seq_len
    eye = r == s
    same = (r // seq_len) == (s // seq_len)
    causal = same & (s <= r)

    def to_row(x_col):
        return jnp.sum(jnp.where(eye, x_col, 0.0), axis=0, keepdims=True)

    gc = gc_ref[...]
    for h in range(HEADS):
        cols = slice(h * HD, (h + 1) * HD)
        ig_col = gc[:, h:h + 1]
        lf_col = _log_sigmoid(gc[:, HEADS + h:HEADS + h + 1])
        lf_row = to_row(lf_col)
        b_col = jnp.sum(jnp.where(causal, lf_row, 0.0), axis=1, keepdims=True)
        bL_col = jnp.sum(jnp.where(same, lf_row, 0.0), axis=1, keepdims=True)
        b_row = to_row(b_col)
        ig_row = to_row(ig_col)
        m0_col = m0_ref[:, h:h + 1]

        logD = jnp.where(causal, b_col - b_row + ig_row, -jnp.inf)
        inter = b_col + m0_col
        mt = jnp.maximum(jnp.max(logD, axis=1, keepdims=True), inter)
        qf = q_ref[:, cols]
        q = qf.astype(BF16)
        ksf = k_ref[:, cols] * (HD ** -0.5)
        v = v_ref[:, cols].astype(BF16)
        S = _dot_nt(q, ksf.astype(BF16)) * jnp.exp(logD - mt)
        w_int = jnp.exp(inter - mt)

        Cq = jnp.zeros((R, HD), F32)
        nq = jnp.zeros((R, 1), F32)
        for g in range(nseq):
            Cq = jnp.where(rseq == g, _dot_nt(q, C0_ref[g, h].astype(BF16)), Cq)
            nq = jnp.where(rseq == g, jnp.sum(qf * n0_ref[g, h], axis=1, keepdims=True), nq)
        num = _dot(S.astype(BF16), v) + w_int * Cq
        den = jnp.sum(S, axis=1, keepdims=True) + w_int * nq
        hh = num / jnp.maximum(jnp.abs(den), jnp.exp(-mt))
        hg = jax.nn.sigmoid(o_ref[:, cols]) * hh
        hm_ref[:, cols] = (_rms(hg) * gh_ref[h]).astype(hm_ref.dtype)

        g_col = bL_col - b_col + ig_col
        gmax_col = jnp.max(jnp.where(same, to_row(g_col), -jnp.inf), axis=1, keepdims=True)
        m_new_col = jnp.maximum(bL_col + m0_col, gmax_col)
        w_old_col = jnp.exp(bL_col + m0_col - m_new_col)
        kw = ksf * jnp.exp(g_col - m_new_col)
        for g in range(nseq):
            kw_g = jnp.where(rseq == g, kw, 0.0)
            w_old = w_old_col[g * seq_len:g * seq_len + 1, :]
            C_ref[g, h] = w_old * C0_ref[g, h] + _dot_tn(v, kw_g.astype(BF16))
            n_ref[g, h] = w_old * n0_ref[g, h] + jnp.sum(kw_g, axis=0, keepdims=True)
        m_ref[:, h * LANES:(h + 1) * LANES] = jnp.broadcast_to(m_new_col, (R, LANES))


def _mlstm_sample(z, gates_c, m0_tok, C0, n0, g_head, batch, seq):
    rows = SAMPLE_SEQS * seq
    m = batch * seq

    def zspec(col):
        return pl.BlockSpec((rows, D), lambda i: (i, col))

    state_c = pl.BlockSpec((SAMPLE_SEQS, HEADS, HD, HD), lambda i: (i, 0, 0, 0))
    state_n = pl.BlockSpec((SAMPLE_SEQS, HEADS, 1, HD), lambda i: (i, 0, 0, 0))
    return pl.pallas_call(
        functools.partial(_mlstm_sample_kernel, seq),
        grid=(batch // SAMPLE_SEQS,),
        in_specs=[zspec(0), zspec(1), zspec(2), zspec(3),
                  pl.BlockSpec((rows, LANES), lambda i: (i, 0)),
                  pl.BlockSpec((rows, HEADS), lambda i: (i, 0)),
                  state_c, state_n,
                  pl.BlockSpec((HEADS, 1, HD), lambda i: (0, 0, 0))],
        out_specs=[pl.BlockSpec((rows, D), lambda i: (i, 0)),
                   state_c, state_n,
                   pl.BlockSpec((rows, HEADS * LANES), lambda i: (i, 0))],
        out_shape=[jax.ShapeDtypeStruct((m, D), BF16),
                   jax.ShapeDtypeStruct((batch, HEADS, HD, HD), F32),
                   jax.ShapeDtypeStruct((batch, HEADS, 1, HD), F32),
                   jax.ShapeDtypeStruct((m, HEADS * LANES), F32)],
        compiler_params=pltpu.CompilerParams(
            dimension_semantics=("arbitrary",), vmem_limit_bytes=VMEM_LIMIT),
        name="mlstm_sample",
    )(z, z, z, z, gates_c, m0_tok, C0, n0.reshape(batch, HEADS, 1, HD), g_head.reshape(HEADS, 1, HD))


def _mix_kernel(x_ref, hm_ref, u_ref, v_ref, ga_ref, gb_ref, gt1_ref, sh2_ref, sc2_ref,
                gsgu_ref, bsgu_ref, ws_ref, bs_ref, wpa_ref, wpb_ref, wout_ref, gffn_ref,
                wr_ref, br_ref, x1_ref, h2_ref, choice_ref, gate_ref, vg_ref, cnt_ref, yg_scr):
    tm = x_ref.shape[0]
    u = jax.nn.gelu(u_ref[...].astype(F32))
    vv = jax.nn.gelu(v_ref[...].astype(F32))
    mu = jnp.mean(vv, axis=-1, keepdims=True)
    var = jnp.mean(jnp.square(vv - mu), axis=-1, keepdims=True)
    vg = (vv - mu) * lax.rsqrt(var + NORM_EPS) * gsgu_ref[...] + bsgu_ref[...]
    vg_ref[...] = vg
    vgb = vg.astype(BF16)

    r = lax.broadcasted_iota(jnp.int32, (GM_CHUNK, GM_CHUNK), 0)
    s = lax.broadcasted_iota(jnp.int32, (GM_CHUNK, GM_CHUNK), 1)
    for g in range(GROUPS):
        w = jnp.where(s <= r, ws_ref[g], 0.0).astype(BF16)
        bias = bs_ref[:, g:g + 1]
        for c in range(tm // GM_CHUNK):
            rows = slice(c * GM_CHUNK, (c + 1) * GM_CHUNK)
            cols = slice(g * GD, (g + 1) * GD)
            mixed = _dot(w, vgb[rows, cols]) + bias
            yg_scr[rows, cols] = (u[rows, cols] * mixed).astype(BF16)

    a = _dot(hm_ref[...], wpa_ref[...])
    b = _dot(yg_scr[...], wpb_ref[...])
    merged = (jax.nn.sigmoid(ga_ref[...].astype(F32)) * a
              + jax.nn.sigmoid(gb_ref[...].astype(F32)) * b)
    x1 = x_ref[...] + gt1_ref[...] * _dot(merged.astype(BF16), wout_ref[...])
    x1_ref[...] = x1
    h2 = (_rms(x1) * gffn_ref[...]) * (1.0 + sc2_ref[...]) + sh2_ref[...]
    h2_ref[...] = _pack_rows(h2)

    lane = lax.broadcasted_iota(jnp.int32, (tm, LANES), 1)
    lane_f = lane.astype(F32)
    lg = jnp.where(lane < N_EXPERTS, _dot3(h2, wr_ref[...]) + br_ref[...], -jnp.inf)
    choice = jnp.zeros((tm, LANES), F32)
    vals = []
    for k in range(TOP_K):
        mx = jnp.max(lg, axis=1, keepdims=True)
        sel = lane_f == jnp.min(jnp.where(lg == mx, lane_f, float(LANES)), axis=1, keepdims=True)
        choice = jnp.where(sel, k + 1.0, choice)
        vals.append(mx)
        lg = jnp.where(sel, -jnp.inf, lg)
    choice_ref[...] = choice
    ex = [jnp.exp(v - vals[0]) for v in vals]
    denom = sum(ex)
    gates = jnp.zeros((tm, LANES), F32)
    for k in range(TOP_K):
        gates = jnp.where(lane == k, ex[k] / denom, gates)
    gate_ref[...] = gates[:, :TOP_K]

    @pl.when(pl.program_id(0) == 0)
    def _():
        cnt_ref[...] = jnp.zeros_like(cnt_ref)

    cnt_ref[...] += jnp.sum(jnp.where(choice > 0.0, 1.0, 0.0), axis=0, keepdims=True)


def _mix(x, hm, z, mod, per_token, tokens_per_seq, tm, p, keep_v_rows):
    m = x.shape[0]
    zcol = 4
    v_rows = m if keep_v_rows else tm
    v_spec = pl.BlockSpec((tm, D), (lambda i: (i, 0)) if keep_v_rows else (lambda i: (0, 0)))

    def zspec(blk):
        return pl.BlockSpec((tm, D), lambda i: (i, blk))

    def full(shape):
        return pl.BlockSpec(shape, lambda i: (0,) * len(shape))

    row = pl.BlockSpec((tm, D), lambda i: (i, 0))
    return pl.pallas_call(
        _mix_kernel,
        grid=(m // tm,),
        in_specs=[row, row, zspec(zcol), zspec(zcol + 1), zspec(zcol + 2), zspec(zcol + 3),
                  _mod_spec(per_token, tm, tokens_per_seq, 2),
                  _mod_spec(per_token, tm, tokens_per_seq, 3),
                  _mod_spec(per_token, tm, tokens_per_seq, 4),
                  full((1, D)), full((1, D)),
                  full((GROUPS, GM_CHUNK, GM_CHUNK)), full((GM_CHUNK, GROUPS)),
                  full((D, D)), full((D, D)), full((D, D)), full((1, D)),
                  full((D, LANES)), full((1, LANES))],
        out_specs=[row, pl.BlockSpec((tm, PACKED), lambda i: (i, 0)),
                   pl.BlockSpec((tm, LANES), lambda i: (i, 0)), pl.BlockSpec((tm, TOP_K), lambda i: (i, 0)),
                   v_spec, full((1, LANES))],
        out_shape=[jax.ShapeDtypeStruct((m, D), F32),
                   jax.ShapeDtypeStruct((m, PACKED), jnp.int32),
                   jax.ShapeDtypeStruct((m, LANES), F32),
                   jax.ShapeDtypeStruct((m, TOP_K), F32),
                   jax.ShapeDtypeStruct((v_rows, D), F32),
                   jax.ShapeDtypeStruct((1, LANES), F32)],
        scratch_shapes=[pltpu.VMEM((tm, D), BF16)],
        compiler_params=pltpu.CompilerParams(
            dimension_semantics=("arbitrary",), vmem_limit_bytes=VMEM_LIMIT),
        name="mix",
    )(x, hm, z, z, z, z, mod, mod, mod, p["g_sgu"], p["b_sgu"], p["w_s"], p["b_s"],
      p["w_pa"], p["w_pb"], p["w_out"], p["g_ffn"], p["w_router"], p["b_router"])


def _route_kernel(tiles_a, cha_ref, chb_ref, cnta_ref, cntb_ref, dest_ref, ex_ref, base_scr):
    tm = cha_ref.shape[0]
    lane = lax.broadcasted_iota(jnp.int32, (tm, LANES), 1)
    choice = jnp.where(pl.program_id(0) < tiles_a, cha_ref[...], chb_ref[...])
    onehot = jnp.where(choice > 0.0, 1.0, 0.0)

    @pl.when(pl.program_id(0) == 0)
    def _():
        cnt = cnta_ref[...] + cntb_ref[...]
        padded = jnp.floor((cnt + (MOE_ROWS - 1)) * (1.0 / MOE_ROWS)) * MOE_ROWS
        r = lax.broadcasted_iota(jnp.int32, (LANES, LANES), 0)
        s = lax.broadcasted_iota(jnp.int32, (LANES, LANES), 1)
        padded_col = jnp.sum(jnp.where(r == s, padded, 0.0), axis=1, keepdims=True)
        pstart = jnp.sum(jnp.where(r < s, padded_col, 0.0), axis=0, keepdims=True)
        base_scr[...] = pstart
        trow = lax.broadcasted_iota(jnp.int32, ex_ref.shape, 0)
        table = jnp.where(trow == 0, pstart * (1.0 / MOE_ROWS),
                          jnp.where(trow == 1, padded * (1.0 / MOE_ROWS), jnp.where(trow == 2, cnt, 0.0)))
        ex_ref[...] = table.astype(jnp.int32)

    r = lax.broadcasted_iota(jnp.int32, (tm, tm), 0)
    s = lax.broadcasted_iota(jnp.int32, (tm, tm), 1)
    before = _dot((s < r).astype(BF16), onehot.astype(BF16))
    slot = before + base_scr[...]
    dest = jnp.zeros((tm, LANES), F32)
    for k in range(TOP_K):
        d_k = jnp.sum(jnp.where(choice == k + 1.0, slot, 0.0), axis=1, keepdims=True)
        dest = jnp.where(lane == k, d_k, dest)
    dest_ref[...] = dest[:, :TOP_K].astype(jnp.int32)
    base_scr[...] += jnp.sum(onehot, axis=0, keepdims=True)


def _route(choice_a, choice_b, cnt_a, cnt_b):
    tm = min(ROW_TILE, choice_a.shape[0], choice_b.shape[0])
    assert choice_a.shape[0] % tm == 0 and choice_b.shape[0] % tm == 0
    tiles_a, tiles_b = choice_a.shape[0] // tm, choice_b.shape[0] // tm
    n = (tiles_a + tiles_b) * tm
    fixed = lambda i: (0, 0)
    return pl.pallas_call(
        functools.partial(_route_kernel, tiles_a),
        grid=(tiles_a + tiles_b,),
        in_specs=[pl.BlockSpec((tm, LANES), lambda i: (jnp.minimum(i, tiles_a - 1), 0)),
                  pl.BlockSpec((tm, LANES), lambda i: (jnp.maximum(i - tiles_a, 0), 0)),
                  pl.BlockSpec((1, LANES), fixed), pl.BlockSpec((1, LANES), fixed)],
        out_specs=[pl.BlockSpec((tm, TOP_K), lambda i: (i, 0)), pl.BlockSpec((8, LANES), fixed)],
        out_shape=[jax.ShapeDtypeStruct((n, TOP_K), jnp.int32),
                   jax.ShapeDtypeStruct((8, LANES), jnp.int32)],
        scratch_shapes=[pltpu.VMEM((1, LANES), F32)],
        compiler_params=pltpu.CompilerParams(dimension_semantics=("arbitrary",)),
        name="route",
    )(choice_a, choice_b, cnt_a, cnt_b)


def _sc_worker_base(per_worker):
    return (lax.axis_index("s") * SC_CORES + lax.axis_index("c")) * per_worker


def _sc_scatter_rows(rows_a, rows_b, idx_flat, n_out):
    na, nb = rows_a.shape[0], rows_b.shape[0]
    width, dtype = rows_a.shape[1], rows_a.dtype
    n = na + nb
    per_a, per_b = na // SC_WORKERS, nb // SC_WORKERS
    chunk = SC_SCATTER_CHUNK
    n_chunks = per_a // chunk
    assert per_a * SC_WORKERS == na and per_b * SC_WORKERS == nb and per_b % 8 == 0 and per_b <= chunk
    assert n_chunks * chunk == per_a and n_chunks % 2 == 0
    mesh = plsc.VectorSubcoreMesh(core_axis_name="c", subcore_axis_name="s")

    @functools.partial(
        pl.kernel, mesh=mesh,
        out_type=jax.ShapeDtypeStruct((n_out, width), dtype),
        scratch_types=[pltpu.VMEM((chunk,), jnp.int32)] * TOP_K + [pltpu.VMEM((per_b,), jnp.int32)]
                      + [pltpu.VMEM((chunk, width), dtype), pltpu.VMEM((chunk, width), dtype),
                         pltpu.VMEM((per_b, width), dtype)]
                      + [pltpu.SemaphoreType.DMA] * 3,
    )
    def scatter(a_hbm, b_hbm, idx_hbm, out_hbm, i0, i1, i2, i3, ib, rows0, rows1, rowsb, rsem0, rsem1, wsem):
        base = _sc_worker_base(per_a)
        idx_bufs = (i0, i1, i2, i3)
        bufs = ((rows0, rsem0), (rows1, rsem1))

        def off(j):
            return pl.multiple_of(base + j * chunk, 8)

        def read(j, buf):
            rows_v, sem = buf
            return pltpu.make_async_copy(a_hbm.at[pl.ds(off(j), chunk)], rows_v, sem)

        def spread(j, buf):
            rows_v, _ = buf
            read(j, buf).wait()
            for k in range(TOP_K):
                pltpu.sync_copy(idx_hbm.at[pl.ds(pl.multiple_of(k * n + off(j), 8), chunk)], idx_bufs[k])
            for k in range(TOP_K):
                pltpu.make_async_copy(rows_v, out_hbm.at[idx_bufs[k]], wsem).start()
            for k in range(TOP_K):
                pltpu.make_async_copy(rows_v, out_hbm.at[idx_bufs[k]], wsem).wait()

        read(0, bufs[0]).start()

        @pl.loop(0, n_chunks, step=2)
        def _(j):
            read(j + 1, bufs[1]).start()
            spread(j, bufs[0])

            @pl.when(j + 2 < n_chunks)
            def _():
                read(j + 2, bufs[0]).start()
            spread(j + 1, bufs[1])

        off_b = pl.multiple_of(_sc_worker_base(per_b), 8)
        pltpu.sync_copy(b_hbm.at[pl.ds(off_b, per_b)], rowsb)
        for k in range(TOP_K):
            pltpu.sync_copy(idx_hbm.at[pl.ds(pl.multiple_of(k * n + na + off_b, 8), per_b)], ib)
            pltpu.async_copy(rowsb, out_hbm.at[ib], wsem).wait()

    return scatter(rows_a, rows_b, idx_flat)


def _sc_gather_rows(table, idx_flat):
    b = idx_flat.shape[0]
    width, dtype = table.shape[1], table.dtype
    per_worker = b // SC_WORKERS
    chunk = SC_GATHER_CHUNK
    n_chunks = per_worker // chunk
    assert per_worker * SC_WORKERS == b and n_chunks * chunk == per_worker and n_chunks % 2 == 0
    mesh = plsc.VectorSubcoreMesh(core_axis_name="c", subcore_axis_name="s")

    @functools.partial(
        pl.kernel, mesh=mesh,
        out_type=jax.ShapeDtypeStruct((b, width), dtype),
        scratch_types=[pltpu.VMEM((chunk,), jnp.int32), pltpu.VMEM((chunk,), jnp.int32),
                       pltpu.VMEM((chunk, width), dtype), pltpu.VMEM((chunk, width), dtype),
                       pltpu.SemaphoreType.DMA, pltpu.SemaphoreType.DMA],
    )
    def gather(table_hbm, idx_hbm, out_hbm, idx0, idx1, rows0, rows1, sem0, sem1):
        base = _sc_worker_base(per_worker)
        bufs = ((idx0, rows0, sem0), (idx1, rows1, sem1))

        def off(j):
            return pl.multiple_of(base + j * chunk, 8)

        def start(j, buf):
            idx_v, rows_v, sem = buf
            pltpu.sync_copy(idx_hbm.at[pl.ds(off(j), chunk)], idx_v)
            pltpu.make_async_copy(table_hbm.at[idx_v], rows_v, sem).start()

        def finish(j, buf):
            idx_v, rows_v, sem = buf
            pltpu.make_async_copy(table_hbm.at[idx_v], rows_v, sem).wait()
            pltpu.sync_copy(rows_v, out_hbm.at[pl.ds(off(j), chunk)])

        start(0, bufs[0])

        @pl.loop(0, n_chunks, step=2)
        def _(j):
            start(j + 1, bufs[1])
            finish(j, bufs[0])

            @pl.when(j + 2 < n_chunks)
            def _():
                start(j + 2, bufs[0])
            finish(j + 1, bufs[1])

    return gather(table, idx_flat)


def _moe_kernel(first_ref, nblk_ref, cnt_ref, xs_hbm, wgu_ref, bgu_ref, wd_ref, bd_ref, out_hbm,
                xbuf, obuf, wgu_bf, wd_bf, xsem, osem):
    e = pl.program_id(0)
    first, nblk, cnt = first_ref[e], nblk_ref[e], cnt_ref[e]

    def rows(j):
        return pl.ds(pl.multiple_of((first + j) * MOE_ROWS, MOE_ROWS), MOE_ROWS)

    def x_copy(j, slot):
        return pltpu.make_async_copy(xs_hbm.at[rows(j)], xbuf.at[slot], xsem.at[slot])

    def o_copy(j, slot):
        return pltpu.make_async_copy(obuf.at[slot], out_hbm.at[rows(j)], osem.at[slot])

    @pl.when(nblk > 0)
    def _():
        x_copy(0, 0).start(priority=ROW_DMA_PRIORITY)
        wgu_bf[...] = wgu_ref[...].astype(BF16)
        wd_bf[...] = wd_ref[...].astype(BF16)

        def block(j, carry):
            slot = j % 2
            x_copy(j, slot).wait()

            @pl.when(j + 1 < nblk)
            def _():
                x_copy(j + 1, 1 - slot).start(priority=ROW_DMA_PRIORITY)

            @pl.when(j >= 2)
            def _():
                o_copy(j - 2, slot).wait()

            row = lax.broadcasted_iota(jnp.int32, (MOE_ROWS, 1), 0)
            x = _unpack_rows(jnp.where(row < cnt - j * MOE_ROWS, xbuf[slot], 0)).astype(BF16)
            gu = _dot(x, wgu_bf[...]) + bgu_ref[...]
            gate = jnp.minimum(gu[:, :D_FF], SWIGLU_LIMIT)
            up = jnp.clip(gu[:, D_FF:], -SWIGLU_LIMIT, SWIGLU_LIMIT)
            act = gate * jax.nn.sigmoid(SWIGLU_ALPHA * gate) * (up + 1.0)
            obuf[slot] = _pack_rows(_dot(act.astype(BF16), wd_bf[...]) + bd_ref[...])
            o_copy(j, slot).start(priority=ROW_DMA_PRIORITY)
            return carry

        lax.fori_loop(0, nblk, block, 0)

        @pl.when(nblk >= 2)
        def _():
            o_copy(nblk - 2, nblk % 2).wait()
        o_copy(nblk - 1, (nblk - 1) % 2).wait()


def _moe(xs, first_block, n_blocks, counts, w_gu, b_gu, w_down, b_down):
    grid_spec = pltpu.PrefetchScalarGridSpec(
        num_scalar_prefetch=3,
        grid=(N_EXPERTS,),
        in_specs=[pl.BlockSpec(memory_space=pl.ANY),
                  pl.BlockSpec((None, D, 2 * D_FF), lambda e, *_: (e, 0, 0)),
                  pl.BlockSpec((None, 1, 2 * D_FF), lambda e, *_: (e, 0, 0)),
                  pl.BlockSpec((None, D_FF, D), lambda e, *_: (e, 0, 0)),
                  pl.BlockSpec((None, 1, D), lambda e, *_: (e, 0, 0))],
        out_specs=pl.BlockSpec(memory_space=pl.ANY),
        scratch_shapes=[pltpu.VMEM((2, MOE_ROWS, PACKED), jnp.int32), pltpu.VMEM((2, MOE_ROWS, PACKED), jnp.int32),
                        pltpu.VMEM((D, 2 * D_FF), BF16), pltpu.VMEM((D_FF, D), BF16),
                        pltpu.SemaphoreType.DMA((2,)), pltpu.SemaphoreType.DMA((2,))],
    )
    return pl.pallas_call(
        _moe_kernel,
        grid_spec=grid_spec,
        out_shape=jax.ShapeDtypeStruct(xs.shape, jnp.int32),
        compiler_params=pltpu.CompilerParams(
            dimension_semantics=("arbitrary",), vmem_limit_bytes=MOE_VMEM_LIMIT),
        name="moe",
    )(first_block, n_blocks, counts, xs, w_gu, b_gu.reshape(N_EXPERTS, 1, 2 * D_FF), w_down,
      b_down.reshape(N_EXPERTS, 1, D))


def _final_kernel(x1_ref, yk_ref, gate_ref, gt2_ref, g_ref, o_ref):
    y2 = gate_ref[:, 0:1] * _unpack_rows(yk_ref[0])
    for k in range(1, TOP_K):
        y2 = y2 + gate_ref[:, k:k + 1] * _unpack_rows(yk_ref[k])
    o_ref[...] = _rms(x1_ref[...] + gt2_ref[...] * y2) * g_ref[...]


def _final(x1, yk, gates, row_off, mod, per_token, tokens_per_seq, g_final):
    m = x1.shape[0]
    tm = min(ROW_TILE, m)
    assert m % tm == 0 and row_off % tm == 0 and (per_token or tokens_per_seq % tm == 0)
    off = row_off // tm
    return pl.pallas_call(
        _final_kernel,
        grid=(m // tm,),
        in_specs=[pl.BlockSpec((tm, D), lambda i: (i, 0)),
                  pl.BlockSpec((TOP_K, tm, PACKED), lambda i: (0, i + off, 0)),
                  pl.BlockSpec((tm, TOP_K), lambda i: (i, 0)),
                  _mod_spec(per_token, tm, tokens_per_seq, 5),
                  pl.BlockSpec((1, D), lambda i: (0, 0))],
        out_specs=pl.BlockSpec((tm, D), lambda i: (i, 0)),
        out_shape=jax.ShapeDtypeStruct((m, D), F32),
        compiler_params=pltpu.CompilerParams(
            dimension_semantics=("arbitrary",), vmem_limit_bytes=VMEM_LIMIT),
        name="final",
    )(x1, yk, gates, mod, g_final.reshape(1, D))


def kernel(x_prompt, x_sample, state_C, state_n, state_m, c_prompt, c_sample, w_ada, b_ada, g_mix, w_in,
           b_if, g_head, g_sgu, b_sgu, w_s, b_s, w_pa, w_pb, w_out, g_ffn, w_router, b_router, w_gu, b_gu,
           w_down, b_down, g_final):
    depth = w_ada.shape[0]
    assert depth == 1
    bp, tp, _ = x_prompt.shape
    bs, ts, _ = x_sample.shape
    mp, ms = bp * tp, bs * ts
    assert tp % ML_CHUNK == 0 and ts <= ML_CHUNK and GM_CHUNK % ts == 0

    w_in0 = w_in[0]
    nqkvo = 4 * D
    w_qkvo = w_in0[:, :nqkvo].astype(BF16)
    w_gate = w_in0[:, nqkvo + 2 * HEADS:].astype(BF16)
    w_if = jnp.pad(w_in0[:, nqkvo:nqkvo + 2 * HEADS], ((0, 0), (0, LANES - 2 * HEADS)))
    b_if_p = jnp.pad(b_if[0], (0, LANES - 2 * HEADS)).reshape(1, LANES)
    reps = GM_CHUNK // ts
    eye_r = jnp.eye(reps, dtype=F32)
    w_s_sample = jnp.einsum("ab,gts->gatbs", eye_r, w_s[0][:, :ts, :ts]).reshape(GROUPS, GM_CHUNK, GM_CHUNK)
    b_s_prompt = b_s[0].T
    b_s_sample = jnp.tile(b_s[0][:, :ts].T, (reps, 1))
    mix_p = {
        "g_sgu": g_sgu[0].reshape(1, D), "b_sgu": b_sgu[0].reshape(1, D),
        "w_pa": w_pa[0].astype(BF16), "w_pb": w_pb[0].astype(BF16), "w_out": w_out[0].astype(BF16),
        "g_ffn": g_ffn[0].reshape(1, D),
        "w_router": jnp.pad(w_router[0], ((0, 0), (0, LANES - N_EXPERTS))),
        "b_router": jnp.pad(b_router[0], (0, LANES - N_EXPERTS)).reshape(1, LANES),
    }
    mix_prompt = dict(mix_p, w_s=w_s[0], b_s=b_s_prompt)
    mix_sample = dict(mix_p, w_s=w_s_sample, b_s=b_s_sample)

    mod = _ada(jnp.concatenate([jnp.repeat(c_sample, ts, axis=0), c_prompt], axis=0), w_ada[0], b_ada[0])
    mod_s = mod
    mod_p = mod[ms:].reshape(bp, 1, N_MOD * D)

    xp = x_prompt.reshape(mp, D)
    xs = x_sample.reshape(ms, D)
    z_p, _, gt_p = _in_proj(xp, mod_p, False, tp, g_mix[0], w_qkvo, w_gate, w_if, b_if_p, 512, BF16)
    z_s, gc_s, _ = _in_proj(xs, mod_s, True, ts, g_mix[0], w_qkvo, w_gate, w_if, b_if_p, 128, F32)

    hm_p, C_p, n_p, m_p = _mlstm_prompt(z_p, gt_p, g_head[0], bp, tp)
    hm_p = hm_p.reshape(mp, D)
    m0_tok = jnp.repeat(state_m[0], ts, axis=0)
    hm_s, C_s, n_s, m_s = _mlstm_sample(z_s, gc_s, m0_tok, state_C[0], state_n[0], g_head[0], bs, ts)

    x1_p, h2_p, ch_p, gates_p, _, cnt_p = _mix(xp, hm_p, z_p, mod_p, False, tp, 256, mix_prompt, False)
    x1_s, h2_s, ch_s, gates_s, vg_s, cnt_s = _mix(xs, hm_s, z_s, mod_s, True, ts, 128, mix_sample, True)

    n_tok = mp + ms
    n_blocks = -(-(n_tok * TOP_K + N_EXPERTS * (MOE_ROWS - 1)) // MOE_ROWS)
    dest, ex_table = _route(ch_p, ch_s, cnt_p, cnt_s)
    dest_kmajor = dest.T.reshape(TOP_K * n_tok)
    xslots = _sc_scatter_rows(h2_p, h2_s, dest_kmajor, n_blocks * MOE_ROWS)
    yb = _moe(xslots, ex_table[0, :N_EXPERTS], ex_table[1, :N_EXPERTS], ex_table[2, :N_EXPERTS],
              w_gu[0], b_gu[0], w_down[0], b_down[0])
    yk = _sc_gather_rows(yb, dest_kmajor).reshape(TOP_K, n_tok, PACKED)

    y_p = _final(x1_p, yk, gates_p, 0, mod_p, False, tp, g_final)
    y_s = _final(x1_s, yk, gates_s, mp, mod_s, True, ts, g_final)

    return (y_p.reshape(bp, tp, D), y_s.reshape(bs, ts, D),
            C_p[None], n_p.reshape(1, bp, HEADS, HD), m_p[:, :, 0, 0][None],
            C_s[None], n_s.reshape(1, bs, HEADS, HD),
            m_s.reshape(bs, ts, HEADS, LANES)[:, 0, :, 0][None],
            vg_s.reshape(1, bs, ts, D))
```

```python
import functools

import jax
import jax.numpy as jnp
from jax import lax
from jax.experimental import pallas as pl
from jax.experimental.pallas import tpu as pltpu
from jax.experimental.pallas import tpu_sc as plsc

F32 = jnp.float32
BF16 = jnp.bfloat16

D = 1024
HEADS = 4
HD = D // HEADS
ML_CHUNK = 512
GROUPS = 4
GD = D // GROUPS
GM_CHUNK = 128
N_EXPERTS = 32
TOP_K = 4
D_FF = D
SWIGLU_LIMIT = 7.0
SWIGLU_ALPHA = 1.702
NORM_EPS = 1e-6
N_MOD = 6
PACKED = D // 2
LANES = 128
SC_CORES = 2
SC_SUBCORES = 16
SC_WORKERS = SC_CORES * SC_SUBCORES
IN_COLS = 2048
ROW_TILE = 512
MOE_ROWS = 256
ROW_DMA_PRIORITY = 1
MLSTM_SEQS = 1
SAMPLE_SEQS = 8
SC_SCATTER_CHUNK = 64
SC_GATHER_CHUNK = 96
VMEM_LIMIT = 48 * 1024 * 1024
MOE_VMEM_LIMIT = 56 * 1024 * 1024


def _dot(a, b):
    return jnp.dot(a, b, preferred_element_type=F32)


def _dot_nt(a, b):
    return lax.dot_general(a, b, (((1,), (1,)), ((), ())), preferred_element_type=F32)


def _dot_tn(a, b):
    return lax.dot_general(a, b, (((0,), (0,)), ((), ())), preferred_element_type=F32)


def _split_bf16(a):
    hi = a.astype(BF16)
    lo = (a - hi.astype(F32)).astype(BF16)
    return hi, lo


def _dot3(a, b):
    ah, al = _split_bf16(a)
    bh, bl = _split_bf16(b)
    return _dot(ah, bh) + (_dot(ah, bl) + _dot(al, bh))


def _log_sigmoid(x):
    return jnp.minimum(x, 0.0) - jnp.log1p(jnp.exp(-jnp.abs(x)))


def _rms(x):
    return x * lax.rsqrt(jnp.mean(x * x, axis=-1, keepdims=True) + NORM_EPS)


def _pack_rows(x):
    bits = lax.bitcast_convert_type(x.astype(BF16).astype(F32), jnp.uint32)
    word = (bits[:, :PACKED] & jnp.uint32(0xFFFF0000)) | (bits[:, PACKED:] >> 16)
    return lax.bitcast_convert_type(word, jnp.int32)


def _unpack_rows(w):
    bits = lax.bitcast_convert_type(w, jnp.uint32)
    left = lax.bitcast_convert_type(bits & jnp.uint32(0xFFFF0000), F32)
    right = lax.bitcast_convert_type(bits << 16, F32)
    return jnp.concatenate([left, right], axis=1)


def _mod_spec(per_token, tm, tokens_per_seq, col):
    if per_token:
        return pl.BlockSpec((tm, D), lambda i, *_: (i, col))
    return pl.BlockSpec((None, 1, D), lambda i, *_: ((i * tm) // tokens_per_seq, 0, col))


def _ada_kernel(c_ref, w_ref, b_ref, o_ref):
    c = c_ref[...]
    s = (c * jax.nn.sigmoid(c)).astype(BF16)
    o_ref[...] = _dot(s, w_ref[...].astype(BF16)) + b_ref[...]


def _ada(c, w, b):
    m, n = c.shape[0], w.shape[1]
    tn = 512
    return pl.pallas_call(
        _ada_kernel,
        grid=(n // tn,),
        in_specs=[pl.BlockSpec((m, D), lambda j: (0, 0)),
                  pl.BlockSpec((D, tn), lambda j: (0, j)),
                  pl.BlockSpec((1, tn), lambda j: (0, j))],
        out_specs=pl.BlockSpec((m, tn), lambda j: (0, j)),
        out_shape=jax.ShapeDtypeStruct((m, n), F32),
        name="ada",
    )(c, w, b.reshape(1, n))


def _in_kernel(x_ref, g_ref, sh_ref, sc_ref, wa_ref, wb_ref, wif_ref, bif_ref, z_ref, gc_ref, gt_ref):
    h = (_rms(x_ref[...]) * g_ref[...]) * (1.0 + sc_ref[...]) + sh_ref[...]
    hb = h.astype(BF16)
    gates = _dot3(h, wif_ref[...]) + bif_ref[...]
    gc_ref[...] = gates
    gt_ref[...] = gates.T[:2 * HEADS, :]
    half = wa_ref.shape[1]
    for w_ref, col0 in ((wa_ref, 0), (wb_ref, half)):
        for c in range(half // IN_COLS):
            cols = slice(c * IN_COLS, (c + 1) * IN_COLS)
            z_ref[:, col0 + c * IN_COLS:col0 + (c + 1) * IN_COLS] = _dot(hb, w_ref[:, cols]).astype(z_ref.dtype)


def _in_proj(x, mod, per_token, tokens_per_seq, g_mix, w_a, w_b, w_if, b_if, tm, z_dtype):
    m = x.shape[0]
    half = w_a.shape[1]
    gt_len = tokens_per_seq if tokens_per_seq % tm == 0 else m
    gt_tiles = gt_len // tm
    resident = functools.partial(pl.BlockSpec, index_map=lambda i: (0, 0), pipeline_mode=pl.Buffered(1))
    return pl.pallas_call(
        _in_kernel,
        grid=(m // tm,),
        in_specs=[pl.BlockSpec((tm, D), lambda i: (i, 0)),
                  pl.BlockSpec((1, D), lambda i: (0, 0)),
                  _mod_spec(per_token, tm, tokens_per_seq, 0),
                  _mod_spec(per_token, tm, tokens_per_seq, 1),
                  resident((D, half)), resident((D, half)),
                  pl.BlockSpec((D, LANES), lambda i: (0, 0)),
                  pl.BlockSpec((1, LANES), lambda i: (0, 0))],
        out_specs=[pl.BlockSpec((tm, 2 * half), lambda i: (i, 0)),
                   pl.BlockSpec((tm, LANES), lambda i: (i, 0)),
                   pl.BlockSpec((None, 2 * HEADS, tm), lambda i: (i // gt_tiles, 0, i % gt_tiles))],
        out_shape=[jax.ShapeDtypeStruct((m, 2 * half), z_dtype),
                   jax.ShapeDtypeStruct((m, LANES), F32),
                   jax.ShapeDtypeStruct((m // gt_len, 2 * HEADS, gt_len), F32)],
        compiler_params=pltpu.CompilerParams(
            dimension_semantics=("arbitrary",), vmem_limit_bytes=MOE_VMEM_LIMIT),
        name="in_proj",
    )(x, g_mix.reshape(1, D), mod, mod, w_a, w_b, w_if, b_if)


def _mlstm_prompt_kernel(q_ref, k_ref, v_ref, o_ref, gt_ref, gh_ref, hm_ref, C_ref, n_ref, m_ref):
    nseq, L = q_ref.shape[0], q_ref.shape[1]

    @pl.when(pl.program_id(1) == 0)
    def _():
        C_ref[...] = jnp.zeros_like(C_ref)
        n_ref[...] = jnp.zeros_like(n_ref)
        m_ref[...] = jnp.zeros_like(m_ref)

    r = lax.broadcasted_iota(jnp.int32, (L, L), 0)
    s = lax.broadcasted_iota(jnp.int32, (L, L), 1)
    eye = r == s
    causal = s <= r

    def to_col(x_row):
        return jnp.sum(jnp.where(eye, x_row, 0.0), axis=1, keepdims=True)

    for b in range(nseq):
        gates = gt_ref[b]
        for h in range(HEADS):
            cols = slice(h * HD, (h + 1) * HD)
            ig_row = gates[h:h + 1, :]
            lf_row = _log_sigmoid(gates[HEADS + h:HEADS + h + 1, :])
            lf_col = to_col(lf_row)
            b_row = jnp.sum(jnp.where(r <= s, lf_col, 0.0), axis=0, keepdims=True)
            b_col = to_col(b_row)
            m_prev = m_ref[b, h][:, :1]

            logD = jnp.where(causal, b_col - b_row + ig_row, -jnp.inf)
            inter = b_col + m_prev
            mt = jnp.maximum(jnp.max(logD, axis=1, keepdims=True), inter)
            q = q_ref[b, :, cols]
            ks = k_ref[b, :, cols] * (HD ** -0.5)
            v = v_ref[b, :, cols]
            S = _dot_nt(q, ks) * jnp.exp(logD - mt)
            w_int = jnp.exp(inter - mt)
            Cmat = C_ref[b, h]
            nvec = n_ref[b, h]
            num = _dot(S.astype(BF16), v) + w_int * _dot_nt(q, Cmat.astype(BF16))
            nq = jnp.sum(q.astype(F32) * nvec, axis=1, keepdims=True)
            den = jnp.sum(S, axis=1, keepdims=True) + w_int * nq
            hh = num / jnp.maximum(jnp.abs(den), jnp.exp(-mt))
            hg = jax.nn.sigmoid(o_ref[b, :, cols].astype(F32)) * hh
            hm_ref[b, :, cols] = (_rms(hg) * gh_ref[h]).astype(hm_ref.dtype)

            bL = b_row[:, L - 1:L]
            g_row = bL - b_row + ig_row
            m_new = jnp.maximum(bL + m_prev, jnp.max(g_row, axis=1, keepdims=True))
            w_old = jnp.exp(bL + m_prev - m_new)
            kw = ks.astype(F32) * to_col(jnp.exp(g_row - m_new))
            C_ref[b, h] = w_old * Cmat + _dot_tn(v, kw.astype(BF16))
            n_ref[b, h] = w_old * nvec + jnp.sum(kw, axis=0, keepdims=True)
            m_ref[b, h] = jnp.broadcast_to(m_new, (1, LANES))


def _mlstm_prompt(z, gates_t, g_head, batch, seq):
    nc = seq // ML_CHUNK
    nb = MLSTM_SEQS
    z3 = z.reshape(batch, seq, z.shape[1])

    def zspec(col):
        return pl.BlockSpec((nb, ML_CHUNK, D), lambda b, c: (b, c, col))

    def state(last):
        return pl.BlockSpec((nb, HEADS) + last, lambda b, c: (b, 0, 0, 0))

    return pl.pallas_call(
        _mlstm_prompt_kernel,
        grid=(batch // nb, nc),
        in_specs=[zspec(0), zspec(1), zspec(2), zspec(3),
                  pl.BlockSpec((nb, 2 * HEADS, ML_CHUNK), lambda b, c: (b, 0, c)),
                  pl.BlockSpec((HEADS, 1, HD), lambda b, c: (0, 0, 0))],
        out_specs=[pl.BlockSpec((nb, ML_CHUNK, D), lambda b, c: (b, c, 0)),
                   state((HD, HD)), state((1, HD)), state((1, LANES))],
        out_shape=[jax.ShapeDtypeStruct((batch, seq, D), BF16),
                   jax.ShapeDtypeStruct((batch, HEADS, HD, HD), F32),
                   jax.ShapeDtypeStruct((batch, HEADS, 1, HD), F32),
                   jax.ShapeDtypeStruct((batch, HEADS, 1, LANES), F32)],
        compiler_params=pltpu.CompilerParams(dimension_semantics=("arbitrary", "arbitrary")),
        name="mlstm_prompt",
    )(z3, z3, z3, z3, gates_t, g_head.reshape(HEADS, 1, HD))


def _mlstm_sample_kernel(seq_len, q_ref, k_ref, v_ref, o_ref, gc_ref, m0_ref, C0_ref, n0_ref, gh_ref,
                         hm_ref, C_ref, n_ref, m_ref):
    R = q_ref.shape[0]
    nseq = R // seq_len
    r = lax.broadcasted_iota(jnp.int32, (R, R), 0)
    s = lax.broadcasted_iota(jnp.int32, (R, R), 1)
    rseq = lax.broadcasted_iota(jnp.int32, (R, 1), 0) // seq_len
    eye = r == s
    same = (r // seq_len) == (s // seq_len)
    causal = same & (s <= r)

    def to_row(x_col):
        return jnp.sum(jnp.where(eye, x_col, 0.0), axis=0, keepdims=True)

    gc = gc_ref[...]
    for h in range(HEADS):
        cols = slice(h * HD, (h + 1) * HD)
        ig_col = gc[:, h:h + 1]
        lf_col = _log_sigmoid(gc[:, HEADS + h:HEADS + h + 1])
        lf_row = to_row(lf_col)
        b_col = jnp.sum(jnp.where(causal, lf_row, 0.0), axis=1, keepdims=True)
        bL_col = jnp.sum(jnp.where(same, lf_row, 0.0), axis=1, keepdims=True)
        b_row = to_row(b_col)
        ig_row = to_row(ig_col)
        m0_col = m0_ref[:, h:h + 1]

        logD = jnp.where(causal, b_col - b_row + ig_row, -jnp.inf)
        inter = b_col + m0_col
        mt = jnp.maximum(jnp.max(logD, axis=1, keepdims=True), inter)
        qf = q_ref[:, cols]
        q = qf.astype(BF16)
        ksf = k_ref[:, cols] * (HD ** -0.5)
        v = v_ref[:, cols].astype(BF16)
        S = _dot_nt(q, ksf.astype(BF16)) * jnp.exp(logD - mt)
        w_int = jnp.exp(inter - mt)

        Cq = jnp.zeros((R, HD), F32)
        nq = jnp.zeros((R, 1), F32)
        for g in range(nseq):
            Cq = jnp.where(rseq == g, _dot_nt(q, C0_ref[g, h].astype(BF16)), Cq)
            nq = jnp.where(rseq == g, jnp.sum(qf * n0_ref[g, h], axis=1, keepdims=True), nq)
        num = _dot(S.astype(BF16), v) + w_int * Cq
        den = jnp.sum(S, axis=1, keepdims=True) + w_int * nq
        hh = num / jnp.maximum(jnp.abs(den), jnp.exp(-mt))
        hg = jax.nn.sigmoid(o_ref[:, cols]) * hh
        hm_ref[:, cols] = (_rms(hg) * gh_ref[h]).astype(hm_ref.dtype)

        g_col = bL_col - b_col + ig_col
        gmax_col = jnp.max(jnp.where(same, to_row(g_col), -jnp.inf), axis=1, keepdims=True)
        m_new_col = jnp.maximum(bL_col + m0_col, gmax_col)
        w_old_col = jnp.exp(bL_col + m0_col - m_new_col)
        kw = ksf * jnp.exp(g_col - m_new_col)
        for g in range(nseq):
            kw_g = jnp.where(rseq == g, kw, 0.0)
            w_old = w_old_col[g * seq_len:g * seq_len + 1, :]
            C_ref[g, h] = w_old * C0_ref[g, h] + _dot_tn(v, kw_g.astype(BF16))
            n_ref[g, h] = w_old * n0_ref[g, h] + jnp.sum(kw_g, axis=0, keepdims=True)
        m_ref[:, h * LANES:(h + 1) * LANES] = jnp.broadcast_to(m_new_col, (R, LANES))


def _mlstm_sample(z, gates_c, m0_tok, C0, n0, g_head, batch, seq):
    rows = SAMPLE_SEQS * seq
    m = batch * seq

    def zspec(col):
        return pl.BlockSpec((rows, D), lambda i: (i, col))

    state_c = pl.BlockSpec((SAMPLE_SEQS, HEADS, HD, HD), lambda i: (i, 0, 0, 0))
    state_n = pl.BlockSpec((SAMPLE_SEQS, HEADS, 1, HD), lambda i: (i, 0, 0, 0))
    return pl.pallas_call(
        functools.partial(_mlstm_sample_kernel, seq),
        grid=(batch // SAMPLE_SEQS,),
        in_specs=[zspec(0), zspec(1), zspec(2), zspec(3),
                  pl.BlockSpec((rows, LANES), lambda i: (i, 0)),
                  pl.BlockSpec((rows, HEADS), lambda i: (i, 0)),
                  state_c, state_n,
                  pl.BlockSpec((HEADS, 1, HD), lambda i: (0, 0, 0))],
        out_specs=[pl.BlockSpec((rows, D), lambda i: (i, 0)),
                   state_c, state_n,
                   pl.BlockSpec((rows, HEADS * LANES), lambda i: (i, 0))],
        out_shape=[jax.ShapeDtypeStruct((m, D), BF16),
                   jax.ShapeDtypeStruct((batch, HEADS, HD, HD), F32),
                   jax.ShapeDtypeStruct((batch, HEADS, 1, HD), F32),
                   jax.ShapeDtypeStruct((m, HEADS * LANES), F32)],
        compiler_params=pltpu.CompilerParams(
            dimension_semantics=("arbitrary",), vmem_limit_bytes=VMEM_LIMIT),
        name="mlstm_sample",
    )(z, z, z, z, gates_c, m0_tok, C0, n0.reshape(batch, HEADS, 1, HD), g_head.reshape(HEADS, 1, HD))


def _mix_kernel(x_ref, hm_ref, u_ref, v_ref, ga_ref, gb_ref, gt1_ref, sh2_ref, sc2_ref,
                gsgu_ref, bsgu_ref, ws_ref, bs_ref, wpa_ref, wpb_ref, wout_ref, gffn_ref,
                wr_ref, br_ref, x1_ref, h2_ref, choice_ref, gate_ref, vg_ref, cnt_ref, yg_scr):
    tm = x_ref.shape[0]
    u = jax.nn.gelu(u_ref[...].astype(F32))
    vv = jax.nn.gelu(v_ref[...].astype(F32))
    mu = jnp.mean(vv, axis=-1, keepdims=True)
    var = jnp.mean(jnp.square(vv - mu), axis=-1, keepdims=True)
    vg = (vv - mu) * lax.rsqrt(var + NORM_EPS) * gsgu_ref[...] + bsgu_ref[...]
    vg_ref[...] = vg
    vgb = vg.astype(BF16)

    r = lax.broadcasted_iota(jnp.int32, (GM_CHUNK, GM_CHUNK), 0)
    s = lax.broadcasted_iota(jnp.int32, (GM_CHUNK, GM_CHUNK), 1)
    for g in range(GROUPS):
        w = jnp.where(s <= r, ws_ref[g], 0.0).astype(BF16)
        bias = bs_ref[:, g:g + 1]
        for c in range(tm // GM_CHUNK):
            rows = slice(c * GM_CHUNK, (c + 1) * GM_CHUNK)
            cols = slice(g * GD, (g + 1) * GD)
            mixed = _dot(w, vgb[rows, cols]) + bias
            yg_scr[rows, cols] = (u[rows, cols] * mixed).astype(BF16)

    a = _dot(hm_ref[...], wpa_ref[...])
    b = _dot(yg_scr[...], wpb_ref[...])
    merged = (jax.nn.sigmoid(ga_ref[...].astype(F32)) * a
              + jax.nn.sigmoid(gb_ref[...].astype(F32)) * b)
    x1 = x_ref[...] + gt1_ref[...] * _dot(merged.astype(BF16), wout_ref[...])
    x1_ref[...] = x1
    h2 = (_rms(x1) * gffn_ref[...]) * (1.0 + sc2_ref[...]) + sh2_ref[...]
    h2_ref[...] = _pack_rows(h2)

    lane = lax.broadcasted_iota(jnp.int32, (tm, LANES), 1)
    lane_f = lane.astype(F32)
    lg = jnp.where(lane < N_EXPERTS, _dot3(h2, wr_ref[...]) + br_ref[...], -jnp.inf)
    choice = jnp.zeros((tm, LANES), F32)
    vals = []
    for k in range(TOP_K):
        mx = jnp.max(lg, axis=1, keepdims=True)
        sel = lane_f == jnp.min(jnp.where(lg == mx, lane_f, float(LANES)), axis=1, keepdims=True)
        choice = jnp.where(sel, k + 1.0, choice)
        vals.append(mx)
        lg = jnp.where(sel, -jnp.inf, lg)
    choice_ref[...] = choice
    ex = [jnp.exp(v - vals[0]) for v in vals]
    denom = sum(ex)
    gates = jnp.zeros((tm, LANES), F32)
    for k in range(TOP_K):
        gates = jnp.where(lane == k, ex[k] / denom, gates)
    gate_ref[...] = gates[:, :TOP_K]

    @pl.when(pl.program_id(0) == 0)
    def _():
        cnt_ref[...] = jnp.zeros_like(cnt_ref)

    cnt_ref[...] += jnp.sum(jnp.where(choice > 0.0, 1.0, 0.0), axis=0, keepdims=True)


def _mix(x, hm, z, mod, per_token, tokens_per_seq, tm, p, keep_v_rows):
    m = x.shape[0]
    zcol = 4
    v_rows = m if keep_v_rows else tm
    v_spec = pl.BlockSpec((tm, D), (lambda i: (i, 0)) if keep_v_rows else (lambda i: (0, 0)))

    def zspec(blk):
        return pl.BlockSpec((tm, D), lambda i: (i, blk))

    def full(shape):
        return pl.BlockSpec(shape, lambda i: (0,) * len(shape))

    row = pl.BlockSpec((tm, D), lambda i: (i, 0))
    return pl.pallas_call(
        _mix_kernel,
        grid=(m // tm,),
        in_specs=[row, row, zspec(zcol), zspec(zcol + 1), zspec(zcol + 2), zspec(zcol + 3),
                  _mod_spec(per_token, tm, tokens_per_seq, 2),
                  _mod_spec(per_token, tm, tokens_per_seq, 3),
                  _mod_spec(per_token, tm, tokens_per_seq, 4),
                  full((1, D)), full((1, D)),
                  full((GROUPS, GM_CHUNK, GM_CHUNK)), full((GM_CHUNK, GROUPS)),
                  full((D, D)), full((D, D)), full((D, D)), full((1, D)),
                  full((D, LANES)), full((1, LANES))],
        out_specs=[row, pl.BlockSpec((tm, PACKED), lambda i: (i, 0)),
                   pl.BlockSpec((tm, LANES), lambda i: (i, 0)), pl.BlockSpec((tm, TOP_K), lambda i: (i, 0)),
                   v_spec, full((1, LANES))],
        out_shape=[jax.ShapeDtypeStruct((m, D), F32),
                   jax.ShapeDtypeStruct((m, PACKED), jnp.int32),
                   jax.ShapeDtypeStruct((m, LANES), F32),
                   jax.ShapeDtypeStruct((m, TOP_K), F32),
                   jax.ShapeDtypeStruct((v_rows, D), F32),
                   jax.ShapeDtypeStruct((1, LANES), F32)],
        scratch_shapes=[pltpu.VMEM((tm, D), BF16)],
        compiler_params=pltpu.CompilerParams(
            dimension_semantics=("arbitrary",), vmem_limit_bytes=VMEM_LIMIT),
        name="mix",
    )(x, hm, z, z, z, z, mod, mod, mod, p["g_sgu"], p["b_sgu"], p["w_s"], p["b_s"],
      p["w_pa"], p["w_pb"], p["w_out"], p["g_ffn"], p["w_router"], p["b_router"])


def _route_kernel(tiles_a, cha_ref, chb_ref, cnta_ref, cntb_ref, dest_ref, ex_ref, base_scr):
    tm = cha_ref.shape[0]
    lane = lax.broadcasted_iota(jnp.int32, (tm, LANES), 1)
    choice = jnp.where(pl.program_id(0) < tiles_a, cha_ref[...], chb_ref[...])
    onehot = jnp.where(choice > 0.0, 1.0, 0.0)

    @pl.when(pl.program_id(0) == 0)
    def _():
        cnt = cnta_ref[...] + cntb_ref[...]
        padded = jnp.floor((cnt + (MOE_ROWS - 1)) * (1.0 / MOE_ROWS)) * MOE_ROWS
        r = lax.broadcasted_iota(jnp.int32, (LANES, LANES), 0)
        s = lax.broadcasted_iota(jnp.int32, (LANES, LANES), 1)
        padded_col = jnp.sum(jnp.where(r == s, padded, 0.0), axis=1, keepdims=True)
        pstart = jnp.sum(jnp.where(r < s, padded_col, 0.0), axis=0, keepdims=True)
        base_scr[...] = pstart
        trow = lax.broadcasted_iota(jnp.int32, ex_ref.shape, 0)
        table = jnp.where(trow == 0, pstart * (1.0 / MOE_ROWS),
                          jnp.where(trow == 1, padded * (1.0 / MOE_ROWS), jnp.where(trow == 2, cnt, 0.0)))
        ex_ref[...] = table.astype(jnp.int32)

    r = lax.broadcasted_iota(jnp.int32, (tm, tm), 0)
    s = lax.broadcasted_iota(jnp.int32, (tm, tm), 1)
    before = _dot((s < r).astype(BF16), onehot.astype(BF16))
    slot = before + base_scr[...]
    dest = jnp.zeros((tm, LANES), F32)
    for k in range(TOP_K):
        d_k = jnp.sum(jnp.where(choice == k + 1.0, slot, 0.0), axis=1, keepdims=True)
        dest = jnp.where(lane == k, d_k, dest)
    dest_ref[...] = dest[:, :TOP_K].astype(jnp.int32)
    base_scr[...] += jnp.sum(onehot, axis=0, keepdims=True)


def _route(choice_a, choice_b, cnt_a, cnt_b):
    tm = min(ROW_TILE, choice_a.shape[0], choice_b.shape[0])
    assert choice_a.shape[0] % tm == 0 and choice_b.shape[0] % tm == 0
    tiles_a, tiles_b = choice_a.shape[0] // tm, choice_b.shape[0] // tm
    n = (tiles_a + tiles_b) * tm
    fixed = lambda i: (0, 0)
    return pl.pallas_call(
        functools.partial(_route_kernel, tiles_a),
        grid=(tiles_a + tiles_b,),
        in_specs=[pl.BlockSpec((tm, LANES), lambda i: (jnp.minimum(i, tiles_a - 1), 0)),
                  pl.BlockSpec((tm, LANES), lambda i: (jnp.maximum(i - tiles_a, 0), 0)),
                  pl.BlockSpec((1, LANES), fixed), pl.BlockSpec((1, LANES), fixed)],
        out_specs=[pl.BlockSpec((tm, TOP_K), lambda i: (i, 0)), pl.BlockSpec((8, LANES), fixed)],
        out_shape=[jax.ShapeDtypeStruct((n, TOP_K), jnp.int32),
                   jax.ShapeDtypeStruct((8, LANES), jnp.int32)],
        scratch_shapes=[pltpu.VMEM((1, LANES), F32)],
        compiler_params=pltpu.CompilerParams(dimension_semantics=("arbitrary",)),
        name="route",
    )(choice_a, choice_b, cnt_a, cnt_b)


def _sc_worker_base(per_worker):
    return (lax.axis_index("s") * SC_CORES + lax.axis_index("c")) * per_worker


def _sc_scatter_rows(rows_a, rows_b, idx_flat, n_out):
    na, nb = rows_a.shape[0], rows_b.shape[0]
    width, dtype = rows_a.shape[1], rows_a.dtype
    n = na + nb
    per_a, per_b = na // SC_WORKERS, nb // SC_WORKERS
    chunk = SC_SCATTER_CHUNK
    n_chunks = per_a // chunk
    assert per_a * SC_WORKERS == na and per_b * SC_WORKERS == nb and per_b % 8 == 0 and per_b <= chunk
    assert n_chunks * chunk == per_a and n_chunks % 2 == 0
    mesh = plsc.VectorSubcoreMesh(core_axis_name="c", subcore_axis_name="s")

    @functools.partial(
        pl.kernel, mesh=mesh,
        out_type=jax.ShapeDtypeStruct((n_out, width), dtype),
        scratch_types=[pltpu.VMEM((chunk,), jnp.int32)] * TOP_K + [pltpu.VMEM((per_b,), jnp.int32)]
                      + [pltpu.VMEM((chunk, width), dtype), pltpu.VMEM((chunk, width), dtype),
                         pltpu.VMEM((per_b, width), dtype)]
                      + [pltpu.SemaphoreType.DMA] * 3,
    )
    def scatter(a_hbm, b_hbm, idx_hbm, out_hbm, i0, i1, i2, i3, ib, rows0, rows1, rowsb, rsem0, rsem1, wsem):
        base = _sc_worker_base(per_a)
        idx_bufs = (i0, i1, i2, i3)
        bufs = ((rows0, rsem0), (rows1, rsem1))

        def off(j):
            return pl.multiple_of(base + j * chunk, 8)

        def read(j, buf):
            rows_v, sem = buf
            return pltpu.make_async_copy(a_hbm.at[pl.ds(off(j), chunk)], rows_v, sem)

        def spread(j, buf):
            rows_v, _ = buf
            read(j, buf).wait()
            for k in range(TOP_K):
                pltpu.sync_copy(idx_hbm.at[pl.ds(pl.multiple_of(k * n + off(j), 8), chunk)], idx_bufs[k])
            for k in range(TOP_K):
                pltpu.make_async_copy(rows_v, out_hbm.at[idx_bufs[k]], wsem).start()
            for k in range(TOP_K):
                pltpu.make_async_copy(rows_v, out_hbm.at[idx_bufs[k]], wsem).wait()

        read(0, bufs[0]).start()

        @pl.loop(0, n_chunks, step=2)
        def _(j):
            read(j + 1, bufs[1]).start()
            spread(j, bufs[0])

            @pl.when(j + 2 < n_chunks)
            def _():
                read(j + 2, bufs[0]).start()
            spread(j + 1, bufs[1])

        off_b = pl.multiple_of(_sc_worker_base(per_b), 8)
        pltpu.sync_copy(b_hbm.at[pl.ds(off_b, per_b)], rowsb)
        for k in range(TOP_K):
            pltpu.sync_copy(idx_hbm.at[pl.ds(pl.multiple_of(k * n + na + off_b, 8), per_b)], ib)
            pltpu.async_copy(rowsb, out_hbm.at[ib], wsem).wait()

    return scatter(rows_a, rows_b, idx_flat)


def _sc_gather_rows(table, idx_flat):
    b = idx_flat.shape[0]
    width, dtype = table.shape[1], table.dtype
    per_worker = b // SC_WORKERS
    chunk = SC_GATHER_CHUNK
    n_chunks = per_worker // chunk
    assert per_worker * SC_WORKERS == b and n_chunks * chunk == per_worker and n_chunks % 2 == 0
    mesh = plsc.VectorSubcoreMesh(core_axis_name="c", subcore_axis_name="s")

    @functools.partial(
        pl.kernel, mesh=mesh,
        out_type=jax.ShapeDtypeStruct((b, width), dtype),
        scratch_types=[pltpu.VMEM((chunk,), jnp.int32), pltpu.VMEM((chunk,), jnp.int32),
                       pltpu.VMEM((chunk, width), dtype), pltpu.VMEM((chunk, width), dtype),
                       pltpu.SemaphoreType.DMA, pltpu.SemaphoreType.DMA],
    )
    def gather(table_hbm, idx_hbm, out_hbm, idx0, idx1, rows0, rows1, sem0, sem1):
        base = _sc_worker_base(per_worker)
        bufs = ((idx0, rows0, sem0), (idx1, rows1, sem1))

        def off(j):
            return pl.multiple_of(base + j * chunk, 8)

        def start(j, buf):
            idx_v, rows_v, sem = buf
            pltpu.sync_copy(idx_hbm.at[pl.ds(off(j), chunk)], idx_v)
            pltpu.make_async_copy(table_hbm.at[idx_v], rows_v, sem).start()

        def finish(j, buf):
            idx_v, rows_v, sem = buf
            pltpu.make_async_copy(table_hbm.at[idx_v], rows_v, sem).wait()
            pltpu.sync_copy(rows_v, out_hbm.at[pl.ds(off(j), chunk)])

        start(0, bufs[0])

        @pl.loop(0, n_chunks, step=2)
        def _(j):
            start(j + 1, bufs[1])
            finish(j, bufs[0])

            @pl.when(j + 2 < n_chunks)
            def _():
                start(j + 2, bufs[0])
            finish(j + 1, bufs[1])

    return gather(table, idx_flat)


def _moe_kernel(first_ref, nblk_ref, cnt_ref, xs_hbm, wgu_ref, bgu_ref, wd_ref, bd_ref, out_hbm,
                xbuf, obuf, wgu_bf, wd_bf, xsem, osem):
    e = pl.program_id(0)
    first, nblk, cnt = first_ref[e], nblk_ref[e], cnt_ref[e]

    def rows(j):
        return pl.ds(pl.multiple_of((first + j) * MOE_ROWS, MOE_ROWS), MOE_ROWS)

    def x_copy(j, slot):
        return pltpu.make_async_copy(xs_hbm.at[rows(j)], xbuf.at[slot], xsem.at[slot])

    def o_copy(j, slot):
        return pltpu.make_async_copy(obuf.at[slot], out_hbm.at[rows(j)], osem.at[slot])

    @pl.when(nblk > 0)
    def _():
        x_copy(0, 0).start(priority=ROW_DMA_PRIORITY)
        wgu_bf[...] = wgu_ref[...].astype(BF16)
        wd_bf[...] = wd_ref[...].astype(BF16)

        def block(j, carry):
            slot = j % 2
            x_copy(j, slot).wait()

            @pl.when(j + 1 < nblk)
            def _():
                x_copy(j + 1, 1 - slot).start(priority=ROW_DMA_PRIORITY)

            @pl.when(j >= 2)
            def _():
                o_copy(j - 2, slot).wait()

            row = lax.broadcasted_iota(jnp.int32, (MOE_ROWS, 1), 0)
            x = _unpack_rows(jnp.where(row < cnt - j * MOE_ROWS, xbuf[slot], 0)).astype(BF16)
            gu = _dot(x, wgu_bf[...]) + bgu_ref[...]
            gate = jnp.minimum(gu[:, :D_FF], SWIGLU_LIMIT)
            up = jnp.clip(gu[:, D_FF:], -SWIGLU_LIMIT, SWIGLU_LIMIT)
            act = gate * jax.nn.sigmoid(SWIGLU_ALPHA * gate) * (up + 1.0)
            obuf[slot] = _pack_rows(_dot(act.astype(BF16), wd_bf[...]) + bd_ref[...])
            o_copy(j, slot).start(priority=ROW_DMA_PRIORITY)
            return carry

        lax.fori_loop(0, nblk, block, 0)

        @pl.when(nblk >= 2)
        def _():
            o_copy(nblk - 2, nblk % 2).wait()
        o_copy(nblk - 1, (nblk - 1) % 2).wait()


def _moe(xs, first_block, n_blocks, counts, w_gu, b_gu, w_down, b_down):
    grid_spec = pltpu.PrefetchScalarGridSpec(
        num_scalar_prefetch=3,
        grid=(N_EXPERTS,),
        in_specs=[pl.BlockSpec(memory_space=pl.ANY),
                  pl.BlockSpec((None, D, 2 * D_FF), lambda e, *_: (e, 0, 0)),
                  pl.BlockSpec((None, 1, 2 * D_FF), lambda e, *_: (e, 0, 0)),
                  pl.BlockSpec((None, D_FF, D), lambda e, *_: (e, 0, 0)),
                  pl.BlockSpec((None, 1, D), lambda e, *_: (e, 0, 0))],
        out_specs=pl.BlockSpec(memory_space=pl.ANY),
        scratch_shapes=[pltpu.VMEM((2, MOE_ROWS, PACKED), jnp.int32), pltpu.VMEM((2, MOE_ROWS, PACKED), jnp.int32),
                        pltpu.VMEM((D, 2 * D_FF), BF16), pltpu.VMEM((D_FF, D), BF16),
                        pltpu.SemaphoreType.DMA((2,)), pltpu.SemaphoreType.DMA((2,))],
    )
    return pl.pallas_call(
        _moe_kernel,
        grid_spec=grid_spec,
        out_shape=jax.ShapeDtypeStruct(xs.shape, jnp.int32),
        compiler_params=pltpu.CompilerParams(
            dimension_semantics=("arbitrary",), vmem_limit_bytes=MOE_VMEM_LIMIT),
        name="moe",
    )(first_block, n_blocks, counts, xs, w_gu, b_gu.reshape(N_EXPERTS, 1, 2 * D_FF), w_down,
      b_down.reshape(N_EXPERTS, 1, D))


def _final_kernel(x1_ref, yk_ref, gate_ref, gt2_ref, g_ref, o_ref):
    y2 = gate_ref[:, 0:1] * _unpack_rows(yk_ref[0])
    for k in range(1, TOP_K):
        y2 = y2 + gate_ref[:, k:k + 1] * _unpack_rows(yk_ref[k])
    o_ref[...] = _rms(x1_ref[...] + gt2_ref[...] * y2) * g_ref[...]


def _final(x1, yk, gates, row_off, mod, per_token, tokens_per_seq, g_final):
    m = x1.shape[0]
    tm = min(ROW_TILE, m)
    assert m % tm == 0 and row_off % tm == 0 and (per_token or tokens_per_seq % tm == 0)
    off = row_off // tm
    return pl.pallas_call(
        _final_kernel,
        grid=(m // tm,),
        in_specs=[pl.BlockSpec((tm, D), lambda i: (i, 0)),
                  pl.BlockSpec((TOP_K, tm, PACKED), lambda i: (0, i + off, 0)),
                  pl.BlockSpec((tm, TOP_K), lambda i: (i, 0)),
                  _mod_spec(per_token, tm, tokens_per_seq, 5),
                  pl.BlockSpec((1, D), lambda i: (0, 0))],
        out_specs=pl.BlockSpec((tm, D), lambda i: (i, 0)),
        out_shape=jax.ShapeDtypeStruct((m, D), F32),
        compiler_params=pltpu.CompilerParams(
            dimension_semantics=("arbitrary",), vmem_limit_bytes=VMEM_LIMIT),
        name="final",
    )(x1, yk, gates, mod, g_final.reshape(1, D))


def kernel(x_prompt, x_sample, state_C, state_n, state_m, c_prompt, c_sample, w_ada, b_ada, g_mix, w_in,
           b_if, g_head, g_sgu, b_sgu, w_s, b_s, w_pa, w_pb, w_out, g_ffn, w_router, b_router, w_gu, b_gu,
           w_down, b_down, g_final):
    depth = w_ada.shape[0]
    assert depth == 1
    bp, tp, _ = x_prompt.shape
    bs, ts, _ = x_sample.shape
    mp, ms = bp * tp, bs * ts
    assert tp % ML_CHUNK == 0 and ts <= ML_CHUNK and GM_CHUNK % ts == 0

    w_in0 = w_in[0]
    nqkvo = 4 * D
    w_qkvo = w_in0[:, :nqkvo].astype(BF16)
    w_gate = w_in0[:, nqkvo + 2 * HEADS:].astype(BF16)
    w_if = jnp.pad(w_in0[:, nqkvo:nqkvo + 2 * HEADS], ((0, 0), (0, LANES - 2 * HEADS)))
    b_if_p = jnp.pad(b_if[0], (0, LANES - 2 * HEADS)).reshape(1, LANES)
    reps = GM_CHUNK // ts
    eye_r = jnp.eye(reps, dtype=F32)
    w_s_sample = jnp.einsum("ab,gts->gatbs", eye_r, w_s[0][:, :ts, :ts]).reshape(GROUPS, GM_CHUNK, GM_CHUNK)
    b_s_prompt = b_s[0].T
    b_s_sample = jnp.tile(b_s[0][:, :ts].T, (reps, 1))
    mix_p = {
        "g_sgu": g_sgu[0].reshape(1, D), "b_sgu": b_sgu[0].reshape(1, D),
        "w_pa": w_pa[0].astype(BF16), "w_pb": w_pb[0].astype(BF16), "w_out": w_out[0].astype(BF16),
        "g_ffn": g_ffn[0].reshape(1, D),
        "w_router": jnp.pad(w_router[0], ((0, 0), (0, LANES - N_EXPERTS))),
        "b_router": jnp.pad(b_router[0], (0, LANES - N_EXPERTS)).reshape(1, LANES),
    }
    mix_prompt = dict(mix_p, w_s=w_s[0], b_s=b_s_prompt)
    mix_sample = dict(mix_p, w_s=w_s_sample, b_s=b_s_sample)

    mod = _ada(jnp.concatenate([jnp.repeat(c_sample, ts, axis=0), c_prompt], axis=0), w_ada[0], b_ada[0])
    mod_s = mod
    mod_p = mod[ms:].reshape(bp, 1, N_MOD * D)

    xp = x_prompt.reshape(mp, D)
    xs = x_sample.reshape(ms, D)
    z_p, _, gt_p = _in_proj(xp, mod_p, False, tp, g_mix[0], w_qkvo, w_gate, w_if, b_if_p, 512, BF16)
    z_s, gc_s, _ = _in_proj(xs, mod_s, True, ts, g_mix[0], w_qkvo, w_gate, w_if, b_if_p, 128, F32)

    hm_p, C_p, n_p, m_p = _mlstm_prompt(z_p, gt_p, g_head[0], bp, tp)
    hm_p = hm_p.reshape(mp, D)
    m0_tok = jnp.repeat(state_m[0], ts, axis=0)
    hm_s, C_s, n_s, m_s = _mlstm_sample(z_s, gc_s, m0_tok, state_C[0], state_n[0], g_head[0], bs, ts)

    x1_p, h2_p, ch_p, gates_p, _, cnt_p = _mix(xp, hm_p, z_p, mod_p, False, tp, 256, mix_prompt, False)
    x1_s, h2_s, ch_s, gates_s, vg_s, cnt_s = _mix(xs, hm_s, z_s, mod_s, True, ts, 128, mix_sample, True)

    n_tok = mp + ms
    n_blocks = -(-(n_tok * TOP_K + N_EXPERTS * (MOE_ROWS - 1)) // MOE_ROWS)
    dest, ex_table = _route(ch_p, ch_s, cnt_p, cnt_s)
    dest_kmajor = dest.T.reshape(TOP_K * n_tok)
    xslots = _sc_scatter_rows(h2_p, h2_s, dest_kmajor, n_blocks * MOE_ROWS)
    yb = _moe(xslots, ex_table[0, :N_EXPERTS], ex_table[1, :N_EXPERTS], ex_table[2, :N_EXPERTS],
              w_gu[0], b_gu[0], w_down[0], b_down[0])
    yk = _sc_gather_rows(yb, dest_kmajor).reshape(TOP_K, n_tok, PACKED)

    y_p = _final(x1_p, yk, gates_p, 0, mod_p, False, tp, g_final)
    y_s = _final(x1_s, yk, gates_s, mp, mod_s, True, ts, g_final)

    return (y_p.reshape(bp, tp, D), y_s.reshape(bs, ts, D),
            C_p[None], n_p.reshape(1, bp, HEADS, HD), m_p[:, :, 0, 0][None],
            C_s[None], n_s.reshape(1, bs, HEADS, HD),
            m_s.reshape(bs, ts, HEADS, LANES)[:, 0, :, 0][None],
            vg_s.reshape(1, bs, ts, D))
```

```python
import functools

import jax
import jax.numpy as jnp
from jax import lax
from jax.experimental import pallas as pl
from jax.experimental.pallas import tpu as pltpu
from jax.experimental.pallas import tpu_sc as plsc

F32 = jnp.float32
BF16 = jnp.bfloat16

D = 1024
HEADS = 4
HD = D // HEADS
ML_CHUNK = 512
GROUPS = 4
GD = D // GROUPS
GM_CHUNK = 128
N_EXPERTS = 32
TOP_K = 4
D_FF = D
SWIGLU_LIMIT = 7.0
SWIGLU_ALPHA = 1.702
NORM_EPS = 1e-6
N_MOD = 6
PACKED = D // 2
LANES = 128
SC_CORES = 2
SC_SUBCORES = 16
SC_WORKERS = SC_CORES * SC_SUBCORES
IN_COLS = 2048
ROW_TILE = 512
FINAL_PARTS = 2
MOE_ROWS = 256
ROW_DMA_PRIORITY = 1
MLSTM_SEQS = 1
SAMPLE_SEQS = 8
SC_SCATTER_CHUNK = 64
SC_GATHER_CHUNK = 96
VMEM_LIMIT = 48 * 1024 * 1024
MOE_VMEM_LIMIT = 56 * 1024 * 1024


def _dot(a, b):
    return jnp.dot(a, b, preferred_element_type=F32)


def _dot_nt(a, b):
    return lax.dot_general(a, b, (((1,), (1,)), ((), ())), preferred_element_type=F32)


def _dot_tn(a, b):
    return lax.dot_general(a, b, (((0,), (0,)), ((), ())), preferred_element_type=F32)


def _split_bf16(a):
    hi = a.astype(BF16)
    lo = (a - hi.astype(F32)).astype(BF16)
    return hi, lo


def _dot3(a, b):
    ah, al = _split_bf16(a)
    bh, bl = _split_bf16(b)
    return _dot(ah, bh) + (_dot(ah, bl) + _dot(al, bh))


def _log_sigmoid(x):
    return jnp.minimum(x, 0.0) - jnp.log1p(jnp.exp(-jnp.abs(x)))


def _rms(x):
    return x * lax.rsqrt(jnp.mean(x * x, axis=-1, keepdims=True) + NORM_EPS)


def _pack_rows(x):
    bits = lax.bitcast_convert_type(x.astype(BF16).astype(F32), jnp.uint32)
    word = (bits[:, :PACKED] & jnp.uint32(0xFFFF0000)) | (bits[:, PACKED:] >> 16)
    return lax.bitcast_convert_type(word, jnp.int32)


def _unpack_rows(w):
    bits = lax.bitcast_convert_type(w, jnp.uint32)
    left = lax.bitcast_convert_type(bits & jnp.uint32(0xFFFF0000), F32)
    right = lax.bitcast_convert_type(bits << 16, F32)
    return jnp.concatenate([left, right], axis=1)


def _mod_spec(per_token, tm, tokens_per_seq, col, tile0=0):
    if per_token:
        return pl.BlockSpec((tm, D), lambda i, *_: (i + tile0, col))
    return pl.BlockSpec((None, 1, D), lambda i, *_: (((i + tile0) * tm) // tokens_per_seq, 0, col))


def _ada_kernel(c_ref, w_ref, b_ref, o_ref):
    c = c_ref[...]
    s = (c * jax.nn.sigmoid(c)).astype(BF16)
    o_ref[...] = _dot(s, w_ref[...].astype(BF16)) + b_ref[...]


def _ada(c, w, b):
    m, n = c.shape[0], w.shape[1]
    tn = 512
    return pl.pallas_call(
        _ada_kernel,
        grid=(n // tn,),
        in_specs=[pl.BlockSpec((m, D), lambda j: (0, 0)),
                  pl.BlockSpec((D, tn), lambda j: (0, j)),
                  pl.BlockSpec((1, tn), lambda j: (0, j))],
        out_specs=pl.BlockSpec((m, tn), lambda j: (0, j)),
        out_shape=jax.ShapeDtypeStruct((m, n), F32),
        name="ada",
    )(c, w, b.reshape(1, n))


def _in_kernel(x_ref, g_ref, sh_ref, sc_ref, wa_ref, wb_ref, wif_ref, bif_ref, z_ref, gc_ref, gt_ref):
    h = (_rms(x_ref[...]) * g_ref[...]) * (1.0 + sc_ref[...]) + sh_ref[...]
    hb = h.astype(BF16)
    gates = _dot3(h, wif_ref[...]) + bif_ref[...]
    gc_ref[...] = gates
    gt_ref[...] = gates.T[:2 * HEADS, :]
    half = wa_ref.shape[1]
    for w_ref, col0 in ((wa_ref, 0), (wb_ref, half)):
        for c in range(half // IN_COLS):
            cols = slice(c * IN_COLS, (c + 1) * IN_COLS)
            z_ref[:, col0 + c * IN_COLS:col0 + (c + 1) * IN_COLS] = _dot(hb, w_ref[:, cols]).astype(z_ref.dtype)


def _in_proj(x, mod, per_token, tokens_per_seq, g_mix, w_a, w_b, w_if, b_if, tm, z_dtype):
    m = x.shape[0]
    half = w_a.shape[1]
    gt_len = tokens_per_seq if tokens_per_seq % tm == 0 else m
    gt_tiles = gt_len // tm
    resident = functools.partial(pl.BlockSpec, index_map=lambda i: (0, 0), pipeline_mode=pl.Buffered(1))
    return pl.pallas_call(
        _in_kernel,
        grid=(m // tm,),
        in_specs=[pl.BlockSpec((tm, D), lambda i: (i, 0)),
                  pl.BlockSpec((1, D), lambda i: (0, 0)),
                  _mod_spec(per_token, tm, tokens_per_seq, 0),
                  _mod_spec(per_token, tm, tokens_per_seq, 1),
                  resident((D, half)), resident((D, half)),
                  pl.BlockSpec((D, LANES), lambda i: (0, 0)),
                  pl.BlockSpec((1, LANES), lambda i: (0, 0))],
        out_specs=[pl.BlockSpec((tm, 2 * half), lambda i: (i, 0)),
                   pl.BlockSpec((tm, LANES), lambda i: (i, 0)),
                   pl.BlockSpec((None, 2 * HEADS, tm), lambda i: (i // gt_tiles, 0, i % gt_tiles))],
        out_shape=[jax.ShapeDtypeStruct((m, 2 * half), z_dtype),
                   jax.ShapeDtypeStruct((m, LANES), F32),
                   jax.ShapeDtypeStruct((m // gt_len, 2 * HEADS, gt_len), F32)],
        compiler_params=pltpu.CompilerParams(
            dimension_semantics=("arbitrary",), vmem_limit_bytes=MOE_VMEM_LIMIT),
        name="in_proj",
    )(x, g_mix.reshape(1, D), mod, mod, w_a, w_b, w_if, b_if)


def _mlstm_prompt_kernel(q_ref, k_ref, v_ref, o_ref, gt_ref, gh_ref, hm_ref, C_ref, n_ref, m_ref):
    nseq, L = q_ref.shape[0], q_ref.shape[1]

    @pl.when(pl.program_id(1) == 0)
    def _():
        C_ref[...] = jnp.zeros_like(C_ref)
        n_ref[...] = jnp.zeros_like(n_ref)
        m_ref[...] = jnp.zeros_like(m_ref)

    r = lax.broadcasted_iota(jnp.int32, (L, L), 0)
    s = lax.broadcasted_iota(jnp.int32, (L, L), 1)
    eye = r == s
    causal = s <= r

    def to_col(x_row):
        return jnp.sum(jnp.where(eye, x_row, 0.0), axis=1, keepdims=True)

    for b in range(nseq):
        gates = gt_ref[b]
        for h in range(HEADS):
            cols = slice(h * HD, (h + 1) * HD)
            ig_row = gates[h:h + 1, :]
            lf_row = _log_sigmoid(gates[HEADS + h:HEADS + h + 1, :])
            lf_col = to_col(lf_row)
            b_row = jnp.sum(jnp.where(r <= s, lf_col, 0.0), axis=0, keepdims=True)
            b_col = to_col(b_row)
            m_prev = m_ref[b, h][:, :1]

            logD = jnp.where(causal, b_col - b_row + ig_row, -jnp.inf)
            inter = b_col + m_prev
            mt = jnp.maximum(jnp.max(logD, axis=1, keepdims=True), inter)
            q = q_ref[b, :, cols]
            ks = k_ref[b, :, cols] * (HD ** -0.5)
            v = v_ref[b, :, cols]
            S = _dot_nt(q, ks) * jnp.exp(logD - mt)
            w_int = jnp.exp(inter - mt)
            Cmat = C_ref[b, h]
            nvec = n_ref[b, h]
            num = _dot(S.astype(BF16), v) + w_int * _dot_nt(q, Cmat.astype(BF16))
            nq = jnp.sum(q.astype(F32) * nvec, axis=1, keepdims=True)
            den = jnp.sum(S, axis=1, keepdims=True) + w_int * nq
            hh = num / jnp.maximum(jnp.abs(den), jnp.exp(-mt))
            hg = jax.nn.sigmoid(o_ref[b, :, cols].astype(F32)) * hh
            hm_ref[b, :, cols] = (_rms(hg) * gh_ref[h]).astype(hm_ref.dtype)

            bL = b_row[:, L - 1:L]
            g_row = bL - b_row + ig_row
            m_new = jnp.maximum(bL + m_prev, jnp.max(g_row, axis=1, keepdims=True))
            w_old = jnp.exp(bL + m_prev - m_new)
            kw = ks.astype(F32) * to_col(jnp.exp(g_row - m_new))
            C_ref[b, h] = w_old * Cmat + _dot_tn(v, kw.astype(BF16))
            n_ref[b, h] = w_old * nvec + jnp.sum(kw, axis=0, keepdims=True)
            m_ref[b, h] = jnp.broadcast_to(m_new, (1, LANES))


def _mlstm_prompt(z, gates_t, g_head, batch, seq):
    nc = seq // ML_CHUNK
    nb = MLSTM_SEQS
    z3 = z.reshape(batch, seq, z.shape[1])

    def zspec(col):
        return pl.BlockSpec((nb, ML_CHUNK, D), lambda b, c: (b, c, col))

    def state(last):
        return pl.BlockSpec((nb, HEADS) + last, lambda b, c: (b, 0, 0, 0))

    return pl.pallas_call(
        _mlstm_prompt_kernel,
        grid=(batch // nb, nc),
        in_specs=[zspec(0), zspec(1), zspec(2), zspec(3),
                  pl.BlockSpec((nb, 2 * HEADS, ML_CHUNK), lambda b, c: (b, 0, c)),
                  pl.BlockSpec((HEADS, 1, HD), lambda b, c: (0, 0, 0))],
        out_specs=[pl.BlockSpec((nb, ML_CHUNK, D), lambda b, c: (b, c, 0)),
                   state((HD, HD)), state((1, HD)), state((1, LANES))],
        out_shape=[jax.ShapeDtypeStruct((batch, seq, D), BF16),
                   jax.ShapeDtypeStruct((batch, HEADS, HD, HD), F32),
                   jax.ShapeDtypeStruct((batch, HEADS, 1, HD), F32),
                   jax.ShapeDtypeStruct((batch, HEADS, 1, LANES), F32)],
        compiler_params=pltpu.CompilerParams(dimension_semantics=("arbitrary", "arbitrary")),
        name="mlstm_prompt",
    )(z3, z3, z3, z3, gates_t, g_head.reshape(HEADS, 1, HD))


def _mlstm_sample_kernel(seq_len, q_ref, k_ref, v_ref, o_ref, gc_ref, m0_ref, C0_ref, n0_ref, gh_ref,
                         hm_ref, C_ref, n_ref, m_ref):
    R = q_ref.shape[0]
    nseq = R // seq_len
    r = lax.broadcasted_iota(jnp.int32, (R, R), 0)
    s = lax.broadcasted_iota(jnp.int32, (R, R), 1)
    rseq = lax.broadcasted_iota(jnp.int32, (R, 1), 0) // seq_len
    eye = r == s
    same = (r // seq_len) == (s // seq_len)
    causal = same & (s <= r)

    def to_row(x_col):
        return jnp.sum(jnp.where(eye, x_col, 0.0), axis=0, keepdims=True)

    gc = gc_ref[...]
    for h in range(HEADS):
        cols = slice(h * HD, (h + 1) * HD)
        ig_col = gc[:, h:h + 1]
        lf_col = _log_sigmoid(gc[:, HEADS + h:HEADS + h + 1])
        lf_row = to_row(lf_col)
        b_col = jnp.sum(jnp.where(causal, lf_row, 0.0), axis=1, keepdims=True)
        bL_col = jnp.sum(jnp.where(same, lf_row, 0.0), axis=1, keepdims=True)
        b_row = to_row(b_col)
        ig_row = to_row(ig_col)
        m0_col = m0_ref[:, h:h + 1]

        logD = jnp.where(causal, b_col - b_row + ig_row, -jnp.inf)
        inter = b_col + m0_col
        mt = jnp.maximum(jnp.max(logD, axis=1, keepdims=True), inter)
        qf = q_ref[:, cols]
        q = qf.astype(BF16)
        ksf = k_ref[:, cols] * (HD ** -0.5)
        v = v_ref[:, cols].astype(BF16)
        S = _dot_nt(q, ksf.astype(BF16)) * jnp.exp(logD - mt)
        w_int = jnp.exp(inter - mt)

        Cq = jnp.zeros((R, HD), F32)
        nq = jnp.zeros((R, 1), F32)
        for g in range(nseq):
            Cq = jnp.where(rseq == g, _dot_nt(q, C0_ref[g, h].astype(BF16)), Cq)
            nq = jnp.where(rseq == g, jnp.sum(qf * n0_ref[g, h], axis=1, keepdims=True), nq)
        num = _dot(S.astype(BF16), v) + w_int * Cq
        den = jnp.sum(S, axis=1, keepdims=True) + w_int * nq
        hh = num / jnp.maximum(jnp.abs(den), jnp.exp(-mt))
        hg = jax.nn.sigmoid(o_ref[:, cols]) * hh
        hm_ref[:, cols] = (_rms(hg) * gh_ref[h]).astype(hm_ref.dtype)

        g_col = bL_col - b_col + ig_col
        gmax_col = jnp.max(jnp.where(same, to_row(g_col), -jnp.inf), axis=1, keepdims=True)
        m_new_col = jnp.maximum(bL_col + m0_col, gmax_col)
        w_old_col = jnp.exp(bL_col + m0_col - m_new_col)
        kw = ksf * jnp.exp(g_col - m_new_col)
        for g in range(nseq):
            kw_g = jnp.where(rseq == g, kw, 0.0)
            w_old = w_old_col[g * seq_len:g * seq_len + 1, :]
            C_ref[g, h] = w_old * C0_ref[g, h] + _dot_tn(v, kw_g.astype(BF16))
            n_ref[g, h] = w_old * n0_ref[g, h] + jnp.sum(kw_g, axis=0, keepdims=True)
        m_ref[:, h * LANES:(h + 1) * LANES] = jnp.broadcast_to(m_new_col, (R, LANES))


def _mlstm_sample(z, gates_c, m0_tok, C0, n0, g_head, batch, seq):
    rows = SAMPLE_SEQS * seq
    m = batch * seq

    def zspec(col):
        return pl.BlockSpec((rows, D), lambda i: (i, col))

    state_c = pl.BlockSpec((SAMPLE_SEQS, HEADS, HD, HD), lambda i: (i, 0, 0, 0))
    state_n = pl.BlockSpec((SAMPLE_SEQS, HEADS, 1, HD), lambda i: (i, 0, 0, 0))
    return pl.pallas_call(
        functools.partial(_mlstm_sample_kernel, seq),
        grid=(batch // SAMPLE_SEQS,),
        in_specs=[zspec(0), zspec(1), zspec(2), zspec(3),
                  pl.BlockSpec((rows, LANES), lambda i: (i, 0)),
                  pl.BlockSpec((rows, HEADS), lambda i: (i, 0)),
                  state_c, state_n,
                  pl.BlockSpec((HEADS, 1, HD), lambda i: (0, 0, 0))],
        out_specs=[pl.BlockSpec((rows, D), lambda i: (i, 0)),
                   state_c, state_n,
                   pl.BlockSpec((rows, HEADS * LANES), lambda i: (i, 0))],
        out_shape=[jax.ShapeDtypeStruct((m, D), BF16),
                   jax.ShapeDtypeStruct((batch, HEADS, HD, HD), F32),
                   jax.ShapeDtypeStruct((batch, HEADS, 1, HD), F32),
                   jax.ShapeDtypeStruct((m, HEADS * LANES), F32)],
        compiler_params=pltpu.CompilerParams(
            dimension_semantics=("arbitrary",), vmem_limit_bytes=VMEM_LIMIT),
        name="mlstm_sample",
    )(z, z, z, z, gates_c, m0_tok, C0, n0.reshape(batch, HEADS, 1, HD), g_head.reshape(HEADS, 1, HD))


def _mix_kernel(x_ref, hm_ref, u_ref, v_ref, ga_ref, gb_ref, gt1_ref, sh2_ref, sc2_ref,
                gsgu_ref, bsgu_ref, ws_ref, bs_ref, wpa_ref, wpb_ref, wout_ref, gffn_ref,
                wr_ref, br_ref, x1_ref, h2_ref, choice_ref, gate_ref, vg_ref, cnt_ref, yg_scr):
    tm = x_ref.shape[0]
    u = jax.nn.gelu(u_ref[...].astype(F32))
    vv = jax.nn.gelu(v_ref[...].astype(F32))
    mu = jnp.mean(vv, axis=-1, keepdims=True)
    var = jnp.mean(jnp.square(vv - mu), axis=-1, keepdims=True)
    vg = (vv - mu) * lax.rsqrt(var + NORM_EPS) * gsgu_ref[...] + bsgu_ref[...]
    vg_ref[...] = vg
    vgb = vg.astype(BF16)

    r = lax.broadcasted_iota(jnp.int32, (GM_CHUNK, GM_CHUNK), 0)
    s = lax.broadcasted_iota(jnp.int32, (GM_CHUNK, GM_CHUNK), 1)
    for g in range(GROUPS):
        w = jnp.where(s <= r, ws_ref[g], 0.0).astype(BF16)
        bias = bs_ref[:, g:g + 1]
        for c in range(tm // GM_CHUNK):
            rows = slice(c * GM_CHUNK, (c + 1) * GM_CHUNK)
            cols = slice(g * GD, (g + 1) * GD)
            mixed = _dot(w, vgb[rows, cols]) + bias
            yg_scr[rows, cols] = (u[rows, cols] * mixed).astype(BF16)

    a = _dot(hm_ref[...], wpa_ref[...])
    b = _dot(yg_scr[...], wpb_ref[...])
    merged = (jax.nn.sigmoid(ga_ref[...].astype(F32)) * a
              + jax.nn.sigmoid(gb_ref[...].astype(F32)) * b)
    x1 = x_ref[...] + gt1_ref[...] * _dot(merged.astype(BF16), wout_ref[...])
    x1_ref[...] = x1
    h2 = (_rms(x1) * gffn_ref[...]) * (1.0 + sc2_ref[...]) + sh2_ref[...]
    h2_ref[...] = _pack_rows(h2)

    lane = lax.broadcasted_iota(jnp.int32, (tm, LANES), 1)
    lane_f = lane.astype(F32)
    lg = jnp.where(lane < N_EXPERTS, _dot3(h2, wr_ref[...]) + br_ref[...], -jnp.inf)
    choice = jnp.zeros((tm, LANES), F32)
    vals = []
    for k in range(TOP_K):
        mx = jnp.max(lg, axis=1, keepdims=True)
        sel = lane_f == jnp.min(jnp.where(lg == mx, lane_f, float(LANES)), axis=1, keepdims=True)
        choice = jnp.where(sel, k + 1.0, choice)
        vals.append(mx)
        lg = jnp.where(sel, -jnp.inf, lg)
    choice_ref[...] = choice
    ex = [jnp.exp(v - vals[0]) for v in vals]
    denom = sum(ex)
    gates = jnp.zeros((tm, LANES), F32)
    for k in range(TOP_K):
        gates = jnp.where(lane == k, ex[k] / denom, gates)
    gate_ref[...] = gates[:, :TOP_K]

    @pl.when(pl.program_id(0) == 0)
    def _():
        cnt_ref[...] = jnp.zeros_like(cnt_ref)

    cnt_ref[...] += jnp.sum(jnp.where(choice > 0.0, 1.0, 0.0), axis=0, keepdims=True)


def _mix(x, hm, z, mod, per_token, tokens_per_seq, tm, p, keep_v_rows):
    m = x.shape[0]
    zcol = 4
    v_rows = m if keep_v_rows else tm
    v_spec = pl.BlockSpec((tm, D), (lambda i: (i, 0)) if keep_v_rows else (lambda i: (0, 0)))

    def zspec(blk):
        return pl.BlockSpec((tm, D), lambda i: (i, blk))

    def full(shape):
        return pl.BlockSpec(shape, lambda i: (0,) * len(shape))

    row = pl.BlockSpec((tm, D), lambda i: (i, 0))
    return pl.pallas_call(
        _mix_kernel,
        grid=(m // tm,),
        in_specs=[row, row, zspec(zcol), zspec(zcol + 1), zspec(zcol + 2), zspec(zcol + 3),
                  _mod_spec(per_token, tm, tokens_per_seq, 2),
                  _mod_spec(per_token, tm, tokens_per_seq, 3),
                  _mod_spec(per_token, tm, tokens_per_seq, 4),
                  full((1, D)), full((1, D)),
                  full((GROUPS, GM_CHUNK, GM_CHUNK)), full((GM_CHUNK, GROUPS)),
                  full((D, D)), full((D, D)), full((D, D)), full((1, D)),
                  full((D, LANES)), full((1, LANES))],
        out_specs=[row, pl.BlockSpec((tm, PACKED), lambda i: (i, 0)),
                   pl.BlockSpec((tm, LANES), lambda i: (i, 0)), pl.BlockSpec((tm, TOP_K), lambda i: (i, 0)),
                   v_spec, full((1, LANES))],
        out_shape=[jax.ShapeDtypeStruct((m, D), F32),
                   jax.ShapeDtypeStruct((m, PACKED), jnp.int32),
                   jax.ShapeDtypeStruct((m, LANES), F32),
                   jax.ShapeDtypeStruct((m, TOP_K), F32),
                   jax.ShapeDtypeStruct((v_rows, D), F32),
                   jax.ShapeDtypeStruct((1, LANES), F32)],
        scratch_shapes=[pltpu.VMEM((tm, D), BF16)],
        compiler_params=pltpu.CompilerParams(
            dimension_semantics=("arbitrary",), vmem_limit_bytes=VMEM_LIMIT),
        name="mix",
    )(x, hm, z, z, z, z, mod, mod, mod, p["g_sgu"], p["b_sgu"], p["w_s"], p["b_s"],
      p["w_pa"], p["w_pb"], p["w_out"], p["g_ffn"], p["w_router"], p["b_router"])


def _route_kernel(tiles_a, cha_ref, chb_ref, cnta_ref, cntb_ref, dest_ref, ex_ref, base_scr):
    tm = cha_ref.shape[0]
    lane = lax.broadcasted_iota(jnp.int32, (tm, LANES), 1)
    choice = jnp.where(pl.program_id(0) < tiles_a, cha_ref[...], chb_ref[...])
    onehot = jnp.where(choice > 0.0, 1.0, 0.0)

    @pl.when(pl.program_id(0) == 0)
    def _():
        cnt = cnta_ref[...] + cntb_ref[...]
        padded = jnp.floor((cnt + (MOE_ROWS - 1)) * (1.0 / MOE_ROWS)) * MOE_ROWS
        r = lax.broadcasted_iota(jnp.int32, (LANES, LANES), 0)
        s = lax.broadcasted_iota(jnp.int32, (LANES, LANES), 1)
        padded_col = jnp.sum(jnp.where(r == s, padded, 0.0), axis=1, keepdims=True)
        pstart = jnp.sum(jnp.where(r < s, padded_col, 0.0), axis=0, keepdims=True)
        base_scr[...] = pstart
        trow = lax.broadcasted_iota(jnp.int32, ex_ref.shape, 0)
        table = jnp.where(trow == 0, pstart * (1.0 / MOE_ROWS),
                          jnp.where(trow == 1, padded * (1.0 / MOE_ROWS), jnp.where(trow == 2, cnt, 0.0)))
        ex_ref[...] = table.astype(jnp.int32)

    r = lax.broadcasted_iota(jnp.int32, (tm, tm), 0)
    s = lax.broadcasted_iota(jnp.int32, (tm, tm), 1)
    before = _dot((s < r).astype(BF16), onehot.astype(BF16))
    slot = before + base_scr[...]
    dest = jnp.zeros((tm, LANES), F32)
    for k in range(TOP_K):
        d_k = jnp.sum(jnp.where(choice == k + 1.0, slot, 0.0), axis=1, keepdims=True)
        dest = jnp.where(lane == k, d_k, dest)
    dest_ref[...] = dest[:, :TOP_K].astype(jnp.int32)
    base_scr[...] += jnp.sum(onehot, axis=0, keepdims=True)


def _route(choice_a, choice_b, cnt_a, cnt_b):
    tm = min(ROW_TILE, choice_a.shape[0], choice_b.shape[0])
    assert choice_a.shape[0] % tm == 0 and choice_b.shape[0] % tm == 0
    tiles_a, tiles_b = choice_a.shape[0] // tm, choice_b.shape[0] // tm
    n = (tiles_a + tiles_b) * tm
    fixed = lambda i: (0, 0)
    return pl.pallas_call(
        functools.partial(_route_kernel, tiles_a),
        grid=(tiles_a + tiles_b,),
        in_specs=[pl.BlockSpec((tm, LANES), lambda i: (jnp.minimum(i, tiles_a - 1), 0)),
                  pl.BlockSpec((tm, LANES), lambda i: (jnp.maximum(i - tiles_a, 0), 0)),
                  pl.BlockSpec((1, LANES), fixed), pl.BlockSpec((1, LANES), fixed)],
        out_specs=[pl.BlockSpec((tm, TOP_K), lambda i: (i, 0)), pl.BlockSpec((8, LANES), fixed)],
        out_shape=[jax.ShapeDtypeStruct((n, TOP_K), jnp.int32),
                   jax.ShapeDtypeStruct((8, LANES), jnp.int32)],
        scratch_shapes=[pltpu.VMEM((1, LANES), F32)],
        compiler_params=pltpu.CompilerParams(dimension_semantics=("arbitrary",)),
        name="route",
    )(choice_a, choice_b, cnt_a, cnt_b)


def _sc_worker_base(per_worker):
    return (lax.axis_index("s") * SC_CORES + lax.axis_index("c")) * per_worker


def _sc_scatter_rows(rows_a, rows_b, idx_flat, n_out):
    na, nb = rows_a.shape[0], rows_b.shape[0]
    width, dtype = rows_a.shape[1], rows_a.dtype
    n = na + nb
    per_a, per_b = na // SC_WORKERS, nb // SC_WORKERS
    chunk = SC_SCATTER_CHUNK
    n_chunks = per_a // chunk
    assert per_a * SC_WORKERS == na and per_b * SC_WORKERS == nb and per_b % 8 == 0 and per_b <= chunk
    assert n_chunks * chunk == per_a and n_chunks % 2 == 0
    mesh = plsc.VectorSubcoreMesh(core_axis_name="c", subcore_axis_name="s")

    @functools.partial(
        pl.kernel, mesh=mesh,
        out_type=jax.ShapeDtypeStruct((n_out, width), dtype),
        scratch_types=[pltpu.VMEM((chunk,), jnp.int32)] * TOP_K + [pltpu.VMEM((per_b,), jnp.int32)]
                      + [pltpu.VMEM((chunk, width), dtype), pltpu.VMEM((chunk, width), dtype),
                         pltpu.VMEM((per_b, width), dtype)]
                      + [pltpu.SemaphoreType.DMA] * 3,
    )
    def scatter(a_hbm, b_hbm, idx_hbm, out_hbm, i0, i1, i2, i3, ib, rows0, rows1, rowsb, rsem0, rsem1, wsem):
        base = _sc_worker_base(per_a)
        idx_bufs = (i0, i1, i2, i3)
        bufs = ((rows0, rsem0), (rows1, rsem1))

        def off(j):
            return pl.multiple_of(base + j * chunk, 8)

        def read(j, buf):
            rows_v, sem = buf
            return pltpu.make_async_copy(a_hbm.at[pl.ds(off(j), chunk)], rows_v, sem)

        def spread(j, buf):
            rows_v, _ = buf
            read(j, buf).wait()
            for k in range(TOP_K):
                pltpu.sync_copy(idx_hbm.at[pl.ds(pl.multiple_of(k * n + off(j), 8), chunk)], idx_bufs[k])
            for k in range(TOP_K):
                pltpu.make_async_copy(rows_v, out_hbm.at[idx_bufs[k]], wsem).start()
            for k in range(TOP_K):
                pltpu.make_async_copy(rows_v, out_hbm.at[idx_bufs[k]], wsem).wait()

        read(0, bufs[0]).start()

        @pl.loop(0, n_chunks, step=2)
        def _(j):
            read(j + 1, bufs[1]).start()
            spread(j, bufs[0])

            @pl.when(j + 2 < n_chunks)
            def _():
                read(j + 2, bufs[0]).start()
            spread(j + 1, bufs[1])

        off_b = pl.multiple_of(_sc_worker_base(per_b), 8)
        pltpu.sync_copy(b_hbm.at[pl.ds(off_b, per_b)], rowsb)
        for k in range(TOP_K):
            pltpu.sync_copy(idx_hbm.at[pl.ds(pl.multiple_of(k * n + na + off_b, 8), per_b)], ib)
            pltpu.async_copy(rowsb, out_hbm.at[ib], wsem).wait()

    return scatter(rows_a, rows_b, idx_flat)


def _sc_gather_rows(table, idx_flat):
    b = idx_flat.shape[0]
    width, dtype = table.shape[1], table.dtype
    per_worker = b // SC_WORKERS
    chunk = max(c for c in range(8, SC_GATHER_CHUNK + 1, 8) if per_worker % (2 * c) == 0)
    n_chunks = per_worker // chunk
    assert per_worker * SC_WORKERS == b
    mesh = plsc.VectorSubcoreMesh(core_axis_name="c", subcore_axis_name="s")

    @functools.partial(
        pl.kernel, mesh=mesh,
        out_type=jax.ShapeDtypeStruct((b, width), dtype),
        scratch_types=[pltpu.VMEM((chunk,), jnp.int32), pltpu.VMEM((chunk,), jnp.int32),
                       pltpu.VMEM((chunk, width), dtype), pltpu.VMEM((chunk, width), dtype),
                       pltpu.SemaphoreType.DMA, pltpu.SemaphoreType.DMA],
    )
    def gather(table_hbm, idx_hbm, out_hbm, idx0, idx1, rows0, rows1, sem0, sem1):
        base = _sc_worker_base(per_worker)
        bufs = ((idx0, rows0, sem0), (idx1, rows1, sem1))

        def off(j):
            return pl.multiple_of(base + j * chunk, 8)

        def start(j, buf):
            idx_v, rows_v, sem = buf
            pltpu.sync_copy(idx_hbm.at[pl.ds(off(j), chunk)], idx_v)
            pltpu.make_async_copy(table_hbm.at[idx_v], rows_v, sem).start()

        def finish(j, buf):
            idx_v, rows_v, sem = buf
            pltpu.make_async_copy(table_hbm.at[idx_v], rows_v, sem).wait()
            pltpu.sync_copy(rows_v, out_hbm.at[pl.ds(off(j), chunk)])

        start(0, bufs[0])

        @pl.loop(0, n_chunks, step=2)
        def _(j):
            start(j + 1, bufs[1])
            finish(j, bufs[0])

            @pl.when(j + 2 < n_chunks)
            def _():
                start(j + 2, bufs[0])
            finish(j + 1, bufs[1])

    return gather(table, idx_flat)


def _moe_kernel(first_ref, nblk_ref, cnt_ref, xs_hbm, wgu_ref, bgu_ref, wd_ref, bd_ref, out_hbm,
                xbuf, obuf, wgu_bf, wd_bf, xsem, osem):
    e = pl.program_id(0)
    first, nblk, cnt = first_ref[e], nblk_ref[e], cnt_ref[e]

    def rows(j):
        return pl.ds(pl.multiple_of((first + j) * MOE_ROWS, MOE_ROWS), MOE_ROWS)

    def x_copy(j, slot):
        return pltpu.make_async_copy(xs_hbm.at[rows(j)], xbuf.at[slot], xsem.at[slot])

    def o_copy(j, slot):
        return pltpu.make_async_copy(obuf.at[slot], out_hbm.at[rows(j)], osem.at[slot])

    @pl.when(nblk > 0)
    def _():
        x_copy(0, 0).start(priority=ROW_DMA_PRIORITY)
        wgu_bf[...] = wgu_ref[...].astype(BF16)
        wd_bf[...] = wd_ref[...].astype(BF16)

        def block(j, carry):
            slot = j % 2
            x_copy(j, slot).wait()

            @pl.when(j + 1 < nblk)
            def _():
                x_copy(j + 1, 1 - slot).start(priority=ROW_DMA_PRIORITY)

            @pl.when(j >= 2)
            def _():
                o_copy(j - 2, slot).wait()

            row = lax.broadcasted_iota(jnp.int32, (MOE_ROWS, 1), 0)
            x = _unpack_rows(jnp.where(row < cnt - j * MOE_ROWS, xbuf[slot], 0)).astype(BF16)
            gu = _dot(x, wgu_bf[...]) + bgu_ref[...]
            gate = jnp.minimum(gu[:, :D_FF], SWIGLU_LIMIT)
            up = jnp.clip(gu[:, D_FF:], -SWIGLU_LIMIT, SWIGLU_LIMIT)
            act = gate * jax.nn.sigmoid(SWIGLU_ALPHA * gate) * (up + 1.0)
            obuf[slot] = _pack_rows(_dot(act.astype(BF16), wd_bf[...]) + bd_ref[...])
            o_copy(j, slot).start(priority=ROW_DMA_PRIORITY)
            return carry

        lax.fori_loop(0, nblk, block, 0)

        @pl.when(nblk >= 2)
        def _():
            o_copy(nblk - 2, nblk % 2).wait()
        o_copy(nblk - 1, (nblk - 1) % 2).wait()


def _moe(xs, first_block, n_blocks, counts, w_gu, b_gu, w_down, b_down):
    grid_spec = pltpu.PrefetchScalarGridSpec(
        num_scalar_prefetch=3,
        grid=(N_EXPERTS,),
        in_specs=[pl.BlockSpec(memory_space=pl.ANY),
                  pl.BlockSpec((None, D, 2 * D_FF), lambda e, *_: (e, 0, 0)),
                  pl.BlockSpec((None, 1, 2 * D_FF), lambda e, *_: (e, 0, 0)),
                  pl.BlockSpec((None, D_FF, D), lambda e, *_: (e, 0, 0)),
                  pl.BlockSpec((None, 1, D), lambda e, *_: (e, 0, 0))],
        out_specs=pl.BlockSpec(memory_space=pl.ANY),
        scratch_shapes=[pltpu.VMEM((2, MOE_ROWS, PACKED), jnp.int32), pltpu.VMEM((2, MOE_ROWS, PACKED), jnp.int32),
                        pltpu.VMEM((D, 2 * D_FF), BF16), pltpu.VMEM((D_FF, D), BF16),
                        pltpu.SemaphoreType.DMA((2,)), pltpu.SemaphoreType.DMA((2,))],
    )
    return pl.pallas_call(
        _moe_kernel,
        grid_spec=grid_spec,
        out_shape=jax.ShapeDtypeStruct(xs.shape, jnp.int32),
        compiler_params=pltpu.CompilerParams(
            dimension_semantics=("arbitrary",), vmem_limit_bytes=MOE_VMEM_LIMIT),
        name="moe",
    )(first_block, n_blocks, counts, xs, w_gu, b_gu.reshape(N_EXPERTS, 1, 2 * D_FF), w_down,
      b_down.reshape(N_EXPERTS, 1, D))


def _final_kernel(x1_ref, yk_ref, gate_ref, gt2_ref, g_ref, *rest):
    o_ref = rest[-1]
    y2 = gate_ref[:, 0:1] * _unpack_rows(yk_ref[0])
    for k in range(1, TOP_K):
        y2 = y2 + gate_ref[:, k:k + 1] * _unpack_rows(yk_ref[k])
    o_ref[...] = _rms(x1_ref[...] + gt2_ref[...] * y2) * g_ref[...]


def _final(x1, yk_part, gates, tile0, mod, per_token, tokens_per_seq, g_final, y_prev=None):
    m = x1.shape[0]
    rows = yk_part.shape[1]
    tm = min(ROW_TILE, rows)
    assert rows % tm == 0 and (per_token or tokens_per_seq % tm == 0)
    here = lambda i: (i + tile0, 0)
    in_specs = [pl.BlockSpec((tm, D), here),
                pl.BlockSpec((TOP_K, tm, PACKED), lambda i: (0, i, 0)),
                pl.BlockSpec((tm, TOP_K), here),
                _mod_spec(per_token, tm, tokens_per_seq, 5, tile0),
                pl.BlockSpec((1, D), lambda i: (0, 0))]
    args = [x1, yk_part, gates, mod, g_final.reshape(1, D)]
    aliases = {}
    if y_prev is not None:
        in_specs.append(pl.BlockSpec(memory_space=pl.ANY))
        args.append(y_prev)
        aliases = {len(args) - 1: 0}
    return pl.pallas_call(
        _final_kernel,
        grid=(rows // tm,),
        in_specs=in_specs,
        out_specs=pl.BlockSpec((tm, D), here),
        out_shape=jax.ShapeDtypeStruct((m, D), F32),
        input_output_aliases=aliases,
        compiler_params=pltpu.CompilerParams(
            dimension_semantics=("arbitrary",), vmem_limit_bytes=VMEM_LIMIT),
        name="final",
    )(*args)


def kernel(x_prompt, x_sample, state_C, state_n, state_m, c_prompt, c_sample, w_ada, b_ada, g_mix, w_in,
           b_if, g_head, g_sgu, b_sgu, w_s, b_s, w_pa, w_pb, w_out, g_ffn, w_router, b_router, w_gu, b_gu,
           w_down, b_down, g_final):
    depth = w_ada.shape[0]
    assert depth == 1
    bp, tp, _ = x_prompt.shape
    bs, ts, _ = x_sample.shape
    mp, ms = bp * tp, bs * ts
    assert tp % ML_CHUNK == 0 and ts <= ML_CHUNK and GM_CHUNK % ts == 0

    w_in0 = w_in[0]
    nqkvo = 4 * D
    w_qkvo = w_in0[:, :nqkvo].astype(BF16)
    w_gate = w_in0[:, nqkvo + 2 * HEADS:].astype(BF16)
    w_if = jnp.pad(w_in0[:, nqkvo:nqkvo + 2 * HEADS], ((0, 0), (0, LANES - 2 * HEADS)))
    b_if_p = jnp.pad(b_if[0], (0, LANES - 2 * HEADS)).reshape(1, LANES)
    reps = GM_CHUNK // ts
    eye_r = jnp.eye(reps, dtype=F32)
    w_s_sample = jnp.einsum("ab,gts->gatbs", eye_r, w_s[0][:, :ts, :ts]).reshape(GROUPS, GM_CHUNK, GM_CHUNK)
    b_s_prompt = b_s[0].T
    b_s_sample = jnp.tile(b_s[0][:, :ts].T, (reps, 1))
    mix_p = {
        "g_sgu": g_sgu[0].reshape(1, D), "b_sgu": b_sgu[0].reshape(1, D),
        "w_pa": w_pa[0].astype(BF16), "w_pb": w_pb[0].astype(BF16), "w_out": w_out[0].astype(BF16),
        "g_ffn": g_ffn[0].reshape(1, D),
        "w_router": jnp.pad(w_router[0], ((0, 0), (0, LANES - N_EXPERTS))),
        "b_router": jnp.pad(b_router[0], (0, LANES - N_EXPERTS)).reshape(1, LANES),
    }
    mix_prompt = dict(mix_p, w_s=w_s[0], b_s=b_s_prompt)
    mix_sample = dict(mix_p, w_s=w_s_sample, b_s=b_s_sample)

    mod = _ada(jnp.concatenate([jnp.repeat(c_sample, ts, axis=0), c_prompt], axis=0), w_ada[0], b_ada[0])
    mod_s = mod
    mod_p = mod[ms:].reshape(bp, 1, N_MOD * D)

    xp = x_prompt.reshape(mp, D)
    xs = x_sample.reshape(ms, D)
    z_p, _, gt_p = _in_proj(xp, mod_p, False, tp, g_mix[0], w_qkvo, w_gate, w_if, b_if_p, 512, BF16)
    z_s, gc_s, _ = _in_proj(xs, mod_s, True, ts, g_mix[0], w_qkvo, w_gate, w_if, b_if_p, 128, F32)

    hm_p, C_p, n_p, m_p = _mlstm_prompt(z_p, gt_p, g_head[0], bp, tp)
    hm_p = hm_p.reshape(mp, D)
    m0_tok = jnp.repeat(state_m[0], ts, axis=0)
    hm_s, C_s, n_s, m_s = _mlstm_sample(z_s, gc_s, m0_tok, state_C[0], state_n[0], g_head[0], bs, ts)

    x1_p, h2_p, ch_p, gates_p, _, cnt_p = _mix(xp, hm_p, z_p, mod_p, False, tp, 256, mix_prompt, False)
    x1_s, h2_s, ch_s, gates_s, vg_s, cnt_s = _mix(xs, hm_s, z_s, mod_s, True, ts, 128, mix_sample, True)

    n_tok = mp + ms
    n_blocks = -(-(n_tok * TOP_K + N_EXPERTS * (MOE_ROWS - 1)) // MOE_ROWS)
    dest, ex_table = _route(ch_p, ch_s, cnt_p, cnt_s)
    dest_kmajor = dest.T.reshape(TOP_K * n_tok)
    xslots = _sc_scatter_rows(h2_p, h2_s, dest_kmajor, n_blocks * MOE_ROWS)
    yb = _moe(xslots, ex_table[0, :N_EXPERTS], ex_table[1, :N_EXPERTS], ex_table[2, :N_EXPERTS],
              w_gu[0], b_gu[0], w_down[0], b_down[0])
    dest_t = dest.T
    part = mp // FINAL_PARTS
    y_p = None
    for j in range(FINAL_PARTS):
        idx = dest_t[:, j * part:(j + 1) * part].reshape(TOP_K * part)
        yk = _sc_gather_rows(yb, idx).reshape(TOP_K, part, PACKED)
        y_p = _final(x1_p, yk, gates_p, j * part // min(ROW_TILE, part), mod_p, False, tp, g_final, y_p)
    yk = _sc_gather_rows(yb, dest_t[:, mp:].reshape(TOP_K * ms)).reshape(TOP_K, ms, PACKED)
    y_s = _final(x1_s, yk, gates_s, 0, mod_s, True, ts, g_final)

    return (y_p.reshape(bp, tp, D), y_s.reshape(bs, ts, D),
            C_p[None], n_p.reshape(1, bp, HEADS, HD), m_p[:, :, 0, 0][None],
            C_s[None], n_s.reshape(1, bs, HEADS, HD),
            m_s.reshape(bs, ts, HEADS, LANES)[:, 0, :, 0][None],
            vg_s.reshape(1, bs, ts, D))
```

```python
import functools

import jax
import jax.numpy as jnp
from jax import lax
from jax.experimental import pallas as pl
from jax.experimental.pallas import tpu as pltpu
from jax.experimental.pallas import tpu_sc as plsc

F32 = jnp.float32
BF16 = jnp.bfloat16

D = 1024
HEADS = 4
HD = D // HEADS
ML_CHUNK = 512
GROUPS = 4
GD = D // GROUPS
GM_CHUNK = 128
N_EXPERTS = 32
TOP_K = 4
D_FF = D
SWIGLU_LIMIT = 7.0
SWIGLU_ALPHA = 1.702
NORM_EPS = 1e-6
N_MOD = 6
PACKED = D // 2
LANES = 128
SC_CORES = 2
SC_SUBCORES = 16
SC_WORKERS = SC_CORES * SC_SUBCORES
IN_COLS = 2048
ROW_TILE = 512
MOE_ROWS = 256
MOE_X_BUFS = 4
ROW_DMA_PRIORITY = 1
MLSTM_SEQS = 1
SAMPLE_SEQS = 8
SC_SCATTER_CHUNK = 64
SC_GATHER_CHUNK = 96
VMEM_LIMIT = 48 * 1024 * 1024
MOE_VMEM_LIMIT = 56 * 1024 * 1024


def _dot(a, b):
    return jnp.dot(a, b, preferred_element_type=F32)


def _dot_nt(a, b):
    return lax.dot_general(a, b, (((1,), (1,)), ((), ())), preferred_element_type=F32)


def _dot_tn(a, b):
    return lax.dot_general(a, b, (((0,), (0,)), ((), ())), preferred_element_type=F32)


def _split_bf16(a):
    hi = a.astype(BF16)
    lo = (a - hi.astype(F32)).astype(BF16)
    return hi, lo


def _dot3(a, b):
    ah, al = _split_bf16(a)
    bh, bl = _split_bf16(b)
    return _dot(ah, bh) + (_dot(ah, bl) + _dot(al, bh))


def _log_sigmoid(x):
    return jnp.minimum(x, 0.0) - jnp.log1p(jnp.exp(-jnp.abs(x)))


def _rms(x):
    return x * lax.rsqrt(jnp.mean(x * x, axis=-1, keepdims=True) + NORM_EPS)


def _pack_rows(x):
    bits = lax.bitcast_convert_type(x.astype(BF16).astype(F32), jnp.uint32)
    word = (bits[:, :PACKED] & jnp.uint32(0xFFFF0000)) | (bits[:, PACKED:] >> 16)
    return lax.bitcast_convert_type(word, jnp.int32)


def _unpack_rows(w):
    bits = lax.bitcast_convert_type(w, jnp.uint32)
    left = lax.bitcast_convert_type(bits & jnp.uint32(0xFFFF0000), F32)
    right = lax.bitcast_convert_type(bits << 16, F32)
    return jnp.concatenate([left, right], axis=1)


def _mod_spec(per_token, tm, tokens_per_seq, col):
    if per_token:
        return pl.BlockSpec((tm, D), lambda i, *_: (i, col))
    return pl.BlockSpec((None, 1, D), lambda i, *_: ((i * tm) // tokens_per_seq, 0, col))


def _ada_kernel(c_ref, w_ref, b_ref, o_ref):
    c = c_ref[...]
    s = (c * jax.nn.sigmoid(c)).astype(BF16)
    o_ref[...] = _dot(s, w_ref[...].astype(BF16)) + b_ref[...]


def _ada(c, w, b):
    m, n = c.shape[0], w.shape[1]
    tn = 512
    return pl.pallas_call(
        _ada_kernel,
        grid=(n // tn,),
        in_specs=[pl.BlockSpec((m, D), lambda j: (0, 0)),
                  pl.BlockSpec((D, tn), lambda j: (0, j)),
                  pl.BlockSpec((1, tn), lambda j: (0, j))],
        out_specs=pl.BlockSpec((m, tn), lambda j: (0, j)),
        out_shape=jax.ShapeDtypeStruct((m, n), F32),
        name="ada",
    )(c, w, b.reshape(1, n))


def _in_kernel(x_ref, g_ref, sh_ref, sc_ref, wa_ref, wb_ref, wif_ref, bif_ref, z_ref, gc_ref, gt_ref):
    h = (_rms(x_ref[...]) * g_ref[...]) * (1.0 + sc_ref[...]) + sh_ref[...]
    hb = h.astype(BF16)
    gates = _dot3(h, wif_ref[...]) + bif_ref[...]
    gc_ref[...] = gates
    gt_ref[...] = gates.T[:2 * HEADS, :]
    half = wa_ref.shape[1]
    for w_ref, col0 in ((wa_ref, 0), (wb_ref, half)):
        for c in range(half // IN_COLS):
            cols = slice(c * IN_COLS, (c + 1) * IN_COLS)
            z_ref[:, col0 + c * IN_COLS:col0 + (c + 1) * IN_COLS] = _dot(hb, w_ref[:, cols]).astype(z_ref.dtype)


def _in_proj(x, mod, per_token, tokens_per_seq, g_mix, w_a, w_b, w_if, b_if, tm, z_dtype):
    m = x.shape[0]
    half = w_a.shape[1]
    gt_len = tokens_per_seq if tokens_per_seq % tm == 0 else m
    gt_tiles = gt_len // tm
    resident = functools.partial(pl.BlockSpec, index_map=lambda i: (0, 0), pipeline_mode=pl.Buffered(1))
    return pl.pallas_call(
        _in_kernel,
        grid=(m // tm,),
        in_specs=[pl.BlockSpec((tm, D), lambda i: (i, 0)),
                  pl.BlockSpec((1, D), lambda i: (0, 0)),
                  _mod_spec(per_token, tm, tokens_per_seq, 0),
                  _mod_spec(per_token, tm, tokens_per_seq, 1),
                  resident((D, half)), resident((D, half)),
                  pl.BlockSpec((D, LANES), lambda i: (0, 0)),
                  pl.BlockSpec((1, LANES), lambda i: (0, 0))],
        out_specs=[pl.BlockSpec((tm, 2 * half), lambda i: (i, 0)),
                   pl.BlockSpec((tm, LANES), lambda i: (i, 0)),
                   pl.BlockSpec((None, 2 * HEADS, tm), lambda i: (i // gt_tiles, 0, i % gt_tiles))],
        out_shape=[jax.ShapeDtypeStruct((m, 2 * half), z_dtype),
                   jax.ShapeDtypeStruct((m, LANES), F32),
                   jax.ShapeDtypeStruct((m // gt_len, 2 * HEADS, gt_len), F32)],
        compiler_params=pltpu.CompilerParams(
            dimension_semantics=("arbitrary",), vmem_limit_bytes=MOE_VMEM_LIMIT),
        name="in_proj",
    )(x, g_mix.reshape(1, D), mod, mod, w_a, w_b, w_if, b_if)


def _mlstm_prompt_kernel(q_ref, k_ref, v_ref, o_ref, gt_ref, gh_ref, hm_ref, C_ref, n_ref, m_ref):
    nseq, L = q_ref.shape[0], q_ref.shape[1]

    @pl.when(pl.program_id(1) == 0)
    def _():
        C_ref[...] = jnp.zeros_like(C_ref)
        n_ref[...] = jnp.zeros_like(n_ref)
        m_ref[...] = jnp.zeros_like(m_ref)

    r = lax.broadcasted_iota(jnp.int32, (L, L), 0)
    s = lax.broadcasted_iota(jnp.int32, (L, L), 1)
    eye = r == s
    causal = s <= r

    def to_col(x_row):
        return jnp.sum(jnp.where(eye, x_row, 0.0), axis=1, keepdims=True)

    for b in range(nseq):
        gates = gt_ref[b]
        for h in range(HEADS):
            cols = slice(h * HD, (h + 1) * HD)
            ig_row = gates[h:h + 1, :]
            lf_row = _log_sigmoid(gates[HEADS + h:HEADS + h + 1, :])
            lf_col = to_col(lf_row)
            b_row = jnp.sum(jnp.where(r <= s, lf_col, 0.0), axis=0, keepdims=True)
            b_col = to_col(b_row)
            m_prev = m_ref[b, h][:, :1]

            logD = jnp.where(causal, b_col - b_row + ig_row, -jnp.inf)
            inter = b_col + m_prev
            mt = jnp.maximum(jnp.max(logD, axis=1, keepdims=True), inter)
            q = q_ref[b, :, cols]
            ks = k_ref[b, :, cols] * (HD ** -0.5)
            v = v_ref[b, :, cols]
            S = _dot_nt(q, ks) * jnp.exp(logD - mt)
            w_int = jnp.exp(inter - mt)
            Cmat = C_ref[b, h]
            nvec = n_ref[b, h]
            num = _dot(S.astype(BF16), v) + w_int * _dot_nt(q, Cmat.astype(BF16))
            nq = jnp.sum(q.astype(F32) * nvec, axis=1, keepdims=True)
            den = jnp.sum(S, axis=1, keepdims=True) + w_int * nq
            hh = num / jnp.maximum(jnp.abs(den), jnp.exp(-mt))
            hg = jax.nn.sigmoid(o_ref[b, :, cols].astype(F32)) * hh
            hm_ref[b, :, cols] = (_rms(hg) * gh_ref[h]).astype(hm_ref.dtype)

            bL = b_row[:, L - 1:L]
            g_row = bL - b_row + ig_row
            m_new = jnp.maximum(bL + m_prev, jnp.max(g_row, axis=1, keepdims=True))
            w_old = jnp.exp(bL + m_prev - m_new)
            kw = ks.astype(F32) * to_col(jnp.exp(g_row - m_new))
            C_ref[b, h] = w_old * Cmat + _dot_tn(v, kw.astype(BF16))
            n_ref[b, h] = w_old * nvec + jnp.sum(kw, axis=0, keepdims=True)
            m_ref[b, h] = jnp.broadcast_to(m_new, (1, LANES))


def _mlstm_prompt(z, gates_t, g_head, batch, seq):
    nc = seq // ML_CHUNK
    nb = MLSTM_SEQS
    z3 = z.reshape(batch, seq, z.shape[1])

    def zspec(col):
        return pl.BlockSpec((nb, ML_CHUNK, D), lambda b, c: (b, c, col))

    def state(last):
        return pl.BlockSpec((nb, HEADS) + last, lambda b, c: (b, 0, 0, 0))

    return pl.pallas_call(
        _mlstm_prompt_kernel,
        grid=(batch // nb, nc),
        in_specs=[zspec(0), zspec(1), zspec(2), zspec(3),
                  pl.BlockSpec((nb, 2 * HEADS, ML_CHUNK), lambda b, c: (b, 0, c)),
                  pl.BlockSpec((HEADS, 1, HD), lambda b, c: (0, 0, 0))],
        out_specs=[pl.BlockSpec((nb, ML_CHUNK, D), lambda b, c: (b, c, 0)),
                   state((HD, HD)), state((1, HD)), state((1, LANES))],
        out_shape=[jax.ShapeDtypeStruct((batch, seq, D), BF16),
                   jax.ShapeDtypeStruct((batch, HEADS, HD, HD), F32),
                   jax.ShapeDtypeStruct((batch, HEADS, 1, HD), F32),
                   jax.ShapeDtypeStruct((batch, HEADS, 1, LANES), F32)],
        compiler_params=pltpu.CompilerParams(dimension_semantics=("arbitrary", "arbitrary")),
        name="mlstm_prompt",
    )(z3, z3, z3, z3, gates_t, g_head.reshape(HEADS, 1, HD))


def _mlstm_sample_kernel(seq_len, q_ref, k_ref, v_ref, o_ref, gc_ref, m0_ref, C0_ref, n0_ref, gh_ref,
                         hm_ref, C_ref, n_ref, m_ref):
    R = q_ref.shape[0]
    nseq = R // seq_len
    r = lax.broadcasted_iota(jnp.int32, (R, R), 0)
    s = lax.broadcasted_iota(jnp.int32, (R, R), 1)
    rseq = lax.broadcasted_iota(jnp.int32, (R, 1), 0) // seq_len
    eye = r == s
    same = (r // seq_len) == (s // seq_len)
    causal = same & (s <= r)

    def to_row(x_col):
        return jnp.sum(jnp.where(eye, x_col, 0.0), axis=0, keepdims=True)

    gc = gc_ref[...]
    for h in range(HEADS):
        cols = slice(h * HD, (h + 1) * HD)
        ig_col = gc[:, h:h + 1]
        lf_col = _log_sigmoid(gc[:, HEADS + h:HEADS + h + 1])
        lf_row = to_row(lf_col)
        b_col = jnp.sum(jnp.where(causal, lf_row, 0.0), axis=1, keepdims=True)
        bL_col = jnp.sum(jnp.where(same, lf_row, 0.0), axis=1, keepdims=True)
        b_row = to_row(b_col)
        ig_row = to_row(ig_col)
        m0_col = m0_ref[:, h:h + 1]

        logD = jnp.where(causal, b_col - b_row + ig_row, -jnp.inf)
        inter = b_col + m0_col
        mt = jnp.maximum(jnp.max(logD, axis=1, keepdims=True), inter)
        qf = q_ref[:, cols]
        q = qf.astype(BF16)
        ksf = k_ref[:, cols] * (HD ** -0.5)
        v = v_ref[:, cols].astype(BF16)
        S = _dot_nt(q, ksf.astype(BF16)) * jnp.exp(logD - mt)
        w_int = jnp.exp(inter - mt)

        Cq = jnp.zeros((R, HD), F32)
        nq = jnp.zeros((R, 1), F32)
        for g in range(nseq):
            Cq = jnp.where(rseq == g, _dot_nt(q, C0_ref[g, h].astype(BF16)), Cq)
            nq = jnp.where(rseq == g, jnp.sum(qf * n0_ref[g, h], axis=1, keepdims=True), nq)
        num = _dot(S.astype(BF16), v) + w_int * Cq
        den = jnp.sum(S, axis=1, keepdims=True) + w_int * nq
        hh = num / jnp.maximum(jnp.abs(den), jnp.exp(-mt))
        hg = jax.nn.sigmoid(o_ref[:, cols]) * hh
        hm_ref[:, cols] = (_rms(hg) * gh_ref[h]).astype(hm_ref.dtype)

        g_col = bL_col - b_col + ig_col
        gmax_col = jnp.max(jnp.where(same, to_row(g_col), -jnp.inf), axis=1, keepdims=True)
        m_new_col = jnp.maximum(bL_col + m0_col, gmax_col)
        w_old_col = jnp.exp(bL_col + m0_col - m_new_col)
        kw = ksf * jnp.exp(g_col - m_new_col)
        for g in range(nseq):
            kw_g = jnp.where(rseq == g, kw, 0.0)
            w_old = w_old_col[g * seq_len:g * seq_len + 1, :]
            C_ref[g, h] = w_old * C0_ref[g, h] + _dot_tn(v, kw_g.astype(BF16))
            n_ref[g, h] = w_old * n0_ref[g, h] + jnp.sum(kw_g, axis=0, keepdims=True)
        m_ref[:, h * LANES:(h + 1) * LANES] = jnp.broadcast_to(m_new_col, (R, LANES))


def _mlstm_sample(z, gates_c, m0_tok, C0, n0, g_head, batch, seq):
    rows = SAMPLE_SEQS * seq
    m = batch * seq

    def zspec(col):
        return pl.BlockSpec((rows, D), lambda i: (i, col))

    state_c = pl.BlockSpec((SAMPLE_SEQS, HEADS, HD, HD), lambda i: (i, 0, 0, 0))
    state_n = pl.BlockSpec((SAMPLE_SEQS, HEADS, 1, HD), lambda i: (i, 0, 0, 0))
    return pl.pallas_call(
        functools.partial(_mlstm_sample_kernel, seq),
        grid=(batch // SAMPLE_SEQS,),
        in_specs=[zspec(0), zspec(1), zspec(2), zspec(3),
                  pl.BlockSpec((rows, LANES), lambda i: (i, 0)),
                  pl.BlockSpec((rows, HEADS), lambda i: (i, 0)),
                  state_c, state_n,
                  pl.BlockSpec((HEADS, 1, HD), lambda i: (0, 0, 0))],
        out_specs=[pl.BlockSpec((rows, D), lambda i: (i, 0)),
                   state_c, state_n,
                   pl.BlockSpec((rows, HEADS * LANES), lambda i: (i, 0))],
        out_shape=[jax.ShapeDtypeStruct((m, D), BF16),
                   jax.ShapeDtypeStruct((batch, HEADS, HD, HD), F32),
                   jax.ShapeDtypeStruct((batch, HEADS, 1, HD), F32),
                   jax.ShapeDtypeStruct((m, HEADS * LANES), F32)],
        compiler_params=pltpu.CompilerParams(
            dimension_semantics=("arbitrary",), vmem_limit_bytes=VMEM_LIMIT),
        name="mlstm_sample",
    )(z, z, z, z, gates_c, m0_tok, C0, n0.reshape(batch, HEADS, 1, HD), g_head.reshape(HEADS, 1, HD))


def _mix_kernel(x_ref, hm_ref, u_ref, v_ref, ga_ref, gb_ref, gt1_ref, sh2_ref, sc2_ref,
                gsgu_ref, bsgu_ref, ws_ref, bs_ref, wpa_ref, wpb_ref, wout_ref, gffn_ref,
                wr_ref, br_ref, x1_ref, h2_ref, choice_ref, gate_ref, vg_ref, cnt_ref, yg_scr):
    tm = x_ref.shape[0]
    u = jax.nn.gelu(u_ref[...].astype(F32))
    vv = jax.nn.gelu(v_ref[...].astype(F32))
    mu = jnp.mean(vv, axis=-1, keepdims=True)
    var = jnp.mean(jnp.square(vv - mu), axis=-1, keepdims=True)
    vg = (vv - mu) * lax.rsqrt(var + NORM_EPS) * gsgu_ref[...] + bsgu_ref[...]
    vg_ref[...] = vg
    vgb = vg.astype(BF16)

    r = lax.broadcasted_iota(jnp.int32, (GM_CHUNK, GM_CHUNK), 0)
    s = lax.broadcasted_iota(jnp.int32, (GM_CHUNK, GM_CHUNK), 1)
    for g in range(GROUPS):
        w = jnp.where(s <= r, ws_ref[g], 0.0).astype(BF16)
        bias = bs_ref[:, g:g + 1]
        for c in range(tm // GM_CHUNK):
            rows = slice(c * GM_CHUNK, (c + 1) * GM_CHUNK)
            cols = slice(g * GD, (g + 1) * GD)
            mixed = _dot(w, vgb[rows, cols]) + bias
            yg_scr[rows, cols] = (u[rows, cols] * mixed).astype(BF16)

    a = _dot(hm_ref[...], wpa_ref[...])
    b = _dot(yg_scr[...], wpb_ref[...])
    merged = (jax.nn.sigmoid(ga_ref[...].astype(F32)) * a
              + jax.nn.sigmoid(gb_ref[...].astype(F32)) * b)
    x1 = x_ref[...] + gt1_ref[...] * _dot(merged.astype(BF16), wout_ref[...])
    x1_ref[...] = x1
    h2 = (_rms(x1) * gffn_ref[...]) * (1.0 + sc2_ref[...]) + sh2_ref[...]
    h2_ref[...] = _pack_rows(h2)

    lane = lax.broadcasted_iota(jnp.int32, (tm, LANES), 1)
    lane_f = lane.astype(F32)
    lg = jnp.where(lane < N_EXPERTS, _dot3(h2, wr_ref[...]) + br_ref[...], -jnp.inf)
    choice = jnp.zeros((tm, LANES), F32)
    vals = []
    for k in range(TOP_K):
        mx = jnp.max(lg, axis=1, keepdims=True)
        sel = lane_f == jnp.min(jnp.where(lg == mx, lane_f, float(LANES)), axis=1, keepdims=True)
        choice = jnp.where(sel, k + 1.0, choice)
        vals.append(mx)
        lg = jnp.where(sel, -jnp.inf, lg)
    choice_ref[...] = choice
    ex = [jnp.exp(v - vals[0]) for v in vals]
    denom = sum(ex)
    gates = jnp.zeros((tm, LANES), F32)
    for k in range(TOP_K):
        gates = jnp.where(lane == k, ex[k] / denom, gates)
    gate_ref[...] = gates[:, :TOP_K]

    @pl.when(pl.program_id(0) == 0)
    def _():
        cnt_ref[...] = jnp.zeros_like(cnt_ref)

    cnt_ref[...] += jnp.sum(jnp.where(choice > 0.0, 1.0, 0.0), axis=0, keepdims=True)


def _mix(x, hm, z, mod, per_token, tokens_per_seq, tm, p, keep_v_rows):
    m = x.shape[0]
    zcol = 4
    v_rows = m if keep_v_rows else tm
    v_spec = pl.BlockSpec((tm, D), (lambda i: (i, 0)) if keep_v_rows else (lambda i: (0, 0)))

    def zspec(blk):
        return pl.BlockSpec((tm, D), lambda i: (i, blk))

    def full(shape):
        return pl.BlockSpec(shape, lambda i: (0,) * len(shape))

    row = pl.BlockSpec((tm, D), lambda i: (i, 0))
    return pl.pallas_call(
        _mix_kernel,
        grid=(m // tm,),
        in_specs=[row, row, zspec(zcol), zspec(zcol + 1), zspec(zcol + 2), zspec(zcol + 3),
                  _mod_spec(per_token, tm, tokens_per_seq, 2),
                  _mod_spec(per_token, tm, tokens_per_seq, 3),
                  _mod_spec(per_token, tm, tokens_per_seq, 4),
                  full((1, D)), full((1, D)),
                  full((GROUPS, GM_CHUNK, GM_CHUNK)), full((GM_CHUNK, GROUPS)),
                  full((D, D)), full((D, D)), full((D, D)), full((1, D)),
                  full((D, LANES)), full((1, LANES))],
        out_specs=[row, pl.BlockSpec((tm, PACKED), lambda i: (i, 0)),
                   pl.BlockSpec((tm, LANES), lambda i: (i, 0)), pl.BlockSpec((tm, TOP_K), lambda i: (i, 0)),
                   v_spec, full((1, LANES))],
        out_shape=[jax.ShapeDtypeStruct((m, D), F32),
                   jax.ShapeDtypeStruct((m, PACKED), jnp.int32),
                   jax.ShapeDtypeStruct((m, LANES), F32),
                   jax.ShapeDtypeStruct((m, TOP_K), F32),
                   jax.ShapeDtypeStruct((v_rows, D), F32),
                   jax.ShapeDtypeStruct((1, LANES), F32)],
        scratch_shapes=[pltpu.VMEM((tm, D), BF16)],
        compiler_params=pltpu.CompilerParams(
            dimension_semantics=("arbitrary",), vmem_limit_bytes=VMEM_LIMIT),
        name="mix",
    )(x, hm, z, z, z, z, mod, mod, mod, p["g_sgu"], p["b_sgu"], p["w_s"], p["b_s"],
      p["w_pa"], p["w_pb"], p["w_out"], p["g_ffn"], p["w_router"], p["b_router"])


def _route_kernel(tiles_a, cha_ref, chb_ref, cnta_ref, cntb_ref, dest_ref, ex_ref, base_scr):
    tm = cha_ref.shape[0]
    lane = lax.broadcasted_iota(jnp.int32, (tm, LANES), 1)
    choice = jnp.where(pl.program_id(0) < tiles_a, cha_ref[...], chb_ref[...])
    onehot = jnp.where(choice > 0.0, 1.0, 0.0)

    @pl.when(pl.program_id(0) == 0)
    def _():
        cnt = cnta_ref[...] + cntb_ref[...]
        padded = jnp.floor((cnt + (MOE_ROWS - 1)) * (1.0 / MOE_ROWS)) * MOE_ROWS
        r = lax.broadcasted_iota(jnp.int32, (LANES, LANES), 0)
        s = lax.broadcasted_iota(jnp.int32, (LANES, LANES), 1)
        padded_col = jnp.sum(jnp.where(r == s, padded, 0.0), axis=1, keepdims=True)
        pstart = jnp.sum(jnp.where(r < s, padded_col, 0.0), axis=0, keepdims=True)
        base_scr[...] = pstart
        trow = lax.broadcasted_iota(jnp.int32, ex_ref.shape, 0)
        table = jnp.where(trow == 0, pstart * (1.0 / MOE_ROWS),
                          jnp.where(trow == 1, padded * (1.0 / MOE_ROWS), jnp.where(trow == 2, cnt, 0.0)))
        ex_ref[...] = table.astype(jnp.int32)

    r = lax.broadcasted_iota(jnp.int32, (tm, tm), 0)
    s = lax.broadcasted_iota(jnp.int32, (tm, tm), 1)
    before = _dot((s < r).astype(BF16), onehot.astype(BF16))
    slot = before + base_scr[...]
    dest = jnp.zeros((tm, LANES), F32)
    for k in range(TOP_K):
        d_k = jnp.sum(jnp.where(choice == k + 1.0, slot, 0.0), axis=1, keepdims=True)
        dest = jnp.where(lane == k, d_k, dest)
    dest_ref[...] = dest[:, :TOP_K].astype(jnp.int32)
    base_scr[...] += jnp.sum(onehot, axis=0, keepdims=True)


def _route(choice_a, choice_b, cnt_a, cnt_b):
    tm = min(ROW_TILE, choice_a.shape[0], choice_b.shape[0])
    assert choice_a.shape[0] % tm == 0 and choice_b.shape[0] % tm == 0
    tiles_a, tiles_b = choice_a.shape[0] // tm, choice_b.shape[0] // tm
    n = (tiles_a + tiles_b) * tm
    fixed = lambda i: (0, 0)
    return pl.pallas_call(
        functools.partial(_route_kernel, tiles_a),
        grid=(tiles_a + tiles_b,),
        in_specs=[pl.BlockSpec((tm, LANES), lambda i: (jnp.minimum(i, tiles_a - 1), 0)),
                  pl.BlockSpec((tm, LANES), lambda i: (jnp.maximum(i - tiles_a, 0), 0)),
                  pl.BlockSpec((1, LANES), fixed), pl.BlockSpec((1, LANES), fixed)],
        out_specs=[pl.BlockSpec((tm, TOP_K), lambda i: (i, 0)), pl.BlockSpec((8, LANES), fixed)],
        out_shape=[jax.ShapeDtypeStruct((n, TOP_K), jnp.int32),
                   jax.ShapeDtypeStruct((8, LANES), jnp.int32)],
        scratch_shapes=[pltpu.VMEM((1, LANES), F32)],
        compiler_params=pltpu.CompilerParams(dimension_semantics=("arbitrary",)),
        name="route",
    )(choice_a, choice_b, cnt_a, cnt_b)


def _sc_worker_base(per_worker):
    return (lax.axis_index("s") * SC_CORES + lax.axis_index("c")) * per_worker


def _sc_scatter_rows(rows_a, rows_b, idx_flat, n_out):
    na, nb = rows_a.shape[0], rows_b.shape[0]
    width, dtype = rows_a.shape[1], rows_a.dtype
    n = na + nb
    per_a, per_b = na // SC_WORKERS, nb // SC_WORKERS
    chunk = SC_SCATTER_CHUNK
    n_chunks = per_a // chunk
    assert per_a * SC_WORKERS == na and per_b * SC_WORKERS == nb and per_b % 8 == 0 and per_b <= chunk
    assert n_chunks * chunk == per_a and n_chunks % 2 == 0
    mesh = plsc.VectorSubcoreMesh(core_axis_name="c", subcore_axis_name="s")

    @functools.partial(
        pl.kernel, mesh=mesh,
        out_type=jax.ShapeDtypeStruct((n_out, width), dtype),
        scratch_types=[pltpu.VMEM((chunk,), jnp.int32)] * TOP_K + [pltpu.VMEM((per_b,), jnp.int32)]
                      + [pltpu.VMEM((chunk, width), dtype), pltpu.VMEM((chunk, width), dtype),
                         pltpu.VMEM((per_b, width), dtype)]
                      + [pltpu.SemaphoreType.DMA] * 3,
    )
    def scatter(a_hbm, b_hbm, idx_hbm, out_hbm, i0, i1, i2, i3, ib, rows0, rows1, rowsb, rsem0, rsem1, wsem):
        base = _sc_worker_base(per_a)
        idx_bufs = (i0, i1, i2, i3)
        bufs = ((rows0, rsem0), (rows1, rsem1))

        def off(j):
            return pl.multiple_of(base + j * chunk, 8)

        def read(j, buf):
            rows_v, sem = buf
            return pltpu.make_async_copy(a_hbm.at[pl.ds(off(j), chunk)], rows_v, sem)

        def spread(j, buf):
            rows_v, _ = buf
            read(j, buf).wait()
            for k in range(TOP_K):
                pltpu.sync_copy(idx_hbm.at[pl.ds(pl.multiple_of(k * n + off(j), 8), chunk)], idx_bufs[k])
            for k in range(TOP_K):
                pltpu.make_async_copy(rows_v, out_hbm.at[idx_bufs[k]], wsem).start()
            for k in range(TOP_K):
                pltpu.make_async_copy(rows_v, out_hbm.at[idx_bufs[k]], wsem).wait()

        read(0, bufs[0]).start()

        @pl.loop(0, n_chunks, step=2)
        def _(j):
            read(j + 1, bufs[1]).start()
            spread(j, bufs[0])

            @pl.when(j + 2 < n_chunks)
            def _():
                read(j + 2, bufs[0]).start()
            spread(j + 1, bufs[1])

        off_b = pl.multiple_of(_sc_worker_base(per_b), 8)
        pltpu.sync_copy(b_hbm.at[pl.ds(off_b, per_b)], rowsb)
        for k in range(TOP_K):
            pltpu.sync_copy(idx_hbm.at[pl.ds(pl.multiple_of(k * n + na + off_b, 8), per_b)], ib)
            pltpu.async_copy(rowsb, out_hbm.at[ib], wsem).wait()

    return scatter(rows_a, rows_b, idx_flat)


def _sc_gather_rows(table, idx_flat):
    b = idx_flat.shape[0]
    width, dtype = table.shape[1], table.dtype
    per_worker = b // SC_WORKERS
    chunk = SC_GATHER_CHUNK
    n_chunks = per_worker // chunk
    assert per_worker * SC_WORKERS == b and n_chunks * chunk == per_worker and n_chunks % 2 == 0
    mesh = plsc.VectorSubcoreMesh(core_axis_name="c", subcore_axis_name="s")

    @functools.partial(
        pl.kernel, mesh=mesh,
        out_type=jax.ShapeDtypeStruct((b, width), dtype),
        scratch_types=[pltpu.VMEM((chunk,), jnp.int32), pltpu.VMEM((chunk,), jnp.int32),
                       pltpu.VMEM((chunk, width), dtype), pltpu.VMEM((chunk, width), dtype),
                       pltpu.SemaphoreType.DMA, pltpu.SemaphoreType.DMA],
    )
    def gather(table_hbm, idx_hbm, out_hbm, idx0, idx1, rows0, rows1, sem0, sem1):
        base = _sc_worker_base(per_worker)
        bufs = ((idx0, rows0, sem0), (idx1, rows1, sem1))

        def off(j):
            return pl.multiple_of(base + j * chunk, 8)

        def start(j, buf):
            idx_v, rows_v, sem = buf
            pltpu.sync_copy(idx_hbm.at[pl.ds(off(j), chunk)], idx_v)
            pltpu.make_async_copy(table_hbm.at[idx_v], rows_v, sem).start()

        def finish(j, buf):
            idx_v, rows_v, sem = buf
            pltpu.make_async_copy(table_hbm.at[idx_v], rows_v, sem).wait()
            pltpu.sync_copy(rows_v, out_hbm.at[pl.ds(off(j), chunk)])

        start(0, bufs[0])

        @pl.loop(0, n_chunks, step=2)
        def _(j):
            start(j + 1, bufs[1])
            finish(j, bufs[0])

            @pl.when(j + 2 < n_chunks)
            def _():
                start(j + 2, bufs[0])
            finish(j + 1, bufs[1])

    return gather(table, idx_flat)


def _moe_kernel(first_ref, nblk_ref, cnt_ref, xs_hbm, wgu_ref, bgu_ref, wd_ref, bd_ref, out_hbm,
                xbuf, obuf, wgu_bf, wd_bf, xsem, osem):
    e = pl.program_id(0)
    first, nblk, cnt = first_ref[e], nblk_ref[e], cnt_ref[e]

    def rows(j):
        return pl.ds(pl.multiple_of((first + j) * MOE_ROWS, MOE_ROWS), MOE_ROWS)

    def x_copy(j, slot):
        return pltpu.make_async_copy(xs_hbm.at[rows(j)], xbuf.at[slot], xsem.at[slot])

    def o_copy(j, slot):
        return pltpu.make_async_copy(obuf.at[slot], out_hbm.at[rows(j)], osem.at[slot])

    @pl.when(nblk > 0)
    def _():
        for j0 in range(MOE_X_BUFS - 1):
            @pl.when(j0 < nblk)
            def _():
                x_copy(j0, j0).start(priority=ROW_DMA_PRIORITY)
        wgu_bf[...] = wgu_ref[...].astype(BF16)
        wd_bf[...] = wd_ref[...].astype(BF16)

        def block(j, carry):
            xslot = j % MOE_X_BUFS
            slot = j % 2
            x_copy(j, xslot).wait()

            @pl.when(j + MOE_X_BUFS - 1 < nblk)
            def _():
                x_copy(j + MOE_X_BUFS - 1, (j + MOE_X_BUFS - 1) % MOE_X_BUFS).start(priority=ROW_DMA_PRIORITY)

            @pl.when(j >= 2)
            def _():
                o_copy(j - 2, slot).wait()

            row = lax.broadcasted_iota(jnp.int32, (MOE_ROWS, 1), 0)
            x = _unpack_rows(jnp.where(row < cnt - j * MOE_ROWS, xbuf[xslot], 0)).astype(BF16)
            gu = _dot(x, wgu_bf[...]) + bgu_ref[...]
            gate = jnp.minimum(gu[:, :D_FF], SWIGLU_LIMIT)
            up = jnp.clip(gu[:, D_FF:], -SWIGLU_LIMIT, SWIGLU_LIMIT)
            act = gate * jax.nn.sigmoid(SWIGLU_ALPHA * gate) * (up + 1.0)
            obuf[slot] = _pack_rows(_dot(act.astype(BF16), wd_bf[...]) + bd_ref[...])
            o_copy(j, slot).start(priority=ROW_DMA_PRIORITY)
            return carry

        lax.fori_loop(0, nblk, block, 0)

        @pl.when(nblk >= 2)
        def _():
            o_copy(nblk - 2, nblk % 2).wait()
        o_copy(nblk - 1, (nblk - 1) % 2).wait()


def _moe(xs, first_block, n_blocks, counts, w_gu, b_gu, w_down, b_down):
    grid_spec = pltpu.PrefetchScalarGridSpec(
        num_scalar_prefetch=3,
        grid=(N_EXPERTS,),
        in_specs=[pl.BlockSpec(memory_space=pl.ANY),
                  pl.BlockSpec((None, D, 2 * D_FF), lambda e, *_: (e, 0, 0)),
                  pl.BlockSpec((None, 1, 2 * D_FF), lambda e, *_: (e, 0, 0)),
                  pl.BlockSpec((None, D_FF, D), lambda e, *_: (e, 0, 0)),
                  pl.BlockSpec((None, 1, D), lambda e, *_: (e, 0, 0))],
        out_specs=pl.BlockSpec(memory_space=pl.ANY),
        scratch_shapes=[pltpu.VMEM((MOE_X_BUFS, MOE_ROWS, PACKED), jnp.int32),
                        pltpu.VMEM((2, MOE_ROWS, PACKED), jnp.int32),
                        pltpu.VMEM((D, 2 * D_FF), BF16), pltpu.VMEM((D_FF, D), BF16),
                        pltpu.SemaphoreType.DMA((MOE_X_BUFS,)), pltpu.SemaphoreType.DMA((2,))],
    )
    return pl.pallas_call(
        _moe_kernel,
        grid_spec=grid_spec,
        out_shape=jax.ShapeDtypeStruct(xs.shape, jnp.int32),
        compiler_params=pltpu.CompilerParams(
            dimension_semantics=("arbitrary",), vmem_limit_bytes=MOE_VMEM_LIMIT),
        name="moe",
    )(first_block, n_blocks, counts, xs, w_gu, b_gu.reshape(N_EXPERTS, 1, 2 * D_FF), w_down,
      b_down.reshape(N_EXPERTS, 1, D))


def _final_kernel(x1_ref, yk_ref, gate_ref, gt2_ref, g_ref, o_ref):
    y2 = gate_ref[:, 0:1] * _unpack_rows(yk_ref[0])
    for k in range(1, TOP_K):
        y2 = y2 + gate_ref[:, k:k + 1] * _unpack_rows(yk_ref[k])
    o_ref[...] = _rms(x1_ref[...] + gt2_ref[...] * y2) * g_ref[...]


def _final(x1, yk, gates, row_off, mod, per_token, tokens_per_seq, g_final):
    m = x1.shape[0]
    tm = min(ROW_TILE, m)
    assert m % tm == 0 and row_off % tm == 0 and (per_token or tokens_per_seq % tm == 0)
    off = row_off // tm
    return pl.pallas_call(
        _final_kernel,
        grid=(m // tm,),
        in_specs=[pl.BlockSpec((tm, D), lambda i: (i, 0)),
                  pl.BlockSpec((TOP_K, tm, PACKED), lambda i: (0, i + off, 0)),
                  pl.BlockSpec((tm, TOP_K), lambda i: (i, 0)),
                  _mod_spec(per_token, tm, tokens_per_seq, 5),
                  pl.BlockSpec((1, D), lambda i: (0, 0))],
        out_specs=pl.BlockSpec((tm, D), lambda i: (i, 0)),
        out_shape=jax.ShapeDtypeStruct((m, D), F32),
        compiler_params=pltpu.CompilerParams(
            dimension_semantics=("arbitrary",), vmem_limit_bytes=VMEM_LIMIT),
        name="final",
    )(x1, yk, gates, mod, g_final.reshape(1, D))


def kernel(x_prompt, x_sample, state_C, state_n, state_m, c_prompt, c_sample, w_ada, b_ada, g_mix, w_in,
           b_if, g_head, g_sgu, b_sgu, w_s, b_s, w_pa, w_pb, w_out, g_ffn, w_router, b_router, w_gu, b_gu,
           w_down, b_down, g_final):
    depth = w_ada.shape[0]
    assert depth == 1
    bp, tp, _ = x_prompt.shape
    bs, ts, _ = x_sample.shape
    mp, ms = bp * tp, bs * ts
    assert tp % ML_CHUNK == 0 and ts <= ML_CHUNK and GM_CHUNK % ts == 0

    w_in0 = w_in[0]
    nqkvo = 4 * D
    w_qkvo = w_in0[:, :nqkvo].astype(BF16)
    w_gate = w_in0[:, nqkvo + 2 * HEADS:].astype(BF16)
    w_if = jnp.pad(w_in0[:, nqkvo:nqkvo + 2 * HEADS], ((0, 0), (0, LANES - 2 * HEADS)))
    b_if_p = jnp.pad(b_if[0], (0, LANES - 2 * HEADS)).reshape(1, LANES)
    reps = GM_CHUNK // ts
    eye_r = jnp.eye(reps, dtype=F32)
    w_s_sample = jnp.einsum("ab,gts->gatbs", eye_r, w_s[0][:, :ts, :ts]).reshape(GROUPS, GM_CHUNK, GM_CHUNK)
    b_s_prompt = b_s[0].T
    b_s_sample = jnp.tile(b_s[0][:, :ts].T, (reps, 1))
    mix_p = {
        "g_sgu": g_sgu[0].reshape(1, D), "b_sgu": b_sgu[0].reshape(1, D),
        "w_pa": w_pa[0].astype(BF16), "w_pb": w_pb[0].astype(BF16), "w_out": w_out[0].astype(BF16),
        "g_ffn": g_ffn[0].reshape(1, D),
        "w_router": jnp.pad(w_router[0], ((0, 0), (0, LANES - N_EXPERTS))),
        "b_router": jnp.pad(b_router[0], (0, LANES - N_EXPERTS)).reshape(1, LANES),
    }
    mix_prompt = dict(mix_p, w_s=w_s[0], b_s=b_s_prompt)
    mix_sample = dict(mix_p, w_s=w_s_sample, b_s=b_s_sample)

    mod = _ada(jnp.concatenate([jnp.repeat(c_sample, ts, axis=0), c_prompt], axis=0), w_ada[0], b_ada[0])
    mod_s = mod
    mod_p = mod[ms:].reshape(bp, 1, N_MOD * D)

    xp = x_prompt.reshape(mp, D)
    xs = x_sample.reshape(ms, D)
    z_p, _, gt_p = _in_proj(xp, mod_p, False, tp, g_mix[0], w_qkvo, w_gate, w_if, b_if_p, 512, BF16)
    z_s, gc_s, _ = _in_proj(xs, mod_s, True, ts, g_mix[0], w_qkvo, w_gate, w_if, b_if_p, 128, F32)

    hm_p, C_p, n_p, m_p = _mlstm_prompt(z_p, gt_p, g_head[0], bp, tp)
    hm_p = hm_p.reshape(mp, D)
    m0_tok = jnp.repeat(state_m[0], ts, axis=0)
    hm_s, C_s, n_s, m_s = _mlstm_sample(z_s, gc_s, m0_tok, state_C[0], state_n[0], g_head[0], bs, ts)

    x1_p, h2_p, ch_p, gates_p, _, cnt_p = _mix(xp, hm_p, z_p, mod_p, False, tp, 256, mix_prompt, False)
    x1_s, h2_s, ch_s, gates_s, vg_s, cnt_s = _mix(xs, hm_s, z_s, mod_s, True, ts, 128, mix_sample, True)

    n_tok = mp + ms
    n_blocks = -(-(n_tok * TOP_K + N_EXPERTS * (MOE_ROWS - 1)) // MOE_ROWS)
    dest, ex_table = _route(ch_p, ch_s, cnt_p, cnt_s)
    dest_kmajor = dest.T.reshape(TOP_K * n_tok)
    xslots = _sc_scatter_rows(h2_p, h2_s, dest_kmajor, n_blocks * MOE_ROWS)
    yb = _moe(xslots, ex_table[0, :N_EXPERTS], ex_table[1, :N_EXPERTS], ex_table[2, :N_EXPERTS],
              w_gu[0], b_gu[0], w_down[0], b_down[0])
    yk = _sc_gather_rows(yb, dest_kmajor).reshape(TOP_K, n_tok, PACKED)

    y_p = _final(x1_p, yk, gates_p, 0, mod_p, False, tp, g_final)
    y_s = _final(x1_s, yk, gates_s, mp, mod_s, True, ts, g_final)

    return (y_p.reshape(bp, tp, D), y_s.reshape(bs, ts, D),
            C_p[None], n_p.reshape(1, bp, HEADS, HD), m_p[:, :, 0, 0][None],
            C_s[None], n_s.reshape(1, bs, HEADS, HD),
            m_s.reshape(bs, ts, HEADS, LANES)[:, 0, :, 0][None],
            vg_s.reshape(1, bs, ts, D))
```

```python
import functools

import jax
import jax.numpy as jnp
from jax import lax
from jax.experimental import pallas as pl
from jax.experimental.pallas import tpu as pltpu
from jax.experimental.pallas import tpu_sc as plsc

F32 = jnp.float32
BF16 = jnp.bfloat16

D = 1024
HEADS = 4
HD = D // HEADS
ML_CHUNK = 512
GROUPS = 4
GD = D // GROUPS
GM_CHUNK = 128
N_EXPERTS = 32
TOP_K = 4
D_FF = D
SWIGLU_LIMIT = 7.0
SWIGLU_ALPHA = 1.702
NORM_EPS = 1e-6
N_MOD = 6
PACKED = D // 2
LANES = 128
SC_CORES = 2
SC_SUBCORES = 16
SC_WORKERS = SC_CORES * SC_SUBCORES
IN_COLS = 2048
ROW_TILE = 512
MOE_ROWS = 256
ROW_DMA_PRIORITY = 1
MLSTM_SEQS = 1
SAMPLE_SEQS = 8
SC_SCATTER_CHUNK = 64
SC_GATHER_CHUNK = 96
VMEM_LIMIT = 48 * 1024 * 1024
MOE_VMEM_LIMIT = 56 * 1024 * 1024


def _dot(a, b):
    return jnp.dot(a, b, preferred_element_type=F32)


def _dot_nt(a, b):
    return lax.dot_general(a, b, (((1,), (1,)), ((), ())), preferred_element_type=F32)


def _dot_tn(a, b):
    return lax.dot_general(a, b, (((0,), (0,)), ((), ())), preferred_element_type=F32)


def _split_bf16(a):
    hi = a.astype(BF16)
    lo = (a - hi.astype(F32)).astype(BF16)
    return hi, lo


def _dot3(a, b):
    ah, al = _split_bf16(a)
    bh, bl = _split_bf16(b)
    return _dot(ah, bh) + (_dot(ah, bl) + _dot(al, bh))


def _log_sigmoid(x):
    return jnp.minimum(x, 0.0) - jnp.log1p(jnp.exp(-jnp.abs(x)))


def _rms(x):
    return x * lax.rsqrt(jnp.mean(x * x, axis=-1, keepdims=True) + NORM_EPS)


def _pack_rows(x):
    bits = lax.bitcast_convert_type(x.astype(BF16).astype(F32), jnp.uint32)
    word = (bits[:, :PACKED] & jnp.uint32(0xFFFF0000)) | (bits[:, PACKED:] >> 16)
    return lax.bitcast_convert_type(word, jnp.int32)


def _unpack_rows(w):
    bits = lax.bitcast_convert_type(w, jnp.uint32)
    left = lax.bitcast_convert_type(bits & jnp.uint32(0xFFFF0000), F32)
    right = lax.bitcast_convert_type(bits << 16, F32)
    return jnp.concatenate([left, right], axis=1)


def _mod_spec(per_token, tm, tokens_per_seq, col):
    if per_token:
        return pl.BlockSpec((tm, D), lambda i, *_: (i, col))
    return pl.BlockSpec((None, 1, D), lambda i, *_: ((i * tm) // tokens_per_seq, 0, col))


def _ada_kernel(c_ref, w_ref, b_ref, o_ref):
    c = c_ref[...]
    s = (c * jax.nn.sigmoid(c)).astype(BF16)
    o_ref[...] = _dot(s, w_ref[...].astype(BF16)) + b_ref[...]


def _ada(c, w, b):
    m, n = c.shape[0], w.shape[1]
    tn = 512
    return pl.pallas_call(
        _ada_kernel,
        grid=(n // tn,),
        in_specs=[pl.BlockSpec((m, D), lambda j: (0, 0)),
                  pl.BlockSpec((D, tn), lambda j: (0, j)),
                  pl.BlockSpec((1, tn), lambda j: (0, j))],
        out_specs=pl.BlockSpec((m, tn), lambda j: (0, j)),
        out_shape=jax.ShapeDtypeStruct((m, n), F32),
        name="ada",
    )(c, w, b.reshape(1, n))


def _in_kernel(x_ref, g_ref, sh_ref, sc_ref, wa_ref, wb_ref, wif_ref, bif_ref, z_ref, gc_ref, gt_ref):
    h = (_rms(x_ref[...]) * g_ref[...]) * (1.0 + sc_ref[...]) + sh_ref[...]
    hb = h.astype(BF16)
    gates = _dot3(h, wif_ref[...]) + bif_ref[...]
    gc_ref[...] = gates
    gt_ref[...] = gates.T[:2 * HEADS, :]
    half = wa_ref.shape[1]
    for w_ref, col0 in ((wa_ref, 0), (wb_ref, half)):
        for c in range(half // IN_COLS):
            cols = slice(c * IN_COLS, (c + 1) * IN_COLS)
            z_ref[:, col0 + c * IN_COLS:col0 + (c + 1) * IN_COLS] = _dot(hb, w_ref[:, cols]).astype(z_ref.dtype)


def _in_proj(x, mod, per_token, tokens_per_seq, g_mix, w_a, w_b, w_if, b_if, tm, z_dtype):
    m = x.shape[0]
    half = w_a.shape[1]
    gt_len = tokens_per_seq if tokens_per_seq % tm == 0 else m
    gt_tiles = gt_len // tm
    resident = functools.partial(pl.BlockSpec, index_map=lambda i: (0, 0), pipeline_mode=pl.Buffered(1))
    return pl.pallas_call(
        _in_kernel,
        grid=(m // tm,),
        in_specs=[pl.BlockSpec((tm, D), lambda i: (i, 0)),
                  pl.BlockSpec((1, D), lambda i: (0, 0)),
                  _mod_spec(per_token, tm, tokens_per_seq, 0),
                  _mod_spec(per_token, tm, tokens_per_seq, 1),
                  resident((D, half)), resident((D, half)),
                  pl.BlockSpec((D, LANES), lambda i: (0, 0)),
                  pl.BlockSpec((1, LANES), lambda i: (0, 0))],
        out_specs=[pl.BlockSpec((tm, 2 * half), lambda i: (i, 0)),
                   pl.BlockSpec((tm, LANES), lambda i: (i, 0)),
                   pl.BlockSpec((None, 2 * HEADS, tm), lambda i: (i // gt_tiles, 0, i % gt_tiles))],
        out_shape=[jax.ShapeDtypeStruct((m, 2 * half), z_dtype),
                   jax.ShapeDtypeStruct((m, LANES), F32),
                   jax.ShapeDtypeStruct((m // gt_len, 2 * HEADS, gt_len), F32)],
        compiler_params=pltpu.CompilerParams(
            dimension_semantics=("arbitrary",), vmem_limit_bytes=MOE_VMEM_LIMIT),
        name="in_proj",
    )(x, g_mix.reshape(1, D), mod, mod, w_a, w_b, w_if, b_if)


def _mlstm_prompt_kernel(q_ref, k_ref, v_ref, o_ref, gt_ref, gh_ref, hm_ref, C_ref, n_ref, m_ref):
    nseq, L = q_ref.shape[0], q_ref.shape[1]

    @pl.when(pl.program_id(1) == 0)
    def _():
        C_ref[...] = jnp.zeros_like(C_ref)
        n_ref[...] = jnp.zeros_like(n_ref)
        m_ref[...] = jnp.zeros_like(m_ref)

    r = lax.broadcasted_iota(jnp.int32, (L, L), 0)
    s = lax.broadcasted_iota(jnp.int32, (L, L), 1)
    eye = r == s
    causal = s <= r

    def to_col(x_row):
        return jnp.sum(jnp.where(eye, x_row, 0.0), axis=1, keepdims=True)

    for b in range(nseq):
        gates = gt_ref[b]
        for h in range(HEADS):
            cols = slice(h * HD, (h + 1) * HD)
            ig_row = gates[h:h + 1, :]
            lf_row = _log_sigmoid(gates[HEADS + h:HEADS + h + 1, :])
            lf_col = to_col(lf_row)
            b_row = jnp.sum(jnp.where(r <= s, lf_col, 0.0), axis=0, keepdims=True)
            b_col = to_col(b_row)
            m_prev = m_ref[b, h][:, :1]

            logD = jnp.where(causal, b_col - b_row + ig_row, -jnp.inf)
            inter = b_col + m_prev
            mt = jnp.maximum(jnp.max(logD, axis=1, keepdims=True), inter)
            q = q_ref[b, :, cols]
            ks = k_ref[b, :, cols] * (HD ** -0.5)
            v = v_ref[b, :, cols]
            S = _dot_nt(q, ks) * jnp.exp(logD - mt)
            w_int = jnp.exp(inter - mt)
            Cmat = C_ref[b, h]
            nvec = n_ref[b, h]
            num = _dot(S.astype(BF16), v) + w_int * _dot_nt(q, Cmat.astype(BF16))
            nq = jnp.sum(q.astype(F32) * nvec, axis=1, keepdims=True)
            den = jnp.sum(S, axis=1, keepdims=True) + w_int * nq
            hh = num / jnp.maximum(jnp.abs(den), jnp.exp(-mt))
            hg = jax.nn.sigmoid(o_ref[b, :, cols].astype(F32)) * hh
            hm_ref[b, :, cols] = (_rms(hg) * gh_ref[h]).astype(hm_ref.dtype)

            bL = b_row[:, L - 1:L]
            g_row = bL - b_row + ig_row
            m_new = jnp.maximum(bL + m_prev, jnp.max(g_row, axis=1, keepdims=True))
            w_old = jnp.exp(bL + m_prev - m_new)
            kw = ks.astype(F32) * to_col(jnp.exp(g_row - m_new))
            C_ref[b, h] = w_old * Cmat + _dot_tn(v, kw.astype(BF16))
            n_ref[b, h] = w_old * nvec + jnp.sum(kw, axis=0, keepdims=True)
            m_ref[b, h] = jnp.broadcast_to(m_new, (1, LANES))


def _mlstm_prompt(z, gates_t, g_head, batch, seq):
    nc = seq // ML_CHUNK
    nb = MLSTM_SEQS
    z3 = z.reshape(batch, seq, z.shape[1])

    def zspec(col):
        return pl.BlockSpec((nb, ML_CHUNK, D), lambda b, c: (b, c, col))

    def state(last):
        return pl.BlockSpec((nb, HEADS) + last, lambda b, c: (b, 0, 0, 0))

    return pl.pallas_call(
        _mlstm_prompt_kernel,
        grid=(batch // nb, nc),
        in_specs=[zspec(0), zspec(1), zspec(2), zspec(3),
                  pl.BlockSpec((nb, 2 * HEADS, ML_CHUNK), lambda b, c: (b, 0, c)),
                  pl.BlockSpec((HEADS, 1, HD), lambda b, c: (0, 0, 0))],
        out_specs=[pl.BlockSpec((nb, ML_CHUNK, D), lambda b, c: (b, c, 0)),
                   state((HD, HD)), state((1, HD)), state((1, LANES))],
        out_shape=[jax.ShapeDtypeStruct((batch, seq, D), BF16),
                   jax.ShapeDtypeStruct((batch, HEADS, HD, HD), F32),
                   jax.ShapeDtypeStruct((batch, HEADS, 1, HD), F32),
                   jax.ShapeDtypeStruct((batch, HEADS, 1, LANES), F32)],
        compiler_params=pltpu.CompilerParams(dimension_semantics=("arbitrary", "arbitrary")),
        name="mlstm_prompt",
    )(z3, z3, z3, z3, gates_t, g_head.reshape(HEADS, 1, HD))


def _mlstm_sample_kernel(seq_len, q_ref, k_ref, v_ref, o_ref, gc_ref, m0_ref, C0_ref, n0_ref, gh_ref,
                         hm_ref, C_ref, n_ref, m_ref):
    R = q_ref.shape[0]
    nseq = R // seq_len
    r = lax.broadcasted_iota(jnp.int32, (R, R), 0)
    s = lax.broadcasted_iota(jnp.int32, (R, R), 1)
    rseq = lax.broadcasted_iota(jnp.int32, (R, 1), 0) // seq_len
    eye = r == s
    same = (r // seq_len) == (s // seq_len)
    causal = same & (s <= r)

    def to_row(x_col):
        return jnp.sum(jnp.where(eye, x_col, 0.0), axis=0, keepdims=True)

    gc = gc_ref[...]
    for h in range(HEADS):
        cols = slice(h * HD, (h + 1) * HD)
        ig_col = gc[:, h:h + 1]
        lf_col = _log_sigmoid(gc[:, HEADS + h:HEADS + h + 1])
        lf_row = to_row(lf_col)
        b_col = jnp.sum(jnp.where(causal, lf_row, 0.0), axis=1, keepdims=True)
        bL_col = jnp.sum(jnp.where(same, lf_row, 0.0), axis=1, keepdims=True)
        b_row = to_row(b_col)
        ig_row = to_row(ig_col)
        m0_col = m0_ref[:, h:h + 1]

        logD = jnp.where(causal, b_col - b_row + ig_row, -jnp.inf)
        inter = b_col + m0_col
        mt = jnp.maximum(jnp.max(logD, axis=1, keepdims=True), inter)
        qf = q_ref[:, cols]
        q = qf.astype(BF16)
        ksf = k_ref[:, cols] * (HD ** -0.5)
        v = v_ref[:, cols].astype(BF16)
        S = _dot_nt(q, ksf.astype(BF16)) * jnp.exp(logD - mt)
        w_int = jnp.exp(inter - mt)

        Cq = jnp.zeros((R, HD), F32)
        nq = jnp.zeros((R, 1), F32)
        for g in range(nseq):
            Cq = jnp.where(rseq == g, _dot_nt(q, C0_ref[g, h].astype(BF16)), Cq)
            nq = jnp.where(rseq == g, jnp.sum(qf * n0_ref[g, h], axis=1, keepdims=True), nq)
        num = _dot(S.astype(BF16), v) + w_int * Cq
        den = jnp.sum(S, axis=1, keepdims=True) + w_int * nq
        hh = num / jnp.maximum(jnp.abs(den), jnp.exp(-mt))
        hg = jax.nn.sigmoid(o_ref[:, cols]) * hh
        hm_ref[:, cols] = (_rms(hg) * gh_ref[h]).astype(hm_ref.dtype)

        g_col = bL_col - b_col + ig_col
        gmax_col = jnp.max(jnp.where(same, to_row(g_col), -jnp.inf), axis=1, keepdims=True)
        m_new_col = jnp.maximum(bL_col + m0_col, gmax_col)
        w_old_col = jnp.exp(bL_col + m0_col - m_new_col)
        kw = ksf * jnp.exp(g_col - m_new_col)
        for g in range(nseq):
            kw_g = jnp.where(rseq == g, kw, 0.0)
            w_old = w_old_col[g * seq_len:g * seq_len + 1, :]
            C_ref[g, h] = w_old * C0_ref[g, h] + _dot_tn(v, kw_g.astype(BF16))
            n_ref[g, h] = w_old * n0_ref[g, h] + jnp.sum(kw_g, axis=0, keepdims=True)
        m_ref[:, h * LANES:(h + 1) * LANES] = jnp.broadcast_to(m_new_col, (R, LANES))


def _mlstm_sample(z, gates_c, m0_tok, C0, n0, g_head, batch, seq):
    rows = SAMPLE_SEQS * seq
    m = batch * seq

    def zspec(col):
        return pl.BlockSpec((rows, D), lambda i: (i, col))

    state_c = pl.BlockSpec((SAMPLE_SEQS, HEADS, HD, HD), lambda i: (i, 0, 0, 0))
    state_n = pl.BlockSpec((SAMPLE_SEQS, HEADS, 1, HD), lambda i: (i, 0, 0, 0))
    return pl.pallas_call(
        functools.partial(_mlstm_sample_kernel, seq),
        grid=(batch // SAMPLE_SEQS,),
        in_specs=[zspec(0), zspec(1), zspec(2), zspec(3),
                  pl.BlockSpec((rows, LANES), lambda i: (i, 0)),
                  pl.BlockSpec((rows, HEADS), lambda i: (i, 0)),
                  state_c, state_n,
                  pl.BlockSpec((HEADS, 1, HD), lambda i: (0, 0, 0))],
        out_specs=[pl.BlockSpec((rows, D), lambda i: (i, 0)),
                   state_c, state_n,
                   pl.BlockSpec((rows, HEADS * LANES), lambda i: (i, 0))],
        out_shape=[jax.ShapeDtypeStruct((m, D), BF16),
                   jax.ShapeDtypeStruct((batch, HEADS, HD, HD), F32),
                   jax.ShapeDtypeStruct((batch, HEADS, 1, HD), F32),
                   jax.ShapeDtypeStruct((m, HEADS * LANES), F32)],
        compiler_params=pltpu.CompilerParams(
            dimension_semantics=("arbitrary",), vmem_limit_bytes=VMEM_LIMIT),
        name="mlstm_sample",
    )(z, z, z, z, gates_c, m0_tok, C0, n0.reshape(batch, HEADS, 1, HD), g_head.reshape(HEADS, 1, HD))


def _mix_kernel(x_ref, hm_ref, u_ref, v_ref, ga_ref, gb_ref, gt1_ref, sh2_ref, sc2_ref,
                gsgu_ref, bsgu_ref, ws_ref, bs_ref, wpa_ref, wpb_ref, wout_ref, gffn_ref,
                wr_ref, br_ref, x1_ref, h2_ref, choice_ref, gate_ref, vg_ref, cnt_ref, yg_scr):
    tm = x_ref.shape[0]
    u = jax.nn.gelu(u_ref[...].astype(F32))
    vv = jax.nn.gelu(v_ref[...].astype(F32))
    mu = jnp.mean(vv, axis=-1, keepdims=True)
    var = jnp.mean(jnp.square(vv - mu), axis=-1, keepdims=True)
    vg = (vv - mu) * lax.rsqrt(var + NORM_EPS) * gsgu_ref[...] + bsgu_ref[...]
    vg_ref[...] = vg
    vgb = vg.astype(BF16)

    r = lax.broadcasted_iota(jnp.int32, (GM_CHUNK, GM_CHUNK), 0)
    s = lax.broadcasted_iota(jnp.int32, (GM_CHUNK, GM_CHUNK), 1)
    for g in range(GROUPS):
        w = jnp.where(s <= r, ws_ref[g], 0.0).astype(BF16)
        bias = bs_ref[:, g:g + 1]
        for c in range(tm // GM_CHUNK):
            rows = slice(c * GM_CHUNK, (c + 1) * GM_CHUNK)
            cols = slice(g * GD, (g + 1) * GD)
            mixed = _dot(w, vgb[rows, cols]) + bias
            yg_scr[rows, cols] = (u[rows, cols] * mixed).astype(BF16)

    a = _dot(hm_ref[...], wpa_ref[...])
    b = _dot(yg_scr[...], wpb_ref[...])
    merged = (jax.nn.sigmoid(ga_ref[...].astype(F32)) * a
              + jax.nn.sigmoid(gb_ref[...].astype(F32)) * b)
    x1 = x_ref[...] + gt1_ref[...] * _dot(merged.astype(BF16), wout_ref[...])
    x1_ref[...] = x1
    h2 = (_rms(x1) * gffn_ref[...]) * (1.0 + sc2_ref[...]) + sh2_ref[...]
    h2_ref[...] = _pack_rows(h2)

    lane = lax.broadcasted_iota(jnp.int32, (tm, LANES), 1)
    lane_f = lane.astype(F32)
    lg = jnp.where(lane < N_EXPERTS, _dot3(h2, wr_ref[...]) + br_ref[...], -jnp.inf)
    choice = jnp.zeros((tm, LANES), F32)
    vals = []
    for k in range(TOP_K):
        mx = jnp.max(lg, axis=1, keepdims=True)
        sel = lane_f == jnp.min(jnp.where(lg == mx, lane_f, float(LANES)), axis=1, keepdims=True)
        choice = jnp.where(sel, k + 1.0, choice)
        vals.append(mx)
        lg = jnp.where(sel, -jnp.inf, lg)
    choice_ref[...] = choice
    ex = [jnp.exp(v - vals[0]) for v in vals]
    denom = sum(ex)
    gates = jnp.zeros((tm, LANES), F32)
    for k in range(TOP_K):
        gates = jnp.where(lane == k, ex[k] / denom, gates)
    gate_ref[...] = gates[:, :TOP_K]

    @pl.when(pl.program_id(0) == 0)
    def _():
        cnt_ref[...] = jnp.zeros_like(cnt_ref)

    cnt_ref[...] += jnp.sum(jnp.where(choice > 0.0, 1.0, 0.0), axis=0, keepdims=True)


def _mix(x, hm, z, mod, per_token, tokens_per_seq, tm, p, keep_v_rows):
    m = x.shape[0]
    zcol = 4
    v_rows = m if keep_v_rows else tm
    v_spec = pl.BlockSpec((tm, D), (lambda i: (i, 0)) if keep_v_rows else (lambda i: (0, 0)))

    def zspec(blk):
        return pl.BlockSpec((tm, D), lambda i: (i, blk))

    def full(shape):
        return pl.BlockSpec(shape, lambda i: (0,) * len(shape))

    row = pl.BlockSpec((tm, D), lambda i: (i, 0))
    return pl.pallas_call(
        _mix_kernel,
        grid=(m // tm,),
        in_specs=[row, row, zspec(zcol), zspec(zcol + 1), zspec(zcol + 2), zspec(zcol + 3),
                  _mod_spec(per_token, tm, tokens_per_seq, 2),
                  _mod_spec(per_token, tm, tokens_per_seq, 3),
                  _mod_spec(per_token, tm, tokens_per_seq, 4),
                  full((1, D)), full((1, D)),
                  full((GROUPS, GM_CHUNK, GM_CHUNK)), full((GM_CHUNK, GROUPS)),
                  full((D, D)), full((D, D)), full((D, D)), full((1, D)),
                  full((D, LANES)), full((1, LANES))],
        out_specs=[row, pl.BlockSpec((tm, PACKED), lambda i: (i, 0)),
                   pl.BlockSpec((tm, LANES), lambda i: (i, 0)), pl.BlockSpec((tm, TOP_K), lambda i: (i, 0)),
                   v_spec, full((1, LANES))],
        out_shape=[jax.ShapeDtypeStruct((m, D), F32),
                   jax.ShapeDtypeStruct((m, PACKED), jnp.int32),
                   jax.ShapeDtypeStruct((m, LANES), F32),
                   jax.ShapeDtypeStruct((m, TOP_K), F32),
                   jax.ShapeDtypeStruct((v_rows, D), F32),
                   jax.ShapeDtypeStruct((1, LANES), F32)],
        scratch_shapes=[pltpu.VMEM((tm, D), BF16)],
        compiler_params=pltpu.CompilerParams(
            dimension_semantics=("arbitrary",), vmem_limit_bytes=VMEM_LIMIT),
        name="mix",
    )(x, hm, z, z, z, z, mod, mod, mod, p["g_sgu"], p["b_sgu"], p["w_s"], p["b_s"],
      p["w_pa"], p["w_pb"], p["w_out"], p["g_ffn"], p["w_router"], p["b_router"])


def _route_kernel(tiles_a, cha_ref, chb_ref, cnta_ref, cntb_ref, dest_ref, ex_ref, base_scr):
    tm = cha_ref.shape[0]
    lane = lax.broadcasted_iota(jnp.int32, (tm, LANES), 1)
    choice = jnp.where(pl.program_id(0) < tiles_a, cha_ref[...], chb_ref[...])
    onehot = jnp.where(choice > 0.0, 1.0, 0.0)

    @pl.when(pl.program_id(0) == 0)
    def _():
        cnt = cnta_ref[...] + cntb_ref[...]
        padded = jnp.floor((cnt + (MOE_ROWS - 1)) * (1.0 / MOE_ROWS)) * MOE_ROWS
        r = lax.broadcasted_iota(jnp.int32, (LANES, LANES), 0)
        s = lax.broadcasted_iota(jnp.int32, (LANES, LANES), 1)
        padded_col = jnp.sum(jnp.where(r == s, padded, 0.0), axis=1, keepdims=True)
        pstart = jnp.sum(jnp.where(r < s, padded_col, 0.0), axis=0, keepdims=True)
        base_scr[...] = pstart
        trow = lax.broadcasted_iota(jnp.int32, ex_ref.shape, 0)
        table = jnp.where(trow == 0, pstart * (1.0 / MOE_ROWS),
                          jnp.where(trow == 1, padded * (1.0 / MOE_ROWS), jnp.where(trow == 2, cnt, 0.0)))
        ex_ref[...] = table.astype(jnp.int32)

    r = lax.broadcasted_iota(jnp.int32, (tm, tm), 0)
    s = lax.broadcasted_iota(jnp.int32, (tm, tm), 1)
    before = _dot((s < r).astype(BF16), onehot.astype(BF16))
    slot = before + base_scr[...]
    dest = jnp.zeros((tm, LANES), F32)
    for k in range(TOP_K):
        d_k = jnp.sum(jnp.where(choice == k + 1.0, slot, 0.0), axis=1, keepdims=True)
        dest = jnp.where(lane == k, d_k, dest)
    dest_ref[...] = dest[:, :TOP_K].astype(jnp.int32)
    base_scr[...] += jnp.sum(onehot, axis=0, keepdims=True)


def _route(choice_a, choice_b, cnt_a, cnt_b):
    tm = min(ROW_TILE, choice_a.shape[0], choice_b.shape[0])
    assert choice_a.shape[0] % tm == 0 and choice_b.shape[0] % tm == 0
    tiles_a, tiles_b = choice_a.shape[0] // tm, choice_b.shape[0] // tm
    n = (tiles_a + tiles_b) * tm
    fixed = lambda i: (0, 0)
    return pl.pallas_call(
        functools.partial(_route_kernel, tiles_a),
        grid=(tiles_a + tiles_b,),
        in_specs=[pl.BlockSpec((tm, LANES), lambda i: (jnp.minimum(i, tiles_a - 1), 0)),
                  pl.BlockSpec((tm, LANES), lambda i: (jnp.maximum(i - tiles_a, 0), 0)),
                  pl.BlockSpec((1, LANES), fixed), pl.BlockSpec((1, LANES), fixed)],
        out_specs=[pl.BlockSpec((tm, TOP_K), lambda i: (i, 0)), pl.BlockSpec((8, LANES), fixed)],
        out_shape=[jax.ShapeDtypeStruct((n, TOP_K), jnp.int32),
                   jax.ShapeDtypeStruct((8, LANES), jnp.int32)],
        scratch_shapes=[pltpu.VMEM((1, LANES), F32)],
        compiler_params=pltpu.CompilerParams(dimension_semantics=("arbitrary",)),
        name="route",
    )(choice_a, choice_b, cnt_a, cnt_b)


def _sc_worker_base(per_worker):
    return (lax.axis_index("s") * SC_CORES + lax.axis_index("c")) * per_worker


def _sc_scatter_rows(rows_a, rows_b, idx_flat, n_out):
    na, nb = rows_a.shape[0], rows_b.shape[0]
    width, dtype = rows_a.shape[1], rows_a.dtype
    n = na + nb
    per_a, per_b = na // SC_WORKERS, nb // SC_WORKERS
    chunk = SC_SCATTER_CHUNK
    n_chunks = per_a // chunk
    assert per_a * SC_WORKERS == na and per_b * SC_WORKERS == nb and per_b % 8 == 0 and per_b <= chunk
    assert n_chunks * chunk == per_a and n_chunks % 2 == 0
    mesh = plsc.VectorSubcoreMesh(core_axis_name="c", subcore_axis_name="s")

    @functools.partial(
        pl.kernel, mesh=mesh,
        out_type=jax.ShapeDtypeStruct((n_out, width), dtype),
        scratch_types=[pltpu.VMEM((chunk,), jnp.int32)] * TOP_K + [pltpu.VMEM((per_b,), jnp.int32)]
                      + [pltpu.VMEM((chunk, width), dtype), pltpu.VMEM((chunk, width), dtype),
                         pltpu.VMEM((per_b, width), dtype)]
                      + [pltpu.SemaphoreType.DMA] * 3,
    )
    def scatter(a_hbm, b_hbm, idx_hbm, out_hbm, i0, i1, i2, i3, ib, rows0, rows1, rowsb, rsem0, rsem1, wsem):
        base = _sc_worker_base(per_a)
        idx_bufs = (i0, i1, i2, i3)
        bufs = ((rows0, rsem0), (rows1, rsem1))

        def off(j):
            return pl.multiple_of(base + j * chunk, 8)

        def read(j, buf):
            rows_v, sem = buf
            return pltpu.make_async_copy(a_hbm.at[pl.ds(off(j), chunk)], rows_v, sem)

        def spread(j, buf):
            rows_v, _ = buf
            read(j, buf).wait()
            for k in range(TOP_K):
                pltpu.sync_copy(idx_hbm.at[pl.ds(pl.multiple_of(k * n + off(j), 8), chunk)], idx_bufs[k])
            for k in range(TOP_K):
                pltpu.make_async_copy(rows_v, out_hbm.at[idx_bufs[k]], wsem).start()
            for k in range(TOP_K):
                pltpu.make_async_copy(rows_v, out_hbm.at[idx_bufs[k]], wsem).wait()

        read(0, bufs[0]).start()

        @pl.loop(0, n_chunks, step=2)
        def _(j):
            read(j + 1, bufs[1]).start()
            spread(j, bufs[0])

            @pl.when(j + 2 < n_chunks)
            def _():
                read(j + 2, bufs[0]).start()
            spread(j + 1, bufs[1])

        off_b = pl.multiple_of(_sc_worker_base(per_b), 8)
        pltpu.sync_copy(b_hbm.at[pl.ds(off_b, per_b)], rowsb)
        for k in range(TOP_K):
            pltpu.sync_copy(idx_hbm.at[pl.ds(pl.multiple_of(k * n + na + off_b, 8), per_b)], ib)
            pltpu.async_copy(rowsb, out_hbm.at[ib], wsem).wait()

    return scatter(rows_a, rows_b, idx_flat)


def _sc_gather_rows(table, idx_flat):
    b = idx_flat.shape[0]
    width, dtype = table.shape[1], table.dtype
    per_worker = b // SC_WORKERS
    chunk = SC_GATHER_CHUNK
    n_chunks = per_worker // chunk
    assert per_worker * SC_WORKERS == b and n_chunks * chunk == per_worker and n_chunks % 2 == 0
    mesh = plsc.VectorSubcoreMesh(core_axis_name="c", subcore_axis_name="s")

    @functools.partial(
        pl.kernel, mesh=mesh,
        out_type=jax.ShapeDtypeStruct((b, width), dtype),
        scratch_types=[pltpu.VMEM((chunk,), jnp.int32), pltpu.VMEM((chunk,), jnp.int32),
                       pltpu.VMEM((chunk, width), dtype), pltpu.VMEM((chunk, width), dtype),
                       pltpu.SemaphoreType.DMA, pltpu.SemaphoreType.DMA],
    )
    def gather(table_hbm, idx_hbm, out_hbm, idx0, idx1, rows0, rows1, sem0, sem1):
        base = _sc_worker_base(per_worker)
        bufs = ((idx0, rows0, sem0), (idx1, rows1, sem1))

        def off(j):
            return pl.multiple_of(base + j * chunk, 8)

        def start(j, buf):
            idx_v, rows_v, sem = buf
            pltpu.sync_copy(idx_hbm.at[pl.ds(off(j), chunk)], idx_v)
            pltpu.make_async_copy(table_hbm.at[idx_v], rows_v, sem).start()

        def finish(j, buf):
            idx_v, rows_v, sem = buf
            pltpu.make_async_copy(table_hbm.at[idx_v], rows_v, sem).wait()
            pltpu.sync_copy(rows_v, out_hbm.at[pl.ds(off(j), chunk)])

        start(0, bufs[0])

        @pl.loop(0, n_chunks, step=2)
        def _(j):
            start(j + 1, bufs[1])
            finish(j, bufs[0])

            @pl.when(j + 2 < n_chunks)
            def _():
                start(j + 2, bufs[0])
            finish(j + 1, bufs[1])

    return gather(table, idx_flat)


def _moe_kernel(first_ref, nblk_ref, cnt_ref, xs_hbm, wgu_ref, bgu_ref, wd_ref, bd_ref, out_hbm,
                xbuf, obuf, wgu_bf, wd_bf, xsem, osem):
    e = pl.program_id(0)
    first, nblk, cnt = first_ref[e], nblk_ref[e], cnt_ref[e]

    def rows(j):
        return pl.ds(pl.multiple_of((first + j) * MOE_ROWS, MOE_ROWS), MOE_ROWS)

    def x_copy(j, slot):
        return pltpu.make_async_copy(xs_hbm.at[rows(j)], xbuf.at[slot], xsem.at[slot])

    def o_copy(j, slot):
        return pltpu.make_async_copy(obuf.at[slot], out_hbm.at[rows(j)], osem.at[slot])

    @pl.when(nblk > 0)
    def _():
        x_copy(0, 0).start(priority=ROW_DMA_PRIORITY)
        wgu_bf[...] = wgu_ref[...].astype(BF16)
        wd_bf[...] = wd_ref[...].astype(BF16)

        def block(j, carry):
            slot = j % 2
            x_copy(j, slot).wait()

            @pl.when(j + 1 < nblk)
            def _():
                x_copy(j + 1, 1 - slot).start(priority=ROW_DMA_PRIORITY)

            @pl.when(j >= 2)
            def _():
                o_copy(j - 2, slot).wait()

            row = lax.broadcasted_iota(jnp.int32, (MOE_ROWS, 1), 0)
            x = _unpack_rows(jnp.where(row < cnt - j * MOE_ROWS, xbuf[slot], 0)).astype(BF16)
            gu = _dot(x, wgu_bf[...]) + bgu_ref[...]
            gate = jnp.minimum(gu[:, :D_FF], SWIGLU_LIMIT)
            up = jnp.clip(gu[:, D_FF:], -SWIGLU_LIMIT, SWIGLU_LIMIT)
            act = gate * jax.nn.sigmoid(SWIGLU_ALPHA * gate) * (up + 1.0)
            obuf[slot] = _pack_rows(_dot(act.astype(BF16), wd_bf[...]) + bd_ref[...])
            o_copy(j, slot).start(priority=ROW_DMA_PRIORITY)
            return carry

        lax.fori_loop(0, nblk, block, 0)

        @pl.when(nblk >= 2)
        def _():
            o_copy(nblk - 2, nblk % 2).wait()
        o_copy(nblk - 1, (nblk - 1) % 2).wait()


def _moe(xs, first_block, n_blocks, counts, w_gu, b_gu, w_down, b_down):
    grid_spec = pltpu.PrefetchScalarGridSpec(
        num_scalar_prefetch=3,
        grid=(N_EXPERTS,),
        in_specs=[pl.BlockSpec(memory_space=pl.ANY),
                  pl.BlockSpec((None, D, 2 * D_FF), lambda e, *_: (e, 0, 0)),
                  pl.BlockSpec((None, 1, 2 * D_FF), lambda e, *_: (e, 0, 0)),
                  pl.BlockSpec((None, D_FF, D), lambda e, *_: (e, 0, 0)),
                  pl.BlockSpec((None, 1, D), lambda e, *_: (e, 0, 0))],
        out_specs=pl.BlockSpec(memory_space=pl.ANY),
        scratch_shapes=[pltpu.VMEM((2, MOE_ROWS, PACKED), jnp.int32), pltpu.VMEM((2, MOE_ROWS, PACKED), jnp.int32),
                        pltpu.VMEM((D, 2 * D_FF), BF16), pltpu.VMEM((D_FF, D), BF16),
                        pltpu.SemaphoreType.DMA((2,)), pltpu.SemaphoreType.DMA((2,))],
    )
    return pl.pallas_call(
        _moe_kernel,
        grid_spec=grid_spec,
        out_shape=jax.ShapeDtypeStruct(xs.shape, jnp.int32),
        compiler_params=pltpu.CompilerParams(
            dimension_semantics=("arbitrary",), vmem_limit_bytes=MOE_VMEM_LIMIT),
        name="moe",
    )(first_block, n_blocks, counts, xs, w_gu, b_gu.reshape(N_EXPERTS, 1, 2 * D_FF), w_down,
      b_down.reshape(N_EXPERTS, 1, D))


def _final_kernel(x1_ref, yk_ref, gate_ref, gt2_ref, g_ref, o_ref):
    y2 = gate_ref[:, 0:1] * _unpack_rows(yk_ref[0])
    for k in range(1, TOP_K):
        y2 = y2 + gate_ref[:, k:k + 1] * _unpack_rows(yk_ref[k])
    o_ref[...] = _rms(x1_ref[...] + gt2_ref[...] * y2) * g_ref[...]


def _final(x1, yk, gates, row_off, mod, per_token, tokens_per_seq, g_final):
    m = x1.shape[0]
    tm = min(ROW_TILE, m)
    assert m % tm == 0 and row_off % tm == 0 and (per_token or tokens_per_seq % tm == 0)
    off = row_off // tm
    return pl.pallas_call(
        _final_kernel,
        grid=(m // tm,),
        in_specs=[pl.BlockSpec((tm, D), lambda i: (i, 0)),
                  pl.BlockSpec((TOP_K, tm, PACKED), lambda i: (0, i + off, 0)),
                  pl.BlockSpec((tm, TOP_K), lambda i: (i, 0)),
                  _mod_spec(per_token, tm, tokens_per_seq, 5),
                  pl.BlockSpec((1, D), lambda i: (0, 0))],
        out_specs=pl.BlockSpec((tm, D), lambda i: (i, 0)),
        out_shape=jax.ShapeDtypeStruct((m, D), F32),
        compiler_params=pltpu.CompilerParams(
            dimension_semantics=("arbitrary",), vmem_limit_bytes=VMEM_LIMIT),
        name="final",
    )(x1, yk, gates, mod, g_final.reshape(1, D))


def kernel(x_prompt, x_sample, state_C, state_n, state_m, c_prompt, c_sample, w_ada, b_ada, g_mix, w_in,
           b_if, g_head, g_sgu, b_sgu, w_s, b_s, w_pa, w_pb, w_out, g_ffn, w_router, b_router, w_gu, b_gu,
           w_down, b_down, g_final):
    depth = w_ada.shape[0]
    assert depth == 1
    bp, tp, _ = x_prompt.shape
    bs, ts, _ = x_sample.shape
    mp, ms = bp * tp, bs * ts
    assert tp % ML_CHUNK == 0 and ts <= ML_CHUNK and GM_CHUNK % ts == 0

    w_in0 = w_in[0]
    nqkvo = 4 * D
    w_qkvo = w_in0[:, :nqkvo].astype(BF16)
    w_gate = w_in0[:, nqkvo + 2 * HEADS:].astype(BF16)
    w_if = jnp.pad(w_in0[:, nqkvo:nqkvo + 2 * HEADS], ((0, 0), (0, LANES - 2 * HEADS)))
    b_if_p = jnp.pad(b_if[0], (0, LANES - 2 * HEADS)).reshape(1, LANES)
    reps = GM_CHUNK // ts
    eye_r = jnp.eye(reps, dtype=F32)
    w_s_sample = jnp.einsum("ab,gts->gatbs", eye_r, w_s[0][:, :ts, :ts]).reshape(GROUPS, GM_CHUNK, GM_CHUNK)
    b_s_prompt = b_s[0].T
    b_s_sample = jnp.tile(b_s[0][:, :ts].T, (reps, 1))
    mix_p = {
        "g_sgu": g_sgu[0].reshape(1, D), "b_sgu": b_sgu[0].reshape(1, D),
        "w_pa": w_pa[0].astype(BF16), "w_pb": w_pb[0].astype(BF16), "w_out": w_out[0].astype(BF16),
        "g_ffn": g_ffn[0].reshape(1, D),
        "w_router": jnp.pad(w_router[0], ((0, 0), (0, LANES - N_EXPERTS))),
        "b_router": jnp.pad(b_router[0], (0, LANES - N_EXPERTS)).reshape(1, LANES),
    }
    mix_prompt = dict(mix_p, w_s=w_s[0], b_s=b_s_prompt)
    mix_sample = dict(mix_p, w_s=w_s_sample, b_s=b_s_sample)

    mod = _ada(jnp.concatenate([jnp.repeat(c_sample, ts, axis=0), c_prompt], axis=0), w_ada[0], b_ada[0])
    mod_s = mod
    mod_p = mod[ms:].reshape(bp, 1, N_MOD * D)

    xp = x_prompt.reshape(mp, D)
    xs = x_sample.reshape(ms, D)
    z_p, _, gt_p = _in_proj(xp, mod_p, False, tp, g_mix[0], w_qkvo, w_gate, w_if, b_if_p, 512, BF16)
    z_s, gc_s, _ = _in_proj(xs, mod_s, True, ts, g_mix[0], w_qkvo, w_gate, w_if, b_if_p, 128, F32)

    hm_p, C_p, n_p, m_p = _mlstm_prompt(z_p, gt_p, g_head[0], bp, tp)
    hm_p = hm_p.reshape(mp, D)
    m0_tok = jnp.repeat(state_m[0], ts, axis=0)
    hm_s, C_s, n_s, m_s = _mlstm_sample(z_s, gc_s, m0_tok, state_C[0], state_n[0], g_head[0], bs, ts)

    x1_p, h2_p, ch_p, gates_p, _, cnt_p = _mix(xp, hm_p, z_p, mod_p, False, tp, 512, mix_prompt, False)
    x1_s, h2_s, ch_s, gates_s, vg_s, cnt_s = _mix(xs, hm_s, z_s, mod_s, True, ts, 128, mix_sample, True)

    n_tok = mp + ms
    n_blocks = -(-(n_tok * TOP_K + N_EXPERTS * (MOE_ROWS - 1)) // MOE_ROWS)
    dest, ex_table = _route(ch_p, ch_s, cnt_p, cnt_s)
    dest_kmajor = dest.T.reshape(TOP_K * n_tok)
    xslots = _sc_scatter_rows(h2_p, h2_s, dest_kmajor, n_blocks * MOE_ROWS)
    yb = _moe(xslots, ex_table[0, :N_EXPERTS], ex_table[1, :N_EXPERTS], ex_table[2, :N_EXPERTS],
              w_gu[0], b_gu[0], w_down[0], b_down[0])
    yk = _sc_gather_rows(yb, dest_kmajor).reshape(TOP_K, n_tok, PACKED)

    y_p = _final(x1_p, yk, gates_p, 0, mod_p, False, tp, g_final)
    y_s = _final(x1_s, yk, gates_s, mp, mod_s, True, ts, g_final)

    return (y_p.reshape(bp, tp, D), y_s.reshape(bs, ts, D),
            C_p[None], n_p.reshape(1, bp, HEADS, HD), m_p[:, :, 0, 0][None],
            C_s[None], n_s.reshape(1, bs, HEADS, HD),
            m_s.reshape(bs, ts, HEADS, LANES)[:, 0, :, 0][None],
            vg_s.reshape(1, bs, ts, D))
```

```python
import functools

import jax
import jax.numpy as jnp
from jax import lax
from jax.experimental import pallas as pl
from jax.experimental.pallas import tpu as pltpu
from jax.experimental.pallas import tpu_sc as plsc

F32 = jnp.float32
BF16 = jnp.bfloat16

D = 1024
HEADS = 4
HD = D // HEADS
ML_CHUNK = 512
GROUPS = 4
GD = D // GROUPS
GM_CHUNK = 128
N_EXPERTS = 32
TOP_K = 4
D_FF = D
SWIGLU_LIMIT = 7.0
SWIGLU_ALPHA = 1.702
NORM_EPS = 1e-6
N_MOD = 6
PACKED = D // 2
LANES = 128
SC_CORES = 2
SC_SUBCORES = 16
SC_WORKERS = SC_CORES * SC_SUBCORES
IN_COLS = 2048
ROW_TILE = 512
MOE_ROWS = 256
ROW_DMA_PRIORITY = 1
MLSTM_SEQS = 1
SAMPLE_SEQS = 8
SC_SCATTER_CHUNK = 64
SC_GATHER_CHUNK = 96
VMEM_LIMIT = 48 * 1024 * 1024
MOE_VMEM_LIMIT = 56 * 1024 * 1024


def _dot(a, b):
    return jnp.dot(a, b, preferred_element_type=F32)


def _dot_nt(a, b):
    return lax.dot_general(a, b, (((1,), (1,)), ((), ())), preferred_element_type=F32)


def _dot_tn(a, b):
    return lax.dot_general(a, b, (((0,), (0,)), ((), ())), preferred_element_type=F32)


def _split_bf16(a):
    hi = a.astype(BF16)
    lo = (a - hi.astype(F32)).astype(BF16)
    return hi, lo


def _dot3(a, b):
    ah, al = _split_bf16(a)
    bh, bl = _split_bf16(b)
    return _dot(ah, bh) + (_dot(ah, bl) + _dot(al, bh))


def _log_sigmoid(x):
    return jnp.minimum(x, 0.0) - jnp.log1p(jnp.exp(-jnp.abs(x)))


def _rms(x):
    return x * lax.rsqrt(jnp.mean(x * x, axis=-1, keepdims=True) + NORM_EPS)


def _pack_rows(x):
    bits = lax.bitcast_convert_type(x.astype(BF16).astype(F32), jnp.uint32)
    word = (bits[:, :PACKED] & jnp.uint32(0xFFFF0000)) | (bits[:, PACKED:] >> 16)
    return lax.bitcast_convert_type(word, jnp.int32)


def _unpack_rows(w):
    bits = lax.bitcast_convert_type(w, jnp.uint32)
    left = lax.bitcast_convert_type(bits & jnp.uint32(0xFFFF0000), F32)
    right = lax.bitcast_convert_type(bits << 16, F32)
    return jnp.concatenate([left, right], axis=1)


def _mod_spec(per_token, tm, tokens_per_seq, col):
    if per_token:
        return pl.BlockSpec((tm, D), lambda i, *_: (i, col))
    return pl.BlockSpec((None, 1, D), lambda i, *_: ((i * tm) // tokens_per_seq, 0, col))


def _ada_kernel(c_ref, w_ref, b_ref, o_ref):
    c = c_ref[...]
    s = (c * jax.nn.sigmoid(c)).astype(BF16)
    o_ref[...] = _dot(s, w_ref[...].astype(BF16)) + b_ref[...]


def _ada(c, w, b):
    m, n = c.shape[0], w.shape[1]
    tn = 512
    return pl.pallas_call(
        _ada_kernel,
        grid=(n // tn,),
        in_specs=[pl.BlockSpec((m, D), lambda j: (0, 0)),
                  pl.BlockSpec((D, tn), lambda j: (0, j)),
                  pl.BlockSpec((1, tn), lambda j: (0, j))],
        out_specs=pl.BlockSpec((m, tn), lambda j: (0, j)),
        out_shape=jax.ShapeDtypeStruct((m, n), F32),
        name="ada",
    )(c, w, b.reshape(1, n))


def _in_kernel(x_ref, g_ref, sh_ref, sc_ref, wa_ref, wb_ref, wif_ref, bif_ref, z_ref, gc_ref, gt_ref):
    h = (_rms(x_ref[...]) * g_ref[...]) * (1.0 + sc_ref[...]) + sh_ref[...]
    hb = h.astype(BF16)
    gates = _dot3(h, wif_ref[...]) + bif_ref[...]
    gc_ref[...] = gates
    gt_ref[...] = gates.T[:2 * HEADS, :]
    half = wa_ref.shape[1]
    for w_ref, col0 in ((wa_ref, 0), (wb_ref, half)):
        for c in range(half // IN_COLS):
            cols = slice(c * IN_COLS, (c + 1) * IN_COLS)
            z_ref[:, col0 + c * IN_COLS:col0 + (c + 1) * IN_COLS] = _dot(hb, w_ref[:, cols]).astype(z_ref.dtype)


def _in_proj(x, mod, per_token, tokens_per_seq, g_mix, w_a, w_b, w_if, b_if, tm, z_dtype):
    m = x.shape[0]
    half = w_a.shape[1]
    gt_len = tokens_per_seq if tokens_per_seq % tm == 0 else m
    gt_tiles = gt_len // tm
    resident = functools.partial(pl.BlockSpec, index_map=lambda i: (0, 0), pipeline_mode=pl.Buffered(1))
    return pl.pallas_call(
        _in_kernel,
        grid=(m // tm,),
        in_specs=[pl.BlockSpec((tm, D), lambda i: (i, 0)),
                  pl.BlockSpec((1, D), lambda i: (0, 0)),
                  _mod_spec(per_token, tm, tokens_per_seq, 0),
                  _mod_spec(per_token, tm, tokens_per_seq, 1),
                  resident((D, half)), resident((D, half)),
                  pl.BlockSpec((D, LANES), lambda i: (0, 0)),
                  pl.BlockSpec((1, LANES), lambda i: (0, 0))],
        out_specs=[pl.BlockSpec((tm, 2 * half), lambda i: (i, 0)),
                   pl.BlockSpec((tm, LANES), lambda i: (i, 0)),
                   pl.BlockSpec((None, 2 * HEADS, tm), lambda i: (i // gt_tiles, 0, i % gt_tiles))],
        out_shape=[jax.ShapeDtypeStruct((m, 2 * half), z_dtype),
                   jax.ShapeDtypeStruct((m, LANES), F32),
                   jax.ShapeDtypeStruct((m // gt_len, 2 * HEADS, gt_len), F32)],
        compiler_params=pltpu.CompilerParams(
            dimension_semantics=("arbitrary",), vmem_limit_bytes=MOE_VMEM_LIMIT),
        name="in_proj",
    )(x, g_mix.reshape(1, D), mod, mod, w_a, w_b, w_if, b_if)


def _mlstm_prompt_kernel(q_ref, k_ref, v_ref, o_ref, gt_ref, gh_ref, hm_ref, C_ref, n_ref, m_ref):
    nseq, L = q_ref.shape[0], q_ref.shape[1]

    @pl.when(pl.program_id(1) == 0)
    def _():
        C_ref[...] = jnp.zeros_like(C_ref)
        n_ref[...] = jnp.zeros_like(n_ref)
        m_ref[...] = jnp.zeros_like(m_ref)

    r = lax.broadcasted_iota(jnp.int32, (L, L), 0)
    s = lax.broadcasted_iota(jnp.int32, (L, L), 1)
    eye = r == s
    causal = s <= r

    def to_col(x_row):
        return jnp.sum(jnp.where(eye, x_row, 0.0), axis=1, keepdims=True)

    for b in range(nseq):
        gates = gt_ref[b]
        for h in range(HEADS):
            cols = slice(h * HD, (h + 1) * HD)
            ig_row = gates[h:h + 1, :]
            lf_row = _log_sigmoid(gates[HEADS + h:HEADS + h + 1, :])
            lf_col = to_col(lf_row)
            b_row = jnp.sum(jnp.where(r <= s, lf_col, 0.0), axis=0, keepdims=True)
            b_col = to_col(b_row)
            m_prev = m_ref[b, h][:, :1]

            logD = jnp.where(causal, b_col - b_row + ig_row, -jnp.inf)
            inter = b_col + m_prev
            mt = jnp.maximum(jnp.max(logD, axis=1, keepdims=True), inter)
            q = q_ref[b, :, cols]
            ks = k_ref[b, :, cols] * (HD ** -0.5)
            v = v_ref[b, :, cols]
            S = _dot_nt(q, ks) * jnp.exp(logD - mt)
            w_int = jnp.exp(inter - mt)
            Cmat = C_ref[b, h]
            nvec = n_ref[b, h]
            num = _dot(S.astype(BF16), v) + w_int * _dot_nt(q, Cmat.astype(BF16))
            nq = jnp.sum(q.astype(F32) * nvec, axis=1, keepdims=True)
            den = jnp.sum(S, axis=1, keepdims=True) + w_int * nq
            hh = num / jnp.maximum(jnp.abs(den), jnp.exp(-mt))
            hg = jax.nn.sigmoid(o_ref[b, :, cols].astype(F32)) * hh
            hm_ref[b, :, cols] = (_rms(hg) * gh_ref[h]).astype(hm_ref.dtype)

            bL = b_row[:, L - 1:L]
            g_row = bL - b_row + ig_row
            m_new = jnp.maximum(bL + m_prev, jnp.max(g_row, axis=1, keepdims=True))
            w_old = jnp.exp(bL + m_prev - m_new)
            kw = ks.astype(F32) * to_col(jnp.exp(g_row - m_new))
            C_ref[b, h] = w_old * Cmat + _dot_tn(v, kw.astype(BF16))
            n_ref[b, h] = w_old * nvec + jnp.sum(kw, axis=0, keepdims=True)
            m_ref[b, h] = jnp.broadcast_to(m_new, (1, LANES))


def _mlstm_prompt(z, gates_t, g_head, batch, seq):
    nc = seq // ML_CHUNK
    nb = MLSTM_SEQS
    z3 = z.reshape(batch, seq, z.shape[1])

    def zspec(col):
        return pl.BlockSpec((nb, ML_CHUNK, D), lambda b, c: (b, c, col))

    def state(last):
        return pl.BlockSpec((nb, HEADS) + last, lambda b, c: (b, 0, 0, 0))

    return pl.pallas_call(
        _mlstm_prompt_kernel,
        grid=(batch // nb, nc),
        in_specs=[zspec(0), zspec(1), zspec(2), zspec(3),
                  pl.BlockSpec((nb, 2 * HEADS, ML_CHUNK), lambda b, c: (b, 0, c)),
                  pl.BlockSpec((HEADS, 1, HD), lambda b, c: (0, 0, 0))],
        out_specs=[pl.BlockSpec((nb, ML_CHUNK, D), lambda b, c: (b, c, 0)),
                   state((HD, HD)), state((1, HD)), state((1, LANES))],
        out_shape=[jax.ShapeDtypeStruct((batch, seq, D), BF16),
                   jax.ShapeDtypeStruct((batch, HEADS, HD, HD), F32),
                   jax.ShapeDtypeStruct((batch, HEADS, 1, HD), F32),
                   jax.ShapeDtypeStruct((batch, HEADS, 1, LANES), F32)],
        compiler_params=pltpu.CompilerParams(dimension_semantics=("arbitrary", "arbitrary")),
        name="mlstm_prompt",
    )(z3, z3, z3, z3, gates_t, g_head.reshape(HEADS, 1, HD))


def _mlstm_sample_kernel(seq_len, q_ref, k_ref, v_ref, o_ref, gc_ref, m0_ref, C0_ref, n0_ref, gh_ref,
                         hm_ref, C_ref, n_ref, m_ref):
    R = q_ref.shape[0]
    nseq = R // seq_len
    r = lax.broadcasted_iota(jnp.int32, (R, R), 0)
    s = lax.broadcasted_iota(jnp.int32, (R, R), 1)
    rseq = lax.broadcasted_iota(jnp.int32, (R, 1), 0) // seq_len
    eye = r == s
    same = (r // seq_len) == (s // seq_len)
    causal = same & (s <= r)

    def to_row(x_col):
        return jnp.sum(jnp.where(eye, x_col, 0.0), axis=0, keepdims=True)

    gc = gc_ref[...]
    for h in range(HEADS):
        cols = slice(h * HD, (h + 1) * HD)
        ig_col = gc[:, h:h + 1]
        lf_col = _log_sigmoid(gc[:, HEADS + h:HEADS + h + 1])
        lf_row = to_row(lf_col)
        b_col = jnp.sum(jnp.where(causal, lf_row, 0.0), axis=1, keepdims=True)
        bL_col = jnp.sum(jnp.where(same, lf_row, 0.0), axis=1, keepdims=True)
        b_row = to_row(b_col)
        ig_row = to_row(ig_col)
        m0_col = m0_ref[:, h:h + 1]

        logD = jnp.where(causal, b_col - b_row + ig_row, -jnp.inf)
        inter = b_col + m0_col
        mt = jnp.maximum(jnp.max(logD, axis=1, keepdims=True), inter)
        qf = q_ref[:, cols]
        q = qf.astype(BF16)
        ksf = k_ref[:, cols] * (HD ** -0.5)
        v = v_ref[:, cols].astype(BF16)
        S = _dot_nt(q, ksf.astype(BF16)) * jnp.exp(logD - mt)
        w_int = jnp.exp(inter - mt)

        Cq = jnp.zeros((R, HD), F32)
        nq = jnp.zeros((R, 1), F32)
        for g in range(nseq):
            Cq = jnp.where(rseq == g, _dot_nt(q, C0_ref[g, h].astype(BF16)), Cq)
            nq = jnp.where(rseq == g, jnp.sum(qf * n0_ref[g, h], axis=1, keepdims=True), nq)
        num = _dot(S.astype(BF16), v) + w_int * Cq
        den = jnp.sum(S, axis=1, keepdims=True) + w_int * nq
        hh = num / jnp.maximum(jnp.abs(den), jnp.exp(-mt))
        hg = jax.nn.sigmoid(o_ref[:, cols]) * hh
        hm_ref[:, cols] = (_rms(hg) * gh_ref[h]).astype(hm_ref.dtype)

        g_col = bL_col - b_col + ig_col
        gmax_col = jnp.max(jnp.where(same, to_row(g_col), -jnp.inf), axis=1, keepdims=True)
        m_new_col = jnp.maximum(bL_col + m0_col, gmax_col)
        w_old_col = jnp.exp(bL_col + m0_col - m_new_col)
        kw = ksf * jnp.exp(g_col - m_new_col)
        for g in range(nseq):
            kw_g = jnp.where(rseq == g, kw, 0.0)
            w_old = w_old_col[g * seq_len:g * seq_len + 1, :]
            C_ref[g, h] = w_old * C0_ref[g, h] + _dot_tn(v, kw_g.astype(BF16))
            n_ref[g, h] = w_old * n0_ref[g, h] + jnp.sum(kw_g, axis=0, keepdims=True)
        m_ref[:, h * LANES:(h + 1) * LANES] = jnp.broadcast_to(m_new_col, (R, LANES))


def _mlstm_sample(z, gates_c, m0_tok, C0, n0, g_head, batch, seq):
    rows = SAMPLE_SEQS * seq
    m = batch * seq

    def zspec(col):
        return pl.BlockSpec((rows, D), lambda i: (i, col))

    state_c = pl.BlockSpec((SAMPLE_SEQS, HEADS, HD, HD), lambda i: (i, 0, 0, 0))
    state_n = pl.BlockSpec((SAMPLE_SEQS, HEADS, 1, HD), lambda i: (i, 0, 0, 0))
    return pl.pallas_call(
        functools.partial(_mlstm_sample_kernel, seq),
        grid=(batch // SAMPLE_SEQS,),
        in_specs=[zspec(0), zspec(1), zspec(2), zspec(3),
                  pl.BlockSpec((rows, LANES), lambda i: (i, 0)),
                  pl.BlockSpec((rows, HEADS), lambda i: (i, 0)),
                  state_c, state_n,
                  pl.BlockSpec((HEADS, 1, HD), lambda i: (0, 0, 0))],
        out_specs=[pl.BlockSpec((rows, D), lambda i: (i, 0)),
                   state_c, state_n,
                   pl.BlockSpec((rows, HEADS * LANES), lambda i: (i, 0))],
        out_shape=[jax.ShapeDtypeStruct((m, D), BF16),
                   jax.ShapeDtypeStruct((batch, HEADS, HD, HD), F32),
                   jax.ShapeDtypeStruct((batch, HEADS, 1, HD), F32),
                   jax.ShapeDtypeStruct((m, HEADS * LANES), F32)],
        compiler_params=pltpu.CompilerParams(
            dimension_semantics=("arbitrary",), vmem_limit_bytes=VMEM_LIMIT),
        name="mlstm_sample",
    )(z, z, z, z, gates_c, m0_tok, C0, n0.reshape(batch, HEADS, 1, HD), g_head.reshape(HEADS, 1, HD))


def _mix_kernel(x_ref, hm_ref, u_ref, v_ref, ga_ref, gb_ref, gt1_ref, sh2_ref, sc2_ref,
                gsgu_ref, bsgu_ref, ws_ref, bs_ref, wpa_ref, wpb_ref, wout_ref, gffn_ref,
                wr_ref, br_ref, x1_ref, h2_ref, choice_ref, gate_ref, vg_ref, cnt_ref, yg_scr):
    tm = x_ref.shape[0]
    u = jax.nn.gelu(u_ref[...].astype(F32))
    vv = jax.nn.gelu(v_ref[...].astype(F32))
    mu = jnp.mean(vv, axis=-1, keepdims=True)
    var = jnp.mean(jnp.square(vv - mu), axis=-1, keepdims=True)
    vg = (vv - mu) * lax.rsqrt(var + NORM_EPS) * gsgu_ref[...] + bsgu_ref[...]
    vg_ref[...] = vg
    vgb = vg.astype(BF16)

    r = lax.broadcasted_iota(jnp.int32, (GM_CHUNK, GM_CHUNK), 0)
    s = lax.broadcasted_iota(jnp.int32, (GM_CHUNK, GM_CHUNK), 1)
    for g in range(GROUPS):
        w = jnp.where(s <= r, ws_ref[g], 0.0).astype(BF16)
        bias = bs_ref[:, g:g + 1]
        for c in range(tm // GM_CHUNK):
            rows = slice(c * GM_CHUNK, (c + 1) * GM_CHUNK)
            cols = slice(g * GD, (g + 1) * GD)
            mixed = _dot(w, vgb[rows, cols]) + bias
            yg_scr[rows, cols] = (u[rows, cols] * mixed).astype(BF16)

    a = _dot(hm_ref[...], wpa_ref[...])
    b = _dot(yg_scr[...], wpb_ref[...])
    merged = (jax.nn.sigmoid(ga_ref[...].astype(F32)) * a
              + jax.nn.sigmoid(gb_ref[...].astype(F32)) * b)
    x1 = x_ref[...] + gt1_ref[...] * _dot(merged.astype(BF16), wout_ref[...])
    x1_ref[...] = x1
    h2 = (_rms(x1) * gffn_ref[...]) * (1.0 + sc2_ref[...]) + sh2_ref[...]
    h2_ref[...] = _pack_rows(h2)

    lane = lax.broadcasted_iota(jnp.int32, (tm, LANES), 1)
    lane_f = lane.astype(F32)
    lg = jnp.where(lane < N_EXPERTS, _dot3(h2, wr_ref[...]) + br_ref[...], -jnp.inf)
    choice = jnp.zeros((tm, LANES), F32)
    vals = []
    for k in range(TOP_K):
        mx = jnp.max(lg, axis=1, keepdims=True)
        sel = lane_f == jnp.min(jnp.where(lg == mx, lane_f, float(LANES)), axis=1, keepdims=True)
        choice = jnp.where(sel, k + 1.0, choice)
        vals.append(mx)
        lg = jnp.where(sel, -jnp.inf, lg)
    choice_ref[...] = choice
    ex = [jnp.exp(v - vals[0]) for v in vals]
    denom = sum(ex)
    gates = jnp.zeros((tm, LANES), F32)
    for k in range(TOP_K):
        gates = jnp.where(lane == k, ex[k] / denom, gates)
    gate_ref[...] = gates[:, :TOP_K]

    @pl.when(pl.program_id(0) == 0)
    def _():
        cnt_ref[...] = jnp.zeros_like(cnt_ref)

    cnt_ref[...] += jnp.sum(jnp.where(choice > 0.0, 1.0, 0.0), axis=0, keepdims=True)


def _mix(x, hm, z, mod, per_token, tokens_per_seq, tm, p, keep_v_rows):
    m = x.shape[0]
    zcol = 4
    v_rows = m if keep_v_rows else tm
    v_spec = pl.BlockSpec((tm, D), (lambda i: (i, 0)) if keep_v_rows else (lambda i: (0, 0)))

    def zspec(blk):
        return pl.BlockSpec((tm, D), lambda i: (i, blk))

    def full(shape):
        return pl.BlockSpec(shape, lambda i: (0,) * len(shape))

    row = pl.BlockSpec((tm, D), lambda i: (i, 0))
    return pl.pallas_call(
        _mix_kernel,
        grid=(m // tm,),
        in_specs=[row, row, zspec(zcol), zspec(zcol + 1), zspec(zcol + 2), zspec(zcol + 3),
                  _mod_spec(per_token, tm, tokens_per_seq, 2),
                  _mod_spec(per_token, tm, tokens_per_seq, 3),
                  _mod_spec(per_token, tm, tokens_per_seq, 4),
                  full((1, D)), full((1, D)),
                  full((GROUPS, GM_CHUNK, GM_CHUNK)), full((GM_CHUNK, GROUPS)),
                  full((D, D)), full((D, D)), full((D, D)), full((1, D)),
                  full((D, LANES)), full((1, LANES))],
        out_specs=[row, pl.BlockSpec((tm, PACKED), lambda i: (i, 0)),
                   pl.BlockSpec((tm, LANES), lambda i: (i, 0)), pl.BlockSpec((tm, TOP_K), lambda i: (i, 0)),
                   v_spec, full((1, LANES))],
        out_shape=[jax.ShapeDtypeStruct((m, D), F32),
                   jax.ShapeDtypeStruct((m, PACKED), jnp.int32),
                   jax.ShapeDtypeStruct((m, LANES), F32),
                   jax.ShapeDtypeStruct((m, TOP_K), F32),
                   jax.ShapeDtypeStruct((v_rows, D), F32),
                   jax.ShapeDtypeStruct((1, LANES), F32)],
        scratch_shapes=[pltpu.VMEM((tm, D), BF16)],
        compiler_params=pltpu.CompilerParams(
            dimension_semantics=("arbitrary",), vmem_limit_bytes=VMEM_LIMIT),
        name="mix",
    )(x, hm, z, z, z, z, mod, mod, mod, p["g_sgu"], p["b_sgu"], p["w_s"], p["b_s"],
      p["w_pa"], p["w_pb"], p["w_out"], p["g_ffn"], p["w_router"], p["b_router"])


def _route_kernel(tiles_a, cha_ref, chb_ref, cnta_ref, cntb_ref, dest_ref, ex_ref, base_scr):
    tm = cha_ref.shape[0]
    lane = lax.broadcasted_iota(jnp.int32, (tm, LANES), 1)
    choice = jnp.where(pl.program_id(0) < tiles_a, cha_ref[...], chb_ref[...])
    onehot = jnp.where(choice > 0.0, 1.0, 0.0)

    @pl.when(pl.program_id(0) == 0)
    def _():
        cnt = cnta_ref[...] + cntb_ref[...]
        padded = jnp.floor((cnt + (MOE_ROWS - 1)) * (1.0 / MOE_ROWS)) * MOE_ROWS
        r = lax.broadcasted_iota(jnp.int32, (LANES, LANES), 0)
        s = lax.broadcasted_iota(jnp.int32, (LANES, LANES), 1)
        padded_col = jnp.sum(jnp.where(r == s, padded, 0.0), axis=1, keepdims=True)
        pstart = jnp.sum(jnp.where(r < s, padded_col, 0.0), axis=0, keepdims=True)
        base_scr[...] = pstart
        trow = lax.broadcasted_iota(jnp.int32, ex_ref.shape, 0)
        table = jnp.where(trow == 0, pstart * (1.0 / MOE_ROWS),
                          jnp.where(trow == 1, padded * (1.0 / MOE_ROWS), jnp.where(trow == 2, cnt, 0.0)))
        ex_ref[...] = table.astype(jnp.int32)

    r = lax.broadcasted_iota(jnp.int32, (tm, tm), 0)
    s = lax.broadcasted_iota(jnp.int32, (tm, tm), 1)
    before = _dot((s < r).astype(BF16), onehot.astype(BF16))
    slot = before + base_scr[...]
    dest = jnp.zeros((tm, LANES), F32)
    for k in range(TOP_K):
        d_k = jnp.sum(jnp.where(choice == k + 1.0, slot, 0.0), axis=1, keepdims=True)
        dest = jnp.where(lane == k, d_k, dest)
    dest_ref[...] = dest[:, :TOP_K].astype(jnp.int32)
    base_scr[...] += jnp.sum(onehot, axis=0, keepdims=True)


def _route(choice_a, choice_b, cnt_a, cnt_b):
    tm = min(ROW_TILE, choice_a.shape[0], choice_b.shape[0])
    assert choice_a.shape[0] % tm == 0 and choice_b.shape[0] % tm == 0
    tiles_a, tiles_b = choice_a.shape[0] // tm, choice_b.shape[0] // tm
    n = (tiles_a + tiles_b) * tm
    fixed = lambda i: (0, 0)
    return pl.pallas_call(
        functools.partial(_route_kernel, tiles_a),
        grid=(tiles_a + tiles_b,),
        in_specs=[pl.BlockSpec((tm, LANES), lambda i: (jnp.minimum(i, tiles_a - 1), 0)),
                  pl.BlockSpec((tm, LANES), lambda i: (jnp.maximum(i - tiles_a, 0), 0)),
                  pl.BlockSpec((1, LANES), fixed), pl.BlockSpec((1, LANES), fixed)],
        out_specs=[pl.BlockSpec((tm, TOP_K), lambda i: (i, 0)), pl.BlockSpec((8, LANES), fixed)],
        out_shape=[jax.ShapeDtypeStruct((n, TOP_K), jnp.int32),
                   jax.ShapeDtypeStruct((8, LANES), jnp.int32)],
        scratch_shapes=[pltpu.VMEM((1, LANES), F32)],
        compiler_params=pltpu.CompilerParams(dimension_semantics=("arbitrary",)),
        name="route",
    )(choice_a, choice_b, cnt_a, cnt_b)


def _sc_worker_base(per_worker):
    return (lax.axis_index("s") * SC_CORES + lax.axis_index("c")) * per_worker


def _sc_scatter_rows(rows_a, rows_b, idx_flat, n_out):
    na, nb = rows_a.shape[0], rows_b.shape[0]
    width, dtype = rows_a.shape[1], rows_a.dtype
    n = na + nb
    per_a, per_b = na // SC_WORKERS, nb // SC_WORKERS
    chunk = SC_SCATTER_CHUNK
    n_chunks = per_a // chunk
    assert per_a * SC_WORKERS == na and per_b * SC_WORKERS == nb and per_b % 8 == 0 and per_b <= chunk
    assert n_chunks * chunk == per_a and n_chunks % 2 == 0
    mesh = plsc.VectorSubcoreMesh(core_axis_name="c", subcore_axis_name="s")

    @functools.partial(
        pl.kernel, mesh=mesh,
        out_type=jax.ShapeDtypeStruct((n_out, width), dtype),
        scratch_types=[pltpu.VMEM((chunk,), jnp.int32)] * TOP_K + [pltpu.VMEM((per_b,), jnp.int32)]
                      + [pltpu.VMEM((chunk, width), dtype), pltpu.VMEM((chunk, width), dtype),
                         pltpu.VMEM((per_b, width), dtype)]
                      + [pltpu.SemaphoreType.DMA] * 3,
    )
    def scatter(a_hbm, b_hbm, idx_hbm, out_hbm, i0, i1, i2, i3, ib, rows0, rows1, rowsb, rsem0, rsem1, wsem):
        base = _sc_worker_base(per_a)
        idx_bufs = (i0, i1, i2, i3)
        bufs = ((rows0, rsem0), (rows1, rsem1))

        def off(j):
            return pl.multiple_of(base + j * chunk, 8)

        def read(j, buf):
            rows_v, sem = buf
            return pltpu.make_async_copy(a_hbm.at[pl.ds(off(j), chunk)], rows_v, sem)

        def spread(j, buf):
            rows_v, _ = buf
            read(j, buf).wait()
            for k in range(TOP_K):
                pltpu.sync_copy(idx_hbm.at[pl.ds(pl.multiple_of(k * n + off(j), 8), chunk)], idx_bufs[k])
            for k in range(TOP_K):
                pltpu.make_async_copy(rows_v, out_hbm.at[idx_bufs[k]], wsem).start()
            for k in range(TOP_K):
                pltpu.make_async_copy(rows_v, out_hbm.at[idx_bufs[k]], wsem).wait()

        read(0, bufs[0]).start()

        @pl.loop(0, n_chunks, step=2)
        def _(j):
            read(j + 1, bufs[1]).start()
            spread(j, bufs[0])

            @pl.when(j + 2 < n_chunks)
            def _():
                read(j + 2, bufs[0]).start()
            spread(j + 1, bufs[1])

        off_b = pl.multiple_of(_sc_worker_base(per_b), 8)
        pltpu.sync_copy(b_hbm.at[pl.ds(off_b, per_b)], rowsb)
        for k in range(TOP_K):
            pltpu.sync_copy(idx_hbm.at[pl.ds(pl.multiple_of(k * n + na + off_b, 8), per_b)], ib)
            pltpu.async_copy(rowsb, out_hbm.at[ib], wsem).wait()

    return scatter(rows_a, rows_b, idx_flat)


def _sc_gather_rows(table, idx_flat):
    b = idx_flat.shape[0]
    width, dtype = table.shape[1], table.dtype
    per_worker = b // SC_WORKERS
    chunk = SC_GATHER_CHUNK
    n_chunks = per_worker // chunk
    assert per_worker * SC_WORKERS == b and n_chunks * chunk == per_worker and n_chunks % 2 == 0
    mesh = plsc.VectorSubcoreMesh(core_axis_name="c", subcore_axis_name="s")

    @functools.partial(
        pl.kernel, mesh=mesh,
        out_type=jax.ShapeDtypeStruct((b, width), dtype),
        scratch_types=[pltpu.VMEM((chunk,), jnp.int32), pltpu.VMEM((chunk,), jnp.int32),
                       pltpu.VMEM((chunk, width), dtype), pltpu.VMEM((chunk, width), dtype),
                       pltpu.SemaphoreType.DMA, pltpu.SemaphoreType.DMA],
    )
    def gather(table_hbm, idx_hbm, out_hbm, idx0, idx1, rows0, rows1, sem0, sem1):
        base = _sc_worker_base(per_worker)
        bufs = ((idx0, rows0, sem0), (idx1, rows1, sem1))

        def off(j):
            return pl.multiple_of(base + j * chunk, 8)

        def start(j, buf):
            idx_v, rows_v, sem = buf
            pltpu.sync_copy(idx_hbm.at[pl.ds(off(j), chunk)], idx_v)
            pltpu.make_async_copy(table_hbm.at[idx_v], rows_v, sem).start()

        def finish(j, buf):
            idx_v, rows_v, sem = buf
            pltpu.make_async_copy(table_hbm.at[idx_v], rows_v, sem).wait()
            pltpu.sync_copy(rows_v, out_hbm.at[pl.ds(off(j), chunk)])

        start(0, bufs[0])

        @pl.loop(0, n_chunks, step=2)
        def _(j):
            start(j + 1, bufs[1])
            finish(j, bufs[0])

            @pl.when(j + 2 < n_chunks)
            def _():
                start(j + 2, bufs[0])
            finish(j + 1, bufs[1])

    return gather(table, idx_flat)


def _moe_kernel(first_ref, nblk_ref, cnt_ref, xs_hbm, wgu_ref, bgu_ref, wd_ref, bd_ref, out_hbm,
                xbuf, obuf, wgu_bf, wd_bf, xsem, osem):
    e = pl.program_id(0)
    first, nblk, cnt = first_ref[e], nblk_ref[e], cnt_ref[e]

    def rows(j):
        return pl.ds(pl.multiple_of((first + j) * MOE_ROWS, MOE_ROWS), MOE_ROWS)

    def x_copy(j, slot):
        return pltpu.make_async_copy(xs_hbm.at[rows(j)], xbuf.at[slot], xsem.at[slot])

    def o_copy(j, slot):
        return pltpu.make_async_copy(obuf.at[slot], out_hbm.at[rows(j)], osem.at[slot])

    @pl.when(nblk > 0)
    def _():
        x_copy(0, 0).start(priority=ROW_DMA_PRIORITY)
        wgu_bf[...] = wgu_ref[...].astype(BF16)
        wd_bf[...] = wd_ref[...].astype(BF16)

        def block(j, carry):
            slot = j % 2
            x_copy(j, slot).wait()

            @pl.when(j + 1 < nblk)
            def _():
                x_copy(j + 1, 1 - slot).start(priority=ROW_DMA_PRIORITY)

            @pl.when(j >= 2)
            def _():
                o_copy(j - 2, slot).wait()

            row = lax.broadcasted_iota(jnp.int32, (MOE_ROWS, 1), 0)
            x = _unpack_rows(jnp.where(row < cnt - j * MOE_ROWS, xbuf[slot], 0)).astype(BF16)
            gu = _dot(x, wgu_bf[...]) + bgu_ref[...]
            gate = jnp.minimum(gu[:, :D_FF], SWIGLU_LIMIT)
            up = jnp.clip(gu[:, D_FF:], -SWIGLU_LIMIT, SWIGLU_LIMIT)
            act = gate * jax.nn.sigmoid(SWIGLU_ALPHA * gate) * (up + 1.0)
            obuf[slot] = _pack_rows(_dot(act.astype(BF16), wd_bf[...]) + bd_ref[...])
            o_copy(j, slot).start(priority=ROW_DMA_PRIORITY)
            return carry

        lax.fori_loop(0, nblk, block, 0)

        @pl.when(nblk >= 2)
        def _():
            o_copy(nblk - 2, nblk % 2).wait()
        o_copy(nblk - 1, (nblk - 1) % 2).wait()


def _moe(xs, first_block, n_blocks, counts, w_gu, b_gu, w_down, b_down):
    grid_spec = pltpu.PrefetchScalarGridSpec(
        num_scalar_prefetch=3,
        grid=(N_EXPERTS,),
        in_specs=[pl.BlockSpec(memory_space=pl.ANY),
                  pl.BlockSpec((None, D, 2 * D_FF), lambda e, *_: (e, 0, 0)),
                  pl.BlockSpec((None, 1, 2 * D_FF), lambda e, *_: (e, 0, 0)),
                  pl.BlockSpec((None, D_FF, D), lambda e, *_: (e, 0, 0)),
                  pl.BlockSpec((None, 1, D), lambda e, *_: (e, 0, 0))],
        out_specs=pl.BlockSpec(memory_space=pl.ANY),
        scratch_shapes=[pltpu.VMEM((2, MOE_ROWS, PACKED), jnp.int32), pltpu.VMEM((2, MOE_ROWS, PACKED), jnp.int32),
                        pltpu.VMEM((D, 2 * D_FF), BF16), pltpu.VMEM((D_FF, D), BF16),
                        pltpu.SemaphoreType.DMA((2,)), pltpu.SemaphoreType.DMA((2,))],
    )
    return pl.pallas_call(
        _moe_kernel,
        grid_spec=grid_spec,
        out_shape=jax.ShapeDtypeStruct(xs.shape, jnp.int32),
        compiler_params=pltpu.CompilerParams(
            dimension_semantics=("arbitrary",), vmem_limit_bytes=MOE_VMEM_LIMIT),
        name="moe",
    )(first_block, n_blocks, counts, xs, w_gu, b_gu.reshape(N_EXPERTS, 1, 2 * D_FF), w_down,
      b_down.reshape(N_EXPERTS, 1, D))


def _final_kernel(x1_ref, yk_ref, gate_ref, gt2_ref, g_ref, o_ref):
    y2 = gate_ref[:, 0:1] * _unpack_rows(yk_ref[0])
    for k in range(1, TOP_K):
        y2 = y2 + gate_ref[:, k:k + 1] * _unpack_rows(yk_ref[k])
    o_ref[...] = _rms(x1_ref[...] + gt2_ref[...] * y2) * g_ref[...]


def _final(x1, yk, gates, row_off, mod, per_token, tokens_per_seq, g_final):
    m = x1.shape[0]
    tm = min(ROW_TILE, m)
    assert m % tm == 0 and row_off % tm == 0 and (per_token or tokens_per_seq % tm == 0)
    off = row_off // tm
    return pl.pallas_call(
        _final_kernel,
        grid=(m // tm,),
        in_specs=[pl.BlockSpec((tm, D), lambda i: (i, 0)),
                  pl.BlockSpec((TOP_K, tm, PACKED), lambda i: (0, i + off, 0)),
                  pl.BlockSpec((tm, TOP_K), lambda i: (i, 0)),
                  _mod_spec(per_token, tm, tokens_per_seq, 5),
                  pl.BlockSpec((1, D), lambda i: (0, 0))],
        out_specs=pl.BlockSpec((tm, D), lambda i: (i, 0)),
        out_shape=jax.ShapeDtypeStruct((m, D), F32),
        compiler_params=pltpu.CompilerParams(
            dimension_semantics=("arbitrary",), vmem_limit_bytes=VMEM_LIMIT),
        name="final",
    )(x1, yk, gates, mod, g_final.reshape(1, D))


def kernel(x_prompt, x_sample, state_C, state_n, state_m, c_prompt, c_sample, w_ada, b_ada, g_mix, w_in,
           b_if, g_head, g_sgu, b_sgu, w_s, b_s, w_pa, w_pb, w_out, g_ffn, w_router, b_router, w_gu, b_gu,
           w_down, b_down, g_final):
    depth = w_ada.shape[0]
    assert depth == 1
    bp, tp, _ = x_prompt.shape
    bs, ts, _ = x_sample.shape
    mp, ms = bp * tp, bs * ts
    assert tp % ML_CHUNK == 0 and ts <= ML_CHUNK and GM_CHUNK % ts == 0

    w_in0 = w_in[0]
    nqkvo = 4 * D
    w_qkvo = w_in0[:, :nqkvo].astype(BF16)
    w_gate = w_in0[:, nqkvo + 2 * HEADS:].astype(BF16)
    w_if = jnp.pad(w_in0[:, nqkvo:nqkvo + 2 * HEADS], ((0, 0), (0, LANES - 2 * HEADS)))
    b_if_p = jnp.pad(b_if[0], (0, LANES - 2 * HEADS)).reshape(1, LANES)
    reps = GM_CHUNK // ts
    eye_r = jnp.eye(reps, dtype=F32)
    w_s_sample = jnp.einsum("ab,gts->gatbs", eye_r, w_s[0][:, :ts, :ts]).reshape(GROUPS, GM_CHUNK, GM_CHUNK)
    b_s_prompt = b_s[0].T
    b_s_sample = jnp.tile(b_s[0][:, :ts].T, (reps, 1))
    mix_p = {
        "g_sgu": g_sgu[0].reshape(1, D), "b_sgu": b_sgu[0].reshape(1, D),
        "w_pa": w_pa[0].astype(BF16), "w_pb": w_pb[0].astype(BF16), "w_out": w_out[0].astype(BF16),
        "g_ffn": g_ffn[0].reshape(1, D),
        "w_router": jnp.pad(w_router[0], ((0, 0), (0, LANES - N_EXPERTS))),
        "b_router": jnp.pad(b_router[0], (0, LANES - N_EXPERTS)).reshape(1, LANES),
    }
    mix_prompt = dict(mix_p, w_s=w_s[0], b_s=b_s_prompt)
    mix_sample = dict(mix_p, w_s=w_s_sample, b_s=b_s_sample)

    mod = _ada(jnp.concatenate([jnp.repeat(c_sample, ts, axis=0), c_prompt], axis=0), w_ada[0], b_ada[0])
    mod_s = mod
    mod_p = mod[ms:].reshape(bp, 1, N_MOD * D)

    xp = x_prompt.reshape(mp, D)
    xs = x_sample.reshape(ms, D)
    z_p, _, gt_p = _in_proj(xp, mod_p, False, tp, g_mix[0], w_qkvo, w_gate, w_if, b_if_p, 512, BF16)
    z_s, gc_s, _ = _in_proj(xs, mod_s, True, ts, g_mix[0], w_qkvo, w_gate, w_if, b_if_p, min(256, ms), F32)

    hm_p, C_p, n_p, m_p = _mlstm_prompt(z_p, gt_p, g_head[0], bp, tp)
    hm_p = hm_p.reshape(mp, D)
    m0_tok = jnp.repeat(state_m[0], ts, axis=0)
    hm_s, C_s, n_s, m_s = _mlstm_sample(z_s, gc_s, m0_tok, state_C[0], state_n[0], g_head[0], bs, ts)

    x1_p, h2_p, ch_p, gates_p, _, cnt_p = _mix(xp, hm_p, z_p, mod_p, False, tp, 512, mix_prompt, False)
    x1_s, h2_s, ch_s, gates_s, vg_s, cnt_s = _mix(xs, hm_s, z_s, mod_s, True, ts, min(512, ms), mix_sample, True)

    n_tok = mp + ms
    n_blocks = -(-(n_tok * TOP_K + N_EXPERTS * (MOE_ROWS - 1)) // MOE_ROWS)
    dest, ex_table = _route(ch_p, ch_s, cnt_p, cnt_s)
    dest_kmajor = dest.T.reshape(TOP_K * n_tok)
    xslots = _sc_scatter_rows(h2_p, h2_s, dest_kmajor, n_blocks * MOE_ROWS)
    yb = _moe(xslots, ex_table[0, :N_EXPERTS], ex_table[1, :N_EXPERTS], ex_table[2, :N_EXPERTS],
              w_gu[0], b_gu[0], w_down[0], b_down[0])
    yk = _sc_gather_rows(yb, dest_kmajor).reshape(TOP_K, n_tok, PACKED)

    y_p = _final(x1_p, yk, gates_p, 0, mod_p, False, tp, g_final)
    y_s = _final(x1_s, yk, gates_s, mp, mod_s, True, ts, g_final)

    return (y_p.reshape(bp, tp, D), y_s.reshape(bs, ts, D),
            C_p[None], n_p.reshape(1, bp, HEADS, HD), m_p[:, :, 0, 0][None],
            C_s[None], n_s.reshape(1, bs, HEADS, HD),
            m_s.reshape(bs, ts, HEADS, LANES)[:, 0, :, 0][None],
            vg_s.reshape(1, bs, ts, D))
```

```python
import functools

import jax
import jax.numpy as jnp
from jax import lax
from jax.experimental import pallas as pl
from jax.experimental.pallas import tpu as pltpu
from jax.experimental.pallas import tpu_sc as plsc

F32 = jnp.float32
BF16 = jnp.bfloat16

D = 1024
HEADS = 4
HD = D // HEADS
ML_CHUNK = 512
GROUPS = 4
GD = D // GROUPS
GM_CHUNK = 128
N_EXPERTS = 32
TOP_K = 4
D_FF = D
SWIGLU_LIMIT = 7.0
SWIGLU_ALPHA = 1.702
NORM_EPS = 1e-6
N_MOD = 6
PACKED = D // 2
LANES = 128
SC_CORES = 2
SC_SUBCORES = 16
SC_WORKERS = SC_CORES * SC_SUBCORES
IN_COLS = 2048
ROW_TILE = 512
MOE_ROWS = 256
ROW_DMA_PRIORITY = 1
MLSTM_SEQS = 1
SAMPLE_SEQS = 8
SC_SCATTER_CHUNK = 64
SC_GATHER_CHUNK = 96
VMEM_LIMIT = 48 * 1024 * 1024
MOE_VMEM_LIMIT = 56 * 1024 * 1024


def _dot(a, b):
    return jnp.dot(a, b, preferred_element_type=F32)


def _dot_nt(a, b):
    return lax.dot_general(a, b, (((1,), (1,)), ((), ())), preferred_element_type=F32)


def _dot_tn(a, b):
    return lax.dot_general(a, b, (((0,), (0,)), ((), ())), preferred_element_type=F32)


def _split_bf16(a):
    hi = a.astype(BF16)
    lo = (a - hi.astype(F32)).astype(BF16)
    return hi, lo


def _dot3(a, b):
    ah, al = _split_bf16(a)
    bh, bl = _split_bf16(b)
    return _dot(ah, bh) + (_dot(ah, bl) + _dot(al, bh))


def _log_sigmoid(x):
    return jnp.minimum(x, 0.0) - jnp.log1p(jnp.exp(-jnp.abs(x)))


def _rms(x):
    return x * lax.rsqrt(jnp.mean(x * x, axis=-1, keepdims=True) + NORM_EPS)


def _pack_rows(x):
    bits = lax.bitcast_convert_type(x.astype(BF16).astype(F32), jnp.uint32)
    word = (bits[:, :PACKED] & jnp.uint32(0xFFFF0000)) | (bits[:, PACKED:] >> 16)
    return lax.bitcast_convert_type(word, jnp.int32)


def _unpack_rows(w):
    bits = lax.bitcast_convert_type(w, jnp.uint32)
    left = lax.bitcast_convert_type(bits & jnp.uint32(0xFFFF0000), F32)
    right = lax.bitcast_convert_type(bits << 16, F32)
    return jnp.concatenate([left, right], axis=1)


def _mod_spec(per_token, tm, tokens_per_seq, col):
    if per_token:
        return pl.BlockSpec((tm, D), lambda i, *_: (i, col))
    return pl.BlockSpec((None, 1, D), lambda i, *_: ((i * tm) // tokens_per_seq, 0, col))


def _ada_kernel(c_ref, w_ref, b_ref, o_ref):
    c = c_ref[...]
    s = (c * jax.nn.sigmoid(c)).astype(BF16)
    o_ref[...] = _dot(s, w_ref[...].astype(BF16)) + b_ref[...]


def _ada(c, w, b):
    m, n = c.shape[0], w.shape[1]
    tn = 512
    return pl.pallas_call(
        _ada_kernel,
        grid=(n // tn,),
        in_specs=[pl.BlockSpec((m, D), lambda j: (0, 0)),
                  pl.BlockSpec((D, tn), lambda j: (0, j)),
                  pl.BlockSpec((1, tn), lambda j: (0, j))],
        out_specs=pl.BlockSpec((m, tn), lambda j: (0, j)),
        out_shape=jax.ShapeDtypeStruct((m, n), F32),
        name="ada",
    )(c, w, b.reshape(1, n))


def _in_kernel(x_ref, g_ref, sh_ref, sc_ref, wa_ref, wb_ref, wif_ref, bif_ref, z_ref, gc_ref, gt_ref):
    h = (_rms(x_ref[...]) * g_ref[...]) * (1.0 + sc_ref[...]) + sh_ref[...]
    hb = h.astype(BF16)
    gates = _dot3(h, wif_ref[...]) + bif_ref[...]
    gc_ref[...] = gates
    gt_ref[...] = gates.T[:2 * HEADS, :]
    half = wa_ref.shape[1]
    for w_ref, col0 in ((wa_ref, 0), (wb_ref, half)):
        for c in range(half // IN_COLS):
            cols = slice(c * IN_COLS, (c + 1) * IN_COLS)
            z_ref[:, col0 + c * IN_COLS:col0 + (c + 1) * IN_COLS] = _dot(hb, w_ref[:, cols]).astype(z_ref.dtype)


def _in_proj(x, mod, per_token, tokens_per_seq, g_mix, w_a, w_b, w_if, b_if, tm, z_dtype):
    m = x.shape[0]
    half = w_a.shape[1]
    gt_len = tokens_per_seq if tokens_per_seq % tm == 0 else m
    gt_tiles = gt_len // tm
    resident = functools.partial(pl.BlockSpec, index_map=lambda i: (0, 0), pipeline_mode=pl.Buffered(1))
    return pl.pallas_call(
        _in_kernel,
        grid=(m // tm,),
        in_specs=[pl.BlockSpec((tm, D), lambda i: (i, 0)),
                  pl.BlockSpec((1, D), lambda i: (0, 0)),
                  _mod_spec(per_token, tm, tokens_per_seq, 0),
                  _mod_spec(per_token, tm, tokens_per_seq, 1),
                  resident((D, half)), resident((D, half)),
                  pl.BlockSpec((D, LANES), lambda i: (0, 0)),
                  pl.BlockSpec((1, LANES), lambda i: (0, 0))],
        out_specs=[pl.BlockSpec((tm, 2 * half), lambda i: (i, 0)),
                   pl.BlockSpec((tm, LANES), lambda i: (i, 0)),
                   pl.BlockSpec((None, 2 * HEADS, tm), lambda i: (i // gt_tiles, 0, i % gt_tiles))],
        out_shape=[jax.ShapeDtypeStruct((m, 2 * half), z_dtype),
                   jax.ShapeDtypeStruct((m, LANES), F32),
                   jax.ShapeDtypeStruct((m // gt_len, 2 * HEADS, gt_len), F32)],
        compiler_params=pltpu.CompilerParams(
            dimension_semantics=("arbitrary",), vmem_limit_bytes=MOE_VMEM_LIMIT),
        name="in_proj",
    )(x, g_mix.reshape(1, D), mod, mod, w_a, w_b, w_if, b_if)


def _mlstm_prompt_kernel(q_ref, k_ref, v_ref, o_ref, gt_ref, gh_ref, hm_ref, C_ref, n_ref, m_ref):
    nseq, L = q_ref.shape[0], q_ref.shape[1]

    @pl.when(pl.program_id(1) == 0)
    def _():
        C_ref[...] = jnp.zeros_like(C_ref)
        n_ref[...] = jnp.zeros_like(n_ref)
        m_ref[...] = jnp.zeros_like(m_ref)

    r = lax.broadcasted_iota(jnp.int32, (L, L), 0)
    s = lax.broadcasted_iota(jnp.int32, (L, L), 1)
    eye = r == s
    causal = s <= r

    def to_col(x_row):
        return jnp.sum(jnp.where(eye, x_row, 0.0), axis=1, keepdims=True)

    for b in range(nseq):
        gates = gt_ref[b]
        for h in range(HEADS):
            cols = slice(h * HD, (h + 1) * HD)
            ig_row = gates[h:h + 1, :]
            lf_row = _log_sigmoid(gates[HEADS + h:HEADS + h + 1, :])
            lf_col = to_col(lf_row)
            b_row = jnp.sum(jnp.where(r <= s, lf_col, 0.0), axis=0, keepdims=True)
            b_col = to_col(b_row)
            m_prev = m_ref[b, h][:, :1]

            logD = jnp.where(causal, b_col - b_row + ig_row, -jnp.inf)
            inter = b_col + m_prev
            mt = jnp.maximum(jnp.max(logD, axis=1, keepdims=True), inter)
            q = q_ref[b, :, cols]
            ks = k_ref[b, :, cols] * (HD ** -0.5)
            v = v_ref[b, :, cols]
            S = _dot_nt(q, ks) * jnp.exp(logD - mt)
            w_int = jnp.exp(inter - mt)
            Cmat = C_ref[b, h]
            nvec = n_ref[b, h]
            num = _dot(S.astype(BF16), v) + w_int * _dot_nt(q, Cmat.astype(BF16))
            n_hi, n_lo = _split_bf16(jnp.broadcast_to(nvec, (8, HD)))
            nq = (_dot_nt(q, n_hi) + _dot_nt(q, n_lo))[:, :1]
            den = jnp.sum(S, axis=1, keepdims=True) + w_int * nq
            hh = num / jnp.maximum(jnp.abs(den), jnp.exp(-mt))
            hg = jax.nn.sigmoid(o_ref[b, :, cols].astype(F32)) * hh
            hm_ref[b, :, cols] = (_rms(hg) * gh_ref[h]).astype(hm_ref.dtype)

            bL = b_row[:, L - 1:L]
            g_row = bL - b_row + ig_row
            m_new = jnp.maximum(bL + m_prev, jnp.max(g_row, axis=1, keepdims=True))
            w_old = jnp.exp(bL + m_prev - m_new)
            kw = ks.astype(F32) * to_col(jnp.exp(g_row - m_new))
            C_ref[b, h] = w_old * Cmat + _dot_tn(v, kw.astype(BF16))
            n_ref[b, h] = w_old * nvec + jnp.sum(kw, axis=0, keepdims=True)
            m_ref[b, h] = jnp.broadcast_to(m_new, (1, LANES))


def _mlstm_prompt(z, gates_t, g_head, batch, seq):
    nc = seq // ML_CHUNK
    nb = MLSTM_SEQS
    z3 = z.reshape(batch, seq, z.shape[1])

    def zspec(col):
        return pl.BlockSpec((nb, ML_CHUNK, D), lambda b, c: (b, c, col))

    def state(last):
        return pl.BlockSpec((nb, HEADS) + last, lambda b, c: (b, 0, 0, 0))

    return pl.pallas_call(
        _mlstm_prompt_kernel,
        grid=(batch // nb, nc),
        in_specs=[zspec(0), zspec(1), zspec(2), zspec(3),
                  pl.BlockSpec((nb, 2 * HEADS, ML_CHUNK), lambda b, c: (b, 0, c)),
                  pl.BlockSpec((HEADS, 1, HD), lambda b, c: (0, 0, 0))],
        out_specs=[pl.BlockSpec((nb, ML_CHUNK, D), lambda b, c: (b, c, 0)),
                   state((HD, HD)), state((1, HD)), state((1, LANES))],
        out_shape=[jax.ShapeDtypeStruct((batch, seq, D), BF16),
                   jax.ShapeDtypeStruct((batch, HEADS, HD, HD), F32),
                   jax.ShapeDtypeStruct((batch, HEADS, 1, HD), F32),
                   jax.ShapeDtypeStruct((batch, HEADS, 1, LANES), F32)],
        compiler_params=pltpu.CompilerParams(dimension_semantics=("arbitrary", "arbitrary")),
        name="mlstm_prompt",
    )(z3, z3, z3, z3, gates_t, g_head.reshape(HEADS, 1, HD))


def _mlstm_sample_kernel(seq_len, q_ref, k_ref, v_ref, o_ref, gc_ref, m0_ref, C0_ref, n0_ref, gh_ref,
                         hm_ref, C_ref, n_ref, m_ref):
    R = q_ref.shape[0]
    nseq = R // seq_len
    r = lax.broadcasted_iota(jnp.int32, (R, R), 0)
    s = lax.broadcasted_iota(jnp.int32, (R, R), 1)
    rseq = lax.broadcasted_iota(jnp.int32, (R, 1), 0) // seq_len
    eye = r == s
    same = (r // seq_len) == (s // seq_len)
    causal = same & (s <= r)

    def to_row(x_col):
        return jnp.sum(jnp.where(eye, x_col, 0.0), axis=0, keepdims=True)

    gc = gc_ref[...]
    for h in range(HEADS):
        cols = slice(h * HD, (h + 1) * HD)
        ig_col = gc[:, h:h + 1]
        lf_col = _log_sigmoid(gc[:, HEADS + h:HEADS + h + 1])
        lf_row = to_row(lf_col)
        b_col = jnp.sum(jnp.where(causal, lf_row, 0.0), axis=1, keepdims=True)
        bL_col = jnp.sum(jnp.where(same, lf_row, 0.0), axis=1, keepdims=True)
        b_row = to_row(b_col)
        ig_row = to_row(ig_col)
        m0_col = m0_ref[:, h:h + 1]

        logD = jnp.where(causal, b_col - b_row + ig_row, -jnp.inf)
        inter = b_col + m0_col
        mt = jnp.maximum(jnp.max(logD, axis=1, keepdims=True), inter)
        qf = q_ref[:, cols]
        q = qf.astype(BF16)
        ksf = k_ref[:, cols] * (HD ** -0.5)
        v = v_ref[:, cols].astype(BF16)
        S = _dot_nt(q, ksf.astype(BF16)) * jnp.exp(logD - mt)
        w_int = jnp.exp(inter - mt)

        Cq = jnp.zeros((R, HD), F32)
        nq = jnp.zeros((R, 1), F32)
        for g in range(nseq):
            Cq = jnp.where(rseq == g, _dot_nt(q, C0_ref[g, h].astype(BF16)), Cq)
            nq = jnp.where(rseq == g, jnp.sum(qf * n0_ref[g, h], axis=1, keepdims=True), nq)
        num = _dot(S.astype(BF16), v) + w_int * Cq
        den = jnp.sum(S, axis=1, keepdims=True) + w_int * nq
        hh = num / jnp.maximum(jnp.abs(den), jnp.exp(-mt))
        hg = jax.nn.sigmoid(o_ref[:, cols]) * hh
        hm_ref[:, cols] = (_rms(hg) * gh_ref[h]).astype(hm_ref.dtype)

        g_col = bL_col - b_col + ig_col
        gmax_col = jnp.max(jnp.where(same, to_row(g_col), -jnp.inf), axis=1, keepdims=True)
        m_new_col = jnp.maximum(bL_col + m0_col, gmax_col)
        w_old_col = jnp.exp(bL_col + m0_col - m_new_col)
        kw = ksf * jnp.exp(g_col - m_new_col)
        for g in range(nseq):
            kw_g = jnp.where(rseq == g, kw, 0.0)
            w_old = w_old_col[g * seq_len:g * seq_len + 1, :]
            C_ref[g, h] = w_old * C0_ref[g, h] + _dot_tn(v, kw_g.astype(BF16))
            n_ref[g, h] = w_old * n0_ref[g, h] + jnp.sum(kw_g, axis=0, keepdims=True)
        m_ref[:, h * LANES:(h + 1) * LANES] = jnp.broadcast_to(m_new_col, (R, LANES))


def _mlstm_sample(z, gates_c, m0_tok, C0, n0, g_head, batch, seq):
    rows = SAMPLE_SEQS * seq
    m = batch * seq

    def zspec(col):
        return pl.BlockSpec((rows, D), lambda i: (i, col))

    state_c = pl.BlockSpec((SAMPLE_SEQS, HEADS, HD, HD), lambda i: (i, 0, 0, 0))
    state_n = pl.BlockSpec((SAMPLE_SEQS, HEADS, 1, HD), lambda i: (i, 0, 0, 0))
    return pl.pallas_call(
        functools.partial(_mlstm_sample_kernel, seq),
        grid=(batch // SAMPLE_SEQS,),
        in_specs=[zspec(0), zspec(1), zspec(2), zspec(3),
                  pl.BlockSpec((rows, LANES), lambda i: (i, 0)),
                  pl.BlockSpec((rows, HEADS), lambda i: (i, 0)),
                  state_c, state_n,
                  pl.BlockSpec((HEADS, 1, HD), lambda i: (0, 0, 0))],
        out_specs=[pl.BlockSpec((rows, D), lambda i: (i, 0)),
                   state_c, state_n,
                   pl.BlockSpec((rows, HEADS * LANES), lambda i: (i, 0))],
        out_shape=[jax.ShapeDtypeStruct((m, D), BF16),
                   jax.ShapeDtypeStruct((batch, HEADS, HD, HD), F32),
                   jax.ShapeDtypeStruct((batch, HEADS, 1, HD), F32),
                   jax.ShapeDtypeStruct((m, HEADS * LANES), F32)],
        compiler_params=pltpu.CompilerParams(
            dimension_semantics=("arbitrary",), vmem_limit_bytes=VMEM_LIMIT),
        name="mlstm_sample",
    )(z, z, z, z, gates_c, m0_tok, C0, n0.reshape(batch, HEADS, 1, HD), g_head.reshape(HEADS, 1, HD))


def _mix_kernel(x_ref, hm_ref, u_ref, v_ref, ga_ref, gb_ref, gt1_ref, sh2_ref, sc2_ref,
                gsgu_ref, bsgu_ref, ws_ref, bs_ref, wpa_ref, wpb_ref, wout_ref, gffn_ref,
                wr_ref, br_ref, x1_ref, h2_ref, choice_ref, gate_ref, vg_ref, cnt_ref, yg_scr):
    tm = x_ref.shape[0]
    u = jax.nn.gelu(u_ref[...].astype(F32))
    vv = jax.nn.gelu(v_ref[...].astype(F32))
    mu = jnp.mean(vv, axis=-1, keepdims=True)
    var = jnp.mean(jnp.square(vv - mu), axis=-1, keepdims=True)
    vg = (vv - mu) * lax.rsqrt(var + NORM_EPS) * gsgu_ref[...] + bsgu_ref[...]
    vg_ref[...] = vg
    vgb = vg.astype(BF16)

    r = lax.broadcasted_iota(jnp.int32, (GM_CHUNK, GM_CHUNK), 0)
    s = lax.broadcasted_iota(jnp.int32, (GM_CHUNK, GM_CHUNK), 1)
    for g in range(GROUPS):
        w = jnp.where(s <= r, ws_ref[g], 0.0).astype(BF16)
        bias = bs_ref[:, g:g + 1]
        for c in range(tm // GM_CHUNK):
            rows = slice(c * GM_CHUNK, (c + 1) * GM_CHUNK)
            cols = slice(g * GD, (g + 1) * GD)
            mixed = _dot(w, vgb[rows, cols]) + bias
            yg_scr[rows, cols] = (u[rows, cols] * mixed).astype(BF16)

    a = _dot(hm_ref[...], wpa_ref[...])
    b = _dot(yg_scr[...], wpb_ref[...])
    merged = (jax.nn.sigmoid(ga_ref[...].astype(F32)) * a
              + jax.nn.sigmoid(gb_ref[...].astype(F32)) * b)
    x1 = x_ref[...] + gt1_ref[...] * _dot(merged.astype(BF16), wout_ref[...])
    x1_ref[...] = x1
    h2 = (_rms(x1) * gffn_ref[...]) * (1.0 + sc2_ref[...]) + sh2_ref[...]
    h2_ref[...] = _pack_rows(h2)

    lane = lax.broadcasted_iota(jnp.int32, (tm, LANES), 1)
    lane_f = lane.astype(F32)
    lg = jnp.where(lane < N_EXPERTS, _dot3(h2, wr_ref[...]) + br_ref[...], -jnp.inf)
    choice = jnp.zeros((tm, LANES), F32)
    vals = []
    for k in range(TOP_K):
        mx = jnp.max(lg, axis=1, keepdims=True)
        sel = lane_f == jnp.min(jnp.where(lg == mx, lane_f, float(LANES)), axis=1, keepdims=True)
        choice = jnp.where(sel, k + 1.0, choice)
        vals.append(mx)
        lg = jnp.where(sel, -jnp.inf, lg)
    choice_ref[...] = choice
    ex = [jnp.exp(v - vals[0]) for v in vals]
    denom = sum(ex)
    gates = jnp.zeros((tm, LANES), F32)
    for k in range(TOP_K):
        gates = jnp.where(lane == k, ex[k] / denom, gates)
    gate_ref[...] = gates[:, :TOP_K]

    @pl.when(pl.program_id(0) == 0)
    def _():
        cnt_ref[...] = jnp.zeros_like(cnt_ref)

    cnt_ref[...] += jnp.sum(jnp.where(choice > 0.0, 1.0, 0.0), axis=0, keepdims=True)


def _mix(x, hm, z, mod, per_token, tokens_per_seq, tm, p, keep_v_rows):
    m = x.shape[0]
    zcol = 4
    v_rows = m if keep_v_rows else tm
    v_spec = pl.BlockSpec((tm, D), (lambda i: (i, 0)) if keep_v_rows else (lambda i: (0, 0)))

    def zspec(blk):
        return pl.BlockSpec((tm, D), lambda i: (i, blk))

    def full(shape):
        return pl.BlockSpec(shape, lambda i: (0,) * len(shape))

    row = pl.BlockSpec((tm, D), lambda i: (i, 0))
    return pl.pallas_call(
        _mix_kernel,
        grid=(m // tm,),
        in_specs=[row, row, zspec(zcol), zspec(zcol + 1), zspec(zcol + 2), zspec(zcol + 3),
                  _mod_spec(per_token, tm, tokens_per_seq, 2),
                  _mod_spec(per_token, tm, tokens_per_seq, 3),
                  _mod_spec(per_token, tm, tokens_per_seq, 4),
                  full((1, D)), full((1, D)),
                  full((GROUPS, GM_CHUNK, GM_CHUNK)), full((GM_CHUNK, GROUPS)),
                  full((D, D)), full((D, D)), full((D, D)), full((1, D)),
                  full((D, LANES)), full((1, LANES))],
        out_specs=[row, pl.BlockSpec((tm, PACKED), lambda i: (i, 0)),
                   pl.BlockSpec((tm, LANES), lambda i: (i, 0)), pl.BlockSpec((tm, TOP_K), lambda i: (i, 0)),
                   v_spec, full((1, LANES))],
        out_shape=[jax.ShapeDtypeStruct((m, D), F32),
                   jax.ShapeDtypeStruct((m, PACKED), jnp.int32),
                   jax.ShapeDtypeStruct((m, LANES), F32),
                   jax.ShapeDtypeStruct((m, TOP_K), F32),
                   jax.ShapeDtypeStruct((v_rows, D), F32),
                   jax.ShapeDtypeStruct((1, LANES), F32)],
        scratch_shapes=[pltpu.VMEM((tm, D), BF16)],
        compiler_params=pltpu.CompilerParams(
            dimension_semantics=("arbitrary",), vmem_limit_bytes=VMEM_LIMIT),
        name="mix",
    )(x, hm, z, z, z, z, mod, mod, mod, p["g_sgu"], p["b_sgu"], p["w_s"], p["b_s"],
      p["w_pa"], p["w_pb"], p["w_out"], p["g_ffn"], p["w_router"], p["b_router"])


def _route_kernel(tiles_a, cha_ref, chb_ref, cnta_ref, cntb_ref, dest_ref, ex_ref, base_scr):
    tm = cha_ref.shape[0]
    lane = lax.broadcasted_iota(jnp.int32, (tm, LANES), 1)
    choice = jnp.where(pl.program_id(0) < tiles_a, cha_ref[...], chb_ref[...])
    onehot = jnp.where(choice > 0.0, 1.0, 0.0)

    @pl.when(pl.program_id(0) == 0)
    def _():
        cnt = cnta_ref[...] + cntb_ref[...]
        padded = jnp.floor((cnt + (MOE_ROWS - 1)) * (1.0 / MOE_ROWS)) * MOE_ROWS
        r = lax.broadcasted_iota(jnp.int32, (LANES, LANES), 0)
        s = lax.broadcasted_iota(jnp.int32, (LANES, LANES), 1)
        padded_col = jnp.sum(jnp.where(r == s, padded, 0.0), axis=1, keepdims=True)
        pstart = jnp.sum(jnp.where(r < s, padded_col, 0.0), axis=0, keepdims=True)
        base_scr[...] = pstart
        trow = lax.broadcasted_iota(jnp.int32, ex_ref.shape, 0)
        table = jnp.where(trow == 0, pstart * (1.0 / MOE_ROWS),
                          jnp.where(trow == 1, padded * (1.0 / MOE_ROWS), jnp.where(trow == 2, cnt, 0.0)))
        ex_ref[...] = table.astype(jnp.int32)

    r = lax.broadcasted_iota(jnp.int32, (tm, tm), 0)
    s = lax.broadcasted_iota(jnp.int32, (tm, tm), 1)
    before = _dot((s < r).astype(BF16), onehot.astype(BF16))
    slot = before + base_scr[...]
    dest = jnp.zeros((tm, LANES), F32)
    for k in range(TOP_K):
        d_k = jnp.sum(jnp.where(choice == k + 1.0, slot, 0.0), axis=1, keepdims=True)
        dest = jnp.where(lane == k, d_k, dest)
    dest_ref[...] = dest[:, :TOP_K].astype(jnp.int32)
    base_scr[...] += jnp.sum(onehot, axis=0, keepdims=True)


def _route(choice_a, choice_b, cnt_a, cnt_b):
    tm = min(ROW_TILE, choice_a.shape[0], choice_b.shape[0])
    assert choice_a.shape[0] % tm == 0 and choice_b.shape[0] % tm == 0
    tiles_a, tiles_b = choice_a.shape[0] // tm, choice_b.shape[0] // tm
    n = (tiles_a + tiles_b) * tm
    fixed = lambda i: (0, 0)
    return pl.pallas_call(
        functools.partial(_route_kernel, tiles_a),
        grid=(tiles_a + tiles_b,),
        in_specs=[pl.BlockSpec((tm, LANES), lambda i: (jnp.minimum(i, tiles_a - 1), 0)),
                  pl.BlockSpec((tm, LANES), lambda i: (jnp.maximum(i - tiles_a, 0), 0)),
                  pl.BlockSpec((1, LANES), fixed), pl.BlockSpec((1, LANES), fixed)],
        out_specs=[pl.BlockSpec((tm, TOP_K), lambda i: (i, 0)), pl.BlockSpec((8, LANES), fixed)],
        out_shape=[jax.ShapeDtypeStruct((n, TOP_K), jnp.int32),
                   jax.ShapeDtypeStruct((8, LANES), jnp.int32)],
        scratch_shapes=[pltpu.VMEM((1, LANES), F32)],
        compiler_params=pltpu.CompilerParams(dimension_semantics=("arbitrary",)),
        name="route",
    )(choice_a, choice_b, cnt_a, cnt_b)


def _sc_worker_base(per_worker):
    return (lax.axis_index("s") * SC_CORES + lax.axis_index("c")) * per_worker


def _sc_scatter_rows(rows_a, rows_b, idx_flat, n_out):
    na, nb = rows_a.shape[0], rows_b.shape[0]
    width, dtype = rows_a.shape[1], rows_a.dtype
    n = na + nb
    per_a, per_b = na // SC_WORKERS, nb // SC_WORKERS
    chunk = SC_SCATTER_CHUNK
    n_chunks = per_a // chunk
    assert per_a * SC_WORKERS == na and per_b * SC_WORKERS == nb and per_b % 8 == 0 and per_b <= chunk
    assert n_chunks * chunk == per_a and n_chunks % 2 == 0
    mesh = plsc.VectorSubcoreMesh(core_axis_name="c", subcore_axis_name="s")

    @functools.partial(
        pl.kernel, mesh=mesh,
        out_type=jax.ShapeDtypeStruct((n_out, width), dtype),
        scratch_types=[pltpu.VMEM((chunk,), jnp.int32)] * TOP_K + [pltpu.VMEM((per_b,), jnp.int32)]
                      + [pltpu.VMEM((chunk, width), dtype), pltpu.VMEM((chunk, width), dtype),
                         pltpu.VMEM((per_b, width), dtype)]
                      + [pltpu.SemaphoreType.DMA] * 3,
    )
    def scatter(a_hbm, b_hbm, idx_hbm, out_hbm, i0, i1, i2, i3, ib, rows0, rows1, rowsb, rsem0, rsem1, wsem):
        base = _sc_worker_base(per_a)
        idx_bufs = (i0, i1, i2, i3)
        bufs = ((rows0, rsem0), (rows1, rsem1))

        def off(j):
            return pl.multiple_of(base + j * chunk, 8)

        def read(j, buf):
            rows_v, sem = buf
            return pltpu.make_async_copy(a_hbm.at[pl.ds(off(j), chunk)], rows_v, sem)

        def spread(j, buf):
            rows_v, _ = buf
            read(j, buf).wait()
            for k in range(TOP_K):
                pltpu.sync_copy(idx_hbm.at[pl.ds(pl.multiple_of(k * n + off(j), 8), chunk)], idx_bufs[k])
            for k in range(TOP_K):
                pltpu.make_async_copy(rows_v, out_hbm.at[idx_bufs[k]], wsem).start()
            for k in range(TOP_K):
                pltpu.make_async_copy(rows_v, out_hbm.at[idx_bufs[k]], wsem).wait()

        read(0, bufs[0]).start()

        @pl.loop(0, n_chunks, step=2)
        def _(j):
            read(j + 1, bufs[1]).start()
            spread(j, bufs[0])

            @pl.when(j + 2 < n_chunks)
            def _():
                read(j + 2, bufs[0]).start()
            spread(j + 1, bufs[1])

        off_b = pl.multiple_of(_sc_worker_base(per_b), 8)
        pltpu.sync_copy(b_hbm.at[pl.ds(off_b, per_b)], rowsb)
        for k in range(TOP_K):
            pltpu.sync_copy(idx_hbm.at[pl.ds(pl.multiple_of(k * n + na + off_b, 8), per_b)], ib)
            pltpu.async_copy(rowsb, out_hbm.at[ib], wsem).wait()

    return scatter(rows_a, rows_b, idx_flat)


def _sc_gather_rows(table, idx_flat):
    b = idx_flat.shape[0]
    width, dtype = table.shape[1], table.dtype
    per_worker = b // SC_WORKERS
    chunk = SC_GATHER_CHUNK
    n_chunks = per_worker // chunk
    assert per_worker * SC_WORKERS == b and n_chunks * chunk == per_worker and n_chunks % 2 == 0
    mesh = plsc.VectorSubcoreMesh(core_axis_name="c", subcore_axis_name="s")

    @functools.partial(
        pl.kernel, mesh=mesh,
        out_type=jax.ShapeDtypeStruct((b, width), dtype),
        scratch_types=[pltpu.VMEM((chunk,), jnp.int32), pltpu.VMEM((chunk,), jnp.int32),
                       pltpu.VMEM((chunk, width), dtype), pltpu.VMEM((chunk, width), dtype),
                       pltpu.SemaphoreType.DMA, pltpu.SemaphoreType.DMA],
    )
    def gather(table_hbm, idx_hbm, out_hbm, idx0, idx1, rows0, rows1, sem0, sem1):
        base = _sc_worker_base(per_worker)
        bufs = ((idx0, rows0, sem0), (idx1, rows1, sem1))

        def off(j):
            return pl.multiple_of(base + j * chunk, 8)

        def start(j, buf):
            idx_v, rows_v, sem = buf
            pltpu.sync_copy(idx_hbm.at[pl.ds(off(j), chunk)], idx_v)
            pltpu.make_async_copy(table_hbm.at[idx_v], rows_v, sem).start()

        def finish(j, buf):
            idx_v, rows_v, sem = buf
            pltpu.make_async_copy(table_hbm.at[idx_v], rows_v, sem).wait()
            pltpu.sync_copy(rows_v, out_hbm.at[pl.ds(off(j), chunk)])

        start(0, bufs[0])

        @pl.loop(0, n_chunks, step=2)
        def _(j):
            start(j + 1, bufs[1])
            finish(j, bufs[0])

            @pl.when(j + 2 < n_chunks)
            def _():
                start(j + 2, bufs[0])
            finish(j + 1, bufs[1])

    return gather(table, idx_flat)


def _moe_kernel(first_ref, nblk_ref, cnt_ref, xs_hbm, wgu_ref, bgu_ref, wd_ref, bd_ref, out_hbm,
                xbuf, obuf, wgu_bf, wd_bf, xsem, osem):
    e = pl.program_id(0)
    first, nblk, cnt = first_ref[e], nblk_ref[e], cnt_ref[e]

    def rows(j):
        return pl.ds(pl.multiple_of((first + j) * MOE_ROWS, MOE_ROWS), MOE_ROWS)

    def x_copy(j, slot):
        return pltpu.make_async_copy(xs_hbm.at[rows(j)], xbuf.at[slot], xsem.at[slot])

    def o_copy(j, slot):
        return pltpu.make_async_copy(obuf.at[slot], out_hbm.at[rows(j)], osem.at[slot])

    @pl.when(nblk > 0)
    def _():
        x_copy(0, 0).start(priority=ROW_DMA_PRIORITY)
        wgu_bf[...] = wgu_ref[...].astype(BF16)
        wd_bf[...] = wd_ref[...].astype(BF16)

        def block(j, carry):
            slot = j % 2
            x_copy(j, slot).wait()

            @pl.when(j + 1 < nblk)
            def _():
                x_copy(j + 1, 1 - slot).start(priority=ROW_DMA_PRIORITY)

            @pl.when(j >= 2)
            def _():
                o_copy(j - 2, slot).wait()

            row = lax.broadcasted_iota(jnp.int32, (MOE_ROWS, 1), 0)
            x = _unpack_rows(jnp.where(row < cnt - j * MOE_ROWS, xbuf[slot], 0)).astype(BF16)
            gu = _dot(x, wgu_bf[...]) + bgu_ref[...]
            gate = jnp.minimum(gu[:, :D_FF], SWIGLU_LIMIT)
            up = jnp.clip(gu[:, D_FF:], -SWIGLU_LIMIT, SWIGLU_LIMIT)
            act = gate * jax.nn.sigmoid(SWIGLU_ALPHA * gate) * (up + 1.0)
            obuf[slot] = _pack_rows(_dot(act.astype(BF16), wd_bf[...]) + bd_ref[...])
            o_copy(j, slot).start(priority=ROW_DMA_PRIORITY)
            return carry

        lax.fori_loop(0, nblk, block, 0)

        @pl.when(nblk >= 2)
        def _():
            o_copy(nblk - 2, nblk % 2).wait()
        o_copy(nblk - 1, (nblk - 1) % 2).wait()


def _moe(xs, first_block, n_blocks, counts, w_gu, b_gu, w_down, b_down):
    grid_spec = pltpu.PrefetchScalarGridSpec(
        num_scalar_prefetch=3,
        grid=(N_EXPERTS,),
        in_specs=[pl.BlockSpec(memory_space=pl.ANY),
                  pl.BlockSpec((None, D, 2 * D_FF), lambda e, *_: (e, 0, 0)),
                  pl.BlockSpec((None, 1, 2 * D_FF), lambda e, *_: (e, 0, 0)),
                  pl.BlockSpec((None, D_FF, D), lambda e, *_: (e, 0, 0)),
                  pl.BlockSpec((None, 1, D), lambda e, *_: (e, 0, 0))],
        out_specs=pl.BlockSpec(memory_space=pl.ANY),
        scratch_shapes=[pltpu.VMEM((2, MOE_ROWS, PACKED), jnp.int32), pltpu.VMEM((2, MOE_ROWS, PACKED), jnp.int32),
                        pltpu.VMEM((D, 2 * D_FF), BF16), pltpu.VMEM((D_FF, D), BF16),
                        pltpu.SemaphoreType.DMA((2,)), pltpu.SemaphoreType.DMA((2,))],
    )
    return pl.pallas_call(
        _moe_kernel,
        grid_spec=grid_spec,
        out_shape=jax.ShapeDtypeStruct(xs.shape, jnp.int32),
        compiler_params=pltpu.CompilerParams(
            dimension_semantics=("arbitrary",), vmem_limit_bytes=MOE_VMEM_LIMIT),
        name="moe",
    )(first_block, n_blocks, counts, xs, w_gu, b_gu.reshape(N_EXPERTS, 1, 2 * D_FF), w_down,
      b_down.reshape(N_EXPERTS, 1, D))


def _final_kernel(x1_ref, yk_ref, gate_ref, gt2_ref, g_ref, o_ref):
    y2 = gate_ref[:, 0:1] * _unpack_rows(yk_ref[0])
    for k in range(1, TOP_K):
        y2 = y2 + gate_ref[:, k:k + 1] * _unpack_rows(yk_ref[k])
    o_ref[...] = _rms(x1_ref[...] + gt2_ref[...] * y2) * g_ref[...]


def _final(x1, yk, gates, row_off, mod, per_token, tokens_per_seq, g_final):
    m = x1.shape[0]
    tm = min(ROW_TILE, m)
    assert m % tm == 0 and row_off % tm == 0 and (per_token or tokens_per_seq % tm == 0)
    off = row_off // tm
    return pl.pallas_call(
        _final_kernel,
        grid=(m // tm,),
        in_specs=[pl.BlockSpec((tm, D), lambda i: (i, 0)),
                  pl.BlockSpec((TOP_K, tm, PACKED), lambda i: (0, i + off, 0)),
                  pl.BlockSpec((tm, TOP_K), lambda i: (i, 0)),
                  _mod_spec(per_token, tm, tokens_per_seq, 5),
                  pl.BlockSpec((1, D), lambda i: (0, 0))],
        out_specs=pl.BlockSpec((tm, D), lambda i: (i, 0)),
        out_shape=jax.ShapeDtypeStruct((m, D), F32),
        compiler_params=pltpu.CompilerParams(
            dimension_semantics=("arbitrary",), vmem_limit_bytes=VMEM_LIMIT),
        name="final",
    )(x1, yk, gates, mod, g_final.reshape(1, D))


def kernel(x_prompt, x_sample, state_C, state_n, state_m, c_prompt, c_sample, w_ada, b_ada, g_mix, w_in,
           b_if, g_head, g_sgu, b_sgu, w_s, b_s, w_pa, w_pb, w_out, g_ffn, w_router, b_router, w_gu, b_gu,
           w_down, b_down, g_final):
    depth = w_ada.shape[0]
    assert depth == 1
    bp, tp, _ = x_prompt.shape
    bs, ts, _ = x_sample.shape
    mp, ms = bp * tp, bs * ts
    assert tp % ML_CHUNK == 0 and ts <= ML_CHUNK and GM_CHUNK % ts == 0

    w_in0 = w_in[0]
    nqkvo = 4 * D
    w_qkvo = w_in0[:, :nqkvo].astype(BF16)
    w_gate = w_in0[:, nqkvo + 2 * HEADS:].astype(BF16)
    w_if = jnp.pad(w_in0[:, nqkvo:nqkvo + 2 * HEADS], ((0, 0), (0, LANES - 2 * HEADS)))
    b_if_p = jnp.pad(b_if[0], (0, LANES - 2 * HEADS)).reshape(1, LANES)
    reps = GM_CHUNK // ts
    eye_r = jnp.eye(reps, dtype=F32)
    w_s_sample = jnp.einsum("ab,gts->gatbs", eye_r, w_s[0][:, :ts, :ts]).reshape(GROUPS, GM_CHUNK, GM_CHUNK)
    b_s_prompt = b_s[0].T
    b_s_sample = jnp.tile(b_s[0][:, :ts].T, (reps, 1))
    mix_p = {
        "g_sgu": g_sgu[0].reshape(1, D), "b_sgu": b_sgu[0].reshape(1, D),
        "w_pa": w_pa[0].astype(BF16), "w_pb": w_pb[0].astype(BF16), "w_out": w_out[0].astype(BF16),
        "g_ffn": g_ffn[0].reshape(1, D),
        "w_router": jnp.pad(w_router[0], ((0, 0), (0, LANES - N_EXPERTS))),
        "b_router": jnp.pad(b_router[0], (0, LANES - N_EXPERTS)).reshape(1, LANES),
    }
    mix_prompt = dict(mix_p, w_s=w_s[0], b_s=b_s_prompt)
    mix_sample = dict(mix_p, w_s=w_s_sample, b_s=b_s_sample)

    mod = _ada(jnp.concatenate([jnp.repeat(c_sample, ts, axis=0), c_prompt], axis=0), w_ada[0], b_ada[0])
    mod_s = mod
    mod_p = mod[ms:].reshape(bp, 1, N_MOD * D)

    xp = x_prompt.reshape(mp, D)
    xs = x_sample.reshape(ms, D)
    z_p, _, gt_p = _in_proj(xp, mod_p, False, tp, g_mix[0], w_qkvo, w_gate, w_if, b_if_p, 512, BF16)
    z_s, gc_s, _ = _in_proj(xs, mod_s, True, ts, g_mix[0], w_qkvo, w_gate, w_if, b_if_p, min(256, ms), F32)

    hm_p, C_p, n_p, m_p = _mlstm_prompt(z_p, gt_p, g_head[0], bp, tp)
    hm_p = hm_p.reshape(mp, D)
    m0_tok = jnp.repeat(state_m[0], ts, axis=0)
    hm_s, C_s, n_s, m_s = _mlstm_sample(z_s, gc_s, m0_tok, state_C[0], state_n[0], g_head[0], bs, ts)

    x1_p, h2_p, ch_p, gates_p, _, cnt_p = _mix(xp, hm_p, z_p, mod_p, False, tp, 512, mix_prompt, False)
    x1_s, h2_s, ch_s, gates_s, vg_s, cnt_s = _mix(xs, hm_s, z_s, mod_s, True, ts, min(512, ms), mix_sample, True)

    n_tok = mp + ms
    n_blocks = -(-(n_tok * TOP_K + N_EXPERTS * (MOE_ROWS - 1)) // MOE_ROWS)
    dest, ex_table = _route(ch_p, ch_s, cnt_p, cnt_s)
    dest_kmajor = dest.T.reshape(TOP_K * n_tok)
    xslots = _sc_scatter_rows(h2_p, h2_s, dest_kmajor, n_blocks * MOE_ROWS)
    yb = _moe(xslots, ex_table[0, :N_EXPERTS], ex_table[1, :N_EXPERTS], ex_table[2, :N_EXPERTS],
              w_gu[0], b_gu[0], w_down[0], b_down[0])
    yk = _sc_gather_rows(yb, dest_kmajor).reshape(TOP_K, n_tok, PACKED)

    y_p = _final(x1_p, yk, gates_p, 0, mod_p, False, tp, g_final)
    y_s = _final(x1_s, yk, gates_s, mp, mod_s, True, ts, g_final)

    return (y_p.reshape(bp, tp, D), y_s.reshape(bs, ts, D),
            C_p[None], n_p.reshape(1, bp, HEADS, HD), m_p[:, :, 0, 0][None],
            C_s[None], n_s.reshape(1, bs, HEADS, HD),
            m_s.reshape(bs, ts, HEADS, LANES)[:, 0, :, 0][None],
            vg_s.reshape(1, bs, ts, D))
```

```python
import functools

import jax
import jax.numpy as jnp
from jax import lax
from jax.experimental import pallas as pl
from jax.experimental.pallas import tpu as pltpu
from jax.experimental.pallas import tpu_sc as plsc

F32 = jnp.float32
BF16 = jnp.bfloat16

D = 1024
HEADS = 4
HD = D // HEADS
ML_CHUNK = 512
GROUPS = 4
GD = D // GROUPS
GM_CHUNK = 128
N_EXPERTS = 32
TOP_K = 4
D_FF = D
SWIGLU_LIMIT = 7.0
SWIGLU_ALPHA = 1.702
NORM_EPS = 1e-6
N_MOD = 6
PACKED = D // 2
LANES = 128
SC_CORES = 2
SC_SUBCORES = 16
SC_WORKERS = SC_CORES * SC_SUBCORES
IN_COLS = 2048
ROW_TILE = 512
MOE_ROWS = 256
ROW_DMA_PRIORITY = 1
MLSTM_SEQS = 1
SAMPLE_SEQS = 8
SC_SCATTER_CHUNK = 64
SC_GATHER_CHUNK = 96
VMEM_LIMIT = 48 * 1024 * 1024
MOE_VMEM_LIMIT = 56 * 1024 * 1024


def _dot(a, b):
    return jnp.dot(a, b, preferred_element_type=F32)


def _dot_nt(a, b):
    return lax.dot_general(a, b, (((1,), (1,)), ((), ())), preferred_element_type=F32)


def _dot_tn(a, b):
    return lax.dot_general(a, b, (((0,), (0,)), ((), ())), preferred_element_type=F32)


def _split_bf16(a):
    hi = a.astype(BF16)
    lo = (a - hi.astype(F32)).astype(BF16)
    return hi, lo


def _dot3(a, b):
    ah, al = _split_bf16(a)
    bh, bl = _split_bf16(b)
    return _dot(ah, bh) + (_dot(ah, bl) + _dot(al, bh))


def _log_sigmoid(x):
    return jnp.minimum(x, 0.0) - jnp.log1p(jnp.exp(-jnp.abs(x)))


def _rms(x):
    return x * lax.rsqrt(jnp.mean(x * x, axis=-1, keepdims=True) + NORM_EPS)


def _pack_rows(x):
    bits = lax.bitcast_convert_type(x.astype(BF16).astype(F32), jnp.uint32)
    word = (bits[:, :PACKED] & jnp.uint32(0xFFFF0000)) | (bits[:, PACKED:] >> 16)
    return lax.bitcast_convert_type(word, jnp.int32)


def _unpack_rows(w):
    bits = lax.bitcast_convert_type(w, jnp.uint32)
    left = lax.bitcast_convert_type(bits & jnp.uint32(0xFFFF0000), F32)
    right = lax.bitcast_convert_type(bits << 16, F32)
    return jnp.concatenate([left, right], axis=1)


def _mod_spec(per_token, tm, tokens_per_seq, col):
    if per_token:
        return pl.BlockSpec((tm, D), lambda i, *_: (i, col))
    return pl.BlockSpec((None, 1, D), lambda i, *_: ((i * tm) // tokens_per_seq, 0, col))


def _ada_kernel(c_ref, w_ref, b_ref, o_ref):
    c = c_ref[...]
    s = (c * jax.nn.sigmoid(c)).astype(BF16)
    o_ref[...] = _dot(s, w_ref[...].astype(BF16)) + b_ref[...]


def _ada(c, w, b):
    m, n = c.shape[0], w.shape[1]
    tn = 512
    return pl.pallas_call(
        _ada_kernel,
        grid=(n // tn,),
        in_specs=[pl.BlockSpec((m, D), lambda j: (0, 0)),
                  pl.BlockSpec((D, tn), lambda j: (0, j)),
                  pl.BlockSpec((1, tn), lambda j: (0, j))],
        out_specs=pl.BlockSpec((m, tn), lambda j: (0, j)),
        out_shape=jax.ShapeDtypeStruct((m, n), F32),
        name="ada",
    )(c, w, b.reshape(1, n))


def _in_kernel(x_ref, g_ref, sh_ref, sc_ref, wa_ref, wb_ref, wif_ref, bif_ref, z_ref, gc_ref, gt_ref):
    h = (_rms(x_ref[...]) * g_ref[...]) * (1.0 + sc_ref[...]) + sh_ref[...]
    hb = h.astype(BF16)
    gates = _dot3(h, wif_ref[...]) + bif_ref[...]
    gc_ref[...] = gates
    gt_ref[...] = gates.T[:2 * HEADS, :]
    half = wa_ref.shape[1]
    for w_ref, col0 in ((wa_ref, 0), (wb_ref, half)):
        for c in range(half // IN_COLS):
            cols = slice(c * IN_COLS, (c + 1) * IN_COLS)
            z_ref[:, col0 + c * IN_COLS:col0 + (c + 1) * IN_COLS] = _dot(hb, w_ref[:, cols]).astype(z_ref.dtype)


def _in_proj(x, mod, per_token, tokens_per_seq, g_mix, w_a, w_b, w_if, b_if, tm, z_dtype):
    m = x.shape[0]
    half = w_a.shape[1]
    gt_len = tokens_per_seq if tokens_per_seq % tm == 0 else m
    gt_tiles = gt_len // tm
    resident = functools.partial(pl.BlockSpec, index_map=lambda i: (0, 0), pipeline_mode=pl.Buffered(1))
    return pl.pallas_call(
        _in_kernel,
        grid=(m // tm,),
        in_specs=[pl.BlockSpec((tm, D), lambda i: (i, 0)),
                  pl.BlockSpec((1, D), lambda i: (0, 0)),
                  _mod_spec(per_token, tm, tokens_per_seq, 0),
                  _mod_spec(per_token, tm, tokens_per_seq, 1),
                  resident((D, half)), resident((D, half)),
                  pl.BlockSpec((D, LANES), lambda i: (0, 0)),
                  pl.BlockSpec((1, LANES), lambda i: (0, 0))],
        out_specs=[pl.BlockSpec((tm, 2 * half), lambda i: (i, 0)),
                   pl.BlockSpec((tm, LANES), lambda i: (i, 0)),
                   pl.BlockSpec((None, 2 * HEADS, tm), lambda i: (i // gt_tiles, 0, i % gt_tiles))],
        out_shape=[jax.ShapeDtypeStruct((m, 2 * half), z_dtype),
                   jax.ShapeDtypeStruct((m, LANES), F32),
                   jax.ShapeDtypeStruct((m // gt_len, 2 * HEADS, gt_len), F32)],
        compiler_params=pltpu.CompilerParams(
            dimension_semantics=("arbitrary",), vmem_limit_bytes=MOE_VMEM_LIMIT),
        name="in_proj",
    )(x, g_mix.reshape(1, D), mod, mod, w_a, w_b, w_if, b_if)


def _mlstm_prompt_kernel(q_ref, k_ref, v_ref, o_ref, gt_ref, gh_ref, hm_ref, C_ref, n_ref, m_ref):
    nseq, L = q_ref.shape[0], q_ref.shape[1]

    @pl.when(pl.program_id(1) == 0)
    def _():
        C_ref[...] = jnp.zeros_like(C_ref)
        n_ref[...] = jnp.zeros_like(n_ref)
        m_ref[...] = jnp.zeros_like(m_ref)

    r = lax.broadcasted_iota(jnp.int32, (L, L), 0)
    s = lax.broadcasted_iota(jnp.int32, (L, L), 1)
    eye = r == s
    causal = s <= r

    def to_col(x_row):
        return jnp.sum(jnp.where(eye, x_row, 0.0), axis=1, keepdims=True)

    pairs = [(b, h) for b in range(nseq) for h in range(HEADS)]
    st = {}
    for b, h in pairs:
        gates = gt_ref[b]
        ig_row = gates[h:h + 1, :]
        lf_row = _log_sigmoid(gates[HEADS + h:HEADS + h + 1, :])
        lf_col = to_col(lf_row)
        b_row = jnp.sum(jnp.where(r <= s, lf_col, 0.0), axis=0, keepdims=True)
        b_col = to_col(b_row)
        m_prev = m_ref[b, h][:, :1]
        logD = jnp.where(causal, b_col - b_row + ig_row, -jnp.inf)
        inter = b_col + m_prev
        mt = jnp.maximum(jnp.max(logD, axis=1, keepdims=True), inter)
        bL = b_row[:, L - 1:L]
        g_row = bL - b_row + ig_row
        m_new = jnp.maximum(bL + m_prev, jnp.max(g_row, axis=1, keepdims=True))
        st[b, h] = dict(mt=mt, p=jnp.exp(logD - mt), w_int=jnp.exp(inter - mt), m_new=m_new,
                        w_old=jnp.exp(bL + m_prev - m_new), w_tok=to_col(jnp.exp(g_row - m_new)))
    for b, h in pairs:
        cols = slice(h * HD, (h + 1) * HD)
        d = st[b, h]
        q = q_ref[b, :, cols]
        d["ks"] = k_ref[b, :, cols] * (HD ** -0.5)
        d["S"] = _dot_nt(q, d["ks"]) * d["p"]
        d["Cq"] = _dot_nt(q, C_ref[b, h].astype(BF16))
        n_hi, n_lo = _split_bf16(jnp.broadcast_to(n_ref[b, h], (8, HD)))
        d["nq"] = (_dot_nt(q, n_hi) + _dot_nt(q, n_lo))[:, :1]
    for b, h in pairs:
        cols = slice(h * HD, (h + 1) * HD)
        d = st[b, h]
        num = _dot(d["S"].astype(BF16), v_ref[b, :, cols]) + d["w_int"] * d["Cq"]
        den = jnp.sum(d["S"], axis=1, keepdims=True) + d["w_int"] * d["nq"]
        hh = num / jnp.maximum(jnp.abs(den), jnp.exp(-d["mt"]))
        hg = jax.nn.sigmoid(o_ref[b, :, cols].astype(F32)) * hh
        hm_ref[b, :, cols] = (_rms(hg) * gh_ref[h]).astype(hm_ref.dtype)
    for b, h in pairs:
        cols = slice(h * HD, (h + 1) * HD)
        d = st[b, h]
        kw = d["ks"].astype(F32) * d["w_tok"]
        C_ref[b, h] = d["w_old"] * C_ref[b, h] + _dot_tn(v_ref[b, :, cols], kw.astype(BF16))
        n_ref[b, h] = d["w_old"] * n_ref[b, h] + jnp.sum(kw, axis=0, keepdims=True)
        m_ref[b, h] = jnp.broadcast_to(d["m_new"], (1, LANES))


def _mlstm_prompt(z, gates_t, g_head, batch, seq):
    nc = seq // ML_CHUNK
    nb = MLSTM_SEQS
    z3 = z.reshape(batch, seq, z.shape[1])

    def zspec(col):
        return pl.BlockSpec((nb, ML_CHUNK, D), lambda b, c: (b, c, col))

    def state(last):
        return pl.BlockSpec((nb, HEADS) + last, lambda b, c: (b, 0, 0, 0))

    return pl.pallas_call(
        _mlstm_prompt_kernel,
        grid=(batch // nb, nc),
        in_specs=[zspec(0), zspec(1), zspec(2), zspec(3),
                  pl.BlockSpec((nb, 2 * HEADS, ML_CHUNK), lambda b, c: (b, 0, c)),
                  pl.BlockSpec((HEADS, 1, HD), lambda b, c: (0, 0, 0))],
        out_specs=[pl.BlockSpec((nb, ML_CHUNK, D), lambda b, c: (b, c, 0)),
                   state((HD, HD)), state((1, HD)), state((1, LANES))],
        out_shape=[jax.ShapeDtypeStruct((batch, seq, D), BF16),
                   jax.ShapeDtypeStruct((batch, HEADS, HD, HD), F32),
                   jax.ShapeDtypeStruct((batch, HEADS, 1, HD), F32),
                   jax.ShapeDtypeStruct((batch, HEADS, 1, LANES), F32)],
        compiler_params=pltpu.CompilerParams(dimension_semantics=("arbitrary", "arbitrary")),
        name="mlstm_prompt",
    )(z3, z3, z3, z3, gates_t, g_head.reshape(HEADS, 1, HD))


def _mlstm_sample_kernel(seq_len, q_ref, k_ref, v_ref, o_ref, gc_ref, m0_ref, C0_ref, n0_ref, gh_ref,
                         hm_ref, C_ref, n_ref, m_ref):
    R = q_ref.shape[0]
    nseq = R // seq_len
    r = lax.broadcasted_iota(jnp.int32, (R, R), 0)
    s = lax.broadcasted_iota(jnp.int32, (R, R), 1)
    rseq = lax.broadcasted_iota(jnp.int32, (R, 1), 0) // seq_len
    eye = r == s
    same = (r // seq_len) == (s // seq_len)
    causal = same & (s <= r)

    def to_row(x_col):
        return jnp.sum(jnp.where(eye, x_col, 0.0), axis=0, keepdims=True)

    gc = gc_ref[...]
    for h in range(HEADS):
        cols = slice(h * HD, (h + 1) * HD)
        ig_col = gc[:, h:h + 1]
        lf_col = _log_sigmoid(gc[:, HEADS + h:HEADS + h + 1])
        lf_row = to_row(lf_col)
        b_col = jnp.sum(jnp.where(causal, lf_row, 0.0), axis=1, keepdims=True)
        bL_col = jnp.sum(jnp.where(same, lf_row, 0.0), axis=1, keepdims=True)
        b_row = to_row(b_col)
        ig_row = to_row(ig_col)
        m0_col = m0_ref[:, h:h + 1]

        logD = jnp.where(causal, b_col - b_row + ig_row, -jnp.inf)
        inter = b_col + m0_col
        mt = jnp.maximum(jnp.max(logD, axis=1, keepdims=True), inter)
        qf = q_ref[:, cols]
        q = qf.astype(BF16)
        ksf = k_ref[:, cols] * (HD ** -0.5)
        v = v_ref[:, cols].astype(BF16)
        S = _dot_nt(q, ksf.astype(BF16)) * jnp.exp(logD - mt)
        w_int = jnp.exp(inter - mt)

        Cq = jnp.zeros((R, HD), F32)
        nq = jnp.zeros((R, 1), F32)
        for g in range(nseq):
            Cq = jnp.where(rseq == g, _dot_nt(q, C0_ref[g, h].astype(BF16)), Cq)
            nq = jnp.where(rseq == g, jnp.sum(qf * n0_ref[g, h], axis=1, keepdims=True), nq)
        num = _dot(S.astype(BF16), v) + w_int * Cq
        den = jnp.sum(S, axis=1, keepdims=True) + w_int * nq
        hh = num / jnp.maximum(jnp.abs(den), jnp.exp(-mt))
        hg = jax.nn.sigmoid(o_ref[:, cols]) * hh
        hm_ref[:, cols] = (_rms(hg) * gh_ref[h]).astype(hm_ref.dtype)

        g_col = bL_col - b_col + ig_col
        gmax_col = jnp.max(jnp.where(same, to_row(g_col), -jnp.inf), axis=1, keepdims=True)
        m_new_col = jnp.maximum(bL_col + m0_col, gmax_col)
        w_old_col = jnp.exp(bL_col + m0_col - m_new_col)
        kw = ksf * jnp.exp(g_col - m_new_col)
        for g in range(nseq):
            kw_g = jnp.where(rseq == g, kw, 0.0)
            w_old = w_old_col[g * seq_len:g * seq_len + 1, :]
            C_ref[g, h] = w_old * C0_ref[g, h] + _dot_tn(v, kw_g.astype(BF16))
            n_ref[g, h] = w_old * n0_ref[g, h] + jnp.sum(kw_g, axis=0, keepdims=True)
        m_ref[:, h * LANES:(h + 1) * LANES] = jnp.broadcast_to(m_new_col, (R, LANES))


def _mlstm_sample(z, gates_c, m0_tok, C0, n0, g_head, batch, seq):
    rows = SAMPLE_SEQS * seq
    m = batch * seq

    def zspec(col):
        return pl.BlockSpec((rows, D), lambda i: (i, col))

    state_c = pl.BlockSpec((SAMPLE_SEQS, HEADS, HD, HD), lambda i: (i, 0, 0, 0))
    state_n = pl.BlockSpec((SAMPLE_SEQS, HEADS, 1, HD), lambda i: (i, 0, 0, 0))
    return pl.pallas_call(
        functools.partial(_mlstm_sample_kernel, seq),
        grid=(batch // SAMPLE_SEQS,),
        in_specs=[zspec(0), zspec(1), zspec(2), zspec(3),
                  pl.BlockSpec((rows, LANES), lambda i: (i, 0)),
                  pl.BlockSpec((rows, HEADS), lambda i: (i, 0)),
                  state_c, state_n,
                  pl.BlockSpec((HEADS, 1, HD), lambda i: (0, 0, 0))],
        out_specs=[pl.BlockSpec((rows, D), lambda i: (i, 0)),
                   state_c, state_n,
                   pl.BlockSpec((rows, HEADS * LANES), lambda i: (i, 0))],
        out_shape=[jax.ShapeDtypeStruct((m, D), BF16),
                   jax.ShapeDtypeStruct((batch, HEADS, HD, HD), F32),
                   jax.ShapeDtypeStruct((batch, HEADS, 1, HD), F32),
                   jax.ShapeDtypeStruct((m, HEADS * LANES), F32)],
        compiler_params=pltpu.CompilerParams(
            dimension_semantics=("arbitrary",), vmem_limit_bytes=VMEM_LIMIT),
        name="mlstm_sample",
    )(z, z, z, z, gates_c, m0_tok, C0, n0.reshape(batch, HEADS, 1, HD), g_head.reshape(HEADS, 1, HD))


def _mix_kernel(x_ref, hm_ref, u_ref, v_ref, ga_ref, gb_ref, gt1_ref, sh2_ref, sc2_ref,
                gsgu_ref, bsgu_ref, ws_ref, bs_ref, wpa_ref, wpb_ref, wout_ref, gffn_ref,
                wr_ref, br_ref, x1_ref, h2_ref, choice_ref, gate_ref, vg_ref, cnt_ref, yg_scr):
    tm = x_ref.shape[0]
    u = jax.nn.gelu(u_ref[...].astype(F32))
    vv = jax.nn.gelu(v_ref[...].astype(F32))
    mu = jnp.mean(vv, axis=-1, keepdims=True)
    var = jnp.mean(jnp.square(vv - mu), axis=-1, keepdims=True)
    vg = (vv - mu) * lax.rsqrt(var + NORM_EPS) * gsgu_ref[...] + bsgu_ref[...]
    vg_ref[...] = vg
    vgb = vg.astype(BF16)

    r = lax.broadcasted_iota(jnp.int32, (GM_CHUNK, GM_CHUNK), 0)
    s = lax.broadcasted_iota(jnp.int32, (GM_CHUNK, GM_CHUNK), 1)
    for g in range(GROUPS):
        w = jnp.where(s <= r, ws_ref[g], 0.0).astype(BF16)
        bias = bs_ref[:, g:g + 1]
        for c in range(tm // GM_CHUNK):
            rows = slice(c * GM_CHUNK, (c + 1) * GM_CHUNK)
            cols = slice(g * GD, (g + 1) * GD)
            mixed = _dot(w, vgb[rows, cols]) + bias
            yg_scr[rows, cols] = (u[rows, cols] * mixed).astype(BF16)

    a = _dot(hm_ref[...], wpa_ref[...])
    b = _dot(yg_scr[...], wpb_ref[...])
    merged = (jax.nn.sigmoid(ga_ref[...].astype(F32)) * a
              + jax.nn.sigmoid(gb_ref[...].astype(F32)) * b)
    x1 = x_ref[...] + gt1_ref[...] * _dot(merged.astype(BF16), wout_ref[...])
    x1_ref[...] = x1
    h2 = (_rms(x1) * gffn_ref[...]) * (1.0 + sc2_ref[...]) + sh2_ref[...]
    h2_ref[...] = _pack_rows(h2)

    lane = lax.broadcasted_iota(jnp.int32, (tm, LANES), 1)
    lane_f = lane.astype(F32)
    lg = jnp.where(lane < N_EXPERTS, _dot3(h2, wr_ref[...]) + br_ref[...], -jnp.inf)
    choice = jnp.zeros((tm, LANES), F32)
    vals = []
    for k in range(TOP_K):
        mx = jnp.max(lg, axis=1, keepdims=True)
        sel = lane_f == jnp.min(jnp.where(lg == mx, lane_f, float(LANES)), axis=1, keepdims=True)
        choice = jnp.where(sel, k + 1.0, choice)
        vals.append(mx)
        lg = jnp.where(sel, -jnp.inf, lg)
    choice_ref[...] = choice
    ex = [jnp.exp(v - vals[0]) for v in vals]
    denom = sum(ex)
    gates = jnp.zeros((tm, LANES), F32)
    for k in range(TOP_K):
        gates = jnp.where(lane == k, ex[k] / denom, gates)
    gate_ref[...] = gates[:, :TOP_K]

    @pl.when(pl.program_id(0) == 0)
    def _():
        cnt_ref[...] = jnp.zeros_like(cnt_ref)

    cnt_ref[...] += jnp.sum(jnp.where(choice > 0.0, 1.0, 0.0), axis=0, keepdims=True)


def _mix(x, hm, z, mod, per_token, tokens_per_seq, tm, p, keep_v_rows):
    m = x.shape[0]
    zcol = 4
    v_rows = m if keep_v_rows else tm
    v_spec = pl.BlockSpec((tm, D), (lambda i: (i, 0)) if keep_v_rows else (lambda i: (0, 0)))

    def zspec(blk):
        return pl.BlockSpec((tm, D), lambda i: (i, blk))

    def full(shape):
        return pl.BlockSpec(shape, lambda i: (0,) * len(shape))

    row = pl.BlockSpec((tm, D), lambda i: (i, 0))
    return pl.pallas_call(
        _mix_kernel,
        grid=(m // tm,),
        in_specs=[row, row, zspec(zcol), zspec(zcol + 1), zspec(zcol + 2), zspec(zcol + 3),
                  _mod_spec(per_token, tm, tokens_per_seq, 2),
                  _mod_spec(per_token, tm, tokens_per_seq, 3),
                  _mod_spec(per_token, tm, tokens_per_seq, 4),
                  full((1, D)), full((1, D)),
                  full((GROUPS, GM_CHUNK, GM_CHUNK)), full((GM_CHUNK, GROUPS)),
                  full((D, D)), full((D, D)), full((D, D)), full((1, D)),
                  full((D, LANES)), full((1, LANES))],
        out_specs=[row, pl.BlockSpec((tm, PACKED), lambda i: (i, 0)),
                   pl.BlockSpec((tm, LANES), lambda i: (i, 0)), pl.BlockSpec((tm, TOP_K), lambda i: (i, 0)),
                   v_spec, full((1, LANES))],
        out_shape=[jax.ShapeDtypeStruct((m, D), F32),
                   jax.ShapeDtypeStruct((m, PACKED), jnp.int32),
                   jax.ShapeDtypeStruct((m, LANES), F32),
                   jax.ShapeDtypeStruct((m, TOP_K), F32),
                   jax.ShapeDtypeStruct((v_rows, D), F32),
                   jax.ShapeDtypeStruct((1, LANES), F32)],
        scratch_shapes=[pltpu.VMEM((tm, D), BF16)],
        compiler_params=pltpu.CompilerParams(
            dimension_semantics=("arbitrary",), vmem_limit_bytes=VMEM_LIMIT),
        name="mix",
    )(x, hm, z, z, z, z, mod, mod, mod, p["g_sgu"], p["b_sgu"], p["w_s"], p["b_s"],
      p["w_pa"], p["w_pb"], p["w_out"], p["g_ffn"], p["w_router"], p["b_router"])


def _route_kernel(tiles_a, cha_ref, chb_ref, cnta_ref, cntb_ref, dest_ref, ex_ref, base_scr):
    tm = cha_ref.shape[0]
    lane = lax.broadcasted_iota(jnp.int32, (tm, LANES), 1)
    choice = jnp.where(pl.program_id(0) < tiles_a, cha_ref[...], chb_ref[...])
    onehot = jnp.where(choice > 0.0, 1.0, 0.0)

    @pl.when(pl.program_id(0) == 0)
    def _():
        cnt = cnta_ref[...] + cntb_ref[...]
        padded = jnp.floor((cnt + (MOE_ROWS - 1)) * (1.0 / MOE_ROWS)) * MOE_ROWS
        r = lax.broadcasted_iota(jnp.int32, (LANES, LANES), 0)
        s = lax.broadcasted_iota(jnp.int32, (LANES, LANES), 1)
        padded_col = jnp.sum(jnp.where(r == s, padded, 0.0), axis=1, keepdims=True)
        pstart = jnp.sum(jnp.where(r < s, padded_col, 0.0), axis=0, keepdims=True)
        base_scr[...] = pstart
        trow = lax.broadcasted_iota(jnp.int32, ex_ref.shape, 0)
        table = jnp.where(trow == 0, pstart * (1.0 / MOE_ROWS),
                          jnp.where(trow == 1, padded * (1.0 / MOE_ROWS), jnp.where(trow == 2, cnt, 0.0)))
        ex_ref[...] = table.astype(jnp.int32)

    r = lax.broadcasted_iota(jnp.int32, (tm, tm), 0)
    s = lax.broadcasted_iota(jnp.int32, (tm, tm), 1)
    before = _dot((s < r).astype(BF16), onehot.astype(BF16))
    slot = before + base_scr[...]
    dest = jnp.zeros((tm, LANES), F32)
    for k in range(TOP_K):
        d_k = jnp.sum(jnp.where(choice == k + 1.0, slot, 0.0), axis=1, keepdims=True)
        dest = jnp.where(lane == k, d_k, dest)
    dest_ref[...] = dest[:, :TOP_K].astype(jnp.int32)
    base_scr[...] += jnp.sum(onehot, axis=0, keepdims=True)


def _route(choice_a, choice_b, cnt_a, cnt_b):
    tm = min(ROW_TILE, choice_a.shape[0], choice_b.shape[0])
    assert choice_a.shape[0] % tm == 0 and choice_b.shape[0] % tm == 0
    tiles_a, tiles_b = choice_a.shape[0] // tm, choice_b.shape[0] // tm
    n = (tiles_a + tiles_b) * tm
    fixed = lambda i: (0, 0)
    return pl.pallas_call(
        functools.partial(_route_kernel, tiles_a),
        grid=(tiles_a + tiles_b,),
        in_specs=[pl.BlockSpec((tm, LANES), lambda i: (jnp.minimum(i, tiles_a - 1), 0)),
                  pl.BlockSpec((tm, LANES), lambda i: (jnp.maximum(i - tiles_a, 0), 0)),
                  pl.BlockSpec((1, LANES), fixed), pl.BlockSpec((1, LANES), fixed)],
        out_specs=[pl.BlockSpec((tm, TOP_K), lambda i: (i, 0)), pl.BlockSpec((8, LANES), fixed)],
        out_shape=[jax.ShapeDtypeStruct((n, TOP_K), jnp.int32),
                   jax.ShapeDtypeStruct((8, LANES), jnp.int32)],
        scratch_shapes=[pltpu.VMEM((1, LANES), F32)],
        compiler_params=pltpu.CompilerParams(dimension_semantics=("arbitrary",)),
        name="route",
    )(choice_a, choice_b, cnt_a, cnt_b)


def _sc_worker_base(per_worker):
    return (lax.axis_index("s") * SC_CORES + lax.axis_index("c")) * per_worker


def _sc_scatter_rows(rows_a, rows_b, idx_flat, n_out):
    na, nb = rows_a.shape[0], rows_b.shape[0]
    width, dtype = rows_a.shape[1], rows_a.dtype
    n = na + nb
    per_a, per_b = na // SC_WORKERS, nb // SC_WORKERS
    chunk = SC_SCATTER_CHUNK
    n_chunks = per_a // chunk
    assert per_a * SC_WORKERS == na and per_b * SC_WORKERS == nb and per_b % 8 == 0 and per_b <= chunk
    assert n_chunks * chunk == per_a and n_chunks % 2 == 0
    mesh = plsc.VectorSubcoreMesh(core_axis_name="c", subcore_axis_name="s")

    @functools.partial(
        pl.kernel, mesh=mesh,
        out_type=jax.ShapeDtypeStruct((n_out, width), dtype),
        scratch_types=[pltpu.VMEM((chunk,), jnp.int32)] * TOP_K + [pltpu.VMEM((per_b,), jnp.int32)]
                      + [pltpu.VMEM((chunk, width), dtype), pltpu.VMEM((chunk, width), dtype),
                         pltpu.VMEM((per_b, width), dtype)]
                      + [pltpu.SemaphoreType.DMA] * 3,
    )
    def scatter(a_hbm, b_hbm, idx_hbm, out_hbm, i0, i1, i2, i3, ib, rows0, rows1, rowsb, rsem0, rsem1, wsem):
        base = _sc_worker_base(per_a)
        idx_bufs = (i0, i1, i2, i3)
        bufs = ((rows0, rsem0), (rows1, rsem1))

        def off(j):
            return pl.multiple_of(base + j * chunk, 8)

        def read(j, buf):
            rows_v, sem = buf
            return pltpu.make_async_copy(a_hbm.at[pl.ds(off(j), chunk)], rows_v, sem)

        def spread(j, buf):
            rows_v, _ = buf
            read(j, buf).wait()
            for k in range(TOP_K):
                pltpu.sync_copy(idx_hbm.at[pl.ds(pl.multiple_of(k * n + off(j), 8), chunk)], idx_bufs[k])
            for k in range(TOP_K):
                pltpu.make_async_copy(rows_v, out_hbm.at[idx_bufs[k]], wsem).start()
            for k in range(TOP_K):
                pltpu.make_async_copy(rows_v, out_hbm.at[idx_bufs[k]], wsem).wait()

        read(0, bufs[0]).start()

        @pl.loop(0, n_chunks, step=2)
        def _(j):
            read(j + 1, bufs[1]).start()
            spread(j, bufs[0])

            @pl.when(j + 2 < n_chunks)
            def _():
                read(j + 2, bufs[0]).start()
            spread(j + 1, bufs[1])

        off_b = pl.multiple_of(_sc_worker_base(per_b), 8)
        pltpu.sync_copy(b_hbm.at[pl.ds(off_b, per_b)], rowsb)
        for k in range(TOP_K):
            pltpu.sync_copy(idx_hbm.at[pl.ds(pl.multiple_of(k * n + na + off_b, 8), per_b)], ib)
            pltpu.async_copy(rowsb, out_hbm.at[ib], wsem).wait()

    return scatter(rows_a, rows_b, idx_flat)


def _sc_gather_rows(table, idx_flat):
    b = idx_flat.shape[0]
    width, dtype = table.shape[1], table.dtype
    per_worker = b // SC_WORKERS
    chunk = SC_GATHER_CHUNK
    n_chunks = per_worker // chunk
    assert per_worker * SC_WORKERS == b and n_chunks * chunk == per_worker and n_chunks % 2 == 0
    mesh = plsc.VectorSubcoreMesh(core_axis_name="c", subcore_axis_name="s")

    @functools.partial(
        pl.kernel, mesh=mesh,
        out_type=jax.ShapeDtypeStruct((b, width), dtype),
        scratch_types=[pltpu.VMEM((chunk,), jnp.int32), pltpu.VMEM((chunk,), jnp.int32),
                       pltpu.VMEM((chunk, width), dtype), pltpu.VMEM((chunk, width), dtype),
                       pltpu.SemaphoreType.DMA, pltpu.SemaphoreType.DMA],
    )
    def gather(table_hbm, idx_hbm, out_hbm, idx0, idx1, rows0, rows1, sem0, sem1):
        base = _sc_worker_base(per_worker)
        bufs = ((idx0, rows0, sem0), (idx1, rows1, sem1))

        def off(j):
            return pl.multiple_of(base + j * chunk, 8)

        def start(j, buf):
            idx_v, rows_v, sem = buf
            pltpu.sync_copy(idx_hbm.at[pl.ds(off(j), chunk)], idx_v)
            pltpu.make_async_copy(table_hbm.at[idx_v], rows_v, sem).start()

        def finish(j, buf):
            idx_v, rows_v, sem = buf
            pltpu.make_async_copy(table_hbm.at[idx_v], rows_v, sem).wait()
            pltpu.sync_copy(rows_v, out_hbm.at[pl.ds(off(j), chunk)])

        start(0, bufs[0])

        @pl.loop(0, n_chunks, step=2)
        def _(j):
            start(j + 1, bufs[1])
            finish(j, bufs[0])

            @pl.when(j + 2 < n_chunks)
            def _():
                start(j + 2, bufs[0])
            finish(j + 1, bufs[1])

    return gather(table, idx_flat)


def _moe_kernel(first_ref, nblk_ref, cnt_ref, xs_hbm, wgu_ref, bgu_ref, wd_ref, bd_ref, out_hbm,
                xbuf, obuf, wgu_bf, wd_bf, xsem, osem):
    e = pl.program_id(0)
    first, nblk, cnt = first_ref[e], nblk_ref[e], cnt_ref[e]

    def rows(j):
        return pl.ds(pl.multiple_of((first + j) * MOE_ROWS, MOE_ROWS), MOE_ROWS)

    def x_copy(j, slot):
        return pltpu.make_async_copy(xs_hbm.at[rows(j)], xbuf.at[slot], xsem.at[slot])

    def o_copy(j, slot):
        return pltpu.make_async_copy(obuf.at[slot], out_hbm.at[rows(j)], osem.at[slot])

    @pl.when(nblk > 0)
    def _():
        x_copy(0, 0).start(priority=ROW_DMA_PRIORITY)
        wgu_bf[...] = wgu_ref[...].astype(BF16)
        wd_bf[...] = wd_ref[...].astype(BF16)

        def block(j, carry):
            slot = j % 2
            x_copy(j, slot).wait()

            @pl.when(j + 1 < nblk)
            def _():
                x_copy(j + 1, 1 - slot).start(priority=ROW_DMA_PRIORITY)

            @pl.when(j >= 2)
            def _():
                o_copy(j - 2, slot).wait()

            row = lax.broadcasted_iota(jnp.int32, (MOE_ROWS, 1), 0)
            x = _unpack_rows(jnp.where(row < cnt - j * MOE_ROWS, xbuf[slot], 0)).astype(BF16)
            gu = _dot(x, wgu_bf[...]) + bgu_ref[...]
            gate = jnp.minimum(gu[:, :D_FF], SWIGLU_LIMIT)
            up = jnp.clip(gu[:, D_FF:], -SWIGLU_LIMIT, SWIGLU_LIMIT)
            act = gate * jax.nn.sigmoid(SWIGLU_ALPHA * gate) * (up + 1.0)
            obuf[slot] = _pack_rows(_dot(act.astype(BF16), wd_bf[...]) + bd_ref[...])
            o_copy(j, slot).start(priority=ROW_DMA_PRIORITY)
            return carry

        lax.fori_loop(0, nblk, block, 0)

        @pl.when(nblk >= 2)
        def _():
            o_copy(nblk - 2, nblk % 2).wait()
        o_copy(nblk - 1, (nblk - 1) % 2).wait()


def _moe(xs, first_block, n_blocks, counts, w_gu, b_gu, w_down, b_down):
    grid_spec = pltpu.PrefetchScalarGridSpec(
        num_scalar_prefetch=3,
        grid=(N_EXPERTS,),
        in_specs=[pl.BlockSpec(memory_space=pl.ANY),
                  pl.BlockSpec((None, D, 2 * D_FF), lambda e, *_: (e, 0, 0)),
                  pl.BlockSpec((None, 1, 2 * D_FF), lambda e, *_: (e, 0, 0)),
                  pl.BlockSpec((None, D_FF, D), lambda e, *_: (e, 0, 0)),
                  pl.BlockSpec((None, 1, D), lambda e, *_: (e, 0, 0))],
        out_specs=pl.BlockSpec(memory_space=pl.ANY),
        scratch_shapes=[pltpu.VMEM((2, MOE_ROWS, PACKED), jnp.int32), pltpu.VMEM((2, MOE_ROWS, PACKED), jnp.int32),
                        pltpu.VMEM((D, 2 * D_FF), BF16), pltpu.VMEM((D_FF, D), BF16),
                        pltpu.SemaphoreType.DMA((2,)), pltpu.SemaphoreType.DMA((2,))],
    )
    return pl.pallas_call(
        _moe_kernel,
        grid_spec=grid_spec,
        out_shape=jax.ShapeDtypeStruct(xs.shape, jnp.int32),
        compiler_params=pltpu.CompilerParams(
            dimension_semantics=("arbitrary",), vmem_limit_bytes=MOE_VMEM_LIMIT),
        name="moe",
    )(first_block, n_blocks, counts, xs, w_gu, b_gu.reshape(N_EXPERTS, 1, 2 * D_FF), w_down,
      b_down.reshape(N_EXPERTS, 1, D))


def _final_kernel(x1_ref, yk_ref, gate_ref, gt2_ref, g_ref, o_ref):
    y2 = gate_ref[:, 0:1] * _unpack_rows(yk_ref[0])
    for k in range(1, TOP_K):
        y2 = y2 + gate_ref[:, k:k + 1] * _unpack_rows(yk_ref[k])
    o_ref[...] = _rms(x1_ref[...] + gt2_ref[...] * y2) * g_ref[...]


def _final(x1, yk, gates, row_off, mod, per_token, tokens_per_seq, g_final):
    m = x1.shape[0]
    tm = min(ROW_TILE, m)
    assert m % tm == 0 and row_off % tm == 0 and (per_token or tokens_per_seq % tm == 0)
    off = row_off // tm
    return pl.pallas_call(
        _final_kernel,
        grid=(m // tm,),
        in_specs=[pl.BlockSpec((tm, D), lambda i: (i, 0)),
                  pl.BlockSpec((TOP_K, tm, PACKED), lambda i: (0, i + off, 0)),
                  pl.BlockSpec((tm, TOP_K), lambda i: (i, 0)),
                  _mod_spec(per_token, tm, tokens_per_seq, 5),
                  pl.BlockSpec((1, D), lambda i: (0, 0))],
        out_specs=pl.BlockSpec((tm, D), lambda i: (i, 0)),
        out_shape=jax.ShapeDtypeStruct((m, D), F32),
        compiler_params=pltpu.CompilerParams(
            dimension_semantics=("arbitrary",), vmem_limit_bytes=VMEM_LIMIT),
        name="final",
    )(x1, yk, gates, mod, g_final.reshape(1, D))


def kernel(x_prompt, x_sample, state_C, state_n, state_m, c_prompt, c_sample, w_ada, b_ada, g_mix, w_in,
           b_if, g_head, g_sgu, b_sgu, w_s, b_s, w_pa, w_pb, w_out, g_ffn, w_router, b_router, w_gu, b_gu,
           w_down, b_down, g_final):
    depth = w_ada.shape[0]
    assert depth == 1
    bp, tp, _ = x_prompt.shape
    bs, ts, _ = x_sample.shape
    mp, ms = bp * tp, bs * ts
    assert tp % ML_CHUNK == 0 and ts <= ML_CHUNK and GM_CHUNK % ts == 0

    w_in0 = w_in[0]
    nqkvo = 4 * D
    w_qkvo = w_in0[:, :nqkvo].astype(BF16)
    w_gate = w_in0[:, nqkvo + 2 * HEADS:].astype(BF16)
    w_if = jnp.pad(w_in0[:, nqkvo:nqkvo + 2 * HEADS], ((0, 0), (0, LANES - 2 * HEADS)))
    b_if_p = jnp.pad(b_if[0], (0, LANES - 2 * HEADS)).reshape(1, LANES)
    reps = GM_CHUNK // ts
    eye_r = jnp.eye(reps, dtype=F32)
    w_s_sample = jnp.einsum("ab,gts->gatbs", eye_r, w_s[0][:, :ts, :ts]).reshape(GROUPS, GM_CHUNK, GM_CHUNK)
    b_s_prompt = b_s[0].T
    b_s_sample = jnp.tile(b_s[0][:, :ts].T, (reps, 1))
    mix_p = {
        "g_sgu": g_sgu[0].reshape(1, D), "b_sgu": b_sgu[0].reshape(1, D),
        "w_pa": w_pa[0].astype(BF16), "w_pb": w_pb[0].astype(BF16), "w_out": w_out[0].astype(BF16),
        "g_ffn": g_ffn[0].reshape(1, D),
        "w_router": jnp.pad(w_router[0], ((0, 0), (0, LANES - N_EXPERTS))),
        "b_router": jnp.pad(b_router[0], (0, LANES - N_EXPERTS)).reshape(1, LANES),
    }
    mix_prompt = dict(mix_p, w_s=w_s[0], b_s=b_s_prompt)
    mix_sample = dict(mix_p, w_s=w_s_sample, b_s=b_s_sample)

    mod = _ada(jnp.concatenate([jnp.repeat(c_sample, ts, axis=0), c_prompt], axis=0), w_ada[0], b_ada[0])
    mod_s = mod
    mod_p = mod[ms:].reshape(bp, 1, N_MOD * D)

    xp = x_prompt.reshape(mp, D)
    xs = x_sample.reshape(ms, D)
    z_p, _, gt_p = _in_proj(xp, mod_p, False, tp, g_mix[0], w_qkvo, w_gate, w_if, b_if_p, 512, BF16)
    z_s, gc_s, _ = _in_proj(xs, mod_s, True, ts, g_mix[0], w_qkvo, w_gate, w_if, b_if_p, min(256, ms), F32)

    hm_p, C_p, n_p, m_p = _mlstm_prompt(z_p, gt_p, g_head[0], bp, tp)
    hm_p = hm_p.reshape(mp, D)
    m0_tok = jnp.repeat(state_m[0], ts, axis=0)
    hm_s, C_s, n_s, m_s = _mlstm_sample(z_s, gc_s, m0_tok, state_C[0], state_n[0], g_head[0], bs, ts)

    x1_p, h2_p, ch_p, gates_p, _, cnt_p = _mix(xp, hm_p, z_p, mod_p, False, tp, 512, mix_prompt, False)
    x1_s, h2_s, ch_s, gates_s, vg_s, cnt_s = _mix(xs, hm_s, z_s, mod_s, True, ts, min(512, ms), mix_sample, True)

    n_tok = mp + ms
    n_blocks = -(-(n_tok * TOP_K + N_EXPERTS * (MOE_ROWS - 1)) // MOE_ROWS)
    dest, ex_table = _route(ch_p, ch_s, cnt_p, cnt_s)
    dest_kmajor = dest.T.reshape(TOP_K * n_tok)
    xslots = _sc_scatter_rows(h2_p, h2_s, dest_kmajor, n_blocks * MOE_ROWS)
    yb = _moe(xslots, ex_table[0, :N_EXPERTS], ex_table[1, :N_EXPERTS], ex_table[2, :N_EXPERTS],
              w_gu[0], b_gu[0], w_down[0], b_down[0])
    yk = _sc_gather_rows(yb, dest_kmajor).reshape(TOP_K, n_tok, PACKED)

    y_p = _final(x1_p, yk, gates_p, 0, mod_p, False, tp, g_final)
    y_s = _final(x1_s, yk, gates_s, mp, mod_s, True, ts, g_final)

    return (y_p.reshape(bp, tp, D), y_s.reshape(bs, ts, D),
            C_p[None], n_p.reshape(1, bp, HEADS, HD), m_p[:, :, 0, 0][None],
            C_s[None], n_s.reshape(1, bs, HEADS, HD),
            m_s.reshape(bs, ts, HEADS, LANES)[:, 0, :, 0][None],
            vg_s.reshape(1, bs, ts, D))
```
